```python
import math
import jax, jax.numpy as jnp
from jax import lax
import numpy as np

D_MODEL = 2048
BATCH = 4
SEQ = 2048
DEPTH = 1
DEC_BATCH = 128
DEC_SEQ = 4
PAST_LEN = 16384
PAGE_SIZE = 128

A_WIDTH = D_MODEL // 2
N_HEADS_A = 4
DV_A = A_WIDTH // N_HEADS_A
DK_A = DV_A // 2
B_WIDTH = D_MODEL - A_WIDTH
DK_B = 128
N_HEADS_B = B_WIDTH // DK_B
DV_B = B_WIDTH // N_HEADS_B
D_FF = 5504
CONV_W = 3
CHUNK = 64
EPS = 1e-5
ALPHA = (2 * DEPTH) ** 0.25
BETA = (8 * DEPTH) ** -0.25
PROJ_SIZES = (N_HEADS_A * DK_A, N_HEADS_A * DK_A, A_WIDTH, N_HEADS_A, N_HEADS_A, A_WIDTH,
              N_HEADS_B * DK_B, N_HEADS_B * DK_B, B_WIDTH, B_WIDTH)
D_IN = sum(PROJ_SIZES)

kernel_name = 'hymba_mlstm_hgrn2_convffn_deepnorm_adaln_step'


def _split(z, sizes):
    out = []
    start = 0
    for s in sizes:
        out.append(z[..., start:start + s])
        start += s
    return out


def layer_norm(x, g=None, b=None):
    xf = x.astype(jnp.float32)
    mu = jnp.mean(xf, axis=-1, keepdims=True)
    var = jnp.mean(jnp.square(xf - mu), axis=-1, keepdims=True)
    y = (xf - mu) * lax.rsqrt(var + EPS)
    if g is not None:
        y = y * g.astype(jnp.float32) + b.astype(jnp.float32)
    return y.astype(x.dtype)


def _head_rms(h, g):
    hn = h * lax.rsqrt(jnp.mean(jnp.square(h), axis=-1, keepdims=True) + EPS)
    return hn.reshape(h.shape[0], h.shape[1], -1) * g.astype(jnp.float32)


def _to_chunks(t, L):
    B, T, H, d = t.shape
    return t.reshape(B, T // L, L, H, d).transpose(1, 0, 3, 2, 4)


def _from_chunks(t):
    NC, B, H, L, d = t.shape
    return t.transpose(1, 0, 3, 2, 4).reshape(B, NC * L, H, d)


def _mlstm_chunk(carry, xs):
    C, n, m = carry
    q, k, v, logi, logf = xs
    L = q.shape[2]
    b = jnp.cumsum(logf, axis=-1)
    causal = jnp.tril(jnp.ones((L, L), dtype=bool))
    D = jnp.where(causal, b[..., :, None] - b[..., None, :] + logi[..., None, :], -jnp.inf)
    inter = b + m[..., None]
    m_t = jnp.maximum(inter, jnp.max(D, axis=-1))
    w_inter = jnp.exp(inter - m_t)
    S = jnp.einsum('bhtd,bhsd->bhts', q, k) * jnp.exp(D - m_t[..., None])
    num = w_inter[..., None] * jnp.einsum('bhtd,bhde->bhte', q, C) + jnp.einsum('bhts,bhse->bhte', S, v)
    den = w_inter * jnp.einsum('bhtd,bhd->bht', q, n) + jnp.sum(S, axis=-1)
    h = num / jnp.maximum(jnp.abs(den), jnp.exp(-m_t))[..., None]
    dec_end = b[..., -1:] - b + logi
    inter_end = b[..., -1] + m
    m_new = jnp.maximum(inter_end, jnp.max(dec_end, axis=-1))
    wk = jnp.exp(dec_end - m_new[..., None])
    sc = jnp.exp(inter_end - m_new)
    C_new = sc[..., None, None] * C + jnp.einsum('bhs,bhsd,bhse->bhde', wk, k, v)
    n_new = sc[..., None] * n + jnp.einsum('bhs,bhsd->bhd', wk, k)
    return (C_new, n_new, m_new), h


def _hgrn2_chunk(S, xs):
    q, k, logf, v = xs
    L = q.shape[2]
    b = jnp.cumsum(logf, axis=2)
    causal = jnp.tril(jnp.ones((L, L), dtype=bool))
    dec = jnp.where(causal[:, :, None], b[:, :, :, None, :] - b[:, :, None, :, :], -jnp.inf)
    A = jnp.einsum('bhtd,bhtsd,bhsd->bhts', q, jnp.exp(dec), k)
    o = jnp.einsum('bhtd,bhde->bhte', q * jnp.exp(b), S) + jnp.einsum('bhts,bhse->bhte', A, v)
    wk = k * jnp.exp(b[:, :, -1:] - b)
    S_new = jnp.exp(b[:, :, -1])[..., None] * S + jnp.einsum('bhsd,bhse->bhde', wk, v)
    return S_new, o


def _layer(x, c, C0, n0, m0, S0, buf0, lb, w_ada, b_ada, w_in, b_gate_a, norm_a, norm_b,
           w_out, ln1_g, ln1_b, w_up, conv_w, conv_b, w_down, ln2_g, ln2_b):
    B, T, _ = x.shape
    L = math.gcd(T, CHUNK)
    f32 = jnp.float32
    mod = jax.nn.silu(c) @ w_ada + b_ada
    sh1, sc1, g1, sh2, sc2, g2 = jnp.split(mod[:, None, :], 6, axis=-1)
    h = layer_norm(x) * (1 + sc1) + sh1
    z = (h @ w_in).astype(f32)
    qa, ka, va, ia, fa, oa, fb, qb, vb, gb = _split(z, PROJ_SIZES)
    qa = qa.reshape(B, T, N_HEADS_A, DK_A) * DK_A ** -0.5
    ka = ka.reshape(B, T, N_HEADS_A, DK_A)
    va = va.reshape(B, T, N_HEADS_A, DV_A)
    logi = ia + b_gate_a[0].astype(f32)
    logf = jax.nn.log_sigmoid(fa + b_gate_a[1].astype(f32))
    xs_a = (_to_chunks(qa, L), _to_chunks(ka, L), _to_chunks(va, L),
            _to_chunks(logi[..., None], L)[..., 0], _to_chunks(logf[..., None], L)[..., 0])
    (C1, n1, m1), ha = lax.scan(_mlstm_chunk, (C0.astype(f32), n0.astype(f32), m0.astype(f32)), xs_a)
    ya = _head_rms(_from_chunks(ha), norm_a) * jax.nn.sigmoid(oa)
    lbh = lb.astype(f32).reshape(N_HEADS_B, DK_B)
    fb = fb.reshape(B, T, N_HEADS_B, DK_B)
    logfb = jnp.log(lbh + (1 - lbh) * jax.nn.sigmoid(fb))
    kb = (1 - lbh) * jax.nn.sigmoid(-fb)
    qb = qb.reshape(B, T, N_HEADS_B, DK_B)
    vb = vb.reshape(B, T, N_HEADS_B, DV_B)
    xs_b = (_to_chunks(qb, L), _to_chunks(kb, L), _to_chunks(logfb, L), _to_chunks(vb, L))
    S1, hb = lax.scan(_hgrn2_chunk, S0.astype(f32), xs_b)
    yb = _head_rms(_from_chunks(hb), norm_b) * jax.nn.silu(gb)
    mix = jnp.concatenate([ya, yb], axis=-1).astype(x.dtype) @ w_out
    x1 = layer_norm(ALPHA * x + g1 * mix, ln1_g, ln1_b)
    h2 = layer_norm(x1) * (1 + sc2) + sh2
    a, u = jnp.split(h2 @ w_up, 2, axis=-1)
    ext = jnp.concatenate([buf0.astype(a.dtype), a], axis=1)
    conv = conv_b
    for j in range(CONV_W):
        conv = conv + conv_w[j] * ext[:, j:j + T]
    ff = (jax.nn.gelu(conv, approximate=False) * u) @ w_down
    x2 = layer_norm(ALPHA * x1 + g2 * ff, ln2_g, ln2_b)
    dt = x.dtype
    return x2, (C1.astype(dt), n1.astype(dt), m1.astype(dt), S1.astype(dt), ext[:, T:].astype(dt))


def setup_inputs(seed: int = 0) -> dict:
    key = jax.random.key(seed)
    ks = jax.random.split(key, 26)
    nrm = jax.random.normal
    f32 = jnp.float32
    d_inv = D_MODEL ** -0.5
    inp = {}
    inp['x_prompt'] = nrm(ks[0], (BATCH, SEQ, D_MODEL), f32)
    inp['x_sample'] = nrm(ks[1], (DEC_BATCH, DEC_SEQ, D_MODEL), f32)
    inp['state_mlstm_C'] = 0.5 * nrm(ks[2], (DEPTH, DEC_BATCH, N_HEADS_A, DK_A, DV_A), f32)
    inp['state_mlstm_n'] = nrm(ks[3], (DEPTH, DEC_BATCH, N_HEADS_A, DK_A), f32)
    inp['state_mlstm_m'] = 0.5 * nrm(ks[4], (DEPTH, DEC_BATCH, N_HEADS_A), f32)
    inp['state_hgrn_S'] = 0.5 * nrm(ks[5], (DEPTH, DEC_BATCH, N_HEADS_B, DK_B, DV_B), f32)
    inp['cache_ffn_conv'] = 0.7 * nrm(ks[6], (DEPTH, DEC_BATCH, CONV_W - 1, D_FF), f32)
    inp['c_prompt'] = nrm(ks[7], (BATCH, D_MODEL), f32)
    inp['c_sample'] = nrm(ks[8], (DEC_BATCH, D_MODEL), f32)
    inp['hgrn_lb_logits'] = 0.5 * nrm(ks[9], (DEPTH + 1, N_HEADS_B * DK_B), f32)
    inp['w_ada'] = 0.5 * d_inv * nrm(ks[10], (DEPTH, D_MODEL, 6 * D_MODEL), f32)
    inp['b_ada'] = 0.02 * nrm(ks[11], (DEPTH, 6 * D_MODEL), f32)
    inp['w_in'] = d_inv * nrm(ks[12], (DEPTH, D_MODEL, D_IN), f32)
    i_bias = 0.1 * nrm(ks[13], (DEPTH, 1, N_HEADS_A), f32)
    f_bias = 3.0 + 3.0 * jax.random.uniform(ks[14], (DEPTH, 1, N_HEADS_A), f32)
    inp['b_gate_a'] = jnp.concatenate([i_bias, f_bias], axis=1)
    inp['norm_a'] = 1.0 + 0.02 * nrm(ks[15], (DEPTH, A_WIDTH), f32)
    inp['norm_b'] = 1.0 + 0.02 * nrm(ks[16], (DEPTH, B_WIDTH), f32)
    inp['w_out'] = BETA * d_inv * nrm(ks[17], (DEPTH, D_MODEL, D_MODEL), f32)
    inp['ln1_g'] = 1.0 + 0.02 * nrm(ks[18], (DEPTH, D_MODEL), f32)
    inp['ln1_b'] = 0.02 * nrm(ks[19], (DEPTH, D_MODEL), f32)
    inp['w_up'] = BETA * d_inv * nrm(ks[20], (DEPTH, D_MODEL, 2 * D_FF), f32)
    inp['conv_w'] = CONV_W ** -0.5 * nrm(ks[21], (DEPTH, CONV_W, D_FF), f32)
    inp['conv_b'] = 0.02 * nrm(ks[22], (DEPTH, D_FF), f32)
    inp['w_down'] = BETA * D_FF ** -0.5 * nrm(ks[23], (DEPTH, D_FF, D_MODEL), f32)
    inp['ln2_g'] = 1.0 + 0.02 * nrm(ks[24], (DEPTH, D_MODEL), f32)
    inp['ln2_b'] = 0.02 * nrm(ks[25], (DEPTH, D_MODEL), f32)
    return inp


def reference(x_prompt, x_sample, state_mlstm_C, state_mlstm_n, state_mlstm_m, state_hgrn_S,
              cache_ffn_conv, c_prompt, c_sample, hgrn_lb_logits, w_ada, b_ada, w_in, b_gate_a,
              norm_a, norm_b, w_out, ln1_g, ln1_b, w_up, conv_w, conv_b, w_down, ln2_g, ln2_b):
    lb_all = jnp.cumsum(jax.nn.softmax(hgrn_lb_logits.astype(jnp.float32), axis=0), axis=0)
    Bp = x_prompt.shape[0]
    dt = x_prompt.dtype
    yp, ys = x_prompt, x_sample
    sp_list, ss_list = [], []
    for l in range(DEPTH):
        params = (lb_all[l], w_ada[l], b_ada[l], w_in[l], b_gate_a[l], norm_a[l], norm_b[l], w_out[l],
                  ln1_g[l], ln1_b[l], w_up[l], conv_w[l], conv_b[l], w_down[l], ln2_g[l], ln2_b[l])
        yp, sp = _layer(yp, c_prompt,
                        jnp.zeros((Bp, N_HEADS_A, DK_A, DV_A), dt), jnp.zeros((Bp, N_HEADS_A, DK_A), dt),
                        jnp.zeros((Bp, N_HEADS_A), dt), jnp.zeros((Bp, N_HEADS_B, DK_B, DV_B), dt),
                        jnp.zeros((Bp, CONV_W - 1, D_FF), dt), *params)
        ys, ss = _layer(ys, c_sample, state_mlstm_C[l], state_mlstm_n[l], state_mlstm_m[l],
                        state_hgrn_S[l], cache_ffn_conv[l], *params)
        sp_list.append(sp)
        ss_list.append(ss)
    C_p = jnp.stack([s[0] for s in sp_list])
    n_p = jnp.stack([s[1] for s in sp_list])
    m_p = jnp.stack([s[2] for s in sp_list])
    S_p = jnp.stack([s[3] for s in sp_list])
    conv_p = jnp.stack([s[4] for s in sp_list])
    C_s = jnp.stack([s[0] for s in ss_list])
    n_s = jnp.stack([s[1] for s in ss_list])
    m_s = jnp.stack([s[2] for s in ss_list])
    S_s = jnp.stack([s[3] for s in ss_list])
    conv_s = jnp.stack([s[4] for s in ss_list])
    return (yp, ys, C_p, n_p, m_p, S_p, conv_p, C_s, n_s, m_s, S_s, conv_s)
```

```python
import functools

import numpy as np
import jax
import jax.numpy as jnp
from jax import lax
from jax.experimental import pallas as pl
from jax.experimental.pallas import tpu as pltpu

F32 = jnp.float32
BF16 = jnp.bfloat16

D_MODEL = 2048
N_HEADS_A, DK_A, DV_A = 4, 128, 256
N_HEADS_B, DK_B, DV_B = 8, 128, 128
A_WIDTH = N_HEADS_A * DV_A
B_WIDTH = N_HEADS_B * DV_B
D_FF = 5504
CONV_W = 3
CHUNK = 64
EPS = 1e-5
ALPHA = 2.0 ** 0.25
LANE = 128
D_FF_PAD = 5632
Z_MAIN = 7168
OFF_QA, OFF_KA, OFF_VA, OFF_OA = 0, 512, 1024, 2048
OFF_FB, OFF_QB, OFF_VB, OFF_GB = 3072, 4096, 5120, 6144
VMEM_LIMIT = 56 * 1024 * 1024


def _ln(x):
    mu = jnp.mean(x, axis=-1, keepdims=True)
    xc = x - mu
    var = jnp.mean(xc * xc, axis=-1, keepdims=True)
    return xc * lax.rsqrt(var + EPS)


def _sigmoid(x):
    return 1.0 / (1.0 + jnp.exp(-x))


def _pick_rows(ref, per_row, seq):
    if per_row:
        return ref[...]
    return ref[pl.ds(seq, 1), :]


def _dot(a, b):
    return jnp.dot(a, b, preferred_element_type=F32)


def _dot_nt(a, b):
    return lax.dot_general(a, b, (((1,), (1,)), ((), ())), preferred_element_type=F32)


def _dot_tn(a, b):
    return lax.dot_general(a, b, (((0,), (0,)), ((), ())), preferred_element_type=F32)


def _ada_kernel(c_ref, w_ref, b_ref, o_ref):
    c = c_ref[...]
    s = (c * _sigmoid(c)).astype(BF16)
    o_ref[...] = _dot(s, w_ref[...].astype(BF16)) + b_ref[...]


def _ada(c_all, w_ada, b_ada):
    rows = c_all.shape[0]
    n = w_ada.shape[1]
    bn = 1024
    return pl.pallas_call(
        _ada_kernel,
        grid=(n // bn,),
        in_specs=[pl.BlockSpec((rows, D_MODEL), lambda j: (0, 0)),
                  pl.BlockSpec((D_MODEL, bn), lambda j: (0, j)),
                  pl.BlockSpec((1, bn), lambda j: (0, j))],
        out_specs=pl.BlockSpec((rows, bn), lambda j: (0, j)),
        out_shape=jax.ShapeDtypeStruct((rows, n), F32),
        compiler_params=pltpu.CompilerParams(vmem_limit_bytes=VMEM_LIMIT),
        name="ada",
    )(c_all, w_ada, b_ada)


def _in_kernel(x_ref, sh_ref, sc_ref, w_ref, wg_ref, z_ref, g_ref, h_scr, *, per_row, tiles_per_seq):
    i = pl.program_id(0)
    j = pl.program_id(1)

    @pl.when(j == 0)
    def _():
        seq = i // tiles_per_seq
        sh = _pick_rows(sh_ref, per_row, seq)
        sc = _pick_rows(sc_ref, per_row, seq)
        h = (_ln(x_ref[...]) * (1.0 + sc) + sh).astype(BF16)
        h_scr[...] = h
        g_ref[...] = _dot(h, wg_ref[...])

    z_ref[...] = _dot(h_scr[...], w_ref[...]).astype(BF16)


def _in_proj(x, mod, w_in_p, *, tm, per_row, seq_len):
    tokens = x.shape[0]
    bn = 1024
    if per_row:
        mod_spec = lambda k: pl.BlockSpec((tm, D_MODEL), lambda i, j: (i, k))
    else:
        mod_spec = lambda k: pl.BlockSpec((8, D_MODEL), lambda i, j: (0, k))
    kern = functools.partial(_in_kernel, per_row=per_row, tiles_per_seq=max(seq_len // tm, 1))
    return pl.pallas_call(
        kern,
        grid=(tokens // tm, Z_MAIN // bn),
        in_specs=[pl.BlockSpec((tm, D_MODEL), lambda i, j: (i, 0)),
                  mod_spec(0), mod_spec(1),
                  pl.BlockSpec((D_MODEL, bn), lambda i, j: (0, j)),
                  pl.BlockSpec((D_MODEL, LANE), lambda i, j: (0, Z_MAIN // LANE))],
        out_specs=[pl.BlockSpec((tm, bn), lambda i, j: (i, j)),
                   pl.BlockSpec((tm, LANE), lambda i, j: (i, 0))],
        out_shape=[jax.ShapeDtypeStruct((tokens, Z_MAIN), BF16),
                   jax.ShapeDtypeStruct((tokens, LANE), F32)],
        scratch_shapes=[pltpu.VMEM((tm, D_MODEL), BF16)],
        compiler_params=pltpu.CompilerParams(
            dimension_semantics=("arbitrary", "arbitrary"), vmem_limit_bytes=VMEM_LIMIT),
        name="in_proj",
    )(x, mod, mod, w_in_p, w_in_p)


def _mix_consts(L, Tg):
    t = np.arange(L)[:, None]
    u = np.arange(L)[None, :]
    same = (t // Tg) == (u // Tg)
    mats = [same & (u <= t), same & (u > t)]
    masks = [same & (u <= t), same & (t <= u), same]
    levels = []
    m = Tg // 2
    while m >= 1:
        levels.append(m)
        m //= 2
    for m in levels:
        mats.append(((t // m) % 2 == 1) & (u // m == t // m) & (u <= t))
        if m > 1:
            mats.append(((t // m) % 2 == 0) & (u // m == t // m) & (u > t))
        masks.append(((t // m) % 2 == 1) & ((u // m) % 2 == 0) & (t // (2 * m) == u // (2 * m)))
    mats = np.concatenate(mats, axis=0).astype(np.float32)
    masks = np.stack(masks).astype(np.float32)
    return mats, masks, tuple(levels)


def _mix_kernel(*refs, L, Tg, levels, has_state):
    G = L // Tg
    if has_state:
        (z_ref, g_ref, gbias_ref, lbl_ref, na_ref, nb_ref, mall_ref, msk_ref,
         c_in, n_in, m_in, s_in,
         mix_ref, c_out, n_out, m_out, s_out, e_scr) = refs
        step = pl.program_id(0)
        m_row0 = step * G
    else:
        (z_ref, g_ref, gbias_ref, lbl_ref, na_ref, nb_ref, mall_ref, msk_ref,
         mix_ref, c_out, n_out, m_out, s_out, e_scr) = refs
        c_in, n_in, m_in, s_in = c_out, n_out, m_out, s_out
        m_row0 = pl.program_id(0)

        @pl.when(pl.program_id(1) == 0)
        def _():
            c_out[...] = jnp.zeros(c_out.shape, F32)
            n_out[...] = jnp.zeros(n_out.shape, F32)
            s_out[...] = jnp.zeros(s_out.shape, F32)
            m_out[0, pl.ds(m_row0, G), :] = jnp.zeros((G, N_HEADS_A), F32)

    neg_inf = F32(-jnp.inf)
    causal = msk_ref[0] > 0.5
    causal_t = msk_ref[1] > 0.5
    same = msk_ref[2] > 0.5
    row1 = lax.broadcasted_iota(jnp.int32, (L, 1), 0)
    in_group = [(row1 >= g * Tg) & (row1 < (g + 1) * Tg) for g in range(G)]

    def by_group(vals):
        if G == 1:
            return vals[0]
        out = jnp.where(in_group[0], vals[0], 0.0)
        for g in range(1, G):
            out = out + jnp.where(in_group[g], vals[g], 0.0)
        return out

    def zs(off, w):
        return z_ref[:, off:off + w]

    pre = g_ref[...] + gbias_ref[...]
    lsig = jnp.minimum(pre, 0.0) - jnp.log(1.0 + jnp.exp(-jnp.abs(pre)))
    pre_t = pre.T
    lsig_t = lsig.T
    m_prev = m_in[0, pl.ds(m_row0, G), :]
    scale = DK_A ** -0.5
    m_new_rows = []
    for h in range(N_HEADS_A):
        q = zs(OFF_QA + DK_A * h, DK_A)
        k = zs(OFF_KA + DK_A * h, DK_A)
        v = zs(OFF_VA + DV_A * h, DV_A)
        logi_c = pre[:, h:h + 1]
        logi_r = pre_t[h:h + 1, :]
        lf_c = lsig[:, N_HEADS_A + h:N_HEADS_A + h + 1]
        lf_r = lsig_t[N_HEADS_A + h:N_HEADS_A + h + 1, :]
        b_c = jnp.sum(jnp.where(causal, lf_r, 0.0), axis=1, keepdims=True)
        b_r = jnp.sum(jnp.where(causal_t, lf_c, 0.0), axis=0, keepdims=True)
        tot_c = jnp.sum(jnp.where(same, lf_r, 0.0), axis=1, keepdims=True)
        tot_r = jnp.sum(jnp.where(same, lf_c, 0.0), axis=0, keepdims=True)
        m_col = by_group([m_prev[g:g + 1, h:h + 1] for g in range(G)])
        dmat = jnp.where(causal, b_c - b_r + logi_r, neg_inf)
        inter = b_c + m_col
        m_t = jnp.maximum(inter, jnp.max(dmat, axis=1, keepdims=True))
        w_inter = jnp.exp(inter - m_t)
        smat = _dot_nt(q, k) * (scale * jnp.exp(dmat - m_t))
        q_c = by_group([_dot(q, c_in[0, g, h].astype(BF16)) for g in range(G)]) * scale
        n_rows = by_group([n_in[0, g, h:h + 1, :] for g in range(G)])
        q_n = jnp.sum(q.astype(F32) * n_rows, axis=1, keepdims=True) * scale
        num = w_inter * q_c + _dot(smat.astype(BF16), v)
        den = w_inter * q_n + jnp.sum(smat, axis=1, keepdims=True)
        hh = num / jnp.maximum(jnp.abs(den), jnp.exp(-m_t))
        ms = jnp.mean(hh * hh, axis=1, keepdims=True)
        oa = zs(OFF_OA + DV_A * h, DV_A).astype(F32)
        ya = hh * lax.rsqrt(ms + EPS) * na_ref[:, DV_A * h:DV_A * (h + 1)] * _sigmoid(oa)
        mix_ref[:, DV_A * h:DV_A * (h + 1)] = ya.astype(BF16)
        dec_c = tot_c - b_c + logi_c
        dec_r = tot_r - b_r + logi_r
        m_new_c = jnp.maximum(tot_c + m_col,
                              jnp.max(jnp.where(same, dec_r, neg_inf), axis=1, keepdims=True))
        wk = jnp.exp(dec_c - m_new_c)
        sc = jnp.exp(tot_c + m_col - m_new_c)
        kw = k.astype(F32) * wk
        m_new_h = []
        for g in range(G):
            last = (g + 1) * Tg - 1
            kg = kw if G == 1 else jnp.where(in_group[g], kw, 0.0)
            sc_g = sc[last:last + 1, :]
            c_new = sc_g * c_in[0, g, h] + _dot_tn(kg.astype(BF16), v)
            n_new = sc_g * n_in[0, g, h:h + 1, :] + jnp.sum(kg, axis=0, keepdims=True)
            c_out[0, g, h] = c_new
            n_out[0, g, h:h + 1, :] = n_new
            m_new_h.append(m_new_c[last:last + 1, :])
        m_new_rows.append(m_new_h)
    for g in range(G):
        m_out[0, pl.ds(m_row0 + g, 1), :] = jnp.concatenate(
            [m_new_rows[h][g] for h in range(N_HEADS_A)], axis=1)

    l0 = lbl_ref[0:1, :]
    l1 = lbl_ref[1:2, :]
    lmax = jnp.maximum(l0, l1)
    e0 = jnp.exp(l0 - lmax)
    e1 = jnp.exp(l1 - lmax)
    lb = e0 / (e0 + e1)
    fb = zs(OFF_FB, B_WIDTH).astype(F32)
    e = jnp.exp(-jnp.abs(fb))
    r = 1.0 / (1.0 + e)
    pos = fb >= 0.0
    sig = jnp.where(pos, r, e * r)
    nsig = jnp.where(pos, e * r, r)
    logf = jnp.log(lb + (1.0 - lb) * sig)
    kb = (1.0 - lb) * nsig
    hi = logf.astype(BF16)
    r1 = logf - hi.astype(F32)
    mid = r1.astype(BF16)
    lo = (r1 - mid.astype(F32)).astype(BF16)
    mall = mall_ref[...]
    e_scr[...] = _dot(mall, hi) + _dot(mall, mid) + _dot(mall, lo)

    lvl_rows = []
    row = 2
    for m in levels:
        rq = row
        row += 1
        rk = None
        if m > 1:
            rk = row
            row += 1
        lvl_rows.append((rq, rk))

    dec_t = []
    for g in range(G):
        last = (g + 1) * Tg - 1
        bl = e_scr[last:last + 1, :]
        bl8 = jnp.concatenate([bl[:, DK_B * h:DK_B * (h + 1)] for h in range(N_HEADS_B)], axis=0)
        dec_t.append(jnp.exp(bl8).T)

    for h in range(N_HEADS_B):
        sl = slice(DK_B * h, DK_B * (h + 1))
        q = zs(OFF_QB + DK_B * h, DK_B).astype(F32)
        k = kb[:, sl]
        v = zs(OFF_VB + DV_B * h, DV_B)
        amat = jnp.zeros((L, L), F32)
        for li, m in enumerate(levels):
            rq, rk = lvl_rows[li]
            qt = (q * jnp.exp(e_scr[rq * L:(rq + 1) * L, sl])).astype(BF16)
            if rk is None:
                kt = k.astype(BF16)
            else:
                kt = (k * jnp.exp(e_scr[rk * L:(rk + 1) * L, sl])).astype(BF16)
            amat = amat + msk_ref[3 + li] * _dot_nt(qt, kt)
        diag = jnp.sum(q * k, axis=1, keepdims=True)
        qs = (q * jnp.exp(e_scr[0:L, sl])).astype(BF16)
        o_inter = by_group([_dot(qs, s_in[0, g, h].astype(BF16)) for g in range(G)])
        o = _dot(amat.astype(BF16), v) + diag * v.astype(F32) + o_inter
        ms = jnp.mean(o * o, axis=1, keepdims=True)
        gb = zs(OFF_GB + DV_B * h, DV_B).astype(F32)
        yb = o * lax.rsqrt(ms + EPS) * nb_ref[:, sl] * (gb * _sigmoid(gb))
        mix_ref[:, A_WIDTH + DV_B * h:A_WIDTH + DV_B * (h + 1)] = yb.astype(BF16)
        kw = k * jnp.exp(e_scr[L:2 * L, sl])
        for g in range(G):
            kg = kw if G == 1 else jnp.where(in_group[g], kw, 0.0)
            s_out[0, g, h] = dec_t[g][:, h:h + 1] * s_in[0, g, h] + _dot_tn(kg.astype(BF16), v)


def _mixer(z, gates, gbias, lb_logits, norm_a, norm_b, *, n_seq, seq_len, states=None):
    has_state = states is not None
    if has_state:
        Tg, G = seq_len, 4
        L = Tg * G
        grid = (n_seq // G,)
        row_map = lambda i: (i, 0)
        const2 = lambda i: (0, 0)
        const3 = lambda i: (0, 0, 0)
        st5 = lambda i: (0, i, 0, 0, 0)
        st4 = lambda i: (0, i, 0, 0)
        sems = ("arbitrary",)
    else:
        Tg, G = CHUNK, 1
        L = CHUNK
        nchunks = seq_len // CHUNK
        grid = (n_seq, nchunks)
        row_map = lambda b, c: (b * nchunks + c, 0)
        const2 = lambda b, c: (0, 0)
        const3 = lambda b, c: (0, 0, 0)
        st5 = lambda b, c: (0, b, 0, 0, 0)
        st4 = lambda b, c: (0, b, 0, 0)
        sems = ("arbitrary", "arbitrary")
    mats, masks, levels = _mix_consts(L, Tg)
    n_mats = mats.shape[0] // L
    tokens = n_seq * seq_len
    in_specs = [pl.BlockSpec((L, Z_MAIN), row_map),
                pl.BlockSpec((L, LANE), row_map),
                pl.BlockSpec((1, LANE), const2),
                pl.BlockSpec((2, B_WIDTH), const2),
                pl.BlockSpec((1, A_WIDTH), const2),
                pl.BlockSpec((1, B_WIDTH), const2),
                pl.BlockSpec((n_mats * L, L), const2),
                pl.BlockSpec((masks.shape[0], L, L), const3)]
    args = [z, gates, gbias, lb_logits, norm_a, norm_b, jnp.asarray(mats, BF16), jnp.asarray(masks)]
    c_spec = pl.BlockSpec((1, G, N_HEADS_A, DK_A, DV_A), st5)
    n_spec = pl.BlockSpec((1, G, N_HEADS_A, DK_A), st4)
    m_spec = pl.BlockSpec((1, n_seq, N_HEADS_A), const3)
    s_spec = pl.BlockSpec((1, G, N_HEADS_B, DK_B, DV_B), st5)
    if has_state:
        in_specs += [c_spec, n_spec, m_spec, s_spec]
        args += list(states)
    kern = functools.partial(_mix_kernel, L=L, Tg=Tg, levels=levels, has_state=has_state)
    return pl.pallas_call(
        kern,
        grid=grid,
        in_specs=in_specs,
        out_specs=[pl.BlockSpec((L, D_MODEL), row_map), c_spec, n_spec, m_spec, s_spec],
        out_shape=[jax.ShapeDtypeStruct((tokens, D_MODEL), BF16),
                   jax.ShapeDtypeStruct((1, n_seq, N_HEADS_A, DK_A, DV_A), F32),
                   jax.ShapeDtypeStruct((1, n_seq, N_HEADS_A, DK_A), F32),
                   jax.ShapeDtypeStruct((1, n_seq, N_HEADS_A), F32),
                   jax.ShapeDtypeStruct((1, n_seq, N_HEADS_B, DK_B, DV_B), F32)],
        scratch_shapes=[pltpu.VMEM((n_mats * L, B_WIDTH), F32)],
        compiler_params=pltpu.CompilerParams(dimension_semantics=sems, vmem_limit_bytes=VMEM_LIMIT),
        name="mixer_state" if has_state else "mixer_chunk",
    )(*args)


def _res_kernel(*refs, per_row, tiles_per_seq, emit_h2, nk):
    if emit_h2:
        a_ref, w_ref, x_ref, gm_ref, lg_ref, lb_ref, sh_ref, sc_ref, o_ref, h_ref, acc = refs
    else:
        a_ref, w_ref, x_ref, gm_ref, lg_ref, lb_ref, o_ref, acc = refs
    i = pl.program_id(0)
    k = pl.program_id(1)
    part = _dot(a_ref[...], w_ref[...])

    @pl.when(k == 0)
    def _():
        acc[...] = part

    @pl.when(k > 0)
    def _():
        acc[...] += part

    @pl.when(k == nk - 1)
    def _():
        seq = i // tiles_per_seq
        gm = _pick_rows(gm_ref, per_row, seq)
        x1 = _ln(ALPHA * x_ref[...] + gm * acc[...]) * lg_ref[...] + lb_ref[...]
        o_ref[...] = x1
        if emit_h2:
            sh = _pick_rows(sh_ref, per_row, seq)
            sc = _pick_rows(sc_ref, per_row, seq)
            h_ref[...] = (_ln(x1) * (1.0 + sc) + sh).astype(BF16)


def _res_block(a, w, x, mod, ln_g, ln_b, *, tm, bk, per_row, seq_len, gate_col, emit_h2):
    tokens, kdim = a.shape
    nk = kdim // bk
    if per_row:
        mod_spec = lambda c: pl.BlockSpec((tm, D_MODEL), lambda i, k: (i, c))
    else:
        mod_spec = lambda c: pl.BlockSpec((8, D_MODEL), lambda i, k: (0, c))
    row_spec = pl.BlockSpec((tm, D_MODEL), lambda i, k: (i, 0))
    vec_spec = pl.BlockSpec((1, D_MODEL), lambda i, k: (0, 0))
    in_specs = [pl.BlockSpec((tm, bk), lambda i, k: (i, k)),
                pl.BlockSpec((bk, D_MODEL), lambda i, k: (k, 0)),
                row_spec, mod_spec(gate_col), vec_spec, vec_spec]
    args = [a, w, x, mod, ln_g, ln_b]
    out_specs = [row_spec]
    out_shape = [jax.ShapeDtypeStruct((tokens, D_MODEL), F32)]
    if emit_h2:
        in_specs += [mod_spec(3), mod_spec(4)]
        args += [mod, mod]
        out_specs.append(row_spec)
        out_shape.append(jax.ShapeDtypeStruct((tokens, D_MODEL), BF16))
    kern = functools.partial(_res_kernel, per_row=per_row, tiles_per_seq=max(seq_len // tm, 1),
                             emit_h2=emit_h2, nk=nk)
    return pl.pallas_call(
        kern,
        grid=(tokens // tm, nk),
        in_specs=in_specs,
        out_specs=out_specs,
        out_shape=out_shape,
        scratch_shapes=[pltpu.VMEM((tm, D_MODEL), F32)],
        compiler_params=pltpu.CompilerParams(
            dimension_semantics=("arbitrary", "arbitrary"), vmem_limit_bytes=VMEM_LIMIT),
        name="res_block",
    )(*args)


def _up_kernel(*refs, tm, seq_len, has_cache, tiles_per_seq):
    if has_cache:
        h_ref, wa_ref, wu_ref, cw_ref, cb_ref, ca_ref, cbb_ref, g_ref, a_out, abuf = refs
    else:
        h_ref, wa_ref, wu_ref, cw_ref, cb_ref, g_ref, tail_ref, abuf = refs
    i = pl.program_id(1)
    h = h_ref[...]
    a = _dot(h, wa_ref[...])
    u = _dot(h, wu_ref[...])

    @pl.when(i == 0)
    def _():
        abuf[0:8, :] = jnp.zeros((8, abuf.shape[1]), F32)

    @pl.when(i > 0)
    def _():
        abuf[6:8, :] = abuf[tm + 6:tm + 8, :]

    abuf[8:8 + tm, :] = a
    t = (i * tm + lax.broadcasted_iota(jnp.int32, (tm, 1), 0)) & (seq_len - 1)
    a1 = jnp.where(t >= 1, abuf[7:7 + tm, :], 0.0)
    a2 = jnp.where(t >= 2, abuf[6:6 + tm, :], 0.0)
    if has_cache:
        a1 = a1 + ca_ref[...]
        a2 = a2 + cbb_ref[...]
        a_out[...] = a
    else:
        @pl.when((i + 1) % tiles_per_seq == 0)
        def _():
            tail_ref[0] = a[tm - 2:tm, :]
    conv = cb_ref[...] + cw_ref[0:1, :] * a2 + cw_ref[1:2, :] * a1 + cw_ref[2:3, :] * a
    gl = 0.5 * conv * (1.0 + lax.erf(conv * (2.0 ** -0.5)))
    g_ref[...] = (gl * u).astype(BF16)


def _up_proj(h2, w_up_p, conv_w_p, conv_b_p, *, tm, seq_len, cache=None):
    tokens = h2.shape[0]
    has_cache = cache is not None
    bn = 512
    nj = D_FF_PAD // bn
    tiles_per_seq = max(seq_len // tm, 1)
    n_seq = tokens // seq_len
    in_specs = [pl.BlockSpec((tm, D_MODEL), lambda j, i: (i, 0)),
                pl.BlockSpec((D_MODEL, bn), lambda j, i: (0, j)),
                pl.BlockSpec((D_MODEL, bn), lambda j, i: (0, j + nj)),
                pl.BlockSpec((CONV_W, bn), lambda j, i: (0, j)),
                pl.BlockSpec((1, bn), lambda j, i: (0, j))]
    args = [h2, w_up_p, w_up_p, conv_w_p, conv_b_p]
    tile_spec = pl.BlockSpec((tm, bn), lambda j, i: (i, j))
    out_specs = [tile_spec]
    out_shape = [jax.ShapeDtypeStruct((tokens, D_FF_PAD), BF16)]
    if has_cache:
        in_specs += [tile_spec, tile_spec]
        args += list(cache)
        out_specs.append(tile_spec)
        out_shape.append(jax.ShapeDtypeStruct((tokens, D_FF_PAD), F32))
    else:
        out_specs.append(pl.BlockSpec((1, CONV_W - 1, bn), lambda j, i: (i // tiles_per_seq, 0, j)))
        out_shape.append(jax.ShapeDtypeStruct((n_seq, CONV_W - 1, D_FF_PAD), F32))
    kern = functools.partial(_up_kernel, tm=tm, seq_len=seq_len, has_cache=has_cache,
                             tiles_per_seq=tiles_per_seq)
    return pl.pallas_call(
        kern,
        grid=(nj, tokens // tm),
        in_specs=in_specs,
        out_specs=out_specs,
        out_shape=out_shape,
        scratch_shapes=[pltpu.VMEM((tm + 8, bn), F32)],
        compiler_params=pltpu.CompilerParams(
            dimension_semantics=("arbitrary", "arbitrary"), vmem_limit_bytes=VMEM_LIMIT),
        name="up_proj",
    )(*args)


def _path(x, mod, w, *, n_seq, seq_len, per_row, tm_in, tm_res, tm_up, states=None, cache=None):
    z, gates = _in_proj(x, mod, w["w_in"], tm=tm_in, per_row=per_row, seq_len=seq_len)
    mix, c1, n1, m1, s1 = _mixer(z, gates, w["gbias"], w["lb_logits"], w["norm_a"], w["norm_b"],
                                 n_seq=n_seq, seq_len=seq_len, states=states)
    x1, h2 = _res_block(mix, w["w_out"], x, mod, w["ln1_g"], w["ln1_b"], tm=tm_res, bk=D_MODEL,
                        per_row=per_row, seq_len=seq_len, gate_col=2, emit_h2=True)
    g, conv_out = _up_proj(h2, w["w_up"], w["conv_w"], w["conv_b"], tm=tm_up, seq_len=seq_len, cache=cache)
    (x2,) = _res_block(g, w["w_down"], x1, mod, w["ln2_g"], w["ln2_b"], tm=tm_res, bk=D_FF_PAD // 4,
                       per_row=per_row, seq_len=seq_len, gate_col=5, emit_h2=False)
    return x2, (c1, n1, m1, s1), conv_out


def kernel(x_prompt, x_sample, state_mlstm_C, state_mlstm_n, state_mlstm_m, state_hgrn_S, cache_ffn_conv,
           c_prompt, c_sample, hgrn_lb_logits, w_ada, b_ada, w_in, b_gate_a, norm_a, norm_b, w_out,
           ln1_g, ln1_b, w_up, conv_w, conv_b, w_down, ln2_g, ln2_b):
    bp, tp, _ = x_prompt.shape
    bs, ts, _ = x_sample.shape
    pad_ff = D_FF_PAD - D_FF
    wi = w_in[0]
    n_gate = 2 * N_HEADS_A
    g0 = 2 * N_HEADS_A * DK_A + A_WIDTH
    w_in_p = jnp.concatenate(
        [wi[:, :g0], wi[:, g0 + n_gate:], wi[:, g0:g0 + n_gate],
         jnp.zeros((D_MODEL, LANE - n_gate), F32)], axis=1).astype(BF16)
    wu = w_up[0]
    zpad = jnp.zeros((D_MODEL, pad_ff), F32)
    w_up_p = jnp.concatenate([wu[:, :D_FF], zpad, wu[:, D_FF:], zpad], axis=1).astype(BF16)
    w_down_p = jnp.pad(w_down[0], ((0, pad_ff), (0, 0))).astype(BF16)
    weights = dict(
        w_in=w_in_p, w_out=w_out[0].astype(BF16), w_up=w_up_p, w_down=w_down_p,
        gbias=jnp.pad(b_gate_a[0].reshape(1, n_gate), ((0, 0), (0, LANE - n_gate))),
        lb_logits=hgrn_lb_logits, norm_a=norm_a, norm_b=norm_b,
        ln1_g=ln1_g, ln1_b=ln1_b, ln2_g=ln2_g, ln2_b=ln2_b,
        conv_w=jnp.pad(conv_w[0], ((0, 0), (0, pad_ff))),
        conv_b=jnp.pad(conv_b, ((0, 0), (0, pad_ff))))

    c_all = jnp.concatenate([c_prompt, jnp.zeros((8 - bp, D_MODEL), F32), c_sample], axis=0)
    mod = _ada(c_all, w_ada[0], b_ada)
    mod_p = mod[:8]
    mod_s = jnp.repeat(mod[8:], ts, axis=0)

    yp, st_p, tail_p = _path(x_prompt.reshape(bp * tp, D_MODEL), mod_p, weights, n_seq=bp, seq_len=tp,
                             per_row=False, tm_in=1024, tm_res=512, tm_up=1024)

    cache = jnp.pad(cache_ffn_conv[0], ((0, 0), (0, 0), (0, pad_ff)))
    zrow = jnp.zeros((bs, D_FF_PAD), F32)
    cache_a = jnp.stack([cache[:, 1], zrow, zrow, zrow], axis=1).reshape(bs * ts, D_FF_PAD)
    cache_b = jnp.stack([cache[:, 0], cache[:, 1], zrow, zrow], axis=1).reshape(bs * ts, D_FF_PAD)
    ys, st_s, a_s = _path(x_sample.reshape(bs * ts, D_MODEL), mod_s, weights, n_seq=bs, seq_len=ts,
                          per_row=True, tm_in=512, tm_res=512, tm_up=512,
                          states=(state_mlstm_C, state_mlstm_n, state_mlstm_m, state_hgrn_S),
                          cache=(cache_a, cache_b))

    conv_p = tail_p[None, :, :, :D_FF]
    conv_s = a_s.reshape(bs, ts, D_FF_PAD)[None, :, ts - (CONV_W - 1):, :D_FF]
    return (yp.reshape(bp, tp, D_MODEL), ys.reshape(bs, ts, D_MODEL),
            st_p[0], st_p[1], st_p[2], st_p[3], conv_p,
            st_s[0], st_s[1], st_s[2], st_s[3], conv_s)
```

```python
import functools

import numpy as np
import jax
import jax.numpy as jnp
from jax import lax
from jax.experimental import pallas as pl
from jax.experimental.pallas import tpu as pltpu

F32 = jnp.float32
BF16 = jnp.bfloat16

D_MODEL = 2048
N_HEADS_A, DK_A, DV_A = 4, 128, 256
N_HEADS_B, DK_B, DV_B = 8, 128, 128
A_WIDTH = N_HEADS_A * DV_A
B_WIDTH = N_HEADS_B * DV_B
D_FF = 5504
CONV_W = 3
CHUNK = 64
EPS = 1e-5
ALPHA = 2.0 ** 0.25
LANE = 128
D_FF_PAD = 5632
Z_MAIN = 7168
OFF_QA, OFF_KA, OFF_VA, OFF_OA = 0, 512, 1024, 2048
OFF_FB, OFF_QB, OFF_VB, OFF_GB = 3072, 4096, 5120, 6144
VMEM_LIMIT = 56 * 1024 * 1024
SUB_ROWS = 256


def _ln(x):
    mu = jnp.mean(x, axis=-1, keepdims=True)
    xc = x - mu
    var = jnp.mean(xc * xc, axis=-1, keepdims=True)
    return xc * lax.rsqrt(var + EPS)


def _sigmoid(x):
    return 1.0 / (1.0 + jnp.exp(-x))


def _pick_rows(ref, per_row, seq):
    if per_row:
        return ref[...]
    return ref[pl.ds(seq, 1), :]


def _dot(a, b):
    return jnp.dot(a, b, preferred_element_type=F32)


def _dot_nt(a, b):
    return lax.dot_general(a, b, (((1,), (1,)), ((), ())), preferred_element_type=F32)


def _dot_tn(a, b):
    return lax.dot_general(a, b, (((0,), (0,)), ((), ())), preferred_element_type=F32)


def _ada_kernel(c_ref, w_ref, b_ref, o_ref):
    c = c_ref[...]
    s = (c * _sigmoid(c)).astype(BF16)
    o_ref[...] = _dot(s, w_ref[...].astype(BF16)) + b_ref[...]


def _ada(c_all, w_ada, b_ada):
    rows = c_all.shape[0]
    n = w_ada.shape[1]
    bn = 1024
    return pl.pallas_call(
        _ada_kernel,
        grid=(n // bn,),
        in_specs=[pl.BlockSpec((rows, D_MODEL), lambda j: (0, 0)),
                  pl.BlockSpec((D_MODEL, bn), lambda j: (0, j)),
                  pl.BlockSpec((1, bn), lambda j: (0, j))],
        out_specs=pl.BlockSpec((rows, bn), lambda j: (0, j)),
        out_shape=jax.ShapeDtypeStruct((rows, n), F32),
        compiler_params=pltpu.CompilerParams(vmem_limit_bytes=VMEM_LIMIT),
        name="ada",
    )(c_all, w_ada, b_ada)


def _in_kernel(x_ref, sh_ref, sc_ref, w_ref, wg_ref, z_ref, g_ref, h_scr, *, per_row, tiles_per_seq):
    i = pl.program_id(0)
    j = pl.program_id(1)

    @pl.when(j == 0)
    def _():
        seq = i // tiles_per_seq
        for r in range(0, x_ref.shape[0], SUB_ROWS):
            rows = slice(r, r + SUB_ROWS)
            if per_row:
                sh, sc = sh_ref[rows, :], sc_ref[rows, :]
            else:
                sh, sc = sh_ref[pl.ds(seq, 1), :], sc_ref[pl.ds(seq, 1), :]
            h = (_ln(x_ref[rows, :]) * (1.0 + sc) + sh).astype(BF16)
            h_scr[rows, :] = h
            g_ref[rows, :] = _dot(h, wg_ref[...])
            z_ref[rows, :] = _dot(h, w_ref[...]).astype(BF16)

    @pl.when(j > 0)
    def _():
        z_ref[...] = _dot(h_scr[...], w_ref[...]).astype(BF16)


def _in_proj(x, mod, w_in_p, *, tm, per_row, seq_len):
    tokens = x.shape[0]
    bn = 1024
    if per_row:
        mod_spec = lambda k: pl.BlockSpec((tm, D_MODEL), lambda i, j: (i, k))
    else:
        mod_spec = lambda k: pl.BlockSpec((8, D_MODEL), lambda i, j: (0, k))
    kern = functools.partial(_in_kernel, per_row=per_row, tiles_per_seq=max(seq_len // tm, 1))
    return pl.pallas_call(
        kern,
        grid=(tokens // tm, Z_MAIN // bn),
        in_specs=[pl.BlockSpec((tm, D_MODEL), lambda i, j: (i, 0)),
                  mod_spec(0), mod_spec(1),
                  pl.BlockSpec((D_MODEL, bn), lambda i, j: (0, j)),
                  pl.BlockSpec((D_MODEL, LANE), lambda i, j: (0, Z_MAIN // LANE))],
        out_specs=[pl.BlockSpec((tm, bn), lambda i, j: (i, j)),
                   pl.BlockSpec((tm, LANE), lambda i, j: (i, 0))],
        out_shape=[jax.ShapeDtypeStruct((tokens, Z_MAIN), BF16),
                   jax.ShapeDtypeStruct((tokens, LANE), F32)],
        scratch_shapes=[pltpu.VMEM((tm, D_MODEL), BF16)],
        compiler_params=pltpu.CompilerParams(
            dimension_semantics=("arbitrary", "arbitrary"), vmem_limit_bytes=VMEM_LIMIT),
        name="in_proj",
    )(x, mod, mod, w_in_p, w_in_p)


def _mix_consts(L, Tg):
    t = np.arange(L)[:, None]
    u = np.arange(L)[None, :]
    same = (t // Tg) == (u // Tg)
    mats = [same & (u <= t), same & (u > t)]
    masks = [same & (u <= t), same & (t <= u), same]
    levels = []
    m = Tg // 2
    while m >= 1:
        levels.append(m)
        m //= 2
    for m in levels:
        mats.append(((t // m) % 2 == 1) & (u // m == t // m) & (u <= t))
        if m > 1:
            mats.append(((t // m) % 2 == 0) & (u // m == t // m) & (u > t))
        masks.append(((t // m) % 2 == 1) & ((u // m) % 2 == 0) & (t // (2 * m) == u // (2 * m)))
    mats = np.concatenate(mats, axis=0).astype(np.float32)
    masks = np.stack(masks).astype(np.float32)
    return mats, masks, tuple(levels)


def _mix_kernel(*refs, L, Tg, levels, has_state):
    G = L // Tg
    if has_state:
        (z_ref, g_ref, gbias_ref, lbl_ref, na_ref, nb_ref, mall_ref, msk_ref,
         c_in, n_in, m_in, s_in,
         mix_ref, c_out, n_out, m_out, s_out, e_scr) = refs
        step = pl.program_id(0)
        m_row0 = step * G
    else:
        (z_ref, g_ref, gbias_ref, lbl_ref, na_ref, nb_ref, mall_ref, msk_ref,
         mix_ref, c_out, n_out, m_out, s_out, e_scr) = refs
        c_in, n_in, m_in, s_in = c_out, n_out, m_out, s_out
        m_row0 = pl.program_id(0)

        @pl.when(pl.program_id(1) == 0)
        def _():
            c_out[...] = jnp.zeros(c_out.shape, F32)
            n_out[...] = jnp.zeros(n_out.shape, F32)
            s_out[...] = jnp.zeros(s_out.shape, F32)
            m_out[0, pl.ds(m_row0, G), :] = jnp.zeros((G, N_HEADS_A), F32)

    neg_inf = F32(-jnp.inf)
    causal = msk_ref[0] > 0.5
    causal_t = msk_ref[1] > 0.5
    same = msk_ref[2] > 0.5
    row1 = lax.broadcasted_iota(jnp.int32, (L, 1), 0)
    in_group = [(row1 >= g * Tg) & (row1 < (g + 1) * Tg) for g in range(G)]

    def by_group(vals):
        if G == 1:
            return vals[0]
        out = jnp.where(in_group[0], vals[0], 0.0)
        for g in range(1, G):
            out = out + jnp.where(in_group[g], vals[g], 0.0)
        return out

    def zs(off, w):
        return z_ref[:, off:off + w]

    pre = g_ref[...] + gbias_ref[...]
    lsig = jnp.minimum(pre, 0.0) - jnp.log(1.0 + jnp.exp(-jnp.abs(pre)))
    pre_t = pre.T
    lsig_t = lsig.T
    m_prev = m_in[0, pl.ds(m_row0, G), :]
    scale = DK_A ** -0.5
    m_new_rows = []
    for h in range(N_HEADS_A):
        q = zs(OFF_QA + DK_A * h, DK_A)
        k = zs(OFF_KA + DK_A * h, DK_A)
        v = zs(OFF_VA + DV_A * h, DV_A)
        logi_c = pre[:, h:h + 1]
        logi_r = pre_t[h:h + 1, :]
        lf_c = lsig[:, N_HEADS_A + h:N_HEADS_A + h + 1]
        lf_r = lsig_t[N_HEADS_A + h:N_HEADS_A + h + 1, :]
        b_c = jnp.sum(jnp.where(causal, lf_r, 0.0), axis=1, keepdims=True)
        b_r = jnp.sum(jnp.where(causal_t, lf_c, 0.0), axis=0, keepdims=True)
        tot_c = jnp.sum(jnp.where(same, lf_r, 0.0), axis=1, keepdims=True)
        tot_r = jnp.sum(jnp.where(same, lf_c, 0.0), axis=0, keepdims=True)
        m_col = by_group([m_prev[g:g + 1, h:h + 1] for g in range(G)])
        dmat = jnp.where(causal, b_c - b_r + logi_r, neg_inf)
        inter = b_c + m_col
        m_t = jnp.maximum(inter, jnp.max(dmat, axis=1, keepdims=True))
        w_inter = jnp.exp(inter - m_t)
        smat = _dot_nt(q, k) * (scale * jnp.exp(dmat - m_t))
        q_c = by_group([_dot(q, c_in[0, g, h].astype(BF16)) for g in range(G)]) * scale
        n_rows = by_group([n_in[0, g, h:h + 1, :] for g in range(G)])
        q_n = jnp.sum(q.astype(F32) * n_rows, axis=1, keepdims=True) * scale
        num = w_inter * q_c + _dot(smat.astype(BF16), v)
        den = w_inter * q_n + jnp.sum(smat, axis=1, keepdims=True)
        hh = num / jnp.maximum(jnp.abs(den), jnp.exp(-m_t))
        ms = jnp.mean(hh * hh, axis=1, keepdims=True)
        oa = zs(OFF_OA + DV_A * h, DV_A).astype(F32)
        ya = hh * lax.rsqrt(ms + EPS) * na_ref[:, DV_A * h:DV_A * (h + 1)] * _sigmoid(oa)
        mix_ref[:, DV_A * h:DV_A * (h + 1)] = ya.astype(BF16)
        dec_c = tot_c - b_c + logi_c
        dec_r = tot_r - b_r + logi_r
        m_new_c = jnp.maximum(tot_c + m_col,
                              jnp.max(jnp.where(same, dec_r, neg_inf), axis=1, keepdims=True))
        wk = jnp.exp(dec_c - m_new_c)
        sc = jnp.exp(tot_c + m_col - m_new_c)
        kw = k.astype(F32) * wk
        m_new_h = []
        for g in range(G):
            last = (g + 1) * Tg - 1
            kg = kw if G == 1 else jnp.where(in_group[g], kw, 0.0)
            sc_g = sc[last:last + 1, :]
            c_new = sc_g * c_in[0, g, h] + _dot_tn(kg.astype(BF16), v)
            n_new = sc_g * n_in[0, g, h:h + 1, :] + jnp.sum(kg, axis=0, keepdims=True)
            c_out[0, g, h] = c_new
            n_out[0, g, h:h + 1, :] = n_new
            m_new_h.append(m_new_c[last:last + 1, :])
        m_new_rows.append(m_new_h)
    for g in range(G):
        m_out[0, pl.ds(m_row0 + g, 1), :] = jnp.concatenate(
            [m_new_rows[h][g] for h in range(N_HEADS_A)], axis=1)

    l0 = lbl_ref[0:1, :]
    l1 = lbl_ref[1:2, :]
    lmax = jnp.maximum(l0, l1)
    e0 = jnp.exp(l0 - lmax)
    e1 = jnp.exp(l1 - lmax)
    lb = e0 / (e0 + e1)
    fb = zs(OFF_FB, B_WIDTH).astype(F32)
    e = jnp.exp(-jnp.abs(fb))
    r = 1.0 / (1.0 + e)
    pos = fb >= 0.0
    sig = jnp.where(pos, r, e * r)
    nsig = jnp.where(pos, e * r, r)
    logf = jnp.log(lb + (1.0 - lb) * sig)
    kb = (1.0 - lb) * nsig
    hi = logf.astype(BF16)
    r1 = logf - hi.astype(F32)
    mid = r1.astype(BF16)
    lo = (r1 - mid.astype(F32)).astype(BF16)
    mall = mall_ref[...]
    e_scr[...] = _dot(mall, hi) + _dot(mall, mid) + _dot(mall, lo)

    lvl_rows = []
    row = 2
    for m in levels:
        rq = row
        row += 1
        rk = None
        if m > 1:
            rk = row
            row += 1
        lvl_rows.append((rq, rk))

    dec_t = []
    for g in range(G):
        last = (g + 1) * Tg - 1
        bl = e_scr[last:last + 1, :]
        bl8 = jnp.concatenate([bl[:, DK_B * h:DK_B * (h + 1)] for h in range(N_HEADS_B)], axis=0)
        dec_t.append(jnp.exp(bl8).T)

    for h in range(N_HEADS_B):
        sl = slice(DK_B * h, DK_B * (h + 1))
        q = zs(OFF_QB + DK_B * h, DK_B).astype(F32)
        k = kb[:, sl]
        v = zs(OFF_VB + DV_B * h, DV_B)
        amat = jnp.zeros((L, L), F32)
        for li, m in enumerate(levels):
            rq, rk = lvl_rows[li]
            qt = (q * jnp.exp(e_scr[rq * L:(rq + 1) * L, sl])).astype(BF16)
            if rk is None:
                kt = k.astype(BF16)
            else:
                kt = (k * jnp.exp(e_scr[rk * L:(rk + 1) * L, sl])).astype(BF16)
            amat = amat + msk_ref[3 + li] * _dot_nt(qt, kt)
        diag = jnp.sum(q * k, axis=1, keepdims=True)
        qs = (q * jnp.exp(e_scr[0:L, sl])).astype(BF16)
        o_inter = by_group([_dot(qs, s_in[0, g, h].astype(BF16)) for g in range(G)])
        o = _dot(amat.astype(BF16), v) + diag * v.astype(F32) + o_inter
        ms = jnp.mean(o * o, axis=1, keepdims=True)
        gb = zs(OFF_GB + DV_B * h, DV_B).astype(F32)
        yb = o * lax.rsqrt(ms + EPS) * nb_ref[:, sl] * (gb * _sigmoid(gb))
        mix_ref[:, A_WIDTH + DV_B * h:A_WIDTH + DV_B * (h + 1)] = yb.astype(BF16)
        kw = k * jnp.exp(e_scr[L:2 * L, sl])
        for g in range(G):
            kg = kw if G == 1 else jnp.where(in_group[g], kw, 0.0)
            s_out[0, g, h] = dec_t[g][:, h:h + 1] * s_in[0, g, h] + _dot_tn(kg.astype(BF16), v)


def _mixer(z, gates, gbias, lb_logits, norm_a, norm_b, *, n_seq, seq_len, states=None):
    has_state = states is not None
    if has_state:
        Tg, G = seq_len, 4
        L = Tg * G
        grid = (n_seq // G,)
        row_map = lambda i: (i, 0)
        const2 = lambda i: (0, 0)
        const3 = lambda i: (0, 0, 0)
        st5 = lambda i: (0, i, 0, 0, 0)
        st4 = lambda i: (0, i, 0, 0)
        sems = ("arbitrary",)
    else:
        Tg, G = CHUNK, 1
        L = CHUNK
        nchunks = seq_len // CHUNK
        grid = (n_seq, nchunks)
        row_map = lambda b, c: (b * nchunks + c, 0)
        const2 = lambda b, c: (0, 0)
        const3 = lambda b, c: (0, 0, 0)
        st5 = lambda b, c: (0, b, 0, 0, 0)
        st4 = lambda b, c: (0, b, 0, 0)
        sems = ("arbitrary", "arbitrary")
    mats, masks, levels = _mix_consts(L, Tg)
    n_mats = mats.shape[0] // L
    tokens = n_seq * seq_len
    in_specs = [pl.BlockSpec((L, Z_MAIN), row_map),
                pl.BlockSpec((L, LANE), row_map),
                pl.BlockSpec((1, LANE), const2),
                pl.BlockSpec((2, B_WIDTH), const2),
                pl.BlockSpec((1, A_WIDTH), const2),
                pl.BlockSpec((1, B_WIDTH), const2),
                pl.BlockSpec((n_mats * L, L), const2),
                pl.BlockSpec((masks.shape[0], L, L), const3)]
    args = [z, gates, gbias, lb_logits, norm_a, norm_b, jnp.asarray(mats, BF16), jnp.asarray(masks)]
    c_spec = pl.BlockSpec((1, G, N_HEADS_A, DK_A, DV_A), st5)
    n_spec = pl.BlockSpec((1, G, N_HEADS_A, DK_A), st4)
    m_spec = pl.BlockSpec((1, n_seq, N_HEADS_A), const3)
    s_spec = pl.BlockSpec((1, G, N_HEADS_B, DK_B, DV_B), st5)
    if has_state:
        in_specs += [c_spec, n_spec, m_spec, s_spec]
        args += list(states)
    kern = functools.partial(_mix_kernel, L=L, Tg=Tg, levels=levels, has_state=has_state)
    return pl.pallas_call(
        kern,
        grid=grid,
        in_specs=in_specs,
        out_specs=[pl.BlockSpec((L, D_MODEL), row_map), c_spec, n_spec, m_spec, s_spec],
        out_shape=[jax.ShapeDtypeStruct((tokens, D_MODEL), BF16),
                   jax.ShapeDtypeStruct((1, n_seq, N_HEADS_A, DK_A, DV_A), F32),
                   jax.ShapeDtypeStruct((1, n_seq, N_HEADS_A, DK_A), F32),
                   jax.ShapeDtypeStruct((1, n_seq, N_HEADS_A), F32),
                   jax.ShapeDtypeStruct((1, n_seq, N_HEADS_B, DK_B, DV_B), F32)],
        scratch_shapes=[pltpu.VMEM((n_mats * L, B_WIDTH), F32)],
        compiler_params=pltpu.CompilerParams(dimension_semantics=sems, vmem_limit_bytes=VMEM_LIMIT),
        name="mixer_state" if has_state else "mixer_chunk",
    )(*args)


def _res_kernel(*refs, per_row, tiles_per_seq, emit_h2, nk):
    if emit_h2:
        a_ref, w_ref, x_ref, gm_ref, lg_ref, lb_ref, sh_ref, sc_ref, o_ref, h_ref = refs
    else:
        a_ref, w_ref, x_ref, gm_ref, lg_ref, lb_ref, o_ref = refs
    i = pl.program_id(0)
    k = pl.program_id(1)

    if nk > 1:
        @pl.when(k == 0)
        def _():
            o_ref[...] = _dot(a_ref[...], w_ref[...])

    if nk > 2:
        @pl.when((k > 0) & (k < nk - 1))
        def _():
            o_ref[...] += _dot(a_ref[...], w_ref[...])

    @pl.when(k == nk - 1)
    def _():
        seq = i // tiles_per_seq

        def pick(ref, rows):
            return ref[rows, :] if per_row else ref[pl.ds(seq, 1), :]

        for r in range(0, o_ref.shape[0], SUB_ROWS):
            rows = slice(r, r + SUB_ROWS)
            acc = _dot(a_ref[rows, :], w_ref[...])
            if nk > 1:
                acc = acc + o_ref[rows, :]
            x1 = _ln(ALPHA * x_ref[rows, :] + pick(gm_ref, rows) * acc) * lg_ref[...] + lb_ref[...]
            o_ref[rows, :] = x1
            if emit_h2:
                h_ref[rows, :] = (_ln(x1) * (1.0 + pick(sc_ref, rows)) + pick(sh_ref, rows)).astype(BF16)


def _res_block(a, w, x, mod, ln_g, ln_b, *, tm, bk, per_row, seq_len, gate_col, emit_h2):
    tokens, kdim = a.shape
    nk = kdim // bk
    if per_row:
        mod_spec = lambda c: pl.BlockSpec((tm, D_MODEL), lambda i, k: (i, c))
    else:
        mod_spec = lambda c: pl.BlockSpec((8, D_MODEL), lambda i, k: (0, c))
    row_spec = pl.BlockSpec((tm, D_MODEL), lambda i, k: (i, 0))
    vec_spec = pl.BlockSpec((1, D_MODEL), lambda i, k: (0, 0))
    in_specs = [pl.BlockSpec((tm, bk), lambda i, k: (i, k)),
                pl.BlockSpec((bk, D_MODEL), lambda i, k: (k, 0)),
                row_spec, mod_spec(gate_col), vec_spec, vec_spec]
    args = [a, w, x, mod, ln_g, ln_b]
    out_specs = [row_spec]
    out_shape = [jax.ShapeDtypeStruct((tokens, D_MODEL), F32)]
    if emit_h2:
        in_specs += [mod_spec(3), mod_spec(4)]
        args += [mod, mod]
        out_specs.append(row_spec)
        out_shape.append(jax.ShapeDtypeStruct((tokens, D_MODEL), BF16))
    kern = functools.partial(_res_kernel, per_row=per_row, tiles_per_seq=max(seq_len // tm, 1),
                             emit_h2=emit_h2, nk=nk)
    return pl.pallas_call(
        kern,
        grid=(tokens // tm, nk),
        in_specs=in_specs,
        out_specs=out_specs,
        out_shape=out_shape,
        compiler_params=pltpu.CompilerParams(
            dimension_semantics=("arbitrary", "arbitrary"), vmem_limit_bytes=VMEM_LIMIT),
        name="res_block",
    )(*args)


def _up_kernel(*refs, tm, seq_len, has_cache, tiles_per_seq):
    if has_cache:
        h_ref, wa_ref, wu_ref, cw_ref, cb_ref, ca_ref, cbb_ref, g_ref, a_out, abuf = refs
    else:
        h_ref, wa_ref, wu_ref, cw_ref, cb_ref, g_ref, tail_ref, abuf = refs
    i = pl.program_id(1)

    @pl.when(i == 0)
    def _():
        abuf[0:8, :] = jnp.zeros((8, abuf.shape[1]), F32)

    @pl.when(i > 0)
    def _():
        abuf[6:8, :] = abuf[tm + 6:tm + 8, :]

    for r in range(0, tm, SUB_ROWS):
        rows = slice(r, r + SUB_ROWS)
        h = h_ref[rows, :]
        a = _dot(h, wa_ref[...])
        u = _dot(h, wu_ref[...])
        abuf[8 + r:8 + r + SUB_ROWS, :] = a
        t = (i * tm + r + lax.broadcasted_iota(jnp.int32, (SUB_ROWS, 1), 0)) & (seq_len - 1)
        a1 = jnp.where(t >= 1, abuf[7 + r:7 + r + SUB_ROWS, :], 0.0)
        a2 = jnp.where(t >= 2, abuf[6 + r:6 + r + SUB_ROWS, :], 0.0)
        if has_cache:
            a1 = a1 + ca_ref[rows, :]
            a2 = a2 + cbb_ref[rows, :]
            a_out[rows, :] = a
        conv = cb_ref[...] + cw_ref[0:1, :] * a2 + cw_ref[1:2, :] * a1 + cw_ref[2:3, :] * a
        gl = 0.5 * conv * (1.0 + lax.erf(conv * (2.0 ** -0.5)))
        g_ref[rows, :] = (gl * u).astype(BF16)

    if not has_cache:
        @pl.when((i + 1) % tiles_per_seq == 0)
        def _():
            tail_ref[0] = abuf[tm + 6:tm + 8, :]


def _up_proj(h2, w_up_p, conv_w_p, conv_b_p, *, tm, seq_len, cache=None):
    tokens = h2.shape[0]
    has_cache = cache is not None
    bn = 512
    nj = D_FF_PAD // bn
    tiles_per_seq = max(seq_len // tm, 1)
    n_seq = tokens // seq_len
    in_specs = [pl.BlockSpec((tm, D_MODEL), lambda j, i: (i, 0)),
                pl.BlockSpec((D_MODEL, bn), lambda j, i: (0, j)),
                pl.BlockSpec((D_MODEL, bn), lambda j, i: (0, j + nj)),
                pl.BlockSpec((CONV_W, bn), lambda j, i: (0, j)),
                pl.BlockSpec((1, bn), lambda j, i: (0, j))]
    args = [h2, w_up_p, w_up_p, conv_w_p, conv_b_p]
    tile_spec = pl.BlockSpec((tm, bn), lambda j, i: (i, j))
    out_specs = [tile_spec]
    out_shape = [jax.ShapeDtypeStruct((tokens, D_FF_PAD), BF16)]
    if has_cache:
        in_specs += [tile_spec, tile_spec]
        args += list(cache)
        out_specs.append(tile_spec)
        out_shape.append(jax.ShapeDtypeStruct((tokens, D_FF_PAD), F32))
    else:
        out_specs.append(pl.BlockSpec((1, CONV_W - 1, bn), lambda j, i: (i // tiles_per_seq, 0, j)))
        out_shape.append(jax.ShapeDtypeStruct((n_seq, CONV_W - 1, D_FF_PAD), F32))
    kern = functools.partial(_up_kernel, tm=tm, seq_len=seq_len, has_cache=has_cache,
                             tiles_per_seq=tiles_per_seq)
    return pl.pallas_call(
        kern,
        grid=(nj, tokens // tm),
        in_specs=in_specs,
        out_specs=out_specs,
        out_shape=out_shape,
        scratch_shapes=[pltpu.VMEM((tm + 8, bn), F32)],
        compiler_params=pltpu.CompilerParams(
            dimension_semantics=("arbitrary", "arbitrary"), vmem_limit_bytes=VMEM_LIMIT),
        name="up_proj",
    )(*args)


def _path(x, mod, w, *, n_seq, seq_len, per_row, tm_in, tm_res, tm_up, tm_down, states=None, cache=None):
    z, gates = _in_proj(x, mod, w["w_in"], tm=tm_in, per_row=per_row, seq_len=seq_len)
    mix, c1, n1, m1, s1 = _mixer(z, gates, w["gbias"], w["lb_logits"], w["norm_a"], w["norm_b"],
                                 n_seq=n_seq, seq_len=seq_len, states=states)
    x1, h2 = _res_block(mix, w["w_out"], x, mod, w["ln1_g"], w["ln1_b"], tm=tm_res, bk=D_MODEL,
                        per_row=per_row, seq_len=seq_len, gate_col=2, emit_h2=True)
    g, conv_out = _up_proj(h2, w["w_up"], w["conv_w"], w["conv_b"], tm=tm_up, seq_len=seq_len, cache=cache)
    (x2,) = _res_block(g, w["w_down"], x1, mod, w["ln2_g"], w["ln2_b"], tm=tm_down, bk=D_FF_PAD // 4,
                       per_row=per_row, seq_len=seq_len, gate_col=5, emit_h2=False)
    return x2, (c1, n1, m1, s1), conv_out


def kernel(x_prompt, x_sample, state_mlstm_C, state_mlstm_n, state_mlstm_m, state_hgrn_S, cache_ffn_conv,
           c_prompt, c_sample, hgrn_lb_logits, w_ada, b_ada, w_in, b_gate_a, norm_a, norm_b, w_out,
           ln1_g, ln1_b, w_up, conv_w, conv_b, w_down, ln2_g, ln2_b):
    bp, tp, _ = x_prompt.shape
    bs, ts, _ = x_sample.shape
    pad_ff = D_FF_PAD - D_FF
    wi = w_in[0].astype(BF16)
    n_gate = 2 * N_HEADS_A
    g0 = 2 * N_HEADS_A * DK_A + A_WIDTH
    w_in_p = jnp.concatenate(
        [wi[:, :g0], wi[:, g0 + n_gate:], wi[:, g0:g0 + n_gate],
         jnp.zeros((D_MODEL, LANE - n_gate), BF16)], axis=1)
    wu = w_up[0].astype(BF16)
    zpad = jnp.zeros((D_MODEL, pad_ff), BF16)
    w_up_p = jnp.concatenate([wu[:, :D_FF], zpad, wu[:, D_FF:], zpad], axis=1)
    w_down_p = jnp.pad(w_down[0].astype(BF16), ((0, pad_ff), (0, 0)))
    weights = dict(
        w_in=w_in_p, w_out=w_out[0].astype(BF16), w_up=w_up_p, w_down=w_down_p,
        gbias=jnp.pad(b_gate_a[0].reshape(1, n_gate), ((0, 0), (0, LANE - n_gate))),
        lb_logits=hgrn_lb_logits, norm_a=norm_a, norm_b=norm_b,
        ln1_g=ln1_g, ln1_b=ln1_b, ln2_g=ln2_g, ln2_b=ln2_b,
        conv_w=jnp.pad(conv_w[0], ((0, 0), (0, pad_ff))),
        conv_b=jnp.pad(conv_b, ((0, 0), (0, pad_ff))))

    c_all = jnp.concatenate([c_prompt, jnp.zeros((8 - bp, D_MODEL), F32), c_sample], axis=0)
    mod = _ada(c_all, w_ada[0], b_ada)
    mod_p = mod[:8]
    mod_s = jnp.repeat(mod[8:], ts, axis=0)

    yp, st_p, tail_p = _path(x_prompt.reshape(bp * tp, D_MODEL), mod_p, weights, n_seq=bp, seq_len=tp,
                             per_row=False, tm_in=1024, tm_res=512, tm_up=1024, tm_down=1024)

    cache = jnp.pad(cache_ffn_conv[0], ((0, 0), (0, 0), (0, pad_ff)))
    zrow = jnp.zeros((bs, D_FF_PAD), F32)
    cache_a = jnp.stack([cache[:, 1], zrow, zrow, zrow], axis=1).reshape(bs * ts, D_FF_PAD)
    cache_b = jnp.stack([cache[:, 0], cache[:, 1], zrow, zrow], axis=1).reshape(bs * ts, D_FF_PAD)
    ys, st_s, a_s = _path(x_sample.reshape(bs * ts, D_MODEL), mod_s, weights, n_seq=bs, seq_len=ts,
                          per_row=True, tm_in=512, tm_res=512, tm_up=512, tm_down=512,
                          states=(state_mlstm_C, state_mlstm_n, state_mlstm_m, state_hgrn_S),
                          cache=(cache_a, cache_b))

    conv_p = tail_p[None, :, :, :D_FF]
    conv_s = a_s.reshape(bs, ts, D_FF_PAD)[None, :, ts - (CONV_W - 1):, :D_FF]
    return (yp.reshape(bp, tp, D_MODEL), ys.reshape(bs, ts, D_MODEL),
            st_p[0], st_p[1], st_p[2], st_p[3], conv_p,
            st_s[0], st_s[1], st_s[2], st_s[3], conv_s)
```

```python
import functools

import numpy as np
import jax
import jax.numpy as jnp
from jax import lax
from jax.experimental import pallas as pl
from jax.experimental.pallas import tpu as pltpu

F32 = jnp.float32
BF16 = jnp.bfloat16

D_MODEL = 2048
N_HEADS_A, DK_A, DV_A = 4, 128, 256
N_HEADS_B, DK_B, DV_B = 8, 128, 128
A_WIDTH = N_HEADS_A * DV_A
B_WIDTH = N_HEADS_B * DV_B
N_GATE = 2 * N_HEADS_A
GATE_COL = 2 * N_HEADS_A * DK_A + A_WIDTH
D_IN = GATE_COL + N_GATE + A_WIDTH + 4 * B_WIDTH
D_FF = 5504
CONV_W = 3
EPS = 1e-5
ALPHA = 2.0 ** 0.25
LANE = 128
FF_BLOCK = 512
D_FF_PAD = 5632
Z_MAIN = D_IN - N_GATE
OFF_QA, OFF_KA, OFF_VA, OFF_OA = 0, 512, 1024, 2048
OFF_FB, OFF_QB, OFF_VB, OFF_GB = 3072, 4096, 5120, 6144
VMEM_LIMIT = 56 * 1024 * 1024
PROMPT_CHUNK = 128
SAMPLE_GROUP = 8


def _ln(x):
    mu = jnp.mean(x, axis=-1, keepdims=True)
    xc = x - mu
    var = jnp.mean(xc * xc, axis=-1, keepdims=True)
    return xc * lax.rsqrt(var + EPS)


def _sigmoid(x):
    return 1.0 / (1.0 + jnp.exp(-x))


def _dot(a, b):
    return jnp.dot(a.astype(BF16), b.astype(BF16), preferred_element_type=F32)


def _dot_nt(a, b):
    return lax.dot_general(a.astype(BF16), b.astype(BF16), (((1,), (1,)), ((), ())),
                           preferred_element_type=F32)


def _dot_tn(a, b):
    return lax.dot_general(a.astype(BF16), b.astype(BF16), (((0,), (0,)), ((), ())),
                           preferred_element_type=F32)


def _mod_rows(ref, seq_rows, seq):
    if seq_rows:
        return ref[...]
    return ref[pl.ds(seq, 1), :]


def _ada_kernel(cp_ref, cs_ref, w_ref, b_ref, op_ref, os_ref):
    w = w_ref[...].astype(BF16)
    for c_ref, o_ref in ((cp_ref, op_ref), (cs_ref, os_ref)):
        c = c_ref[...]
        o_ref[...] = _dot(c * _sigmoid(c), w) + b_ref[...]


def _ada(c_p, c_s, w_ada, b_ada):
    n = w_ada.shape[1]
    bn = 1024
    row = lambda r: pl.BlockSpec((r, D_MODEL), lambda j: (0, 0))
    out = lambda r: pl.BlockSpec((r, bn), lambda j: (0, j))
    return pl.pallas_call(
        _ada_kernel,
        grid=(n // bn,),
        in_specs=[row(c_p.shape[0]), row(c_s.shape[0]),
                  pl.BlockSpec((D_MODEL, bn), lambda j: (0, j)),
                  pl.BlockSpec((1, bn), lambda j: (0, j))],
        out_specs=[out(c_p.shape[0]), out(c_s.shape[0])],
        out_shape=[jax.ShapeDtypeStruct((c_p.shape[0], n), F32),
                   jax.ShapeDtypeStruct((c_s.shape[0], n), F32)],
        compiler_params=pltpu.CompilerParams(vmem_limit_bytes=VMEM_LIMIT),
        name="ada",
    )(c_p, c_s, w_ada, b_ada)


def _prep_in_kernel(main_ref, nxt_ref, gate_ref, o_ref, og_ref, *, first_shifted):
    j = pl.program_id(0)

    @pl.when(j == 0)
    def _():
        og_ref[...] = gate_ref[...].astype(BF16)

    @pl.when(j < first_shifted)
    def _():
        o_ref[...] = main_ref[...].astype(BF16)

    @pl.when(j >= first_shifted)
    def _():
        o_ref[...] = jnp.concatenate([main_ref[:, N_GATE:], nxt_ref[:, :N_GATE]], axis=1).astype(BF16)


def _prep_w_in(w_in):
    bn = 1024
    first_shifted = GATE_COL // bn
    last_lane_block = (D_IN - 1) // LANE
    kern = functools.partial(_prep_in_kernel, first_shifted=first_shifted)
    return pl.pallas_call(
        kern,
        grid=(Z_MAIN // bn,),
        in_specs=[pl.BlockSpec((D_MODEL, bn), lambda j: (0, j)),
                  pl.BlockSpec((D_MODEL, LANE), lambda j: (0, jnp.minimum((j + 1) * (bn // LANE), last_lane_block))),
                  pl.BlockSpec((D_MODEL, LANE), lambda j: (0, GATE_COL // LANE))],
        out_specs=[pl.BlockSpec((D_MODEL, bn), lambda j: (0, j)),
                   pl.BlockSpec((D_MODEL, LANE), lambda j: (0, 0))],
        out_shape=[jax.ShapeDtypeStruct((D_MODEL, Z_MAIN), BF16),
                   jax.ShapeDtypeStruct((D_MODEL, LANE), BF16)],
        compiler_params=pltpu.CompilerParams(vmem_limit_bytes=VMEM_LIMIT),
        name="prep_w_in",
    )(w_in, w_in, w_in)


def _in_kernel(x_ref, sh_ref, sc_ref, w_ref, wg_ref, z_ref, g_ref, h_scr, *, sub, seq_rows, tiles_per_seq):
    i = pl.program_id(0)
    j = pl.program_id(1)

    @pl.when(j == 0)
    def _():
        seq = i // tiles_per_seq
        sh = _mod_rows(sh_ref, seq_rows, seq)
        sc = _mod_rows(sc_ref, seq_rows, seq)
        for r in range(0, x_ref.shape[0], sub):
            rows = slice(r, r + sub)
            h = (_ln(x_ref[rows, :]) * (1.0 + sc) + sh).astype(BF16)
            h_scr[rows, :] = h
            g_ref[rows, :] = _dot(h, wg_ref[...])
            z_ref[rows, :] = _dot(h, w_ref[...]).astype(z_ref.dtype)

    @pl.when(j > 0)
    def _():
        z_ref[...] = _dot(h_scr[...], w_ref[...]).astype(z_ref.dtype)


def _in_proj(x, mod, w_main, w_gate, *, tm, sub, seq_rows, seq_len, z_dtype):
    tokens = x.shape[0]
    bn = 1024
    mod_spec = lambda k: pl.BlockSpec((mod.shape[0], D_MODEL), lambda i, j: (0, k))
    kern = functools.partial(_in_kernel, sub=sub, seq_rows=seq_rows, tiles_per_seq=max(seq_len // tm, 1))
    return pl.pallas_call(
        kern,
        grid=(tokens // tm, Z_MAIN // bn),
        in_specs=[pl.BlockSpec((tm, D_MODEL), lambda i, j: (i, 0)),
                  mod_spec(0), mod_spec(1),
                  pl.BlockSpec((D_MODEL, bn), lambda i, j: (0, j)),
                  pl.BlockSpec((D_MODEL, LANE), lambda i, j: (0, 0))],
        out_specs=[pl.BlockSpec((tm, bn), lambda i, j: (i, j)),
                   pl.BlockSpec((tm, LANE), lambda i, j: (i, 0))],
        out_shape=[jax.ShapeDtypeStruct((tokens, Z_MAIN), z_dtype),
                   jax.ShapeDtypeStruct((tokens, LANE), F32)],
        scratch_shapes=[pltpu.VMEM((tm, D_MODEL), BF16)],
        compiler_params=pltpu.CompilerParams(
            dimension_semantics=("arbitrary", "arbitrary"), vmem_limit_bytes=VMEM_LIMIT),
        name="in_proj",
    )(x, mod, mod, w_main, w_gate)


def _mix_consts(L, Tg, t_major):
    G = L // Tg
    r = np.arange(L)
    seq, tim = (r % G, r // G) if t_major else (r // Tg, r % Tg)
    same = seq[:, None] == seq[None, :]
    t = tim[:, None]
    u = tim[None, :]
    mats = [same & (u <= t), same & (u > t)]
    masks = [same & (u <= t), same & (t <= u), same]
    levels = []
    m = Tg // 2
    while m >= 1:
        levels.append(m)
        m //= 2
    for m in levels:
        blk = same & (u // m == t // m)
        odd = (t // m) % 2 == 1
        mats.append(np.where(odd, blk & (u <= t), blk & (u > t)))
        masks.append(same & odd & ((u // m) % 2 == 0) & (t // (2 * m) == u // (2 * m)))
    mats = np.concatenate(mats, axis=0).astype(np.float32)
    masks = np.stack(masks).astype(np.float32)
    last_rows = tuple(int(np.nonzero((seq == g) & (tim == Tg - 1))[0][0]) for g in range(G))
    return mats, masks, tuple(levels), last_rows


def _mix_kernel(*refs, L, Tg, levels, last_rows, t_major, has_state):
    G = L // Tg
    if has_state:
        (z_ref, g_ref, gbias_ref, lbl_ref, na_ref, nb_ref, mall_ref, msk_ref,
         c_in, n_in, m_in, s_in,
         mix_ref, c_out, n_out, m_out, s_out, e_scr) = refs
        m_row0 = pl.program_id(0) * G
    else:
        (z_ref, g_ref, gbias_ref, lbl_ref, na_ref, nb_ref, mall_ref, msk_ref,
         mix_ref, c_out, n_out, m_out, s_out, e_scr) = refs
        c_in, n_in, m_in, s_in = c_out, n_out, m_out, s_out
        m_row0 = pl.program_id(0)

        @pl.when(pl.program_id(1) == 0)
        def _():
            c_out[...] = jnp.zeros(c_out.shape, F32)
            n_out[...] = jnp.zeros(n_out.shape, F32)
            s_out[...] = jnp.zeros(s_out.shape, F32)
            m_out[0, pl.ds(m_row0, G), :] = jnp.zeros((G, N_HEADS_A), F32)

    neg_inf = F32(-jnp.inf)
    causal = msk_ref[0] > 0.5
    causal_t = msk_ref[1] > 0.5
    same = msk_ref[2] > 0.5
    row1 = lax.broadcasted_iota(jnp.int32, (L, 1), 0)
    seq_of_row = (row1 & (G - 1)) if t_major else (row1 // Tg)
    in_group = [seq_of_row == g for g in range(G)]

    def by_group(vals):
        if G == 1:
            return vals[0]
        out = jnp.where(in_group[0], vals[0], 0.0)
        for g in range(1, G):
            out = out + jnp.where(in_group[g], vals[g], 0.0)
        return out

    def zs(off, w):
        if len(z_ref.shape) == 3:
            return z_ref[:, :, off:off + w].reshape(L, w)
        return z_ref[:, off:off + w]

    def put_mix(off, w, val):
        val = val.astype(mix_ref.dtype)
        if len(mix_ref.shape) == 3:
            mix_ref[:, :, off:off + w] = val.reshape(Tg, G, w)
        else:
            mix_ref[:, off:off + w] = val

    gates = g_ref[...].reshape(L, LANE)
    pre = gates + gbias_ref[...]
    lsig = jnp.minimum(pre, 0.0) - jnp.log(1.0 + jnp.exp(-jnp.abs(pre)))
    pre_t = pre.T
    lsig_t = lsig.T
    m_prev = m_in[0, pl.ds(m_row0, G), :]
    scale = DK_A ** -0.5
    m_new_rows = []
    for h in range(N_HEADS_A):
        q = zs(OFF_QA + DK_A * h, DK_A)
        k = zs(OFF_KA + DK_A * h, DK_A)
        v = zs(OFF_VA + DV_A * h, DV_A)
        logi_c = pre[:, h:h + 1]
        logi_r = pre_t[h:h + 1, :]
        lf_c = lsig[:, N_HEADS_A + h:N_HEADS_A + h + 1]
        lf_r = lsig_t[N_HEADS_A + h:N_HEADS_A + h + 1, :]
        b_c = jnp.sum(jnp.where(causal, lf_r, 0.0), axis=1, keepdims=True)
        b_r = jnp.sum(jnp.where(causal_t, lf_c, 0.0), axis=0, keepdims=True)
        tot_c = jnp.sum(jnp.where(same, lf_r, 0.0), axis=1, keepdims=True)
        tot_r = jnp.sum(jnp.where(same, lf_c, 0.0), axis=0, keepdims=True)
        m_col = by_group([m_prev[g:g + 1, h:h + 1] for g in range(G)])
        dmat = jnp.where(causal, b_c - b_r + logi_r, neg_inf)
        inter = b_c + m_col
        m_t = jnp.maximum(inter, jnp.max(dmat, axis=1, keepdims=True))
        w_inter = jnp.exp(inter - m_t)
        smat = _dot_nt(q, k) * (scale * jnp.exp(dmat - m_t))
        q_c = by_group([_dot(q, c_in[0, g, h]) for g in range(G)]) * scale
        n_rows = by_group([n_in[0, g, h:h + 1, :] for g in range(G)])
        q_n = jnp.sum(q.astype(F32) * n_rows, axis=1, keepdims=True) * scale
        num = w_inter * q_c + _dot(smat, v)
        den = w_inter * q_n + jnp.sum(smat, axis=1, keepdims=True)
        hh = num / jnp.maximum(jnp.abs(den), jnp.exp(-m_t))
        ms = jnp.mean(hh * hh, axis=1, keepdims=True)
        oa = zs(OFF_OA + DV_A * h, DV_A).astype(F32)
        ya = hh * lax.rsqrt(ms + EPS) * na_ref[:, DV_A * h:DV_A * (h + 1)] * _sigmoid(oa)
        put_mix(DV_A * h, DV_A, ya)
        dec_c = tot_c - b_c + logi_c
        dec_r = tot_r - b_r + logi_r
        m_new_c = jnp.maximum(tot_c + m_col,
                              jnp.max(jnp.where(same, dec_r, neg_inf), axis=1, keepdims=True))
        wk = jnp.exp(dec_c - m_new_c)
        sc = jnp.exp(tot_c + m_col - m_new_c)
        kw = k.astype(F32) * wk
        m_new_h = []
        for g in range(G):
            last = last_rows[g]
            kg = kw if G == 1 else jnp.where(in_group[g], kw, 0.0)
            sc_g = sc[last:last + 1, :]
            c_new = sc_g * c_in[0, g, h] + _dot_tn(kg, v)
            n_new = sc_g * n_in[0, g, h:h + 1, :] + jnp.sum(kg, axis=0, keepdims=True)
            c_out[0, g, h] = c_new
            n_out[0, g, h:h + 1, :] = n_new
            m_new_h.append(m_new_c[last:last + 1, :])
        m_new_rows.append(m_new_h)
    for g in range(G):
        m_out[0, pl.ds(m_row0 + g, 1), :] = jnp.concatenate(
            [m_new_rows[h][g] for h in range(N_HEADS_A)], axis=1)

    l0 = lbl_ref[0:1, :]
    l1 = lbl_ref[1:2, :]
    lmax = jnp.maximum(l0, l1)
    e0 = jnp.exp(l0 - lmax)
    e1 = jnp.exp(l1 - lmax)
    lb = e0 / (e0 + e1)
    fb = zs(OFF_FB, B_WIDTH).astype(F32)
    e = jnp.exp(-jnp.abs(fb))
    r = 1.0 / (1.0 + e)
    pos = fb >= 0.0
    sig = jnp.where(pos, r, e * r)
    nsig = jnp.where(pos, e * r, r)
    logf = jnp.log(lb + (1.0 - lb) * sig)
    kb = (1.0 - lb) * nsig
    hi = logf.astype(BF16)
    r1 = logf - hi.astype(F32)
    mid = r1.astype(BF16)
    lo = (r1 - mid.astype(F32)).astype(BF16)
    mall = mall_ref[...]
    if L % 16 == 0:
        e_scr[...] = _dot(mall, jnp.concatenate([hi, mid, lo], axis=0))
    else:
        mall = mall[:, :L]
        e_scr[...] = _dot(mall, hi) + _dot(mall, mid) + _dot(mall, lo)

    dec_t = []
    for g in range(G):
        last = last_rows[g]
        bl = e_scr[last:last + 1, :]
        bl8 = jnp.concatenate([bl[:, DK_B * h:DK_B * (h + 1)] for h in range(N_HEADS_B)], axis=0)
        dec_t.append(jnp.exp(bl8).T)

    for h in range(N_HEADS_B):
        sl = slice(DK_B * h, DK_B * (h + 1))
        q = zs(OFF_QB + DK_B * h, DK_B).astype(BF16)
        k = kb[:, sl]
        kbf = k.astype(BF16)
        v = zs(OFF_VB + DV_B * h, DV_B)
        amat = jnp.zeros((L, L), F32)
        for li in range(len(levels)):
            ex = jnp.exp(e_scr[(2 + li) * L:(3 + li) * L, sl]).astype(BF16)
            amat = jnp.where(msk_ref[3 + li] > 0.5, _dot_nt(q * ex, kbf * ex), amat)
        qf = q.astype(F32)
        diag = jnp.sum(qf * k, axis=1, keepdims=True)
        qs = qf * jnp.exp(e_scr[0:L, sl])
        o_inter = by_group([_dot(qs, s_in[0, g, h]) for g in range(G)])
        o = _dot(amat, v) + diag * v.astype(F32) + o_inter
        ms = jnp.mean(o * o, axis=1, keepdims=True)
        gb = zs(OFF_GB + DV_B * h, DV_B).astype(F32)
        yb = o * lax.rsqrt(ms + EPS) * nb_ref[:, sl] * (gb * _sigmoid(gb))
        put_mix(A_WIDTH + DV_B * h, DV_B, yb)
        kw = k * jnp.exp(e_scr[L:2 * L, sl])
        for g in range(G):
            kg = kw if G == 1 else jnp.where(in_group[g], kw, 0.0)
            s_out[0, g, h] = dec_t[g][:, h:h + 1] * s_in[0, g, h] + _dot_tn(kg, v)


def _mixer(z, gates, gbias, lb_logits, norm_a, norm_b, *, n_seq, seq_len, states=None):
    has_state = states is not None
    if has_state:
        Tg, G = seq_len, SAMPLE_GROUP
        L = Tg * G
        grid = (n_seq // G,)
        tile = lambda w: pl.BlockSpec((Tg, G, w), lambda i: (0, i, 0))
        const2 = lambda i: (0, 0)
        const3 = lambda i: (0, 0, 0)
        st5 = lambda i: (0, i, 0, 0, 0)
        st4 = lambda i: (0, i, 0, 0)
        sems = ("arbitrary",)
        mix_shape = jax.ShapeDtypeStruct((Tg, n_seq, D_MODEL), F32)
    else:
        Tg, G = PROMPT_CHUNK, 1
        L = Tg
        nchunks = seq_len // L
        grid = (n_seq, nchunks)
        tile = lambda w: pl.BlockSpec((L, w), lambda b, c: (b * nchunks + c, 0))
        const2 = lambda b, c: (0, 0)
        const3 = lambda b, c: (0, 0, 0)
        st5 = lambda b, c: (0, b, 0, 0, 0)
        st4 = lambda b, c: (0, b, 0, 0)
        sems = ("arbitrary", "arbitrary")
        mix_shape = jax.ShapeDtypeStruct((n_seq * seq_len, D_MODEL), BF16)
    mats, masks, levels, last_rows = _mix_consts(L, Tg, has_state)
    n_mats = mats.shape[0] // L
    if L % 16 == 0:
        mats = np.concatenate([mats, mats, mats], axis=1)
    in_specs = [tile(Z_MAIN), tile(LANE),
                pl.BlockSpec((1, LANE), const2),
                pl.BlockSpec((2, B_WIDTH), const2),
                pl.BlockSpec((1, A_WIDTH), const2),
                pl.BlockSpec((1, B_WIDTH), const2),
                pl.BlockSpec(mats.shape, const2),
                pl.BlockSpec(masks.shape, const3)]
    args = [z, gates, gbias, lb_logits, norm_a, norm_b, jnp.asarray(mats, BF16), jnp.asarray(masks)]
    c_spec = pl.BlockSpec((1, G, N_HEADS_A, DK_A, DV_A), st5)
    n_spec = pl.BlockSpec((1, G, N_HEADS_A, DK_A), st4)
    m_spec = pl.BlockSpec((1, n_seq, N_HEADS_A), const3)
    s_spec = pl.BlockSpec((1, G, N_HEADS_B, DK_B, DV_B), st5)
    if has_state:
        in_specs += [c_spec, n_spec, m_spec, s_spec]
        args += list(states)
    kern = functools.partial(_mix_kernel, L=L, Tg=Tg, levels=levels, last_rows=last_rows,
                             t_major=has_state, has_state=has_state)
    return pl.pallas_call(
        kern,
        grid=grid,
        in_specs=in_specs,
        out_specs=[tile(D_MODEL), c_spec, n_spec, m_spec, s_spec],
        out_shape=[mix_shape,
                   jax.ShapeDtypeStruct((1, n_seq, N_HEADS_A, DK_A, DV_A), F32),
                   jax.ShapeDtypeStruct((1, n_seq, N_HEADS_A, DK_A), F32),
                   jax.ShapeDtypeStruct((1, n_seq, N_HEADS_A), F32),
                   jax.ShapeDtypeStruct((1, n_seq, N_HEADS_B, DK_B, DV_B), F32)],
        scratch_shapes=[pltpu.VMEM((n_mats * L, B_WIDTH), F32)],
        compiler_params=pltpu.CompilerParams(dimension_semantics=sems, vmem_limit_bytes=VMEM_LIMIT),
        name="mixer_state" if has_state else "mixer_chunk",
    )(*args)


def _res_kernel(*refs, sub, seq_rows, tiles_per_seq, emit_h2, nk):
    if emit_h2:
        a_ref, w_ref, x_ref, gm_ref, lg_ref, lb_ref, sh_ref, sc_ref, o_ref, h_ref = refs
    else:
        a_ref, w_ref, x_ref, gm_ref, lg_ref, lb_ref, o_ref = refs
    i = pl.program_id(0)
    k = pl.program_id(1)

    if nk > 1:
        @pl.when(k == 0)
        def _():
            o_ref[...] = _dot(a_ref[...], w_ref[...])

    if nk > 2:
        @pl.when((k > 0) & (k < nk - 1))
        def _():
            o_ref[...] += _dot(a_ref[...], w_ref[...])

    @pl.when(k == nk - 1)
    def _():
        seq = i // tiles_per_seq
        gm = _mod_rows(gm_ref, seq_rows, seq)
        if emit_h2:
            sh = _mod_rows(sh_ref, seq_rows, seq)
            sc = _mod_rows(sc_ref, seq_rows, seq)
        for r in range(0, o_ref.shape[0], sub):
            rows = slice(r, r + sub)
            acc = _dot(a_ref[rows, :], w_ref[...])
            if nk > 1:
                acc = acc + o_ref[rows, :]
            x1 = _ln(ALPHA * x_ref[rows, :] + gm * acc) * lg_ref[...] + lb_ref[...]
            o_ref[rows, :] = x1
            if emit_h2:
                h_ref[rows, :] = (_ln(x1) * (1.0 + sc) + sh).astype(BF16)


def _res_block(a, w, x, mod, ln_g, ln_b, *, tm, bk, sub, seq_rows, seq_len, gate_col, emit_h2):
    tokens, kdim = a.shape
    nk = kdim // bk
    mod_spec = lambda c: pl.BlockSpec((mod.shape[0], D_MODEL), lambda i, k: (0, c))
    row_spec = pl.BlockSpec((tm, D_MODEL), lambda i, k: (i, 0))
    vec_spec = pl.BlockSpec((1, D_MODEL), lambda i, k: (0, 0))
    in_specs = [pl.BlockSpec((tm, bk), lambda i, k: (i, k)),
                pl.BlockSpec((bk, D_MODEL), lambda i, k: (k, 0)),
                row_spec, mod_spec(gate_col), vec_spec, vec_spec]
    args = [a, w, x, mod, ln_g, ln_b]
    out_specs = [row_spec]
    out_shape = [jax.ShapeDtypeStruct((tokens, D_MODEL), F32)]
    if emit_h2:
        in_specs += [mod_spec(3), mod_spec(4)]
        args += [mod, mod]
        out_specs.append(row_spec)
        out_shape.append(jax.ShapeDtypeStruct((tokens, D_MODEL), BF16))
    kern = functools.partial(_res_kernel, sub=sub, seq_rows=seq_rows, tiles_per_seq=max(seq_len // tm, 1),
                             emit_h2=emit_h2, nk=nk)
    return pl.pallas_call(
        kern,
        grid=(tokens // tm, nk),
        in_specs=in_specs,
        out_specs=out_specs,
        out_shape=out_shape,
        compiler_params=pltpu.CompilerParams(
            dimension_semantics=("arbitrary", "arbitrary"), vmem_limit_bytes=VMEM_LIMIT),
        name="res_block",
    )(*args)


def _up_kernel(*refs, tm, sub, seq_len, n_seq, t_major, tiles_per_seq):
    if t_major:
        h_ref, wa_ref, ulo_ref, uhi_ref, cw_ref, cb_ref, cache_ref, g_ref, tail_ref, abuf, wa_scr, wu_scr = refs
    else:
        h_ref, wa_ref, ulo_ref, uhi_ref, cw_ref, cb_ref, g_ref, tail_ref, abuf, wa_scr, wu_scr = refs
    j = pl.program_id(0)
    i = pl.program_id(1)
    bn = wa_scr.shape[1]
    valid = (j * bn + lax.broadcasted_iota(jnp.int32, (1, bn), 1)) < D_FF
    hist = CONV_W - 1 if not t_major else (CONV_W - 1) * n_seq
    base = 8 if not t_major else hist

    @pl.when(i == 0)
    def _():
        split = D_FF % bn
        wu = jnp.concatenate([ulo_ref[:, split:], uhi_ref[:, :split]], axis=1)
        wa_scr[...] = jnp.where(valid, wa_ref[...], 0.0).astype(BF16)
        wu_scr[...] = jnp.where(valid, wu, 0.0).astype(BF16)
        if t_major:
            abuf[0:hist, :] = jnp.where(valid, cache_ref[...].reshape(hist, bn), 0.0)
        else:
            abuf[0:base, :] = jnp.zeros((base, bn), F32)

    if not t_major:
        @pl.when(i > 0)
        def _():
            abuf[base - hist:base, :] = abuf[tm + base - hist:tm + base, :]

    step = n_seq if t_major else 1
    for r in range(0, tm, sub):
        h = h_ref[r:r + sub, :]
        a = _dot(h, wa_scr[...])
        u = _dot(h, wu_scr[...])
        abuf[base + r:base + r + sub, :] = a
        a1 = abuf[base + r - step:base + r - step + sub, :]
        a2 = abuf[base + r - 2 * step:base + r - 2 * step + sub, :]
        if not t_major:
            t = (i * tm + r + lax.broadcasted_iota(jnp.int32, (sub, 1), 0)) & (seq_len - 1)
            a1 = jnp.where(t >= 1, a1, 0.0)
            a2 = jnp.where(t >= 2, a2, 0.0)
        conv = cb_ref[...] + cw_ref[0:1, :] * a2 + cw_ref[1:2, :] * a1 + cw_ref[2:3, :] * a
        gl = 0.5 * conv * (1.0 + lax.erf(conv * (2.0 ** -0.5)))
        g_ref[r:r + sub, :] = (gl * u).astype(BF16)

    if t_major:
        tail_ref[...] = abuf[base + tm - hist:base + tm, :].reshape(tail_ref.shape)
    else:
        @pl.when((i + 1) % tiles_per_seq == 0)
        def _():
            tail_ref[0] = abuf[tm + base - hist:tm + base, :]


def _up_proj(h2, w_up, conv_w_p, conv_b_p, *, tm, sub, seq_len, n_seq, cache=None):
    tokens = h2.shape[0]
    t_major = cache is not None
    bn = FF_BLOCK
    nj = D_FF_PAD // bn
    u_lo = D_FF // bn
    u_last = (2 * D_FF - 1) // bn
    tiles_per_seq = max(seq_len // tm, 1)
    wspec = lambda f: pl.BlockSpec((D_MODEL, bn), f)
    in_specs = [pl.BlockSpec((tm, D_MODEL), lambda j, i: (i, 0)),
                wspec(lambda j, i: (0, j)),
                wspec(lambda j, i: (0, u_lo + j)),
                wspec(lambda j, i: (0, jnp.minimum(u_lo + 1 + j, u_last))),
                pl.BlockSpec((CONV_W, bn), lambda j, i: (0, j)),
                pl.BlockSpec((1, bn), lambda j, i: (0, j))]
    args = [h2, w_up, w_up, w_up, conv_w_p, conv_b_p]
    out_specs = [pl.BlockSpec((tm, bn), lambda j, i: (i, j))]
    out_shape = [jax.ShapeDtypeStruct((tokens, D_FF_PAD), BF16)]
    if t_major:
        hist_spec = pl.BlockSpec((CONV_W - 1, n_seq, bn), lambda j, i: (0, 0, j))
        in_specs.append(hist_spec)
        args.append(cache)
        out_specs.append(hist_spec)
        out_shape.append(jax.ShapeDtypeStruct((CONV_W - 1, n_seq, D_FF_PAD), F32))
        abuf_rows = (CONV_W - 1) * n_seq + tm
    else:
        out_specs.append(pl.BlockSpec((1, CONV_W - 1, bn), lambda j, i: (i // tiles_per_seq, 0, j)))
        out_shape.append(jax.ShapeDtypeStruct((n_seq, CONV_W - 1, D_FF_PAD), F32))
        abuf_rows = tm + 8
    kern = functools.partial(_up_kernel, tm=tm, sub=sub, seq_len=seq_len, n_seq=n_seq, t_major=t_major,
                             tiles_per_seq=tiles_per_seq)
    return pl.pallas_call(
        kern,
        grid=(nj, tokens // tm),
        in_specs=in_specs,
        out_specs=out_specs,
        out_shape=out_shape,
        scratch_shapes=[pltpu.VMEM((abuf_rows, bn), F32),
                        pltpu.VMEM((D_MODEL, bn), BF16),
                        pltpu.VMEM((D_MODEL, bn), BF16)],
        compiler_params=pltpu.CompilerParams(
            dimension_semantics=("arbitrary", "arbitrary"), vmem_limit_bytes=VMEM_LIMIT),
        name="up_proj",
    )(*args)


def _path(x, mod, w, *, n_seq, seq_len, tm_in, tm_res, tm_up, tm_down, states=None, cache=None):
    sample = states is not None
    sub = n_seq if sample else 256
    z, gates = _in_proj(x, mod, w["w_main"], w["w_gate"], tm=tm_in, sub=sub, seq_rows=sample,
                        seq_len=seq_len, z_dtype=F32 if sample else BF16)
    if sample:
        z = z.reshape(seq_len, n_seq, Z_MAIN)
        gates = gates.reshape(seq_len, n_seq, LANE)
    mix, c1, n1, m1, s1 = _mixer(z, gates, w["gbias"], w["lb_logits"], w["norm_a"], w["norm_b"],
                                 n_seq=n_seq, seq_len=seq_len, states=states)
    mix = mix.reshape(n_seq * seq_len, D_MODEL)
    x1, h2 = _res_block(mix, w["w_out"], x, mod, w["ln1_g"], w["ln1_b"], tm=tm_res, bk=D_MODEL, sub=sub,
                        seq_rows=sample, seq_len=seq_len, gate_col=2, emit_h2=True)
    g, conv_out = _up_proj(h2, w["w_up"], w["conv_w"], w["conv_b"], tm=tm_up, sub=sub, seq_len=seq_len,
                           n_seq=n_seq, cache=cache)
    (x2,) = _res_block(g, w["w_down"], x1, mod, w["ln2_g"], w["ln2_b"], tm=tm_down, bk=D_FF_PAD // 4, sub=sub,
                       seq_rows=sample, seq_len=seq_len, gate_col=5, emit_h2=False)
    return x2, (c1, n1, m1, s1), conv_out


def kernel(x_prompt, x_sample, state_mlstm_C, state_mlstm_n, state_mlstm_m, state_hgrn_S, cache_ffn_conv,
           c_prompt, c_sample, hgrn_lb_logits, w_ada, b_ada, w_in, b_gate_a, norm_a, norm_b, w_out,
           ln1_g, ln1_b, w_up, conv_w, conv_b, w_down, ln2_g, ln2_b):
    bp, tp, _ = x_prompt.shape
    bs, ts, _ = x_sample.shape
    pad_ff = D_FF_PAD - D_FF
    w_main, w_gate = _prep_w_in(w_in[0])
    weights = dict(
        w_main=w_main, w_gate=w_gate, w_out=w_out[0].astype(BF16), w_up=w_up[0],
        w_down=jnp.pad(w_down[0].astype(BF16), ((0, pad_ff), (0, 0))),
        gbias=jnp.pad(b_gate_a[0].reshape(1, N_GATE), ((0, 0), (0, LANE - N_GATE))),
        lb_logits=hgrn_lb_logits, norm_a=norm_a, norm_b=norm_b,
        ln1_g=ln1_g, ln1_b=ln1_b, ln2_g=ln2_g, ln2_b=ln2_b,
        conv_w=jnp.pad(conv_w[0], ((0, 0), (0, pad_ff))),
        conv_b=jnp.pad(conv_b, ((0, 0), (0, pad_ff))))

    c_p = jnp.pad(c_prompt, ((0, 8 - bp), (0, 0)))
    mod_p, mod_s = _ada(c_p, c_sample, w_ada[0], b_ada)

    yp, st_p, tail_p = _path(x_prompt.reshape(bp * tp, D_MODEL), mod_p, weights, n_seq=bp, seq_len=tp,
                             tm_in=1024, tm_res=512, tm_up=1024, tm_down=1024)

    xs_t = jnp.swapaxes(x_sample, 0, 1).reshape(ts * bs, D_MODEL)
    cache_t = jnp.swapaxes(cache_ffn_conv[0], 0, 1)
    ys_t, st_s, tail_s = _path(xs_t, mod_s, weights, n_seq=bs, seq_len=ts,
                               tm_in=ts * bs, tm_res=ts * bs, tm_up=ts * bs, tm_down=ts * bs,
                               states=(state_mlstm_C, state_mlstm_n, state_mlstm_m, state_hgrn_S),
                               cache=cache_t)

    ys = jnp.swapaxes(ys_t.reshape(ts, bs, D_MODEL), 0, 1)
    conv_p = tail_p[None, :, :, :D_FF]
    conv_s = jnp.swapaxes(tail_s[:, :, :D_FF], 0, 1)[None]
    return (yp.reshape(bp, tp, D_MODEL), ys,
            st_p[0], st_p[1], st_p[2], st_p[3], conv_p,
            st_s[0], st_s[1], st_s[2], st_s[3], conv_s)
```

```python
import functools

import numpy as np
import jax
import jax.numpy as jnp
from jax import lax
from jax.experimental import pallas as pl
from jax.experimental.pallas import tpu as pltpu

F32 = jnp.float32
BF16 = jnp.bfloat16

D_MODEL = 2048
N_HEADS_A, DK_A, DV_A = 4, 128, 256
N_HEADS_B, DK_B, DV_B = 8, 128, 128
A_WIDTH = N_HEADS_A * DV_A
B_WIDTH = N_HEADS_B * DV_B
N_GATE = 2 * N_HEADS_A
GATE_COL = 2 * N_HEADS_A * DK_A + A_WIDTH
D_IN = GATE_COL + N_GATE + A_WIDTH + 4 * B_WIDTH
D_FF = 5504
CONV_W = 3
EPS = 1e-5
ALPHA = 2.0 ** 0.25
LANE = 128
FF_BLOCK = 512
D_FF_PAD = 5632
Z_MAIN = D_IN - N_GATE
OFF_QA, OFF_KA, OFF_VA, OFF_OA = 0, 512, 1024, 2048
OFF_FB, OFF_QB, OFF_VB, OFF_GB = 3072, 4096, 5120, 6144
VMEM_LIMIT = 56 * 1024 * 1024
PROMPT_CHUNK = 128
SAMPLE_GROUP = 8


def _ln(x):
    mu = jnp.mean(x, axis=-1, keepdims=True)
    xc = x - mu
    var = jnp.mean(xc * xc, axis=-1, keepdims=True)
    return xc * lax.rsqrt(var + EPS)


def _sigmoid(x):
    return 1.0 / (1.0 + jnp.exp(-x))


def _dot(a, b):
    return jnp.dot(a.astype(BF16), b.astype(BF16), preferred_element_type=F32)


def _dot_nt(a, b):
    return lax.dot_general(a.astype(BF16), b.astype(BF16), (((1,), (1,)), ((), ())),
                           preferred_element_type=F32)


def _dot_tn(a, b):
    return lax.dot_general(a.astype(BF16), b.astype(BF16), (((0,), (0,)), ((), ())),
                           preferred_element_type=F32)


def _mod_rows(ref, seq_rows, seq):
    if seq_rows:
        return ref[...]
    return ref[pl.ds(seq, 1), :]


def _ada_kernel(cp_ref, cs_ref, w_ref, b_ref, op_ref, os_ref):
    w = w_ref[...].astype(BF16)
    for c_ref, o_ref in ((cp_ref, op_ref), (cs_ref, os_ref)):
        c = c_ref[...]
        o_ref[...] = _dot(c * _sigmoid(c), w) + b_ref[...]


def _ada(c_p, c_s, w_ada, b_ada):
    n = w_ada.shape[1]
    bn = 1024
    row = lambda r: pl.BlockSpec((r, D_MODEL), lambda j: (0, 0))
    out = lambda r: pl.BlockSpec((r, bn), lambda j: (0, j))
    return pl.pallas_call(
        _ada_kernel,
        grid=(n // bn,),
        in_specs=[row(c_p.shape[0]), row(c_s.shape[0]),
                  pl.BlockSpec((D_MODEL, bn), lambda j: (0, j)),
                  pl.BlockSpec((1, bn), lambda j: (0, j))],
        out_specs=[out(c_p.shape[0]), out(c_s.shape[0])],
        out_shape=[jax.ShapeDtypeStruct((c_p.shape[0], n), F32),
                   jax.ShapeDtypeStruct((c_s.shape[0], n), F32)],
        compiler_params=pltpu.CompilerParams(vmem_limit_bytes=VMEM_LIMIT),
        name="ada",
    )(c_p, c_s, w_ada, b_ada)


def _prep_in_kernel(front_ref, back_ref, gate_ref, o_ref, og_ref, *, first_shifted):
    j = pl.program_id(0)

    @pl.when(j == 0)
    def _():
        og_ref[...] = gate_ref[...].T.astype(BF16)

    def emit(src_ref):
        for r in range(0, src_ref.shape[0], 256):
            o_ref[:, r:r + 256] = src_ref[r:r + 256, :].T.astype(BF16)

    @pl.when(j < first_shifted)
    def _():
        emit(front_ref)

    @pl.when(j >= first_shifted)
    def _():
        emit(back_ref)


def _prep_w_in(w_in_t):
    bn = 1024
    first_shifted = GATE_COL // bn
    kern = functools.partial(_prep_in_kernel, first_shifted=first_shifted)
    return pl.pallas_call(
        kern,
        grid=(Z_MAIN // bn,),
        in_specs=[pl.BlockSpec((bn, D_MODEL), lambda j: (jnp.minimum(j, first_shifted - 1), 0)),
                  pl.BlockSpec((pl.Element(bn), pl.Element(D_MODEL)),
                               lambda j: (pl.multiple_of(jnp.maximum(j, first_shifted) * bn + N_GATE, N_GATE), 0)),
                  pl.BlockSpec((LANE, D_MODEL), lambda j: (GATE_COL // LANE, 0))],
        out_specs=[pl.BlockSpec((D_MODEL, bn), lambda j: (0, j)),
                   pl.BlockSpec((D_MODEL, LANE), lambda j: (0, 0))],
        out_shape=[jax.ShapeDtypeStruct((D_MODEL, Z_MAIN), BF16),
                   jax.ShapeDtypeStruct((D_MODEL, LANE), BF16)],
        compiler_params=pltpu.CompilerParams(vmem_limit_bytes=VMEM_LIMIT),
        name="prep_w_in",
    )(w_in_t, w_in_t, w_in_t)


def _in_kernel(x_ref, sh_ref, sc_ref, w_ref, wg_ref, z_ref, g_ref, h_scr, *, sub, seq_rows, tiles_per_seq):
    i = pl.program_id(0)
    j = pl.program_id(1)

    @pl.when(j == 0)
    def _():
        seq = i // tiles_per_seq
        sh = _mod_rows(sh_ref, seq_rows, seq)
        sc = _mod_rows(sc_ref, seq_rows, seq)
        for r in range(0, x_ref.shape[0], sub):
            rows = slice(r, r + sub)
            h = (_ln(x_ref[rows, :]) * (1.0 + sc) + sh).astype(BF16)
            h_scr[rows, :] = h
            g_ref[rows, :] = _dot(h, wg_ref[...])
            z_ref[rows, :] = _dot(h, w_ref[...]).astype(z_ref.dtype)

    @pl.when(j > 0)
    def _():
        z_ref[...] = _dot(h_scr[...], w_ref[...]).astype(z_ref.dtype)


def _in_proj(x, mod, w_main, w_gate, *, tm, sub, seq_rows, seq_len, z_dtype):
    tokens = x.shape[0]
    bn = 1024
    mod_spec = lambda k: pl.BlockSpec((mod.shape[0], D_MODEL), lambda i, j: (0, k))
    kern = functools.partial(_in_kernel, sub=sub, seq_rows=seq_rows, tiles_per_seq=max(seq_len // tm, 1))
    return pl.pallas_call(
        kern,
        grid=(tokens // tm, Z_MAIN // bn),
        in_specs=[pl.BlockSpec((tm, D_MODEL), lambda i, j: (i, 0)),
                  mod_spec(0), mod_spec(1),
                  pl.BlockSpec((D_MODEL, bn), lambda i, j: (0, j)),
                  pl.BlockSpec((D_MODEL, LANE), lambda i, j: (0, 0))],
        out_specs=[pl.BlockSpec((tm, bn), lambda i, j: (i, j)),
                   pl.BlockSpec((tm, LANE), lambda i, j: (i, 0))],
        out_shape=[jax.ShapeDtypeStruct((tokens, Z_MAIN), z_dtype),
                   jax.ShapeDtypeStruct((tokens, LANE), F32)],
        scratch_shapes=[pltpu.VMEM((tm, D_MODEL), BF16)],
        compiler_params=pltpu.CompilerParams(
            dimension_semantics=("arbitrary", "arbitrary"), vmem_limit_bytes=VMEM_LIMIT),
        name="in_proj",
    )(x, mod, mod, w_main, w_gate)


def _mix_consts(L, Tg, t_major):
    G = L // Tg
    r = np.arange(L)
    seq, tim = (r % G, r // G) if t_major else (r // Tg, r % Tg)
    same = seq[:, None] == seq[None, :]
    t = tim[:, None]
    u = tim[None, :]
    mats = [same & (u <= t), same & (u > t)]
    masks = [same & (u <= t), same & (t <= u), same]
    levels = []
    m = Tg // 2
    while m >= 1:
        levels.append(m)
        m //= 2
    for m in levels:
        blk = same & (u // m == t // m)
        odd = (t // m) % 2 == 1
        mats.append(np.where(odd, blk & (u <= t), blk & (u > t)))
        masks.append(same & odd & ((u // m) % 2 == 0) & (t // (2 * m) == u // (2 * m)))
    mats = np.concatenate(mats, axis=0).astype(np.float32)
    masks = np.stack(masks).astype(np.float32)
    last_rows = tuple(int(np.nonzero((seq == g) & (tim == Tg - 1))[0][0]) for g in range(G))
    return mats, masks, tuple(levels), last_rows


def _mix_kernel(*refs, L, Tg, levels, last_rows, t_major, has_state):
    G = L // Tg
    if has_state:
        (z_ref, g_ref, gbias_ref, lbl_ref, na_ref, nb_ref, mall_ref, msk_ref,
         c_in, n_in, m_in, s_in,
         mix_ref, c_out, n_out, m_out, s_out, e_scr) = refs
        m_row0 = pl.program_id(0) * G
    else:
        (z_ref, g_ref, gbias_ref, lbl_ref, na_ref, nb_ref, mall_ref, msk_ref,
         mix_ref, c_out, n_out, m_out, s_out, e_scr) = refs
        c_in, n_in, m_in, s_in = c_out, n_out, m_out, s_out
        m_row0 = pl.program_id(0)

        @pl.when(pl.program_id(1) == 0)
        def _():
            c_out[...] = jnp.zeros(c_out.shape, F32)
            n_out[...] = jnp.zeros(n_out.shape, F32)
            s_out[...] = jnp.zeros(s_out.shape, F32)
            m_out[0, pl.ds(m_row0, G), :] = jnp.zeros((G, N_HEADS_A), F32)

    neg_inf = F32(-jnp.inf)
    causal = msk_ref[0] > 0.5
    causal_t = msk_ref[1] > 0.5
    same = msk_ref[2] > 0.5
    row1 = lax.broadcasted_iota(jnp.int32, (L, 1), 0)
    seq_of_row = (row1 & (G - 1)) if t_major else (row1 // Tg)
    in_group = [seq_of_row == g for g in range(G)]

    def by_group(vals):
        if G == 1:
            return vals[0]
        out = jnp.where(in_group[0], vals[0], 0.0)
        for g in range(1, G):
            out = out + jnp.where(in_group[g], vals[g], 0.0)
        return out

    def zs(off, w):
        if len(z_ref.shape) == 3:
            return z_ref[:, :, off:off + w].reshape(L, w)
        return z_ref[:, off:off + w]

    def put_mix(off, w, val):
        val = val.astype(mix_ref.dtype)
        if len(mix_ref.shape) == 3:
            mix_ref[:, :, off:off + w] = val.reshape(Tg, G, w)
        else:
            mix_ref[:, off:off + w] = val

    gates = g_ref[...].reshape(L, LANE)
    pre = gates + gbias_ref[...]
    lsig = jnp.minimum(pre, 0.0) - jnp.log(1.0 + jnp.exp(-jnp.abs(pre)))
    pre_t = pre.T
    lsig_t = lsig.T
    m_prev = m_in[0, pl.ds(m_row0, G), :]
    scale = DK_A ** -0.5
    m_new_rows = []
    for h in range(N_HEADS_A):
        q = zs(OFF_QA + DK_A * h, DK_A)
        k = zs(OFF_KA + DK_A * h, DK_A)
        v = zs(OFF_VA + DV_A * h, DV_A)
        logi_c = pre[:, h:h + 1]
        logi_r = pre_t[h:h + 1, :]
        lf_c = lsig[:, N_HEADS_A + h:N_HEADS_A + h + 1]
        lf_r = lsig_t[N_HEADS_A + h:N_HEADS_A + h + 1, :]
        b_c = jnp.sum(jnp.where(causal, lf_r, 0.0), axis=1, keepdims=True)
        b_r = jnp.sum(jnp.where(causal_t, lf_c, 0.0), axis=0, keepdims=True)
        tot_c = jnp.sum(jnp.where(same, lf_r, 0.0), axis=1, keepdims=True)
        tot_r = jnp.sum(jnp.where(same, lf_c, 0.0), axis=0, keepdims=True)
        m_col = by_group([m_prev[g:g + 1, h:h + 1] for g in range(G)])
        dmat = jnp.where(causal, b_c - b_r + logi_r, neg_inf)
        inter = b_c + m_col
        m_t = jnp.maximum(inter, jnp.max(dmat, axis=1, keepdims=True))
        w_inter = jnp.exp(inter - m_t)
        smat = _dot_nt(q, k) * (scale * jnp.exp(dmat - m_t))
        q_c = by_group([_dot(q, c_in[0, g, h]) for g in range(G)]) * scale
        n_rows = by_group([n_in[0, g, h:h + 1, :] for g in range(G)])
        q_n = jnp.sum(q.astype(F32) * n_rows, axis=1, keepdims=True) * scale
        num = w_inter * q_c + _dot(smat, v)
        den = w_inter * q_n + jnp.sum(smat, axis=1, keepdims=True)
        hh = num / jnp.maximum(jnp.abs(den), jnp.exp(-m_t))
        ms = jnp.mean(hh * hh, axis=1, keepdims=True)
        oa = zs(OFF_OA + DV_A * h, DV_A).astype(F32)
        ya = hh * lax.rsqrt(ms + EPS) * na_ref[:, DV_A * h:DV_A * (h + 1)] * _sigmoid(oa)
        put_mix(DV_A * h, DV_A, ya)
        dec_c = tot_c - b_c + logi_c
        dec_r = tot_r - b_r + logi_r
        m_new_c = jnp.maximum(tot_c + m_col,
                              jnp.max(jnp.where(same, dec_r, neg_inf), axis=1, keepdims=True))
        wk = jnp.exp(dec_c - m_new_c)
        sc = jnp.exp(tot_c + m_col - m_new_c)
        kw = k.astype(F32) * wk
        m_new_h = []
        for g in range(G):
            last = last_rows[g]
            kg = kw if G == 1 else jnp.where(in_group[g], kw, 0.0)
            sc_g = sc[last:last + 1, :]
            c_new = sc_g * c_in[0, g, h] + _dot_tn(kg, v)
            n_new = sc_g * n_in[0, g, h:h + 1, :] + jnp.sum(kg, axis=0, keepdims=True)
            c_out[0, g, h] = c_new
            n_out[0, g, h:h + 1, :] = n_new
            m_new_h.append(m_new_c[last:last + 1, :])
        m_new_rows.append(m_new_h)
    for g in range(G):
        m_out[0, pl.ds(m_row0 + g, 1), :] = jnp.concatenate(
            [m_new_rows[h][g] for h in range(N_HEADS_A)], axis=1)

    l0 = lbl_ref[0:1, :]
    l1 = lbl_ref[1:2, :]
    lmax = jnp.maximum(l0, l1)
    e0 = jnp.exp(l0 - lmax)
    e1 = jnp.exp(l1 - lmax)
    lb = e0 / (e0 + e1)
    fb = zs(OFF_FB, B_WIDTH).astype(F32)
    e = jnp.exp(-jnp.abs(fb))
    r = 1.0 / (1.0 + e)
    pos = fb >= 0.0
    sig = jnp.where(pos, r, e * r)
    nsig = jnp.where(pos, e * r, r)
    logf = jnp.log(lb + (1.0 - lb) * sig)
    kb = (1.0 - lb) * nsig
    hi = logf.astype(BF16)
    r1 = logf - hi.astype(F32)
    mid = r1.astype(BF16)
    lo = (r1 - mid.astype(F32)).astype(BF16)
    mall = mall_ref[...]
    if L % 16 == 0:
        e_scr[...] = _dot(mall, jnp.concatenate([hi, mid, lo], axis=0))
    else:
        mall = mall[:, :L]
        e_scr[...] = _dot(mall, hi) + _dot(mall, mid) + _dot(mall, lo)

    dec_t = []
    for g in range(G):
        last = last_rows[g]
        bl = e_scr[last:last + 1, :]
        bl8 = jnp.concatenate([bl[:, DK_B * h:DK_B * (h + 1)] for h in range(N_HEADS_B)], axis=0)
        dec_t.append(jnp.exp(bl8).T)

    for h in range(N_HEADS_B):
        sl = slice(DK_B * h, DK_B * (h + 1))
        q = zs(OFF_QB + DK_B * h, DK_B).astype(BF16)
        k = kb[:, sl]
        kbf = k.astype(BF16)
        v = zs(OFF_VB + DV_B * h, DV_B)
        amat = jnp.zeros((L, L), F32)
        for li in range(len(levels)):
            ex = jnp.exp(e_scr[(2 + li) * L:(3 + li) * L, sl]).astype(BF16)
            amat = jnp.where(msk_ref[3 + li] > 0.5, _dot_nt(q * ex, kbf * ex), amat)
        qf = q.astype(F32)
        diag = jnp.sum(qf * k, axis=1, keepdims=True)
        qs = qf * jnp.exp(e_scr[0:L, sl])
        o_inter = by_group([_dot(qs, s_in[0, g, h]) for g in range(G)])
        o = _dot(amat, v) + diag * v.astype(F32) + o_inter
        ms = jnp.mean(o * o, axis=1, keepdims=True)
        gb = zs(OFF_GB + DV_B * h, DV_B).astype(F32)
        yb = o * lax.rsqrt(ms + EPS) * nb_ref[:, sl] * (gb * _sigmoid(gb))
        put_mix(A_WIDTH + DV_B * h, DV_B, yb)
        kw = k * jnp.exp(e_scr[L:2 * L, sl])
        for g in range(G):
            kg = kw if G == 1 else jnp.where(in_group[g], kw, 0.0)
            s_out[0, g, h] = dec_t[g][:, h:h + 1] * s_in[0, g, h] + _dot_tn(kg, v)


def _mixer(z, gates, gbias, lb_logits, norm_a, norm_b, *, n_seq, seq_len, states=None):
    has_state = states is not None
    if has_state:
        Tg, G = seq_len, SAMPLE_GROUP
        L = Tg * G
        grid = (n_seq // G,)
        tile = lambda w: pl.BlockSpec((Tg, G, w), lambda i: (0, i, 0))
        const2 = lambda i: (0, 0)
        const3 = lambda i: (0, 0, 0)
        st5 = lambda i: (0, i, 0, 0, 0)
        st4 = lambda i: (0, i, 0, 0)
        sems = ("arbitrary",)
        mix_shape = jax.ShapeDtypeStruct((Tg, n_seq, D_MODEL), F32)
    else:
        Tg, G = PROMPT_CHUNK, 1
        L = Tg
        nchunks = seq_len // L
        grid = (n_seq, nchunks)
        tile = lambda w: pl.BlockSpec((L, w), lambda b, c: (b * nchunks + c, 0))
        const2 = lambda b, c: (0, 0)
        const3 = lambda b, c: (0, 0, 0)
        st5 = lambda b, c: (0, b, 0, 0, 0)
        st4 = lambda b, c: (0, b, 0, 0)
        sems = ("arbitrary", "arbitrary")
        mix_shape = jax.ShapeDtypeStruct((n_seq * seq_len, D_MODEL), BF16)
    mats, masks, levels, last_rows = _mix_consts(L, Tg, has_state)
    n_mats = mats.shape[0] // L
    if L % 16 == 0:
        mats = np.concatenate([mats, mats, mats], axis=1)
    in_specs = [tile(Z_MAIN), tile(LANE),
                pl.BlockSpec((1, LANE), const2),
                pl.BlockSpec((2, B_WIDTH), const2),
                pl.BlockSpec((1, A_WIDTH), const2),
                pl.BlockSpec((1, B_WIDTH), const2),
                pl.BlockSpec(mats.shape, const2),
                pl.BlockSpec(masks.shape, const3)]
    args = [z, gates, gbias, lb_logits, norm_a, norm_b, jnp.asarray(mats, BF16), jnp.asarray(masks)]
    c_spec = pl.BlockSpec((1, G, N_HEADS_A, DK_A, DV_A), st5)
    n_spec = pl.BlockSpec((1, G, N_HEADS_A, DK_A), st4)
    m_spec = pl.BlockSpec((1, n_seq, N_HEADS_A), const3)
    s_spec = pl.BlockSpec((1, G, N_HEADS_B, DK_B, DV_B), st5)
    if has_state:
        in_specs += [c_spec, n_spec, m_spec, s_spec]
        args += list(states)
    kern = functools.partial(_mix_kernel, L=L, Tg=Tg, levels=levels, last_rows=last_rows,
                             t_major=has_state, has_state=has_state)
    return pl.pallas_call(
        kern,
        grid=grid,
        in_specs=in_specs,
        out_specs=[tile(D_MODEL), c_spec, n_spec, m_spec, s_spec],
        out_shape=[mix_shape,
                   jax.ShapeDtypeStruct((1, n_seq, N_HEADS_A, DK_A, DV_A), F32),
                   jax.ShapeDtypeStruct((1, n_seq, N_HEADS_A, DK_A), F32),
                   jax.ShapeDtypeStruct((1, n_seq, N_HEADS_A), F32),
                   jax.ShapeDtypeStruct((1, n_seq, N_HEADS_B, DK_B, DV_B), F32)],
        scratch_shapes=[pltpu.VMEM((n_mats * L, B_WIDTH), F32)],
        compiler_params=pltpu.CompilerParams(dimension_semantics=sems, vmem_limit_bytes=VMEM_LIMIT),
        name="mixer_state" if has_state else "mixer_chunk",
    )(*args)


def _res_kernel(*refs, sub, seq_rows, tiles_per_seq, emit_h2, nk):
    if emit_h2:
        a_ref, w_ref, x_ref, gm_ref, lg_ref, lb_ref, sh_ref, sc_ref, o_ref, h_ref = refs
    else:
        a_ref, w_ref, x_ref, gm_ref, lg_ref, lb_ref, o_ref = refs
    i = pl.program_id(0)
    k = pl.program_id(1)

    if nk > 1:
        @pl.when(k == 0)
        def _():
            o_ref[...] = _dot(a_ref[...], w_ref[...])

    if nk > 2:
        @pl.when((k > 0) & (k < nk - 1))
        def _():
            o_ref[...] += _dot(a_ref[...], w_ref[...])

    @pl.when(k == nk - 1)
    def _():
        seq = i // tiles_per_seq
        gm = _mod_rows(gm_ref, seq_rows, seq)
        if emit_h2:
            sh = _mod_rows(sh_ref, seq_rows, seq)
            sc = _mod_rows(sc_ref, seq_rows, seq)
        for r in range(0, o_ref.shape[0], sub):
            rows = slice(r, r + sub)
            acc = _dot(a_ref[rows, :], w_ref[...])
            if nk > 1:
                acc = acc + o_ref[rows, :]
            x1 = _ln(ALPHA * x_ref[rows, :] + gm * acc) * lg_ref[...] + lb_ref[...]
            o_ref[rows, :] = x1
            if emit_h2:
                h_ref[rows, :] = (_ln(x1) * (1.0 + sc) + sh).astype(BF16)


def _res_block(a, w, x, mod, ln_g, ln_b, *, tm, bk, sub, seq_rows, seq_len, gate_col, emit_h2):
    tokens, kdim = a.shape
    nk = kdim // bk
    mod_spec = lambda c: pl.BlockSpec((mod.shape[0], D_MODEL), lambda i, k: (0, c))
    row_spec = pl.BlockSpec((tm, D_MODEL), lambda i, k: (i, 0))
    vec_spec = pl.BlockSpec((1, D_MODEL), lambda i, k: (0, 0))
    in_specs = [pl.BlockSpec((tm, bk), lambda i, k: (i, k)),
                pl.BlockSpec((bk, D_MODEL), lambda i, k: (k, 0)),
                row_spec, mod_spec(gate_col), vec_spec, vec_spec]
    args = [a, w, x, mod, ln_g, ln_b]
    out_specs = [row_spec]
    out_shape = [jax.ShapeDtypeStruct((tokens, D_MODEL), F32)]
    if emit_h2:
        in_specs += [mod_spec(3), mod_spec(4)]
        args += [mod, mod]
        out_specs.append(row_spec)
        out_shape.append(jax.ShapeDtypeStruct((tokens, D_MODEL), BF16))
    kern = functools.partial(_res_kernel, sub=sub, seq_rows=seq_rows, tiles_per_seq=max(seq_len // tm, 1),
                             emit_h2=emit_h2, nk=nk)
    return pl.pallas_call(
        kern,
        grid=(tokens // tm, nk),
        in_specs=in_specs,
        out_specs=out_specs,
        out_shape=out_shape,
        compiler_params=pltpu.CompilerParams(
            dimension_semantics=("arbitrary", "arbitrary"), vmem_limit_bytes=VMEM_LIMIT),
        name="res_block",
    )(*args)


def _up_kernel(*refs, tm, sub, seq_len, n_seq, t_major, tiles_per_seq):
    if t_major:
        h_ref, wa_scr, wu_scr, cw_ref, cb_ref, cache_ref, g_ref, tail_ref, abuf = refs
    else:
        h_ref, wa_ref, ulo_ref, uhi_ref, cw_ref, cb_ref, g_ref, tail_ref, wa_scr, wu_scr, abuf = refs
    j = pl.program_id(0)
    i = pl.program_id(1)
    bn = wa_scr.shape[1]
    valid = (j * bn + lax.broadcasted_iota(jnp.int32, (1, bn), 1)) < D_FF
    hist = CONV_W - 1 if not t_major else (CONV_W - 1) * n_seq
    base = 8 if not t_major else hist

    @pl.when(i == 0)
    def _():
        if t_major:
            abuf[0:hist, :] = jnp.where(valid, cache_ref[...].reshape(hist, bn), 0.0)
        else:
            split = D_FF % bn
            wu = jnp.concatenate([ulo_ref[:, split:], uhi_ref[:, :split]], axis=1)
            wa_scr[...] = jnp.where(valid, wa_ref[...], 0.0).astype(BF16)
            wu_scr[...] = jnp.where(valid, wu, 0.0).astype(BF16)
            abuf[0:base, :] = jnp.zeros((base, bn), F32)

    if not t_major:
        @pl.when(i > 0)
        def _():
            abuf[base - hist:base, :] = abuf[tm + base - hist:tm + base, :]

    step = n_seq if t_major else 1
    for r in range(0, tm, sub):
        h = h_ref[r:r + sub, :]
        a = _dot(h, wa_scr[...])
        u = _dot(h, wu_scr[...])
        abuf[base + r:base + r + sub, :] = a
        a1 = abuf[base + r - step:base + r - step + sub, :]
        a2 = abuf[base + r - 2 * step:base + r - 2 * step + sub, :]
        if not t_major:
            t = (i * tm + r + lax.broadcasted_iota(jnp.int32, (sub, 1), 0)) & (seq_len - 1)
            a1 = jnp.where(t >= 1, a1, 0.0)
            a2 = jnp.where(t >= 2, a2, 0.0)
        conv = cb_ref[...] + cw_ref[0:1, :] * a2 + cw_ref[1:2, :] * a1 + cw_ref[2:3, :] * a
        gl = 0.5 * conv * (1.0 + lax.erf(conv * (2.0 ** -0.5)))
        g_ref[r:r + sub, :] = (gl * u).astype(BF16)

    if t_major:
        tail_ref[...] = abuf[base + tm - hist:base + tm, :].reshape(tail_ref.shape)
    else:
        @pl.when((i + 1) % tiles_per_seq == 0)
        def _():
            tail_ref[0] = abuf[tm + base - hist:tm + base, :]


def _up_proj(h2, w_up, conv_w_p, conv_b_p, *, tm, sub, seq_len, n_seq, cache=None):
    tokens = h2.shape[0]
    t_major = cache is not None
    bn = FF_BLOCK
    nj = D_FF_PAD // bn
    u_lo = D_FF // bn
    u_last = (2 * D_FF - 1) // bn
    tiles_per_seq = max(seq_len // tm, 1)
    wspec = lambda f: pl.BlockSpec((D_MODEL, bn), f)
    col_spec = wspec(lambda j, i: (0, j))
    h_spec = pl.BlockSpec((tm, D_MODEL), lambda j, i: (i, 0))
    conv_specs = [pl.BlockSpec((CONV_W, bn), lambda j, i: (0, j)), pl.BlockSpec((1, bn), lambda j, i: (0, j))]
    out_specs = [pl.BlockSpec((tm, bn), lambda j, i: (i, j))]
    out_shape = [jax.ShapeDtypeStruct((tokens, D_FF_PAD), BF16)]
    if t_major:
        hist_spec = pl.BlockSpec((CONV_W - 1, n_seq, bn), lambda j, i: (0, 0, j))
        in_specs = [h_spec, col_spec, col_spec] + conv_specs + [hist_spec]
        args = [h2, w_up[0], w_up[1], conv_w_p, conv_b_p, cache]
        out_specs.append(hist_spec)
        out_shape.append(jax.ShapeDtypeStruct((CONV_W - 1, n_seq, D_FF_PAD), F32))
        abuf_rows = (CONV_W - 1) * n_seq + tm
    else:
        in_specs = [h_spec, col_spec,
                    wspec(lambda j, i: (0, u_lo + j)),
                    wspec(lambda j, i: (0, jnp.minimum(u_lo + 1 + j, u_last)))] + conv_specs
        args = [h2, w_up, w_up, w_up, conv_w_p, conv_b_p]
        out_specs += [pl.BlockSpec((1, CONV_W - 1, bn), lambda j, i: (i // tiles_per_seq, 0, j)),
                      col_spec, col_spec]
        out_shape += [jax.ShapeDtypeStruct((n_seq, CONV_W - 1, D_FF_PAD), F32),
                      jax.ShapeDtypeStruct((D_MODEL, D_FF_PAD), BF16),
                      jax.ShapeDtypeStruct((D_MODEL, D_FF_PAD), BF16)]
        abuf_rows = tm + 8
    kern = functools.partial(_up_kernel, tm=tm, sub=sub, seq_len=seq_len, n_seq=n_seq, t_major=t_major,
                             tiles_per_seq=tiles_per_seq)
    outs = pl.pallas_call(
        kern,
        grid=(nj, tokens // tm),
        in_specs=in_specs,
        out_specs=out_specs,
        out_shape=out_shape,
        scratch_shapes=[pltpu.VMEM((abuf_rows, bn), F32)],
        compiler_params=pltpu.CompilerParams(
            dimension_semantics=("arbitrary", "arbitrary"), vmem_limit_bytes=VMEM_LIMIT),
        name="up_proj",
    )(*args)
    if t_major:
        return outs[0], outs[1], w_up
    return outs[0], outs[1], (outs[2], outs[3])


def _prep_down_kernel(w_ref, o_ref):
    bk = w_ref.shape[0]
    rows = pl.program_id(0) * bk + lax.broadcasted_iota(jnp.int32, (bk, 1), 0)
    o_ref[...] = jnp.where(rows < D_FF, w_ref[...], 0.0).astype(BF16)


def _prep_w_down(w_down):
    bk = D_FF_PAD // 4
    return pl.pallas_call(
        _prep_down_kernel,
        grid=(D_FF_PAD // bk,),
        in_specs=[pl.BlockSpec((bk, D_MODEL), lambda k: (k, 0))],
        out_specs=pl.BlockSpec((bk, D_MODEL), lambda k: (k, 0)),
        out_shape=jax.ShapeDtypeStruct((D_FF_PAD, D_MODEL), BF16),
        compiler_params=pltpu.CompilerParams(vmem_limit_bytes=VMEM_LIMIT),
        name="prep_w_down",
    )(w_down)


def _path(x, mod, w, *, n_seq, seq_len, tm_in, tm_res, tm_up, tm_down, states=None, cache=None):
    sample = states is not None
    sub = n_seq if sample else 256
    z, gates = _in_proj(x, mod, w["w_main"], w["w_gate"], tm=tm_in, sub=sub, seq_rows=sample,
                        seq_len=seq_len, z_dtype=F32 if sample else BF16)
    if sample:
        z = z.reshape(seq_len, n_seq, Z_MAIN)
        gates = gates.reshape(seq_len, n_seq, LANE)
    mix, c1, n1, m1, s1 = _mixer(z, gates, w["gbias"], w["lb_logits"], w["norm_a"], w["norm_b"],
                                 n_seq=n_seq, seq_len=seq_len, states=states)
    mix = mix.reshape(n_seq * seq_len, D_MODEL)
    x1, h2 = _res_block(mix, w["w_out"], x, mod, w["ln1_g"], w["ln1_b"], tm=tm_res, bk=D_MODEL, sub=sub,
                        seq_rows=sample, seq_len=seq_len, gate_col=2, emit_h2=True)
    g, conv_out, w_up_bf = _up_proj(h2, w["w_up"], w["conv_w"], w["conv_b"], tm=tm_up, sub=sub,
                                    seq_len=seq_len, n_seq=n_seq, cache=cache)
    (x2,) = _res_block(g, w["w_down"], x1, mod, w["ln2_g"], w["ln2_b"], tm=tm_down, bk=D_FF_PAD // 4, sub=sub,
                       seq_rows=sample, seq_len=seq_len, gate_col=5, emit_h2=False)
    return x2, (c1, n1, m1, s1), conv_out, w_up_bf


def kernel(x_prompt, x_sample, state_mlstm_C, state_mlstm_n, state_mlstm_m, state_hgrn_S, cache_ffn_conv,
           c_prompt, c_sample, hgrn_lb_logits, w_ada, b_ada, w_in, b_gate_a, norm_a, norm_b, w_out,
           ln1_g, ln1_b, w_up, conv_w, conv_b, w_down, ln2_g, ln2_b):
    bp, tp, _ = x_prompt.shape
    bs, ts, _ = x_sample.shape
    pad_ff = D_FF_PAD - D_FF
    w_main, w_gate = _prep_w_in(jnp.transpose(w_in[0]))
    weights = dict(
        w_main=w_main, w_gate=w_gate, w_out=w_out[0].astype(BF16), w_up=w_up[0],
        w_down=_prep_w_down(w_down[0]),
        gbias=jnp.pad(b_gate_a[0].reshape(1, N_GATE), ((0, 0), (0, LANE - N_GATE))),
        lb_logits=hgrn_lb_logits, norm_a=norm_a, norm_b=norm_b,
        ln1_g=ln1_g, ln1_b=ln1_b, ln2_g=ln2_g, ln2_b=ln2_b,
        conv_w=jnp.pad(conv_w[0], ((0, 0), (0, pad_ff))),
        conv_b=jnp.pad(conv_b, ((0, 0), (0, pad_ff))))

    c_p = jnp.pad(c_prompt, ((0, 8 - bp), (0, 0)))
    mod_p, mod_s = _ada(c_p, c_sample, w_ada[0], b_ada)

    yp, st_p, tail_p, w_up_bf = _path(x_prompt.reshape(bp * tp, D_MODEL), mod_p, weights, n_seq=bp,
                                      seq_len=tp, tm_in=1024, tm_res=512, tm_up=1024, tm_down=1024)

    xs_t = jnp.swapaxes(x_sample, 0, 1).reshape(ts * bs, D_MODEL)
    cache_t = jnp.swapaxes(cache_ffn_conv[0], 0, 1)
    ys_t, st_s, tail_s, _ = _path(xs_t, mod_s, dict(weights, w_up=w_up_bf), n_seq=bs, seq_len=ts,
                                  tm_in=ts * bs, tm_res=ts * bs, tm_up=ts * bs, tm_down=ts * bs,
                                  states=(state_mlstm_C, state_mlstm_n, state_mlstm_m, state_hgrn_S),
                                  cache=cache_t)

    ys = jnp.swapaxes(ys_t.reshape(ts, bs, D_MODEL), 0, 1)
    conv_p = tail_p[None, :, :, :D_FF]
    conv_s = jnp.swapaxes(tail_s[:, :, :D_FF], 0, 1)[None]
    return (yp.reshape(bp, tp, D_MODEL), ys,
            st_p[0], st_p[1], st_p[2], st_p[3], conv_p,
            st_s[0], st_s[1], st_s[2], st_s[3], conv_s)
```

```python
import functools

import numpy as np
import jax
import jax.numpy as jnp
from jax import lax
from jax.experimental import pallas as pl
from jax.experimental.pallas import tpu as pltpu

F32 = jnp.float32
BF16 = jnp.bfloat16

D_MODEL = 2048
N_HEADS_A, DK_A, DV_A = 4, 128, 256
N_HEADS_B, DK_B, DV_B = 8, 128, 128
A_WIDTH = N_HEADS_A * DV_A
B_WIDTH = N_HEADS_B * DV_B
N_GATE = 2 * N_HEADS_A
GATE_COL = 2 * N_HEADS_A * DK_A + A_WIDTH
D_IN = GATE_COL + N_GATE + A_WIDTH + 4 * B_WIDTH
D_FF = 5504
CONV_W = 3
EPS = 1e-5
ALPHA = 2.0 ** 0.25
LOG2_E = 1.4426950408889634
LANE = 128
FF_BLOCK = 512
D_FF_PAD = 5632
Z_MAIN = D_IN - N_GATE
OFF_QA, OFF_KA, OFF_VA, OFF_OA = 0, 512, 1024, 2048
OFF_FB, OFF_QB, OFF_VB, OFF_GB = 3072, 4096, 5120, 6144
VMEM_LIMIT = 56 * 1024 * 1024
PROMPT_CHUNK = 128
SAMPLE_GROUP = 8


def _ln(x):
    mu = jnp.mean(x, axis=-1, keepdims=True)
    xc = x - mu
    var = jnp.mean(xc * xc, axis=-1, keepdims=True)
    return xc * lax.rsqrt(var + EPS)


def _sigmoid(x):
    return 1.0 / (1.0 + jnp.exp(-x))


def _dot(a, b):
    return jnp.dot(a.astype(BF16), b.astype(BF16), preferred_element_type=F32)


def _dot_nt(a, b):
    return lax.dot_general(a.astype(BF16), b.astype(BF16), (((1,), (1,)), ((), ())),
                           preferred_element_type=F32)


def _dot_tn(a, b):
    return lax.dot_general(a.astype(BF16), b.astype(BF16), (((0,), (0,)), ((), ())),
                           preferred_element_type=F32)


def _mod_rows(ref, seq_rows, seq):
    if seq_rows:
        return ref[...]
    return ref[pl.ds(seq, 1), :]


def _ada_kernel(cp_ref, cs_ref, w_ref, b_ref, op_ref, os_ref):
    w = w_ref[...].astype(BF16)
    for c_ref, o_ref in ((cp_ref, op_ref), (cs_ref, os_ref)):
        c = c_ref[...]
        o_ref[...] = _dot(c * _sigmoid(c), w) + b_ref[...]


def _ada(c_p, c_s, w_ada, b_ada):
    n = w_ada.shape[1]
    bn = 1024
    row = lambda r: pl.BlockSpec((r, D_MODEL), lambda j: (0, 0))
    out = lambda r: pl.BlockSpec((r, bn), lambda j: (0, j))
    return pl.pallas_call(
        _ada_kernel,
        grid=(n // bn,),
        in_specs=[row(c_p.shape[0]), row(c_s.shape[0]),
                  pl.BlockSpec((D_MODEL, bn), lambda j: (0, j)),
                  pl.BlockSpec((1, bn), lambda j: (0, j))],
        out_specs=[out(c_p.shape[0]), out(c_s.shape[0])],
        out_shape=[jax.ShapeDtypeStruct((c_p.shape[0], n), F32),
                   jax.ShapeDtypeStruct((c_s.shape[0], n), F32)],
        compiler_params=pltpu.CompilerParams(vmem_limit_bytes=VMEM_LIMIT),
        name="ada",
    )(c_p, c_s, w_ada, b_ada)


def _prep_in_kernel(front_ref, back_ref, gate_ref, o_ref, og_ref, *, first_shifted):
    j = pl.program_id(0)

    @pl.when(j == 0)
    def _():
        og_ref[...] = gate_ref[...].T.astype(BF16)

    def emit(src_ref):
        for r in range(0, src_ref.shape[0], 256):
            o_ref[:, r:r + 256] = src_ref[r:r + 256, :].T.astype(BF16)

    @pl.when(j < first_shifted)
    def _():
        emit(front_ref)

    @pl.when(j >= first_shifted)
    def _():
        emit(back_ref)


def _prep_w_in(w_in_t):
    bn = 1024
    first_shifted = GATE_COL // bn
    kern = functools.partial(_prep_in_kernel, first_shifted=first_shifted)
    return pl.pallas_call(
        kern,
        grid=(Z_MAIN // bn,),
        in_specs=[pl.BlockSpec((bn, D_MODEL), lambda j: (jnp.minimum(j, first_shifted - 1), 0)),
                  pl.BlockSpec((pl.Element(bn), pl.Element(D_MODEL)),
                               lambda j: (pl.multiple_of(jnp.maximum(j, first_shifted) * bn + N_GATE, N_GATE), 0)),
                  pl.BlockSpec((LANE, D_MODEL), lambda j: (GATE_COL // LANE, 0))],
        out_specs=[pl.BlockSpec((D_MODEL, bn), lambda j: (0, j)),
                   pl.BlockSpec((D_MODEL, LANE), lambda j: (0, 0))],
        out_shape=[jax.ShapeDtypeStruct((D_MODEL, Z_MAIN), BF16),
                   jax.ShapeDtypeStruct((D_MODEL, LANE), BF16)],
        compiler_params=pltpu.CompilerParams(vmem_limit_bytes=VMEM_LIMIT),
        name="prep_w_in",
    )(w_in_t, w_in_t, w_in_t)


def _in_kernel(x_ref, sh_ref, sc_ref, w_ref, wg_ref, z_ref, g_ref, h_scr, *, sub, seq_rows, tiles_per_seq):
    i = pl.program_id(0)
    j = pl.program_id(1)

    @pl.when(j == 0)
    def _():
        seq = i // tiles_per_seq
        sh = _mod_rows(sh_ref, seq_rows, seq)
        sc = _mod_rows(sc_ref, seq_rows, seq)
        for r in range(0, x_ref.shape[0], sub):
            rows = slice(r, r + sub)
            h = (_ln(x_ref[rows, :]) * (1.0 + sc) + sh).astype(BF16)
            h_scr[rows, :] = h
            g_ref[rows, :] = _dot(h, wg_ref[...])
            z_ref[rows, :] = _dot(h, w_ref[...]).astype(z_ref.dtype)

    @pl.when(j > 0)
    def _():
        z_ref[...] = _dot(h_scr[...], w_ref[...]).astype(z_ref.dtype)


def _in_proj(x, mod, w_main, w_gate, *, tm, sub, seq_rows, seq_len, z_dtype):
    tokens = x.shape[0]
    bn = 1024
    mod_spec = lambda k: pl.BlockSpec((mod.shape[0], D_MODEL), lambda i, j: (0, k))
    kern = functools.partial(_in_kernel, sub=min(sub, tm), seq_rows=seq_rows, tiles_per_seq=max(seq_len // tm, 1))
    return pl.pallas_call(
        kern,
        grid=(tokens // tm, Z_MAIN // bn),
        in_specs=[pl.BlockSpec((tm, D_MODEL), lambda i, j: (i, 0)),
                  mod_spec(0), mod_spec(1),
                  pl.BlockSpec((D_MODEL, bn), lambda i, j: (0, j)),
                  pl.BlockSpec((D_MODEL, LANE), lambda i, j: (0, 0))],
        out_specs=[pl.BlockSpec((tm, bn), lambda i, j: (i, j)),
                   pl.BlockSpec((tm, LANE), lambda i, j: (i, 0))],
        out_shape=[jax.ShapeDtypeStruct((tokens, Z_MAIN), z_dtype),
                   jax.ShapeDtypeStruct((tokens, LANE), F32)],
        scratch_shapes=[pltpu.VMEM((tm, D_MODEL), BF16)],
        compiler_params=pltpu.CompilerParams(
            dimension_semantics=("arbitrary", "arbitrary"), vmem_limit_bytes=VMEM_LIMIT),
        name="in_proj",
    )(x, mod, mod, w_main, w_gate)


def _mix_consts(L, Tg, t_major):
    G = L // Tg
    r = np.arange(L)
    seq, tim = (r % G, r // G) if t_major else (r // Tg, r % Tg)
    same = seq[:, None] == seq[None, :]
    t = tim[:, None]
    u = tim[None, :]
    mats = [same & (u <= t), same & (u > t)]
    masks = [same & (u <= t), same & (t <= u), same]
    levels = []
    m = Tg // 2
    while m >= 1:
        levels.append(m)
        m //= 2
    for m in levels:
        blk = same & (u // m == t // m)
        odd = (t // m) % 2 == 1
        mats.append(np.where(odd, blk & (u <= t), blk & (u > t)))
        masks.append(same & odd & ((u // m) % 2 == 0) & (t // (2 * m) == u // (2 * m)))
    mats = np.concatenate(mats, axis=0).astype(np.float32)
    masks = np.stack(masks).astype(np.float32)
    last_rows = tuple(int(np.nonzero((seq == g) & (tim == Tg - 1))[0][0]) for g in range(G))
    return mats, masks, tuple(levels), last_rows


def _mix_kernel(z_ref, g_ref, gbias_ref, lbl_ref, na_ref, nb_ref, mall_ref, msk_ref,
                c_in, n_in, m_in, s_in, mix_ref, c_out, n_out, m_out, s_out, e_scr,
                *, L, Tg, levels, last_rows):
    G = L // Tg

    def zs(off, w):
        return z_ref[:, :, off:off + w].reshape(L, w)

    def put_mix(off, w, val):
        mix_ref[:, :, off:off + w] = val.astype(mix_ref.dtype).reshape(Tg, G, w)

    _mix_body(zs, put_mix, g_ref[...].reshape(L, LANE), gbias_ref, lbl_ref, na_ref, nb_ref, mall_ref, msk_ref,
              c_in, n_in, m_in, s_in, c_out, n_out, m_out, s_out, e_scr, pl.program_id(0) * G,
              L=L, Tg=Tg, levels=levels, last_rows=last_rows, t_major=True)


def _mixchunk_kernel(z_ref, g_ref, gbias_ref, lbl_ref, na_ref, nb_ref, mall_ref, msk_ref,
                     mix_ref, c_out, n_out, m_out, s_out, e_scr, *, L, levels, last_rows):
    b = pl.program_id(0)

    @pl.when(pl.program_id(1) == 0)
    def _():
        c_out[...] = jnp.zeros(c_out.shape, F32)
        n_out[...] = jnp.zeros(n_out.shape, F32)
        s_out[...] = jnp.zeros(s_out.shape, F32)
        m_out[0, pl.ds(b, 1), :] = jnp.zeros((1, N_HEADS_A), F32)

    def zs(off, w):
        return z_ref[:, off:off + w]

    def put_mix(off, w, val):
        mix_ref[:, off:off + w] = val.astype(mix_ref.dtype)

    _mix_body(zs, put_mix, g_ref[...], gbias_ref, lbl_ref, na_ref, nb_ref, mall_ref, msk_ref,
              c_out, n_out, m_out, s_out, c_out, n_out, m_out, s_out, e_scr, b,
              L=L, Tg=L, levels=levels, last_rows=last_rows, t_major=False)


def _mix_body(zs, put_mix, gates, gbias_ref, lbl_ref, na_ref, nb_ref, mall_ref, msk_ref,
              c_in, n_in, m_in, s_in, c_out, n_out, m_out, s_out, e_scr, m_row0,
              *, L, Tg, levels, last_rows, t_major):
    G = L // Tg
    neg_inf = F32(-jnp.inf)
    causal = msk_ref[0] > 0.5
    causal_t = msk_ref[1] > 0.5
    same = msk_ref[2] > 0.5
    row1 = lax.broadcasted_iota(jnp.int32, (L, 1), 0)
    seq_of_row = (row1 & (G - 1)) if t_major else (row1 // Tg)
    in_group = [seq_of_row == g for g in range(G)]

    def by_group(vals):
        if G == 1:
            return vals[0]
        out = jnp.where(in_group[0], vals[0], 0.0)
        for g in range(1, G):
            out = out + jnp.where(in_group[g], vals[g], 0.0)
        return out

    pre = gates + gbias_ref[...]
    lsig = jnp.minimum(pre, 0.0) - jnp.log(1.0 + jnp.exp(-jnp.abs(pre)))
    pre_t = pre.T
    lsig_t = lsig.T
    m_prev = m_in[0, pl.ds(m_row0, G), :]
    scale = DK_A ** -0.5
    m_new_rows = []
    for h in range(N_HEADS_A):
        q = zs(OFF_QA + DK_A * h, DK_A)
        k = zs(OFF_KA + DK_A * h, DK_A)
        v = zs(OFF_VA + DV_A * h, DV_A)
        logi_c = pre[:, h:h + 1]
        logi_r = pre_t[h:h + 1, :]
        lf_c = lsig[:, N_HEADS_A + h:N_HEADS_A + h + 1]
        lf_r = lsig_t[N_HEADS_A + h:N_HEADS_A + h + 1, :]
        b_c = jnp.sum(jnp.where(causal, lf_r, 0.0), axis=1, keepdims=True)
        b_r = jnp.sum(jnp.where(causal_t, lf_c, 0.0), axis=0, keepdims=True)
        tot_c = jnp.sum(jnp.where(same, lf_r, 0.0), axis=1, keepdims=True)
        tot_r = jnp.sum(jnp.where(same, lf_c, 0.0), axis=0, keepdims=True)
        m_col = by_group([m_prev[g:g + 1, h:h + 1] for g in range(G)])
        dmat = jnp.where(causal, b_c - b_r + logi_r, neg_inf)
        inter = b_c + m_col
        m_t = jnp.maximum(inter, jnp.max(dmat, axis=1, keepdims=True))
        w_inter = jnp.exp(inter - m_t)
        smat = _dot_nt(q, k) * (scale * jnp.exp(dmat - m_t))
        q_c = by_group([_dot(q, c_in[0, g, h]) for g in range(G)]) * scale
        n_rows = by_group([n_in[0, g, h:h + 1, :] for g in range(G)])
        q_n = jnp.sum(q.astype(F32) * n_rows, axis=1, keepdims=True) * scale
        num = w_inter * q_c + _dot(smat, v)
        den = w_inter * q_n + jnp.sum(smat, axis=1, keepdims=True)
        hh = num / jnp.maximum(jnp.abs(den), jnp.exp(-m_t))
        ms = jnp.mean(hh * hh, axis=1, keepdims=True)
        oa = zs(OFF_OA + DV_A * h, DV_A).astype(F32)
        ya = hh * lax.rsqrt(ms + EPS) * na_ref[:, DV_A * h:DV_A * (h + 1)] * _sigmoid(oa)
        put_mix(DV_A * h, DV_A, ya)
        dec_c = tot_c - b_c + logi_c
        dec_r = tot_r - b_r + logi_r
        m_new_c = jnp.maximum(tot_c + m_col,
                              jnp.max(jnp.where(same, dec_r, neg_inf), axis=1, keepdims=True))
        wk = jnp.exp(dec_c - m_new_c)
        sc = jnp.exp(tot_c + m_col - m_new_c)
        kw = k.astype(F32) * wk
        m_new_h = []
        for g in range(G):
            last = last_rows[g]
            kg = kw if G == 1 else jnp.where(in_group[g], kw, 0.0)
            sc_g = sc[last:last + 1, :]
            c_new = sc_g * c_in[0, g, h] + _dot_tn(kg, v)
            n_new = sc_g * n_in[0, g, h:h + 1, :] + jnp.sum(kg, axis=0, keepdims=True)
            c_out[0, g, h] = c_new
            n_out[0, g, h:h + 1, :] = n_new
            m_new_h.append(m_new_c[last:last + 1, :])
        m_new_rows.append(m_new_h)
    for g in range(G):
        m_out[0, pl.ds(m_row0 + g, 1), :] = jnp.concatenate(
            [m_new_rows[h][g] for h in range(N_HEADS_A)], axis=1)

    l0 = lbl_ref[0:1, :]
    l1 = lbl_ref[1:2, :]
    lmax = jnp.maximum(l0, l1)
    e0 = jnp.exp(l0 - lmax)
    e1 = jnp.exp(l1 - lmax)
    lb = e0 / (e0 + e1)
    fb = zs(OFF_FB, B_WIDTH).astype(F32)
    e = jnp.exp(-jnp.abs(fb))
    r = 1.0 / (1.0 + e)
    pos = fb >= 0.0
    sig = jnp.where(pos, r, e * r)
    nsig = jnp.where(pos, e * r, r)
    logf = jnp.log(lb + (1.0 - lb) * sig) * LOG2_E
    kb = (1.0 - lb) * nsig
    hi = logf.astype(BF16)
    r1 = logf - hi.astype(F32)
    mid = r1.astype(BF16)
    lo = (r1 - mid.astype(F32)).astype(BF16)
    mall = mall_ref[...]
    if L % 16 == 0:
        e_scr[...] = _dot(mall, jnp.concatenate([hi, mid, lo], axis=0))
    else:
        mall = mall[:, :L]
        e_scr[...] = _dot(mall, hi) + _dot(mall, mid) + _dot(mall, lo)

    dec_t = []
    for g in range(G):
        last = last_rows[g]
        bl = e_scr[last:last + 1, :]
        bl8 = jnp.concatenate([bl[:, DK_B * h:DK_B * (h + 1)] for h in range(N_HEADS_B)], axis=0)
        dec_t.append(jnp.exp2(bl8).T)

    for h in range(N_HEADS_B):
        sl = slice(DK_B * h, DK_B * (h + 1))
        q = zs(OFF_QB + DK_B * h, DK_B).astype(BF16)
        k = kb[:, sl]
        kbf = k.astype(BF16)
        v = zs(OFF_VB + DV_B * h, DV_B)
        amat = jnp.zeros((L, L), F32)
        for li in range(len(levels)):
            ex = jnp.exp2(e_scr[(2 + li) * L:(3 + li) * L, sl]).astype(BF16)
            amat = jnp.where(msk_ref[3 + li] > 0.5, _dot_nt(q * ex, kbf * ex), amat)
        qf = q.astype(F32)
        diag = jnp.sum(qf * k, axis=1, keepdims=True)
        qs = qf * jnp.exp2(e_scr[0:L, sl])
        o_inter = by_group([_dot(qs, s_in[0, g, h]) for g in range(G)])
        o = _dot(amat, v) + diag * v.astype(F32) + o_inter
        ms = jnp.mean(o * o, axis=1, keepdims=True)
        gb = zs(OFF_GB + DV_B * h, DV_B).astype(F32)
        yb = o * lax.rsqrt(ms + EPS) * nb_ref[:, sl] * (gb * _sigmoid(gb))
        put_mix(A_WIDTH + DV_B * h, DV_B, yb)
        kw = k * jnp.exp2(e_scr[L:2 * L, sl])
        for g in range(G):
            kg = kw if G == 1 else jnp.where(in_group[g], kw, 0.0)
            s_out[0, g, h] = dec_t[g][:, h:h + 1] * s_in[0, g, h] + _dot_tn(kg, v)


def _mixer_args(L, Tg, t_major, const2, const3, gbias, lb_logits, norm_a, norm_b):
    mats, masks, levels, last_rows = _mix_consts(L, Tg, t_major)
    n_mats = mats.shape[0] // L
    if L % 16 == 0:
        mats = np.concatenate([mats, mats, mats], axis=1)
    specs = [pl.BlockSpec((1, LANE), const2),
             pl.BlockSpec((2, B_WIDTH), const2),
             pl.BlockSpec((1, A_WIDTH), const2),
             pl.BlockSpec((1, B_WIDTH), const2),
             pl.BlockSpec(mats.shape, const2),
             pl.BlockSpec(masks.shape, const3)]
    args = [gbias, lb_logits, norm_a, norm_b, jnp.asarray(mats, BF16), jnp.asarray(masks)]
    return specs, args, levels, last_rows, n_mats


def _state_shapes(n_seq):
    return [jax.ShapeDtypeStruct((1, n_seq, N_HEADS_A, DK_A, DV_A), F32),
            jax.ShapeDtypeStruct((1, n_seq, N_HEADS_A, DK_A), F32),
            jax.ShapeDtypeStruct((1, n_seq, N_HEADS_A), F32),
            jax.ShapeDtypeStruct((1, n_seq, N_HEADS_B, DK_B, DV_B), F32)]


def _state_specs(G, n_seq, seq_of_step):
    return [pl.BlockSpec((1, G, N_HEADS_A, DK_A, DV_A), lambda *ids: (0, seq_of_step(*ids), 0, 0, 0)),
            pl.BlockSpec((1, G, N_HEADS_A, DK_A), lambda *ids: (0, seq_of_step(*ids), 0, 0)),
            pl.BlockSpec((1, n_seq, N_HEADS_A), lambda *ids: (0, 0, 0)),
            pl.BlockSpec((1, G, N_HEADS_B, DK_B, DV_B), lambda *ids: (0, seq_of_step(*ids), 0, 0, 0))]


def _mixer(z, gates, gbias, lb_logits, norm_a, norm_b, *, n_seq, seq_len, states):
    Tg, G = seq_len, SAMPLE_GROUP
    L = Tg * G
    tile = lambda w: pl.BlockSpec((Tg, G, w), lambda i: (0, i, 0))
    const_specs, const_args, levels, last_rows, n_mats = _mixer_args(
        L, Tg, True, lambda i: (0, 0), lambda i: (0, 0, 0), gbias, lb_logits, norm_a, norm_b)
    st_specs = _state_specs(G, n_seq, lambda i: i)
    kern = functools.partial(_mix_kernel, L=L, Tg=Tg, levels=levels, last_rows=last_rows)
    return pl.pallas_call(
        kern,
        grid=(n_seq // G,),
        in_specs=[tile(Z_MAIN), tile(LANE)] + const_specs + st_specs,
        out_specs=[tile(D_MODEL)] + st_specs,
        out_shape=[jax.ShapeDtypeStruct((Tg, n_seq, D_MODEL), F32)] + _state_shapes(n_seq),
        scratch_shapes=[pltpu.VMEM((n_mats * L, B_WIDTH), F32)],
        compiler_params=pltpu.CompilerParams(dimension_semantics=("arbitrary",), vmem_limit_bytes=VMEM_LIMIT),
        name="mixer_state",
    )(z, gates, *const_args, *states)


def _mixer_chunk(z, gates, gbias, lb_logits, norm_a, norm_b, *, n_seq, seq_len):
    L = PROMPT_CHUNK
    nchunks = seq_len // L
    tile = lambda w: pl.BlockSpec((L, w), lambda b, c: (b * nchunks + c, 0))
    const_specs, const_args, levels, last_rows, n_mats = _mixer_args(
        L, L, False, lambda b, c: (0, 0), lambda b, c: (0, 0, 0), gbias, lb_logits, norm_a, norm_b)
    st_specs = _state_specs(1, n_seq, lambda b, c: b)
    kern = functools.partial(_mixchunk_kernel, L=L, levels=levels, last_rows=last_rows)
    return pl.pallas_call(
        kern,
        grid=(n_seq, nchunks),
        in_specs=[tile(Z_MAIN), tile(LANE)] + const_specs,
        out_specs=[tile(D_MODEL)] + st_specs,
        out_shape=[jax.ShapeDtypeStruct((n_seq * seq_len, D_MODEL), BF16)] + _state_shapes(n_seq),
        scratch_shapes=[pltpu.VMEM((n_mats * L, B_WIDTH), F32)],
        compiler_params=pltpu.CompilerParams(dimension_semantics=("arbitrary", "arbitrary"),
                                             vmem_limit_bytes=VMEM_LIMIT),
        name="mixer_chunk",
    )(z, gates, *const_args)


def _res_kernel(*refs, sub, seq_rows, tiles_per_seq, emit_h2, nk):
    if emit_h2:
        a_ref, w_ref, x_ref, gm_ref, lg_ref, lb_ref, sh_ref, sc_ref, o_ref, h_ref = refs
    else:
        a_ref, w_ref, x_ref, gm_ref, lg_ref, lb_ref, o_ref = refs
    i = pl.program_id(0)
    k = pl.program_id(1)

    if nk > 1:
        @pl.when(k == 0)
        def _():
            o_ref[...] = _dot(a_ref[...], w_ref[...])

    if nk > 2:
        @pl.when((k > 0) & (k < nk - 1))
        def _():
            o_ref[...] += _dot(a_ref[...], w_ref[...])

    @pl.when(k == nk - 1)
    def _():
        seq = i // tiles_per_seq
        gm = _mod_rows(gm_ref, seq_rows, seq)
        if emit_h2:
            sh = _mod_rows(sh_ref, seq_rows, seq)
            sc = _mod_rows(sc_ref, seq_rows, seq)
        for r in range(0, o_ref.shape[0], sub):
            rows = slice(r, r + sub)
            acc = _dot(a_ref[rows, :], w_ref[...])
            if nk > 1:
                acc = acc + o_ref[rows, :]
            x1 = _ln(ALPHA * x_ref[rows, :] + gm * acc) * lg_ref[...] + lb_ref[...]
            o_ref[rows, :] = x1
            if emit_h2:
                h_ref[rows, :] = (_ln(x1) * (1.0 + sc) + sh).astype(BF16)


def _res_block(a, w, x, mod, ln_g, ln_b, *, tm, bk, sub, seq_rows, seq_len, gate_col, emit_h2):
    tokens, kdim = a.shape
    nk = kdim // bk
    mod_spec = lambda c: pl.BlockSpec((mod.shape[0], D_MODEL), lambda i, k: (0, c))
    row_spec = pl.BlockSpec((tm, D_MODEL), lambda i, k: (i, 0))
    vec_spec = pl.BlockSpec((1, D_MODEL), lambda i, k: (0, 0))
    in_specs = [pl.BlockSpec((tm, bk), lambda i, k: (i, k)),
                pl.BlockSpec((bk, D_MODEL), lambda i, k: (k, 0)),
                row_spec, mod_spec(gate_col), vec_spec, vec_spec]
    args = [a, w, x, mod, ln_g, ln_b]
    out_specs = [row_spec]
    out_shape = [jax.ShapeDtypeStruct((tokens, D_MODEL), F32)]
    if emit_h2:
        in_specs += [mod_spec(3), mod_spec(4)]
        args += [mod, mod]
        out_specs.append(row_spec)
        out_shape.append(jax.ShapeDtypeStruct((tokens, D_MODEL), BF16))
    kern = functools.partial(_res_kernel, sub=sub, seq_rows=seq_rows, tiles_per_seq=max(seq_len // tm, 1),
                             emit_h2=emit_h2, nk=nk)
    return pl.pallas_call(
        kern,
        grid=(tokens // tm, nk),
        in_specs=in_specs,
        out_specs=out_specs,
        out_shape=out_shape,
        compiler_params=pltpu.CompilerParams(
            dimension_semantics=("arbitrary", "arbitrary"), vmem_limit_bytes=VMEM_LIMIT),
        name="res_block",
    )(*args)


def _up_kernel(*refs, tm, sub, seq_len, n_seq, t_major, tiles_per_seq):
    if t_major:
        h_ref, wa_scr, wu_scr, cw_ref, cb_ref, cache_ref, g_ref, tail_ref, abuf = refs
    else:
        (h_ref, wa_ref, ulo_ref, uhi_ref, wd_ref, cw_ref, cb_ref,
         g_ref, tail_ref, wa_scr, wu_scr, wd_out, carry) = refs
    j = pl.program_id(0)
    i = pl.program_id(1)
    bn = wa_scr.shape[1]
    valid = (j * bn + lax.broadcasted_iota(jnp.int32, (1, bn), 1)) < D_FF
    hist = (CONV_W - 1) * n_seq

    @pl.when(i == 0)
    def _():
        if t_major:
            abuf[0:hist, :] = jnp.where(valid, cache_ref[...].reshape(hist, bn), 0.0)
        else:
            split = D_FF % bn
            wu = jnp.concatenate([ulo_ref[:, split:], uhi_ref[:, :split]], axis=1)
            wa_scr[...] = jnp.where(valid, wa_ref[...], 0.0).astype(BF16)
            wu_scr[...] = jnp.where(valid, wu, 0.0).astype(BF16)
            wd_rows = j * bn + lax.broadcasted_iota(jnp.int32, (bn, 1), 0)
            wd_out[...] = jnp.where(wd_rows < D_FF, wd_ref[...], 0.0).astype(BF16)

    if not t_major:
        @pl.when(i % tiles_per_seq == 0)
        def _():
            carry[...] = jnp.zeros(carry.shape, F32)

    row = lax.broadcasted_iota(jnp.int32, (sub, 1), 0)
    for r in range(0, tm, sub):
        h = h_ref[r:r + sub, :]
        a = _dot(h, wa_scr[...])
        u = _dot(h, wu_scr[...])
        if t_major:
            abuf[hist + r:hist + r + sub, :] = a
            a1 = abuf[hist + r - n_seq:hist + r - n_seq + sub, :]
            a2 = abuf[hist + r - 2 * n_seq:hist + r - 2 * n_seq + sub, :]
        else:
            p1 = carry[7:8, :]
            p2 = carry[6:7, :]
            a1 = jnp.where(row == 0, p1, pltpu.roll(a, 1, axis=0))
            a2 = jnp.where(row == 0, p2, jnp.where(row == 1, p1, pltpu.roll(a, 2, axis=0)))
            carry[...] = a[sub - 8:sub, :]
        conv = cb_ref[...] + cw_ref[0:1, :] * a2 + cw_ref[1:2, :] * a1 + cw_ref[2:3, :] * a
        gl = 0.5 * conv * (1.0 + lax.erf(conv * (2.0 ** -0.5)))
        g_ref[r:r + sub, :] = (gl * u).astype(BF16)

    if t_major:
        tail_ref[...] = abuf[tm:tm + hist, :].reshape(tail_ref.shape)
    else:
        @pl.when((i + 1) % tiles_per_seq == 0)
        def _():
            tail_ref[0] = carry[8 - (CONV_W - 1):8, :]


def _up_proj(h2, w_up, w_down, conv_w_p, conv_b_p, *, tm, sub, seq_len, n_seq, cache=None):
    tokens = h2.shape[0]
    t_major = cache is not None
    bn = FF_BLOCK
    nj = D_FF_PAD // bn
    u_lo = D_FF // bn
    u_last = (2 * D_FF - 1) // bn
    tiles_per_seq = max(seq_len // tm, 1)
    wspec = lambda f: pl.BlockSpec((D_MODEL, bn), f)
    col_spec = wspec(lambda j, i: (0, j))
    h_spec = pl.BlockSpec((tm, D_MODEL), lambda j, i: (i, 0))
    conv_specs = [pl.BlockSpec((CONV_W, bn), lambda j, i: (0, j)), pl.BlockSpec((1, bn), lambda j, i: (0, j))]
    out_specs = [pl.BlockSpec((tm, bn), lambda j, i: (i, j))]
    out_shape = [jax.ShapeDtypeStruct((tokens, D_FF_PAD), BF16)]
    if t_major:
        hist_spec = pl.BlockSpec((CONV_W - 1, n_seq, bn), lambda j, i: (0, 0, j))
        in_specs = [h_spec, col_spec, col_spec] + conv_specs + [hist_spec]
        args = [h2, w_up[0], w_up[1], conv_w_p, conv_b_p, cache]
        out_specs.append(hist_spec)
        out_shape.append(jax.ShapeDtypeStruct((CONV_W - 1, n_seq, D_FF_PAD), F32))
        scratch_rows = (CONV_W - 1) * n_seq + tm
    else:
        row_spec = pl.BlockSpec((bn, D_MODEL), lambda j, i: (j, 0))
        in_specs = [h_spec, col_spec,
                    wspec(lambda j, i: (0, u_lo + j)),
                    wspec(lambda j, i: (0, jnp.minimum(u_lo + 1 + j, u_last))),
                    row_spec] + conv_specs
        args = [h2, w_up, w_up, w_up, w_down, conv_w_p, conv_b_p]
        out_specs += [pl.BlockSpec((1, CONV_W - 1, bn), lambda j, i: (i // tiles_per_seq, 0, j)),
                      col_spec, col_spec, row_spec]
        out_shape += [jax.ShapeDtypeStruct((n_seq, CONV_W - 1, D_FF_PAD), F32),
                      jax.ShapeDtypeStruct((D_MODEL, D_FF_PAD), BF16),
                      jax.ShapeDtypeStruct((D_MODEL, D_FF_PAD), BF16),
                      jax.ShapeDtypeStruct((D_FF_PAD, D_MODEL), BF16)]
        scratch_rows = 8
    kern = functools.partial(_up_kernel, tm=tm, sub=sub, seq_len=seq_len, n_seq=n_seq, t_major=t_major,
                             tiles_per_seq=tiles_per_seq)
    outs = pl.pallas_call(
        kern,
        grid=(nj, tokens // tm),
        in_specs=in_specs,
        out_specs=out_specs,
        out_shape=out_shape,
        scratch_shapes=[pltpu.VMEM((scratch_rows, bn), F32)],
        compiler_params=pltpu.CompilerParams(
            dimension_semantics=("arbitrary", "arbitrary"), vmem_limit_bytes=VMEM_LIMIT),
        name="up_proj",
    )(*args)
    if t_major:
        return outs[0], outs[1], w_up, w_down
    return outs[0], outs[1], (outs[2], outs[3]), outs[4]


def _path(x, mod, w, *, n_seq, seq_len, tm_in, tm_res, tm_up, tm_down, states=None, cache=None):
    sample = states is not None
    sub = n_seq if sample else 256
    mixer_w = (w["gbias"], w["lb_logits"], w["norm_a"], w["norm_b"])
    z, gates = _in_proj(x, mod, w["w_main"], w["w_gate"], tm=tm_in, sub=sub, seq_rows=sample,
                        seq_len=seq_len, z_dtype=F32 if sample else BF16)
    if sample:
        mix, c1, n1, m1, s1 = _mixer(z.reshape(seq_len, n_seq, Z_MAIN), gates.reshape(seq_len, n_seq, LANE),
                                     *mixer_w, n_seq=n_seq, seq_len=seq_len, states=states)
        mix = mix.reshape(n_seq * seq_len, D_MODEL)
    else:
        mix, c1, n1, m1, s1 = _mixer_chunk(z, gates, *mixer_w, n_seq=n_seq, seq_len=seq_len)
    x1, h2 = _res_block(mix, w["w_out"], x, mod, w["ln1_g"], w["ln1_b"], tm=tm_res, bk=D_MODEL, sub=sub,
                        seq_rows=sample, seq_len=seq_len, gate_col=2, emit_h2=True)
    g, conv_out, w_up_bf, w_down_bf = _up_proj(h2, w["w_up"], w["w_down"], w["conv_w"], w["conv_b"], tm=tm_up,
                                               sub=sub, seq_len=seq_len, n_seq=n_seq, cache=cache)
    (x2,) = _res_block(g, w_down_bf, x1, mod, w["ln2_g"], w["ln2_b"], tm=tm_down, bk=D_FF_PAD // 4, sub=sub,
                       seq_rows=sample, seq_len=seq_len, gate_col=5, emit_h2=False)
    return x2, (c1, n1, m1, s1), conv_out, dict(w_up=w_up_bf, w_down=w_down_bf)


def kernel(x_prompt, x_sample, state_mlstm_C, state_mlstm_n, state_mlstm_m, state_hgrn_S, cache_ffn_conv,
           c_prompt, c_sample, hgrn_lb_logits, w_ada, b_ada, w_in, b_gate_a, norm_a, norm_b, w_out,
           ln1_g, ln1_b, w_up, conv_w, conv_b, w_down, ln2_g, ln2_b):
    bp, tp, _ = x_prompt.shape
    bs, ts, _ = x_sample.shape
    pad_ff = D_FF_PAD - D_FF
    w_main, w_gate = _prep_w_in(jnp.transpose(w_in[0]))
    weights = dict(
        w_main=w_main, w_gate=w_gate, w_out=w_out[0].astype(BF16), w_up=w_up[0],
        w_down=w_down[0],
        gbias=jnp.pad(b_gate_a[0].reshape(1, N_GATE), ((0, 0), (0, LANE - N_GATE))),
        lb_logits=hgrn_lb_logits, norm_a=norm_a, norm_b=norm_b,
        ln1_g=ln1_g, ln1_b=ln1_b, ln2_g=ln2_g, ln2_b=ln2_b,
        conv_w=jnp.pad(conv_w[0], ((0, 0), (0, pad_ff))),
        conv_b=jnp.pad(conv_b, ((0, 0), (0, pad_ff))))

    c_p = jnp.pad(c_prompt, ((0, 8 - bp), (0, 0)))
    mod_p, mod_s = _ada(c_p, c_sample, w_ada[0], b_ada)

    yp, st_p, tail_p, w_ffn_bf = _path(x_prompt.reshape(bp * tp, D_MODEL), mod_p, weights, n_seq=bp,
                                       seq_len=tp, tm_in=1024, tm_res=512, tm_up=1024, tm_down=1024)

    xs_t = jnp.swapaxes(x_sample, 0, 1).reshape(ts * bs, D_MODEL)
    cache_t = jnp.swapaxes(cache_ffn_conv[0], 0, 1)
    ys_t, st_s, tail_s, _ = _path(xs_t, mod_s, dict(weights, **w_ffn_bf), n_seq=bs, seq_len=ts,
                                  tm_in=ts * bs, tm_res=ts * bs, tm_up=ts * bs, tm_down=ts * bs,
                                  states=(state_mlstm_C, state_mlstm_n, state_mlstm_m, state_hgrn_S),
                                  cache=cache_t)

    ys = jnp.swapaxes(ys_t.reshape(ts, bs, D_MODEL), 0, 1)
    conv_p = tail_p[None, :, :, :D_FF]
    conv_s = jnp.swapaxes(tail_s[:, :, :D_FF], 0, 1)[None]
    return (yp.reshape(bp, tp, D_MODEL), ys,
            st_p[0], st_p[1], st_p[2], st_p[3], conv_p,
            st_s[0], st_s[1], st_s[2], st_s[3], conv_s)
```

```python
import functools

import numpy as np
import jax
import jax.numpy as jnp
from jax import lax
from jax.experimental import pallas as pl
from jax.experimental.pallas import tpu as pltpu

F32 = jnp.float32
BF16 = jnp.bfloat16

D_MODEL = 2048
N_HEADS_A, DK_A, DV_A = 4, 128, 256
N_HEADS_B, DK_B, DV_B = 8, 128, 128
A_WIDTH = N_HEADS_A * DV_A
B_WIDTH = N_HEADS_B * DV_B
N_GATE = 2 * N_HEADS_A
GATE_COL = 2 * N_HEADS_A * DK_A + A_WIDTH
D_IN = GATE_COL + N_GATE + A_WIDTH + 4 * B_WIDTH
D_FF = 5504
CONV_W = 3
EPS = 1e-5
ALPHA = 2.0 ** 0.25
LOG2_E = 1.4426950408889634
LANE = 128
FF_BLOCK = 512
D_FF_PAD = 5632
Z_MAIN = D_IN - N_GATE
OFF_QA, OFF_KA, OFF_VA, OFF_OA = 0, 512, 1024, 2048
OFF_FB, OFF_QB, OFF_VB, OFF_GB = 3072, 4096, 5120, 6144
VMEM_LIMIT = 56 * 1024 * 1024
PROMPT_CHUNK = 128
PROMPT_SEQS_PER_STEP = 2
SAMPLE_GROUP = 8


def _ln(x):
    mu = jnp.mean(x, axis=-1, keepdims=True)
    xc = x - mu
    var = jnp.mean(xc * xc, axis=-1, keepdims=True)
    return xc * lax.rsqrt(var + EPS)


def _sigmoid(x):
    return 1.0 / (1.0 + jnp.exp(-x))


def _dot(a, b):
    return jnp.dot(a.astype(BF16), b.astype(BF16), preferred_element_type=F32)


def _dot_nt(a, b):
    return lax.dot_general(a.astype(BF16), b.astype(BF16), (((1,), (1,)), ((), ())),
                           preferred_element_type=F32)


def _dot_tn(a, b):
    return lax.dot_general(a.astype(BF16), b.astype(BF16), (((0,), (0,)), ((), ())),
                           preferred_element_type=F32)


def _mod_rows(ref, seq_rows, seq):
    if seq_rows:
        return ref[...]
    return ref[pl.ds(seq, 1), :]


def _ada_kernel(cp_ref, cs_ref, w_ref, b_ref, op_ref, os_ref):
    w = w_ref[...].astype(BF16)
    for c_ref, o_ref in ((cp_ref, op_ref), (cs_ref, os_ref)):
        c = c_ref[...]
        o_ref[...] = _dot(c * _sigmoid(c), w) + b_ref[...]


def _ada(c_p, c_s, w_ada, b_ada):
    n = w_ada.shape[1]
    bn = 1024
    row = lambda r: pl.BlockSpec((r, D_MODEL), lambda j: (0, 0))
    out = lambda r: pl.BlockSpec((r, bn), lambda j: (0, j))
    return pl.pallas_call(
        _ada_kernel,
        grid=(n // bn,),
        in_specs=[row(c_p.shape[0]), row(c_s.shape[0]),
                  pl.BlockSpec((D_MODEL, bn), lambda j: (0, j)),
                  pl.BlockSpec((1, bn), lambda j: (0, j))],
        out_specs=[out(c_p.shape[0]), out(c_s.shape[0])],
        out_shape=[jax.ShapeDtypeStruct((c_p.shape[0], n), F32),
                   jax.ShapeDtypeStruct((c_s.shape[0], n), F32)],
        compiler_params=pltpu.CompilerParams(vmem_limit_bytes=VMEM_LIMIT),
        name="ada",
    )(c_p, c_s, w_ada, b_ada)


def _prep_in_kernel(front_ref, back_ref, gate_ref, o_ref, og_ref, *, first_shifted):
    j = pl.program_id(0)

    @pl.when(j == 0)
    def _():
        og_ref[...] = gate_ref[...].T.astype(BF16)

    def emit(src_ref):
        for r in range(0, src_ref.shape[0], 256):
            o_ref[:, r:r + 256] = src_ref[r:r + 256, :].T.astype(BF16)

    @pl.when(j < first_shifted)
    def _():
        emit(front_ref)

    @pl.when(j >= first_shifted)
    def _():
        emit(back_ref)


def _prep_w_in(w_in_t):
    bn = 1024
    first_shifted = GATE_COL // bn
    kern = functools.partial(_prep_in_kernel, first_shifted=first_shifted)
    return pl.pallas_call(
        kern,
        grid=(Z_MAIN // bn,),
        in_specs=[pl.BlockSpec((bn, D_MODEL), lambda j: (jnp.minimum(j, first_shifted - 1), 0)),
                  pl.BlockSpec((pl.Element(bn), pl.Element(D_MODEL)),
                               lambda j: (pl.multiple_of(jnp.maximum(j, first_shifted) * bn + N_GATE, N_GATE), 0)),
                  pl.BlockSpec((LANE, D_MODEL), lambda j: (GATE_COL // LANE, 0))],
        out_specs=[pl.BlockSpec((D_MODEL, bn), lambda j: (0, j)),
                   pl.BlockSpec((D_MODEL, LANE), lambda j: (0, 0))],
        out_shape=[jax.ShapeDtypeStruct((D_MODEL, Z_MAIN), BF16),
                   jax.ShapeDtypeStruct((D_MODEL, LANE), BF16)],
        compiler_params=pltpu.CompilerParams(vmem_limit_bytes=VMEM_LIMIT),
        name="prep_w_in",
    )(w_in_t, w_in_t, w_in_t)


def _in_kernel(x_ref, sh_ref, sc_ref, w_ref, wg_ref, z_ref, g_ref, h_scr, *, sub, seq_rows, tiles_per_seq):
    i = pl.program_id(0)
    j = pl.program_id(1)

    @pl.when(j == 0)
    def _():
        seq = i // tiles_per_seq
        sh = _mod_rows(sh_ref, seq_rows, seq)
        sc = _mod_rows(sc_ref, seq_rows, seq)
        for r in range(0, x_ref.shape[0], sub):
            rows = slice(r, r + sub)
            h = (_ln(x_ref[rows, :]) * (1.0 + sc) + sh).astype(BF16)
            h_scr[rows, :] = h
            g_ref[rows, :] = _dot(h, wg_ref[...])
            z_ref[rows, :] = _dot(h, w_ref[...]).astype(z_ref.dtype)

    @pl.when(j > 0)
    def _():
        z_ref[...] = _dot(h_scr[...], w_ref[...]).astype(z_ref.dtype)


def _in_proj(x, mod, w_main, w_gate, *, tm, sub, seq_rows, seq_len, z_dtype):
    tokens = x.shape[0]
    bn = 1024
    mod_spec = lambda k: pl.BlockSpec((mod.shape[0], D_MODEL), lambda i, j: (0, k))
    kern = functools.partial(_in_kernel, sub=min(sub, tm), seq_rows=seq_rows, tiles_per_seq=max(seq_len // tm, 1))
    return pl.pallas_call(
        kern,
        grid=(tokens // tm, Z_MAIN // bn),
        in_specs=[pl.BlockSpec((tm, D_MODEL), lambda i, j: (i, 0)),
                  mod_spec(0), mod_spec(1),
                  pl.BlockSpec((D_MODEL, bn), lambda i, j: (0, j)),
                  pl.BlockSpec((D_MODEL, LANE), lambda i, j: (0, 0))],
        out_specs=[pl.BlockSpec((tm, bn), lambda i, j: (i, j)),
                   pl.BlockSpec((tm, LANE), lambda i, j: (i, 0))],
        out_shape=[jax.ShapeDtypeStruct((tokens, Z_MAIN), z_dtype),
                   jax.ShapeDtypeStruct((tokens, LANE), F32)],
        scratch_shapes=[pltpu.VMEM((tm, D_MODEL), BF16)],
        compiler_params=pltpu.CompilerParams(
            dimension_semantics=("arbitrary", "arbitrary"), vmem_limit_bytes=VMEM_LIMIT),
        name="in_proj",
    )(x, mod, mod, w_main, w_gate)


def _mix_consts(L, Tg, t_major):
    G = L // Tg
    r = np.arange(L)
    seq, tim = (r % G, r // G) if t_major else (r // Tg, r % Tg)
    same = seq[:, None] == seq[None, :]
    t = tim[:, None]
    u = tim[None, :]
    mats = [same & (u <= t), same & (u > t)]
    masks = [same & (u <= t), same & (t <= u), same]
    levels = []
    m = Tg // 2
    while m >= 1:
        levels.append(m)
        m //= 2
    for m in levels:
        blk = same & (u // m == t // m)
        odd = (t // m) % 2 == 1
        mats.append(np.where(odd, blk & (u <= t), blk & (u > t)))
        masks.append(same & odd & ((u // m) % 2 == 0) & (t // (2 * m) == u // (2 * m)))
    mats = np.concatenate(mats, axis=0).astype(np.float32)
    masks = np.stack(masks).astype(np.float32)
    last_rows = tuple(int(np.nonzero((seq == g) & (tim == Tg - 1))[0][0]) for g in range(G))
    return mats, masks, tuple(levels), last_rows


def _mix_kernel(z_ref, g_ref, gbias_ref, lbl_ref, na_ref, nb_ref, mall_ref, msk_ref,
                c_in, n_in, m_in, s_in, mix_ref, c_out, n_out, m_out, s_out, e_scr,
                *, L, Tg, levels, last_rows):
    G = L // Tg

    def zs(off, w):
        return z_ref[:, :, off:off + w].reshape(L, w)

    def put_mix(off, w, val):
        mix_ref[:, :, off:off + w] = val.astype(mix_ref.dtype).reshape(Tg, G, w)

    _mix_body(zs, put_mix, g_ref[...].reshape(L, LANE), gbias_ref, lbl_ref, na_ref, nb_ref, mall_ref, msk_ref,
              c_in, n_in, m_in, s_in, c_out, n_out, m_out, s_out, e_scr, pl.program_id(0) * G,
              L=L, Tg=Tg, levels=levels, last_rows=last_rows, t_major=True)


def _mixchunk_kernel(z_ref, g_ref, gbias_ref, lbl_ref, na_ref, nb_ref, mall_ref, msk_ref,
                     mix_ref, c_out, n_out, m_out, s_out, e_scr, *, L, levels, last_rows):
    nb = z_ref.shape[0]
    b0 = pl.program_id(0) * nb

    @pl.when(pl.program_id(1) == 0)
    def _():
        c_out[...] = jnp.zeros(c_out.shape, F32)
        n_out[...] = jnp.zeros(n_out.shape, F32)
        s_out[...] = jnp.zeros(s_out.shape, F32)
        m_out[0, pl.ds(b0, nb), :] = jnp.zeros((nb, N_HEADS_A), F32)

    for k in range(nb):
        def zs(off, w, k=k):
            return z_ref[k, :, off:off + w]

        def put_mix(off, w, val, k=k):
            mix_ref[k, :, off:off + w] = val.astype(mix_ref.dtype)

        st = [r.at[:, pl.ds(k, 1)] for r in (c_out, n_out, s_out)]
        _mix_body(zs, put_mix, g_ref[k], gbias_ref, lbl_ref, na_ref, nb_ref, mall_ref, msk_ref,
                  st[0], st[1], m_out, st[2], st[0], st[1], m_out, st[2], e_scr.at[k], b0 + k,
                  L=L, Tg=L, levels=levels, last_rows=last_rows, t_major=False)


def _mix_body(zs, put_mix, gates, gbias_ref, lbl_ref, na_ref, nb_ref, mall_ref, msk_ref,
              c_in, n_in, m_in, s_in, c_out, n_out, m_out, s_out, e_scr, m_row0,
              *, L, Tg, levels, last_rows, t_major):
    G = L // Tg
    neg_inf = F32(-jnp.inf)
    causal = msk_ref[0] > 0.5
    causal_t = msk_ref[1] > 0.5
    same = msk_ref[2] > 0.5
    row1 = lax.broadcasted_iota(jnp.int32, (L, 1), 0)
    seq_of_row = (row1 & (G - 1)) if t_major else (row1 // Tg)
    in_group = [seq_of_row == g for g in range(G)]

    def by_group(vals):
        if G == 1:
            return vals[0]
        out = jnp.where(in_group[0], vals[0], 0.0)
        for g in range(1, G):
            out = out + jnp.where(in_group[g], vals[g], 0.0)
        return out

    pre = gates + gbias_ref[...]
    lsig = jnp.minimum(pre, 0.0) - jnp.log(1.0 + jnp.exp(-jnp.abs(pre)))
    pre_t = pre.T
    lsig_t = lsig.T
    m_prev = m_in[0, pl.ds(m_row0, G), :]
    scale = DK_A ** -0.5
    m_new_rows = []
    for h in range(N_HEADS_A):
        q = zs(OFF_QA + DK_A * h, DK_A)
        k = zs(OFF_KA + DK_A * h, DK_A)
        v = zs(OFF_VA + DV_A * h, DV_A)
        logi_c = pre[:, h:h + 1]
        logi_r = pre_t[h:h + 1, :]
        lf_c = lsig[:, N_HEADS_A + h:N_HEADS_A + h + 1]
        lf_r = lsig_t[N_HEADS_A + h:N_HEADS_A + h + 1, :]
        b_c = jnp.sum(jnp.where(causal, lf_r, 0.0), axis=1, keepdims=True)
        b_r = jnp.sum(jnp.where(causal_t, lf_c, 0.0), axis=0, keepdims=True)
        tot_c = jnp.sum(jnp.where(same, lf_r, 0.0), axis=1, keepdims=True)
        tot_r = jnp.sum(jnp.where(same, lf_c, 0.0), axis=0, keepdims=True)
        m_col = by_group([m_prev[g:g + 1, h:h + 1] for g in range(G)])
        dmat = jnp.where(causal, b_c - b_r + logi_r, neg_inf)
        inter = b_c + m_col
        m_t = jnp.maximum(inter, jnp.max(dmat, axis=1, keepdims=True))
        w_inter = jnp.exp(inter - m_t)
        smat = _dot_nt(q, k) * (scale * jnp.exp(dmat - m_t))
        q_c = by_group([_dot(q, c_in[0, g, h]) for g in range(G)]) * scale
        n_rows = by_group([n_in[0, g, h:h + 1, :] for g in range(G)])
        q_n = jnp.sum(q.astype(F32) * n_rows, axis=1, keepdims=True) * scale
        num = w_inter * q_c + _dot(smat, v)
        den = w_inter * q_n + jnp.sum(smat, axis=1, keepdims=True)
        hh = num / jnp.maximum(jnp.abs(den), jnp.exp(-m_t))
        ms = jnp.mean(hh * hh, axis=1, keepdims=True)
        oa = zs(OFF_OA + DV_A * h, DV_A).astype(F32)
        ya = hh * lax.rsqrt(ms + EPS) * na_ref[:, DV_A * h:DV_A * (h + 1)] * _sigmoid(oa)
        put_mix(DV_A * h, DV_A, ya)
        dec_c = tot_c - b_c + logi_c
        dec_r = tot_r - b_r + logi_r
        m_new_c = jnp.maximum(tot_c + m_col,
                              jnp.max(jnp.where(same, dec_r, neg_inf), axis=1, keepdims=True))
        wk = jnp.exp(dec_c - m_new_c)
        sc = jnp.exp(tot_c + m_col - m_new_c)
        kw = k.astype(F32) * wk
        m_new_h = []
        for g in range(G):
            last = last_rows[g]
            kg = kw if G == 1 else jnp.where(in_group[g], kw, 0.0)
            sc_g = sc[last:last + 1, :]
            c_new = sc_g * c_in[0, g, h] + _dot_tn(kg, v)
            n_new = sc_g * n_in[0, g, h:h + 1, :] + jnp.sum(kg, axis=0, keepdims=True)
            c_out[0, g, h] = c_new
            n_out[0, g, h:h + 1, :] = n_new
            m_new_h.append(m_new_c[last:last + 1, :])
        m_new_rows.append(m_new_h)
    for g in range(G):
        m_out[0, pl.ds(m_row0 + g, 1), :] = jnp.concatenate(
            [m_new_rows[h][g] for h in range(N_HEADS_A)], axis=1)

    l0 = lbl_ref[0:1, :]
    l1 = lbl_ref[1:2, :]
    lmax = jnp.maximum(l0, l1)
    e0 = jnp.exp(l0 - lmax)
    e1 = jnp.exp(l1 - lmax)
    lb = e0 / (e0 + e1)
    fb = zs(OFF_FB, B_WIDTH).astype(F32)
    e = jnp.exp(-jnp.abs(fb))
    r = 1.0 / (1.0 + e)
    pos = fb >= 0.0
    sig = jnp.where(pos, r, e * r)
    nsig = jnp.where(pos, e * r, r)
    logf = jnp.log(lb + (1.0 - lb) * sig) * LOG2_E
    kb = (1.0 - lb) * nsig
    hi = logf.astype(BF16)
    r1 = logf - hi.astype(F32)
    mid = r1.astype(BF16)
    lo = (r1 - mid.astype(F32)).astype(BF16)
    mall = mall_ref[...]
    if L % 16 == 0:
        e_scr[...] = _dot(mall, jnp.concatenate([hi, mid, lo], axis=0))
    else:
        mall = mall[:, :L]
        e_scr[...] = _dot(mall, hi) + _dot(mall, mid) + _dot(mall, lo)

    dec_t = []
    for g in range(G):
        last = last_rows[g]
        bl = e_scr[last:last + 1, :]
        bl8 = jnp.concatenate([bl[:, DK_B * h:DK_B * (h + 1)] for h in range(N_HEADS_B)], axis=0)
        dec_t.append(jnp.exp2(bl8).T)

    for h in range(N_HEADS_B):
        sl = slice(DK_B * h, DK_B * (h + 1))
        q = zs(OFF_QB + DK_B * h, DK_B).astype(BF16)
        k = kb[:, sl]
        kbf = k.astype(BF16)
        v = zs(OFF_VB + DV_B * h, DV_B)
        amat = jnp.zeros((L, L), F32)
        for li in range(len(levels)):
            ex = jnp.exp2(e_scr[(2 + li) * L:(3 + li) * L, sl]).astype(BF16)
            amat = jnp.where(msk_ref[3 + li] > 0.5, _dot_nt(q * ex, kbf * ex), amat)
        qf = q.astype(F32)
        diag = jnp.sum(qf * k, axis=1, keepdims=True)
        qs = qf * jnp.exp2(e_scr[0:L, sl])
        o_inter = by_group([_dot(qs, s_in[0, g, h]) for g in range(G)])
        o = _dot(amat, v) + diag * v.astype(F32) + o_inter
        ms = jnp.mean(o * o, axis=1, keepdims=True)
        gb = zs(OFF_GB + DV_B * h, DV_B).astype(F32)
        yb = o * lax.rsqrt(ms + EPS) * nb_ref[:, sl] * (gb * _sigmoid(gb))
        put_mix(A_WIDTH + DV_B * h, DV_B, yb)
        kw = k * jnp.exp2(e_scr[L:2 * L, sl])
        for g in range(G):
            kg = kw if G == 1 else jnp.where(in_group[g], kw, 0.0)
            s_out[0, g, h] = dec_t[g][:, h:h + 1] * s_in[0, g, h] + _dot_tn(kg, v)


def _mixer_args(L, Tg, t_major, const2, const3, gbias, lb_logits, norm_a, norm_b):
    mats, masks, levels, last_rows = _mix_consts(L, Tg, t_major)
    n_mats = mats.shape[0] // L
    if L % 16 == 0:
        mats = np.concatenate([mats, mats, mats], axis=1)
    specs = [pl.BlockSpec((1, LANE), const2),
             pl.BlockSpec((2, B_WIDTH), const2),
             pl.BlockSpec((1, A_WIDTH), const2),
             pl.BlockSpec((1, B_WIDTH), const2),
             pl.BlockSpec(mats.shape, const2),
             pl.BlockSpec(masks.shape, const3)]
    args = [gbias, lb_logits, norm_a, norm_b, jnp.asarray(mats, BF16), jnp.asarray(masks)]
    return specs, args, levels, last_rows, n_mats


def _state_shapes(n_seq):
    return [jax.ShapeDtypeStruct((1, n_seq, N_HEADS_A, DK_A, DV_A), F32),
            jax.ShapeDtypeStruct((1, n_seq, N_HEADS_A, DK_A), F32),
            jax.ShapeDtypeStruct((1, n_seq, N_HEADS_A), F32),
            jax.ShapeDtypeStruct((1, n_seq, N_HEADS_B, DK_B, DV_B), F32)]


def _state_specs(G, n_seq, seq_of_step):
    return [pl.BlockSpec((1, G, N_HEADS_A, DK_A, DV_A), lambda *ids: (0, seq_of_step(*ids), 0, 0, 0)),
            pl.BlockSpec((1, G, N_HEADS_A, DK_A), lambda *ids: (0, seq_of_step(*ids), 0, 0)),
            pl.BlockSpec((1, n_seq, N_HEADS_A), lambda *ids: (0, 0, 0)),
            pl.BlockSpec((1, G, N_HEADS_B, DK_B, DV_B), lambda *ids: (0, seq_of_step(*ids), 0, 0, 0))]


def _mixer(z, gates, gbias, lb_logits, norm_a, norm_b, *, n_seq, seq_len, states):
    Tg, G = seq_len, SAMPLE_GROUP
    L = Tg * G
    tile = lambda w: pl.BlockSpec((Tg, G, w), lambda i: (0, i, 0))
    const_specs, const_args, levels, last_rows, n_mats = _mixer_args(
        L, Tg, True, lambda i: (0, 0), lambda i: (0, 0, 0), gbias, lb_logits, norm_a, norm_b)
    st_specs = _state_specs(G, n_seq, lambda i: i)
    kern = functools.partial(_mix_kernel, L=L, Tg=Tg, levels=levels, last_rows=last_rows)
    return pl.pallas_call(
        kern,
        grid=(n_seq // G,),
        in_specs=[tile(Z_MAIN), tile(LANE)] + const_specs + st_specs,
        out_specs=[tile(D_MODEL)] + st_specs,
        out_shape=[jax.ShapeDtypeStruct((Tg, n_seq, D_MODEL), F32)] + _state_shapes(n_seq),
        scratch_shapes=[pltpu.VMEM((n_mats * L, B_WIDTH), F32)],
        compiler_params=pltpu.CompilerParams(dimension_semantics=("arbitrary",), vmem_limit_bytes=VMEM_LIMIT),
        name="mixer_state",
    )(z, gates, *const_args, *states)


def _mixer_chunk(z, gates, gbias, lb_logits, norm_a, norm_b, *, n_seq, seq_len):
    L = PROMPT_CHUNK
    nb = PROMPT_SEQS_PER_STEP
    nchunks = seq_len // L
    tile = lambda w: pl.BlockSpec((nb, L, w), lambda b, c: (b, c, 0))
    const_specs, const_args, levels, last_rows, n_mats = _mixer_args(
        L, L, False, lambda b, c: (0, 0), lambda b, c: (0, 0, 0), gbias, lb_logits, norm_a, norm_b)
    st_specs = _state_specs(nb, n_seq, lambda b, c: b)
    kern = functools.partial(_mixchunk_kernel, L=L, levels=levels, last_rows=last_rows)
    outs = pl.pallas_call(
        kern,
        grid=(n_seq // nb, nchunks),
        in_specs=[tile(Z_MAIN), tile(LANE)] + const_specs,
        out_specs=[tile(D_MODEL)] + st_specs,
        out_shape=[jax.ShapeDtypeStruct((n_seq, seq_len, D_MODEL), BF16)] + _state_shapes(n_seq),
        scratch_shapes=[pltpu.VMEM((nb, n_mats * L, B_WIDTH), F32)],
        compiler_params=pltpu.CompilerParams(dimension_semantics=("arbitrary", "arbitrary"),
                                             vmem_limit_bytes=VMEM_LIMIT),
        name="mixer_chunk",
    )(z.reshape(n_seq, seq_len, Z_MAIN), gates.reshape(n_seq, seq_len, LANE), *const_args)
    return [outs[0].reshape(n_seq * seq_len, D_MODEL)] + list(outs[1:])


def _res_kernel(*refs, sub, seq_rows, tiles_per_seq, emit_h2, nk):
    if emit_h2:
        a_ref, w_ref, x_ref, gm_ref, lg_ref, lb_ref, sh_ref, sc_ref, o_ref, h_ref = refs
    else:
        a_ref, w_ref, x_ref, gm_ref, lg_ref, lb_ref, o_ref = refs
    i = pl.program_id(0)
    k = pl.program_id(1)

    if nk > 1:
        @pl.when(k == 0)
        def _():
            o_ref[...] = _dot(a_ref[...], w_ref[...])

    if nk > 2:
        @pl.when((k > 0) & (k < nk - 1))
        def _():
            o_ref[...] += _dot(a_ref[...], w_ref[...])

    @pl.when(k == nk - 1)
    def _():
        seq = i // tiles_per_seq
        gm = _mod_rows(gm_ref, seq_rows, seq)
        if emit_h2:
            sh = _mod_rows(sh_ref, seq_rows, seq)
            sc = _mod_rows(sc_ref, seq_rows, seq)
        for r in range(0, o_ref.shape[0], sub):
            rows = slice(r, r + sub)
            acc = _dot(a_ref[rows, :], w_ref[...])
            if nk > 1:
                acc = acc + o_ref[rows, :]
            x1 = _ln(ALPHA * x_ref[rows, :] + gm * acc) * lg_ref[...] + lb_ref[...]
            o_ref[rows, :] = x1
            if emit_h2:
                h_ref[rows, :] = (_ln(x1) * (1.0 + sc) + sh).astype(BF16)


def _res_block(a, w, x, mod, ln_g, ln_b, *, tm, bk, sub, seq_rows, seq_len, gate_col, emit_h2):
    tokens, kdim = a.shape
    nk = kdim // bk
    mod_spec = lambda c: pl.BlockSpec((mod.shape[0], D_MODEL), lambda i, k: (0, c))
    row_spec = pl.BlockSpec((tm, D_MODEL), lambda i, k: (i, 0))
    vec_spec = pl.BlockSpec((1, D_MODEL), lambda i, k: (0, 0))
    in_specs = [pl.BlockSpec((tm, bk), lambda i, k: (i, k)),
                pl.BlockSpec((bk, D_MODEL), lambda i, k: (k, 0)),
                row_spec, mod_spec(gate_col), vec_spec, vec_spec]
    args = [a, w, x, mod, ln_g, ln_b]
    out_specs = [row_spec]
    out_shape = [jax.ShapeDtypeStruct((tokens, D_MODEL), F32)]
    if emit_h2:
        in_specs += [mod_spec(3), mod_spec(4)]
        args += [mod, mod]
        out_specs.append(row_spec)
        out_shape.append(jax.ShapeDtypeStruct((tokens, D_MODEL), BF16))
    kern = functools.partial(_res_kernel, sub=sub, seq_rows=seq_rows, tiles_per_seq=max(seq_len // tm, 1),
                             emit_h2=emit_h2, nk=nk)
    return pl.pallas_call(
        kern,
        grid=(tokens // tm, nk),
        in_specs=in_specs,
        out_specs=out_specs,
        out_shape=out_shape,
        compiler_params=pltpu.CompilerParams(
            dimension_semantics=("arbitrary", "arbitrary"), vmem_limit_bytes=VMEM_LIMIT),
        name="res_block",
    )(*args)


def _up_kernel(*refs, tm, sub, seq_len, n_seq, t_major, tiles_per_seq):
    if t_major:
        h_ref, wa_scr, wu_scr, cw_ref, cb_ref, cache_ref, g_ref, tail_ref, abuf = refs
    else:
        (h_ref, wa_ref, ulo_ref, uhi_ref, wd_ref, cw_ref, cb_ref,
         g_ref, tail_ref, wa_scr, wu_scr, wd_out, carry) = refs
    j = pl.program_id(0)
    i = pl.program_id(1)
    bn = wa_scr.shape[1]
    valid = (j * bn + lax.broadcasted_iota(jnp.int32, (1, bn), 1)) < D_FF
    hist = (CONV_W - 1) * n_seq

    @pl.when(i == 0)
    def _():
        if t_major:
            abuf[0:hist, :] = jnp.where(valid, cache_ref[...].reshape(hist, bn), 0.0)
        else:
            split = D_FF % bn
            wu = jnp.concatenate([ulo_ref[:, split:], uhi_ref[:, :split]], axis=1)
            wa_scr[...] = jnp.where(valid, wa_ref[...], 0.0).astype(BF16)
            wu_scr[...] = jnp.where(valid, wu, 0.0).astype(BF16)
            wd_rows = j * bn + lax.broadcasted_iota(jnp.int32, (bn, 1), 0)
            wd_out[...] = jnp.where(wd_rows < D_FF, wd_ref[...], 0.0).astype(BF16)

    if not t_major:
        @pl.when(i % tiles_per_seq == 0)
        def _():
            carry[...] = jnp.zeros(carry.shape, F32)

    row8 = lax.broadcasted_iota(jnp.int32, (8, 1), 0)
    for r in range(0, tm, sub):
        h = h_ref[r:r + sub, :]
        a = _dot(h, wa_scr[...])
        u = _dot(h, wu_scr[...])
        if t_major:
            abuf[hist + r:hist + r + sub, :] = a
            a1 = abuf[hist + r - n_seq:hist + r - n_seq + sub, :]
            a2 = abuf[hist + r - 2 * n_seq:hist + r - 2 * n_seq + sub, :]
        else:
            p1 = carry[7:8, :]
            p2 = carry[6:7, :]
            r1 = pltpu.roll(a, 1, axis=0)
            r2 = pltpu.roll(a, 2, axis=0)
            a1 = jnp.concatenate([jnp.where(row8 == 0, p1, r1[0:8, :]), r1[8:, :]], axis=0)
            a2 = jnp.concatenate(
                [jnp.where(row8 == 0, p2, jnp.where(row8 == 1, p1, r2[0:8, :])), r2[8:, :]], axis=0)
            carry[...] = a[sub - 8:sub, :]
        conv = cb_ref[...] + cw_ref[0:1, :] * a2 + cw_ref[1:2, :] * a1 + cw_ref[2:3, :] * a
        gl = 0.5 * conv * (1.0 + lax.erf(conv * (2.0 ** -0.5)))
        g_ref[r:r + sub, :] = (gl * u).astype(BF16)

    if t_major:
        tail_ref[...] = abuf[tm:tm + hist, :].reshape(tail_ref.shape)
    else:
        @pl.when((i + 1) % tiles_per_seq == 0)
        def _():
            tail_ref[0] = carry[8 - (CONV_W - 1):8, :]


def _up_proj(h2, w_up, w_down, conv_w_p, conv_b_p, *, tm, sub, seq_len, n_seq, cache=None):
    tokens = h2.shape[0]
    t_major = cache is not None
    bn = FF_BLOCK
    nj = D_FF_PAD // bn
    u_lo = D_FF // bn
    u_last = (2 * D_FF - 1) // bn
    tiles_per_seq = max(seq_len // tm, 1)
    wspec = lambda f: pl.BlockSpec((D_MODEL, bn), f)
    col_spec = wspec(lambda j, i: (0, j))
    h_spec = pl.BlockSpec((tm, D_MODEL), lambda j, i: (i, 0))
    conv_specs = [pl.BlockSpec((CONV_W, bn), lambda j, i: (0, j)), pl.BlockSpec((1, bn), lambda j, i: (0, j))]
    out_specs = [pl.BlockSpec((tm, bn), lambda j, i: (i, j))]
    out_shape = [jax.ShapeDtypeStruct((tokens, D_FF_PAD), BF16)]
    if t_major:
        hist_spec = pl.BlockSpec((CONV_W - 1, n_seq, bn), lambda j, i: (0, 0, j))
        in_specs = [h_spec, col_spec, col_spec] + conv_specs + [hist_spec]
        args = [h2, w_up[0], w_up[1], conv_w_p, conv_b_p, cache]
        out_specs.append(hist_spec)
        out_shape.append(jax.ShapeDtypeStruct((CONV_W - 1, n_seq, D_FF_PAD), F32))
        scratch_rows = (CONV_W - 1) * n_seq + tm
    else:
        row_spec = pl.BlockSpec((bn, D_MODEL), lambda j, i: (j, 0))
        in_specs = [h_spec, col_spec,
                    wspec(lambda j, i: (0, u_lo + j)),
                    wspec(lambda j, i: (0, jnp.minimum(u_lo + 1 + j, u_last))),
                    row_spec] + conv_specs
        args = [h2, w_up, w_up, w_up, w_down, conv_w_p, conv_b_p]
        out_specs += [pl.BlockSpec((1, CONV_W - 1, bn), lambda j, i: (i // tiles_per_seq, 0, j)),
                      col_spec, col_spec, row_spec]
        out_shape += [jax.ShapeDtypeStruct((n_seq, CONV_W - 1, D_FF_PAD), F32),
                      jax.ShapeDtypeStruct((D_MODEL, D_FF_PAD), BF16),
                      jax.ShapeDtypeStruct((D_MODEL, D_FF_PAD), BF16),
                      jax.ShapeDtypeStruct((D_FF_PAD, D_MODEL), BF16)]
        scratch_rows = 8
    kern = functools.partial(_up_kernel, tm=tm, sub=sub, seq_len=seq_len, n_seq=n_seq, t_major=t_major,
                             tiles_per_seq=tiles_per_seq)
    outs = pl.pallas_call(
        kern,
        grid=(nj, tokens // tm),
        in_specs=in_specs,
        out_specs=out_specs,
        out_shape=out_shape,
        scratch_shapes=[pltpu.VMEM((scratch_rows, bn), F32)],
        compiler_params=pltpu.CompilerParams(
            dimension_semantics=("arbitrary", "arbitrary"), vmem_limit_bytes=VMEM_LIMIT),
        name="up_proj",
    )(*args)
    if t_major:
        return outs[0], outs[1], w_up, w_down
    return outs[0], outs[1], (outs[2], outs[3]), outs[4]


def _path(x, mod, w, *, n_seq, seq_len, tm_in, tm_res, tm_up, tm_down, states=None, cache=None):
    sample = states is not None
    sub = n_seq if sample else 256
    mixer_w = (w["gbias"], w["lb_logits"], w["norm_a"], w["norm_b"])
    z, gates = _in_proj(x, mod, w["w_main"], w["w_gate"], tm=tm_in, sub=sub, seq_rows=sample,
                        seq_len=seq_len, z_dtype=F32 if sample else BF16)
    if sample:
        mix, c1, n1, m1, s1 = _mixer(z.reshape(seq_len, n_seq, Z_MAIN), gates.reshape(seq_len, n_seq, LANE),
                                     *mixer_w, n_seq=n_seq, seq_len=seq_len, states=states)
        mix = mix.reshape(n_seq * seq_len, D_MODEL)
    else:
        mix, c1, n1, m1, s1 = _mixer_chunk(z, gates, *mixer_w, n_seq=n_seq, seq_len=seq_len)
    x1, h2 = _res_block(mix, w["w_out"], x, mod, w["ln1_g"], w["ln1_b"], tm=tm_res, bk=D_MODEL, sub=sub,
                        seq_rows=sample, seq_len=seq_len, gate_col=2, emit_h2=True)
    g, conv_out, w_up_bf, w_down_bf = _up_proj(h2, w["w_up"], w["w_down"], w["conv_w"], w["conv_b"], tm=tm_up,
                                               sub=sub, seq_len=seq_len, n_seq=n_seq, cache=cache)
    (x2,) = _res_block(g, w_down_bf, x1, mod, w["ln2_g"], w["ln2_b"], tm=tm_down, bk=D_FF_PAD // 4, sub=sub,
                       seq_rows=sample, seq_len=seq_len, gate_col=5, emit_h2=False)
    return x2, (c1, n1, m1, s1), conv_out, dict(w_up=w_up_bf, w_down=w_down_bf)


def kernel(x_prompt, x_sample, state_mlstm_C, state_mlstm_n, state_mlstm_m, state_hgrn_S, cache_ffn_conv,
           c_prompt, c_sample, hgrn_lb_logits, w_ada, b_ada, w_in, b_gate_a, norm_a, norm_b, w_out,
           ln1_g, ln1_b, w_up, conv_w, conv_b, w_down, ln2_g, ln2_b):
    bp, tp, _ = x_prompt.shape
    bs, ts, _ = x_sample.shape
    pad_ff = D_FF_PAD - D_FF
    w_main, w_gate = _prep_w_in(jnp.transpose(w_in[0]))
    weights = dict(
        w_main=w_main, w_gate=w_gate, w_out=w_out[0].astype(BF16), w_up=w_up[0],
        w_down=w_down[0],
        gbias=jnp.pad(b_gate_a[0].reshape(1, N_GATE), ((0, 0), (0, LANE - N_GATE))),
        lb_logits=hgrn_lb_logits, norm_a=norm_a, norm_b=norm_b,
        ln1_g=ln1_g, ln1_b=ln1_b, ln2_g=ln2_g, ln2_b=ln2_b,
        conv_w=jnp.pad(conv_w[0], ((0, 0), (0, pad_ff))),
        conv_b=jnp.pad(conv_b, ((0, 0), (0, pad_ff))))

    c_p = jnp.pad(c_prompt, ((0, 8 - bp), (0, 0)))
    mod_p, mod_s = _ada(c_p, c_sample, w_ada[0], b_ada)

    yp, st_p, tail_p, w_ffn_bf = _path(x_prompt.reshape(bp * tp, D_MODEL), mod_p, weights, n_seq=bp,
                                       seq_len=tp, tm_in=1024, tm_res=512, tm_up=1024, tm_down=1024)

    xs_t = jnp.swapaxes(x_sample, 0, 1).reshape(ts * bs, D_MODEL)
    cache_t = jnp.swapaxes(cache_ffn_conv[0], 0, 1)
    ys_t, st_s, tail_s, _ = _path(xs_t, mod_s, dict(weights, **w_ffn_bf), n_seq=bs, seq_len=ts,
                                  tm_in=ts * bs, tm_res=ts * bs, tm_up=ts * bs, tm_down=ts * bs,
                                  states=(state_mlstm_C, state_mlstm_n, state_mlstm_m, state_hgrn_S),
                                  cache=cache_t)

    ys = jnp.swapaxes(ys_t.reshape(ts, bs, D_MODEL), 0, 1)
    conv_p = tail_p[None, :, :, :D_FF]
    conv_s = jnp.swapaxes(tail_s[:, :, :D_FF], 0, 1)[None]
    return (yp.reshape(bp, tp, D_MODEL), ys,
            st_p[0], st_p[1], st_p[2], st_p[3], conv_p,
            st_s[0], st_s[1], st_s[2], st_s[3], conv_s)
```

```python
import functools

import numpy as np
import jax
import jax.numpy as jnp
from jax import lax
from jax.experimental import pallas as pl
from jax.experimental.pallas import tpu as pltpu

F32 = jnp.float32
BF16 = jnp.bfloat16

D_MODEL = 2048
N_HEADS_A, DK_A, DV_A = 4, 128, 256
N_HEADS_B, DK_B, DV_B = 8, 128, 128
A_WIDTH = N_HEADS_A * DV_A
B_WIDTH = N_HEADS_B * DV_B
N_GATE = 2 * N_HEADS_A
GATE_COL = 2 * N_HEADS_A * DK_A + A_WIDTH
D_IN = GATE_COL + N_GATE + A_WIDTH + 4 * B_WIDTH
D_FF = 5504
CONV_W = 3
EPS = 1e-5
ALPHA = 2.0 ** 0.25
LOG2_E = 1.4426950408889634
LANE = 128
FF_BLOCK = 512
D_FF_PAD = 5632
Z_MAIN = D_IN - N_GATE
OFF_QA, OFF_KA, OFF_VA, OFF_OA = 0, 512, 1024, 2048
OFF_FB, OFF_QB, OFF_VB, OFF_GB = 3072, 4096, 5120, 6144
VMEM_LIMIT = 56 * 1024 * 1024
PROMPT_CHUNK = 128
PROMPT_SEQS_PER_STEP = 2
SAMPLE_GROUP = 8


def _ln(x):
    mu = jnp.mean(x, axis=-1, keepdims=True)
    xc = x - mu
    var = jnp.mean(xc * xc, axis=-1, keepdims=True)
    return xc * lax.rsqrt(var + EPS)


def _sigmoid(x):
    return 1.0 / (1.0 + jnp.exp(-x))


def _dot(a, b):
    return jnp.dot(a.astype(BF16), b.astype(BF16), preferred_element_type=F32)


def _dot_nt(a, b):
    return lax.dot_general(a.astype(BF16), b.astype(BF16), (((1,), (1,)), ((), ())),
                           preferred_element_type=F32)


def _dot_tn(a, b):
    return lax.dot_general(a.astype(BF16), b.astype(BF16), (((0,), (0,)), ((), ())),
                           preferred_element_type=F32)


def _mod_rows(ref, seq_rows, seq):
    if seq_rows:
        return ref[...]
    return ref[pl.ds(seq, 1), :]


def _ada_kernel(cp_ref, cs_ref, w_ref, b_ref, op_ref, os_ref):
    w = w_ref[...].astype(BF16)
    for c_ref, o_ref in ((cp_ref, op_ref), (cs_ref, os_ref)):
        c = c_ref[...]
        o_ref[...] = _dot(c * _sigmoid(c), w) + b_ref[...]


def _ada(c_p, c_s, w_ada, b_ada):
    n = w_ada.shape[1]
    bn = 1024
    row = lambda r: pl.BlockSpec((r, D_MODEL), lambda j: (0, 0))
    out = lambda r: pl.BlockSpec((r, bn), lambda j: (0, j))
    return pl.pallas_call(
        _ada_kernel,
        grid=(n // bn,),
        in_specs=[row(c_p.shape[0]), row(c_s.shape[0]),
                  pl.BlockSpec((D_MODEL, bn), lambda j: (0, j)),
                  pl.BlockSpec((1, bn), lambda j: (0, j))],
        out_specs=[out(c_p.shape[0]), out(c_s.shape[0])],
        out_shape=[jax.ShapeDtypeStruct((c_p.shape[0], n), F32),
                   jax.ShapeDtypeStruct((c_s.shape[0], n), F32)],
        compiler_params=pltpu.CompilerParams(vmem_limit_bytes=VMEM_LIMIT),
        name="ada",
    )(c_p, c_s, w_ada, b_ada)


def _prep_in_kernel(front_ref, back_ref, gate_ref, o_ref, og_ref, *, first_shifted):
    j = pl.program_id(0)

    @pl.when(j == 0)
    def _():
        og_ref[...] = gate_ref[...].T.astype(BF16)

    def emit(src_ref):
        for r in range(0, src_ref.shape[0], 256):
            o_ref[:, r:r + 256] = src_ref[r:r + 256, :].T.astype(BF16)

    @pl.when(j < first_shifted)
    def _():
        emit(front_ref)

    @pl.when(j >= first_shifted)
    def _():
        emit(back_ref)


def _prep_w_in(w_in_t):
    bn = 1024
    first_shifted = GATE_COL // bn
    kern = functools.partial(_prep_in_kernel, first_shifted=first_shifted)
    return pl.pallas_call(
        kern,
        grid=(Z_MAIN // bn,),
        in_specs=[pl.BlockSpec((bn, D_MODEL), lambda j: (jnp.minimum(j, first_shifted - 1), 0)),
                  pl.BlockSpec((pl.Element(bn), pl.Element(D_MODEL)),
                               lambda j: (pl.multiple_of(jnp.maximum(j, first_shifted) * bn + N_GATE, N_GATE), 0)),
                  pl.BlockSpec((LANE, D_MODEL), lambda j: (GATE_COL // LANE, 0))],
        out_specs=[pl.BlockSpec((D_MODEL, bn), lambda j: (0, j)),
                   pl.BlockSpec((D_MODEL, LANE), lambda j: (0, 0))],
        out_shape=[jax.ShapeDtypeStruct((D_MODEL, Z_MAIN), BF16),
                   jax.ShapeDtypeStruct((D_MODEL, LANE), BF16)],
        compiler_params=pltpu.CompilerParams(vmem_limit_bytes=VMEM_LIMIT),
        name="prep_w_in",
    )(w_in_t, w_in_t, w_in_t)


def _in_kernel(x_ref, sh_ref, sc_ref, w_ref, wg_ref, z_ref, g_ref, h_scr, *, sub, seq_rows, tiles_per_seq):
    i = pl.program_id(0)
    j = pl.program_id(1)

    @pl.when(j == 0)
    def _():
        seq = i // tiles_per_seq
        sh = _mod_rows(sh_ref, seq_rows, seq)
        sc = _mod_rows(sc_ref, seq_rows, seq)
        for r in range(0, x_ref.shape[0], sub):
            rows = slice(r, r + sub)
            h = (_ln(x_ref[rows, :]) * (1.0 + sc) + sh).astype(BF16)
            h_scr[rows, :] = h
            g_ref[rows, :] = _dot(h, wg_ref[...])
            z_ref[rows, :] = _dot(h, w_ref[...]).astype(z_ref.dtype)

    @pl.when(j > 0)
    def _():
        z_ref[...] = _dot(h_scr[...], w_ref[...]).astype(z_ref.dtype)


def _in_proj(x, mod, w_main, w_gate, *, tm, sub, seq_rows, seq_len, z_dtype):
    tokens = x.shape[0]
    bn = 1024
    mod_spec = lambda k: pl.BlockSpec((mod.shape[0], D_MODEL), lambda i, j: (0, k))
    kern = functools.partial(_in_kernel, sub=min(sub, tm), seq_rows=seq_rows, tiles_per_seq=max(seq_len // tm, 1))
    return pl.pallas_call(
        kern,
        grid=(tokens // tm, Z_MAIN // bn),
        in_specs=[pl.BlockSpec((tm, D_MODEL), lambda i, j: (i, 0)),
                  mod_spec(0), mod_spec(1),
                  pl.BlockSpec((D_MODEL, bn), lambda i, j: (0, j)),
                  pl.BlockSpec((D_MODEL, LANE), lambda i, j: (0, 0))],
        out_specs=[pl.BlockSpec((tm, bn), lambda i, j: (i, j)),
                   pl.BlockSpec((tm, LANE), lambda i, j: (i, 0))],
        out_shape=[jax.ShapeDtypeStruct((tokens, Z_MAIN), z_dtype),
                   jax.ShapeDtypeStruct((tokens, LANE), F32)],
        scratch_shapes=[pltpu.VMEM((tm, D_MODEL), BF16)],
        compiler_params=pltpu.CompilerParams(
            dimension_semantics=("arbitrary", "arbitrary"), vmem_limit_bytes=VMEM_LIMIT),
        name="in_proj",
    )(x, mod, mod, w_main, w_gate)


def _mix_consts(L, Tg, t_major):
    G = L // Tg
    r = np.arange(L)
    seq, tim = (r % G, r // G) if t_major else (r // Tg, r % Tg)
    same = seq[:, None] == seq[None, :]
    t = tim[:, None]
    u = tim[None, :]
    mats = [same & (u <= t), same & (u > t)]
    masks = [same & (u <= t), same & (t <= u), same]
    levels = []
    m = Tg // 2
    while m >= 1:
        levels.append(m)
        m //= 2
    for m in levels:
        blk = same & (u // m == t // m)
        odd = (t // m) % 2 == 1
        mats.append(np.where(odd, blk & (u <= t), blk & (u > t)))
        masks.append(same & odd & ((u // m) % 2 == 0) & (t // (2 * m) == u // (2 * m)))
    mats = np.concatenate(mats, axis=0).astype(np.float32)
    masks = np.stack(masks).astype(np.float32)
    last_rows = tuple(int(np.nonzero((seq == g) & (tim == Tg - 1))[0][0]) for g in range(G))
    return mats, masks, tuple(levels), last_rows


def _mix_kernel(z_ref, g_ref, gbias_ref, lbl_ref, na_ref, nb_ref, mall_ref, msk_ref,
                c_in, n_in, m_in, s_in, mix_ref, c_out, n_out, m_out, s_out, e_scr,
                *, L, Tg, levels, last_rows):
    G = L // Tg

    def zs(off, w):
        return z_ref[:, :, off:off + w].reshape(L, w)

    def put_mix(off, w, val):
        mix_ref[:, :, off:off + w] = val.astype(mix_ref.dtype).reshape(Tg, G, w)

    _mix_body(zs, put_mix, g_ref[...].reshape(L, LANE), gbias_ref, lbl_ref, na_ref, nb_ref, mall_ref, msk_ref,
              c_in, n_in, m_in, s_in, c_out, n_out, m_out, s_out, e_scr, pl.program_id(0) * G,
              L=L, Tg=Tg, levels=levels, last_rows=last_rows, t_major=True)


def _mixchunk_kernel(z_ref, g_ref, gbias_ref, lbl_ref, na_ref, nb_ref, mall_ref, msk_ref,
                     mix_ref, c_out, n_out, m_out, s_out, e_scr, *, L, levels, last_rows):
    nb = z_ref.shape[0]
    b0 = pl.program_id(0) * nb

    @pl.when(pl.program_id(1) == 0)
    def _():
        c_out[...] = jnp.zeros(c_out.shape, F32)
        n_out[...] = jnp.zeros(n_out.shape, F32)
        s_out[...] = jnp.zeros(s_out.shape, F32)
        m_out[0, pl.ds(b0, nb), :] = jnp.zeros((nb, N_HEADS_A), F32)

    for k in range(nb):
        def zs(off, w, k=k):
            return z_ref[k, :, off:off + w]

        def put_mix(off, w, val, k=k):
            mix_ref[k, :, off:off + w] = val.astype(mix_ref.dtype)

        st = [r.at[:, pl.ds(k, 1)] for r in (c_out, n_out, s_out)]
        _mix_body(zs, put_mix, g_ref[k], gbias_ref, lbl_ref, na_ref, nb_ref, mall_ref, msk_ref,
                  st[0], st[1], m_out, st[2], st[0], st[1], m_out, st[2], e_scr.at[k], b0 + k,
                  L=L, Tg=L, levels=levels, last_rows=last_rows, t_major=False)


def _mix_body(zs, put_mix, gates, gbias_ref, lbl_ref, na_ref, nb_ref, mall_ref, msk_ref,
              c_in, n_in, m_in, s_in, c_out, n_out, m_out, s_out, e_scr, m_row0,
              *, L, Tg, levels, last_rows, t_major):
    G = L // Tg
    neg_inf = F32(-jnp.inf)
    causal = msk_ref[0] > 0.5
    causal_t = msk_ref[1] > 0.5
    same = msk_ref[2] > 0.5
    row1 = lax.broadcasted_iota(jnp.int32, (L, 1), 0)
    seq_of_row = (row1 & (G - 1)) if t_major else (row1 // Tg)
    in_group = [seq_of_row == g for g in range(G)]
    col1 = lax.broadcasted_iota(jnp.int32, (1, L), 1)
    seq_of_col = (col1 & (G - 1)) if t_major else (col1 // Tg)
    in_group_lane = [seq_of_col == g for g in range(G)]

    def state_updates(kw, v):
        if G == 1:
            return [_dot_tn(kw, v)]
        kw_t = kw.T
        return [_dot(jnp.where(in_group_lane[g], kw_t, 0.0), v) for g in range(G)]

    def by_group(vals):
        if G == 1:
            return vals[0]
        out = jnp.where(in_group[0], vals[0], 0.0)
        for g in range(1, G):
            out = out + jnp.where(in_group[g], vals[g], 0.0)
        return out

    pre = gates + gbias_ref[...]
    lsig = jnp.minimum(pre, 0.0) - jnp.log(1.0 + jnp.exp(-jnp.abs(pre)))
    pre_t = pre.T
    lsig_t = lsig.T
    m_prev = m_in[0, pl.ds(m_row0, G), :]
    scale = DK_A ** -0.5
    m_new_rows = []
    for h in range(N_HEADS_A):
        q = zs(OFF_QA + DK_A * h, DK_A)
        k = zs(OFF_KA + DK_A * h, DK_A)
        v = zs(OFF_VA + DV_A * h, DV_A)
        logi_c = pre[:, h:h + 1]
        logi_r = pre_t[h:h + 1, :]
        lf_c = lsig[:, N_HEADS_A + h:N_HEADS_A + h + 1]
        lf_r = lsig_t[N_HEADS_A + h:N_HEADS_A + h + 1, :]
        b_c = jnp.sum(jnp.where(causal, lf_r, 0.0), axis=1, keepdims=True)
        b_r = jnp.sum(jnp.where(causal_t, lf_c, 0.0), axis=0, keepdims=True)
        tot_c = jnp.sum(jnp.where(same, lf_r, 0.0), axis=1, keepdims=True)
        tot_r = jnp.sum(jnp.where(same, lf_c, 0.0), axis=0, keepdims=True)
        m_col = by_group([m_prev[g:g + 1, h:h + 1] for g in range(G)])
        dmat = jnp.where(causal, b_c - b_r + logi_r, neg_inf)
        inter = b_c + m_col
        m_t = jnp.maximum(inter, jnp.max(dmat, axis=1, keepdims=True))
        w_inter = jnp.exp(inter - m_t)
        smat = _dot_nt(q, k) * (scale * jnp.exp(dmat - m_t))
        q_c = by_group([_dot(q, c_in[0, g, h]) for g in range(G)]) * scale
        n_rows = by_group([n_in[0, g, h:h + 1, :] for g in range(G)])
        q_n = jnp.sum(q.astype(F32) * n_rows, axis=1, keepdims=True) * scale
        num = w_inter * q_c + _dot(smat, v)
        den = w_inter * q_n + jnp.sum(smat, axis=1, keepdims=True)
        hh = num / jnp.maximum(jnp.abs(den), jnp.exp(-m_t))
        ms = jnp.mean(hh * hh, axis=1, keepdims=True)
        oa = zs(OFF_OA + DV_A * h, DV_A).astype(F32)
        ya = hh * lax.rsqrt(ms + EPS) * na_ref[:, DV_A * h:DV_A * (h + 1)] * _sigmoid(oa)
        put_mix(DV_A * h, DV_A, ya)
        dec_c = tot_c - b_c + logi_c
        dec_r = tot_r - b_r + logi_r
        m_new_c = jnp.maximum(tot_c + m_col,
                              jnp.max(jnp.where(same, dec_r, neg_inf), axis=1, keepdims=True))
        wk = jnp.exp(dec_c - m_new_c)
        sc = jnp.exp(tot_c + m_col - m_new_c)
        kw = k.astype(F32) * wk
        upd = state_updates(kw, v)
        m_new_h = []
        for g in range(G):
            last = last_rows[g]
            kg = kw if G == 1 else jnp.where(in_group[g], kw, 0.0)
            sc_g = sc[last:last + 1, :]
            c_new = sc_g * c_in[0, g, h] + upd[g]
            n_new = sc_g * n_in[0, g, h:h + 1, :] + jnp.sum(kg, axis=0, keepdims=True)
            c_out[0, g, h] = c_new
            n_out[0, g, h:h + 1, :] = n_new
            m_new_h.append(m_new_c[last:last + 1, :])
        m_new_rows.append(m_new_h)
    for g in range(G):
        m_out[0, pl.ds(m_row0 + g, 1), :] = jnp.concatenate(
            [m_new_rows[h][g] for h in range(N_HEADS_A)], axis=1)

    l0 = lbl_ref[0:1, :]
    l1 = lbl_ref[1:2, :]
    lmax = jnp.maximum(l0, l1)
    e0 = jnp.exp(l0 - lmax)
    e1 = jnp.exp(l1 - lmax)
    lb = e0 / (e0 + e1)
    fb = zs(OFF_FB, B_WIDTH).astype(F32)
    e = jnp.exp(-jnp.abs(fb))
    r = 1.0 / (1.0 + e)
    pos = fb >= 0.0
    sig = jnp.where(pos, r, e * r)
    nsig = jnp.where(pos, e * r, r)
    logf = jnp.log(lb + (1.0 - lb) * sig) * LOG2_E
    kb = (1.0 - lb) * nsig
    hi = logf.astype(BF16)
    r1 = logf - hi.astype(F32)
    mid = r1.astype(BF16)
    lo = (r1 - mid.astype(F32)).astype(BF16)
    mall = mall_ref[...]
    if L % 16 == 0:
        e_scr[...] = _dot(mall, jnp.concatenate([hi, mid, lo], axis=0))
    else:
        mall = mall[:, :L]
        e_scr[...] = _dot(mall, hi) + _dot(mall, mid) + _dot(mall, lo)

    dec_t = []
    for g in range(G):
        last = last_rows[g]
        bl = e_scr[last:last + 1, :]
        bl8 = jnp.concatenate([bl[:, DK_B * h:DK_B * (h + 1)] for h in range(N_HEADS_B)], axis=0)
        dec_t.append(jnp.exp2(bl8).T)

    for h in range(N_HEADS_B):
        sl = slice(DK_B * h, DK_B * (h + 1))
        q = zs(OFF_QB + DK_B * h, DK_B).astype(BF16)
        k = kb[:, sl]
        kbf = k.astype(BF16)
        v = zs(OFF_VB + DV_B * h, DV_B)
        amat = jnp.zeros((L, L), F32)
        for li in range(len(levels)):
            ex = jnp.exp2(e_scr[(2 + li) * L:(3 + li) * L, sl]).astype(BF16)
            amat = jnp.where(msk_ref[3 + li] > 0.5, _dot_nt(q * ex, kbf * ex), amat)
        qf = q.astype(F32)
        diag = jnp.sum(qf * k, axis=1, keepdims=True)
        qs = qf * jnp.exp2(e_scr[0:L, sl])
        o_inter = by_group([_dot(qs, s_in[0, g, h]) for g in range(G)])
        o = _dot(amat, v) + diag * v.astype(F32) + o_inter
        ms = jnp.mean(o * o, axis=1, keepdims=True)
        gb = zs(OFF_GB + DV_B * h, DV_B).astype(F32)
        yb = o * lax.rsqrt(ms + EPS) * nb_ref[:, sl] * (gb * _sigmoid(gb))
        put_mix(A_WIDTH + DV_B * h, DV_B, yb)
        upd = state_updates(k * jnp.exp2(e_scr[L:2 * L, sl]), v)
        for g in range(G):
            s_out[0, g, h] = dec_t[g][:, h:h + 1] * s_in[0, g, h] + upd[g]


def _mixer_args(L, Tg, t_major, const2, const3, gbias, lb_logits, norm_a, norm_b):
    mats, masks, levels, last_rows = _mix_consts(L, Tg, t_major)
    n_mats = mats.shape[0] // L
    if L % 16 == 0:
        mats = np.concatenate([mats, mats, mats], axis=1)
    specs = [pl.BlockSpec((1, LANE), const2),
             pl.BlockSpec((2, B_WIDTH), const2),
             pl.BlockSpec((1, A_WIDTH), const2),
             pl.BlockSpec((1, B_WIDTH), const2),
             pl.BlockSpec(mats.shape, const2),
             pl.BlockSpec(masks.shape, const3)]
    args = [gbias, lb_logits, norm_a, norm_b, jnp.asarray(mats, BF16), jnp.asarray(masks)]
    return specs, args, levels, last_rows, n_mats


def _state_shapes(n_seq):
    return [jax.ShapeDtypeStruct((1, n_seq, N_HEADS_A, DK_A, DV_A), F32),
            jax.ShapeDtypeStruct((1, n_seq, N_HEADS_A, DK_A), F32),
            jax.ShapeDtypeStruct((1, n_seq, N_HEADS_A), F32),
            jax.ShapeDtypeStruct((1, n_seq, N_HEADS_B, DK_B, DV_B), F32)]


def _state_specs(G, n_seq, seq_of_step):
    return [pl.BlockSpec((1, G, N_HEADS_A, DK_A, DV_A), lambda *ids: (0, seq_of_step(*ids), 0, 0, 0)),
            pl.BlockSpec((1, G, N_HEADS_A, DK_A), lambda *ids: (0, seq_of_step(*ids), 0, 0)),
            pl.BlockSpec((1, n_seq, N_HEADS_A), lambda *ids: (0, 0, 0)),
            pl.BlockSpec((1, G, N_HEADS_B, DK_B, DV_B), lambda *ids: (0, seq_of_step(*ids), 0, 0, 0))]


def _mixer(z, gates, gbias, lb_logits, norm_a, norm_b, *, n_seq, seq_len, states):
    Tg, G = seq_len, SAMPLE_GROUP
    L = Tg * G
    tile = lambda w: pl.BlockSpec((Tg, G, w), lambda i: (0, i, 0))
    const_specs, const_args, levels, last_rows, n_mats = _mixer_args(
        L, Tg, True, lambda i: (0, 0), lambda i: (0, 0, 0), gbias, lb_logits, norm_a, norm_b)
    st_specs = _state_specs(G, n_seq, lambda i: i)
    kern = functools.partial(_mix_kernel, L=L, Tg=Tg, levels=levels, last_rows=last_rows)
    return pl.pallas_call(
        kern,
        grid=(n_seq // G,),
        in_specs=[tile(Z_MAIN), tile(LANE)] + const_specs + st_specs,
        out_specs=[tile(D_MODEL)] + st_specs,
        out_shape=[jax.ShapeDtypeStruct((Tg, n_seq, D_MODEL), F32)] + _state_shapes(n_seq),
        scratch_shapes=[pltpu.VMEM((n_mats * L, B_WIDTH), F32)],
        compiler_params=pltpu.CompilerParams(dimension_semantics=("arbitrary",), vmem_limit_bytes=VMEM_LIMIT),
        name="mixer_state",
    )(z, gates, *const_args, *states)


def _mixer_chunk(z, gates, gbias, lb_logits, norm_a, norm_b, *, n_seq, seq_len):
    L = PROMPT_CHUNK
    nb = PROMPT_SEQS_PER_STEP
    nchunks = seq_len // L
    tile = lambda w: pl.BlockSpec((nb, L, w), lambda b, c: (b, c, 0))
    const_specs, const_args, levels, last_rows, n_mats = _mixer_args(
        L, L, False, lambda b, c: (0, 0), lambda b, c: (0, 0, 0), gbias, lb_logits, norm_a, norm_b)
    st_specs = _state_specs(nb, n_seq, lambda b, c: b)
    kern = functools.partial(_mixchunk_kernel, L=L, levels=levels, last_rows=last_rows)
    outs = pl.pallas_call(
        kern,
        grid=(n_seq // nb, nchunks),
        in_specs=[tile(Z_MAIN), tile(LANE)] + const_specs,
        out_specs=[tile(D_MODEL)] + st_specs,
        out_shape=[jax.ShapeDtypeStruct((n_seq, seq_len, D_MODEL), BF16)] + _state_shapes(n_seq),
        scratch_shapes=[pltpu.VMEM((nb, n_mats * L, B_WIDTH), F32)],
        compiler_params=pltpu.CompilerParams(dimension_semantics=("arbitrary", "arbitrary"),
                                             vmem_limit_bytes=VMEM_LIMIT),
        name="mixer_chunk",
    )(z.reshape(n_seq, seq_len, Z_MAIN), gates.reshape(n_seq, seq_len, LANE), *const_args)
    return [outs[0].reshape(n_seq * seq_len, D_MODEL)] + list(outs[1:])


def _res_kernel(*refs, sub, seq_rows, tiles_per_seq, emit_h2, nk):
    if emit_h2:
        a_ref, w_ref, x_ref, gm_ref, lg_ref, lb_ref, sh_ref, sc_ref, o_ref, h_ref = refs
    else:
        a_ref, w_ref, x_ref, gm_ref, lg_ref, lb_ref, o_ref = refs
    i = pl.program_id(0)
    k = pl.program_id(1)

    if nk > 1:
        @pl.when(k == 0)
        def _():
            o_ref[...] = _dot(a_ref[...], w_ref[...])

    if nk > 2:
        @pl.when((k > 0) & (k < nk - 1))
        def _():
            o_ref[...] += _dot(a_ref[...], w_ref[...])

    @pl.when(k == nk - 1)
    def _():
        seq = i // tiles_per_seq
        gm = _mod_rows(gm_ref, seq_rows, seq)
        if emit_h2:
            sh = _mod_rows(sh_ref, seq_rows, seq)
            sc = _mod_rows(sc_ref, seq_rows, seq)
        for r in range(0, o_ref.shape[0], sub):
            rows = slice(r, r + sub)
            acc = _dot(a_ref[rows, :], w_ref[...])
            if nk > 1:
                acc = acc + o_ref[rows, :]
            x1 = _ln(ALPHA * x_ref[rows, :] + gm * acc) * lg_ref[...] + lb_ref[...]
            o_ref[rows, :] = x1
            if emit_h2:
                h_ref[rows, :] = (_ln(x1) * (1.0 + sc) + sh).astype(BF16)


def _res_block(a, w, x, mod, ln_g, ln_b, *, tm, bk, sub, seq_rows, seq_len, gate_col, emit_h2):
    tokens, kdim = a.shape
    nk = kdim // bk
    mod_spec = lambda c: pl.BlockSpec((mod.shape[0], D_MODEL), lambda i, k: (0, c))
    row_spec = pl.BlockSpec((tm, D_MODEL), lambda i, k: (i, 0))
    vec_spec = pl.BlockSpec((1, D_MODEL), lambda i, k: (0, 0))
    in_specs = [pl.BlockSpec((tm, bk), lambda i, k: (i, k)),
                pl.BlockSpec((bk, D_MODEL), lambda i, k: (k, 0)),
                row_spec, mod_spec(gate_col), vec_spec, vec_spec]
    args = [a, w, x, mod, ln_g, ln_b]
    out_specs = [row_spec]
    out_shape = [jax.ShapeDtypeStruct((tokens, D_MODEL), F32)]
    if emit_h2:
        in_specs += [mod_spec(3), mod_spec(4)]
        args += [mod, mod]
        out_specs.append(row_spec)
        out_shape.append(jax.ShapeDtypeStruct((tokens, D_MODEL), BF16))
    kern = functools.partial(_res_kernel, sub=sub, seq_rows=seq_rows, tiles_per_seq=max(seq_len // tm, 1),
                             emit_h2=emit_h2, nk=nk)
    return pl.pallas_call(
        kern,
        grid=(tokens // tm, nk),
        in_specs=in_specs,
        out_specs=out_specs,
        out_shape=out_shape,
        compiler_params=pltpu.CompilerParams(
            dimension_semantics=("arbitrary", "arbitrary"), vmem_limit_bytes=VMEM_LIMIT),
        name="res_block",
    )(*args)


def _up_kernel(*refs, tm, sub, seq_len, n_seq, t_major, tiles_per_seq):
    if t_major:
        h_ref, wa_scr, wu_scr, cw_ref, cb_ref, cache_ref, g_ref, tail_ref, abuf = refs
    else:
        (h_ref, wa_ref, ulo_ref, uhi_ref, wd_ref, cw_ref, cb_ref,
         g_ref, tail_ref, wa_scr, wu_scr, wd_out, carry) = refs
    j = pl.program_id(0)
    i = pl.program_id(1)
    bn = wa_scr.shape[1]
    valid = (j * bn + lax.broadcasted_iota(jnp.int32, (1, bn), 1)) < D_FF
    hist = (CONV_W - 1) * n_seq

    @pl.when(i == 0)
    def _():
        if t_major:
            abuf[0:hist, :] = jnp.where(valid, cache_ref[...].reshape(hist, bn), 0.0)
        else:
            split = D_FF % bn
            wu = jnp.concatenate([ulo_ref[:, split:], uhi_ref[:, :split]], axis=1)
            wa_scr[...] = jnp.where(valid, wa_ref[...], 0.0).astype(BF16)
            wu_scr[...] = jnp.where(valid, wu, 0.0).astype(BF16)
            wd_rows = j * bn + lax.broadcasted_iota(jnp.int32, (bn, 1), 0)
            wd_out[...] = jnp.where(wd_rows < D_FF, wd_ref[...], 0.0).astype(BF16)

    if not t_major:
        @pl.when(i % tiles_per_seq == 0)
        def _():
            carry[...] = jnp.zeros(carry.shape, F32)

    row8 = lax.broadcasted_iota(jnp.int32, (8, 1), 0)
    for r in range(0, tm, sub):
        h = h_ref[r:r + sub, :]
        a = _dot(h, wa_scr[...])
        u = _dot(h, wu_scr[...])
        if t_major:
            abuf[hist + r:hist + r + sub, :] = a
            a1 = abuf[hist + r - n_seq:hist + r - n_seq + sub, :]
            a2 = abuf[hist + r - 2 * n_seq:hist + r - 2 * n_seq + sub, :]
        else:
            p1 = carry[7:8, :]
            p2 = carry[6:7, :]
            r1 = pltpu.roll(a, 1, axis=0)
            r2 = pltpu.roll(a, 2, axis=0)
            a1 = jnp.concatenate([jnp.where(row8 == 0, p1, r1[0:8, :]), r1[8:, :]], axis=0)
            a2 = jnp.concatenate(
                [jnp.where(row8 == 0, p2, jnp.where(row8 == 1, p1, r2[0:8, :])), r2[8:, :]], axis=0)
            carry[...] = a[sub - 8:sub, :]
        conv = cb_ref[...] + cw_ref[0:1, :] * a2 + cw_ref[1:2, :] * a1 + cw_ref[2:3, :] * a
        gl = 0.5 * conv * (1.0 + lax.erf(conv * (2.0 ** -0.5)))
        g_ref[r:r + sub, :] = (gl * u).astype(BF16)

    if t_major:
        tail_ref[...] = abuf[tm:tm + hist, :].reshape(tail_ref.shape)
    else:
        @pl.when((i + 1) % tiles_per_seq == 0)
        def _():
            tail_ref[0] = carry[8 - (CONV_W - 1):8, :]


def _up_proj(h2, w_up, w_down, conv_w_p, conv_b_p, *, tm, sub, seq_len, n_seq, cache=None):
    tokens = h2.shape[0]
    t_major = cache is not None
    bn = FF_BLOCK
    nj = D_FF_PAD // bn
    u_lo = D_FF // bn
    u_last = (2 * D_FF - 1) // bn
    tiles_per_seq = max(seq_len // tm, 1)
    wspec = lambda f: pl.BlockSpec((D_MODEL, bn), f)
    col_spec = wspec(lambda j, i: (0, j))
    h_spec = pl.BlockSpec((tm, D_MODEL), lambda j, i: (i, 0))
    conv_specs = [pl.BlockSpec((CONV_W, bn), lambda j, i: (0, j)), pl.BlockSpec((1, bn), lambda j, i: (0, j))]
    out_specs = [pl.BlockSpec((tm, bn), lambda j, i: (i, j))]
    out_shape = [jax.ShapeDtypeStruct((tokens, D_FF_PAD), BF16)]
    if t_major:
        hist_spec = pl.BlockSpec((CONV_W - 1, n_seq, bn), lambda j, i: (0, 0, j))
        in_specs = [h_spec, col_spec, col_spec] + conv_specs + [hist_spec]
        args = [h2, w_up[0], w_up[1], conv_w_p, conv_b_p, cache]
        out_specs.append(hist_spec)
        out_shape.append(jax.ShapeDtypeStruct((CONV_W - 1, n_seq, D_FF_PAD), F32))
        scratch_rows = (CONV_W - 1) * n_seq + tm
    else:
        row_spec = pl.BlockSpec((bn, D_MODEL), lambda j, i: (j, 0))
        in_specs = [h_spec, col_spec,
                    wspec(lambda j, i: (0, u_lo + j)),
                    wspec(lambda j, i: (0, jnp.minimum(u_lo + 1 + j, u_last))),
                    row_spec] + conv_specs
        args = [h2, w_up, w_up, w_up, w_down, conv_w_p, conv_b_p]
        out_specs += [pl.BlockSpec((1, CONV_W - 1, bn), lambda j, i: (i // tiles_per_seq, 0, j)),
                      col_spec, col_spec, row_spec]
        out_shape += [jax.ShapeDtypeStruct((n_seq, CONV_W - 1, D_FF_PAD), F32),
                      jax.ShapeDtypeStruct((D_MODEL, D_FF_PAD), BF16),
                      jax.ShapeDtypeStruct((D_MODEL, D_FF_PAD), BF16),
                      jax.ShapeDtypeStruct((D_FF_PAD, D_MODEL), BF16)]
        scratch_rows = 8
    kern = functools.partial(_up_kernel, tm=tm, sub=sub, seq_len=seq_len, n_seq=n_seq, t_major=t_major,
                             tiles_per_seq=tiles_per_seq)
    outs = pl.pallas_call(
        kern,
        grid=(nj, tokens // tm),
        in_specs=in_specs,
        out_specs=out_specs,
        out_shape=out_shape,
        scratch_shapes=[pltpu.VMEM((scratch_rows, bn), F32)],
        compiler_params=pltpu.CompilerParams(
            dimension_semantics=("arbitrary", "arbitrary"), vmem_limit_bytes=VMEM_LIMIT),
        name="up_proj",
    )(*args)
    if t_major:
        return outs[0], outs[1], w_up, w_down
    return outs[0], outs[1], (outs[2], outs[3]), outs[4]


def _path(x, mod, w, *, n_seq, seq_len, tm_in, tm_res, tm_up, tm_down, states=None, cache=None):
    sample = states is not None
    sub = n_seq if sample else 256
    mixer_w = (w["gbias"], w["lb_logits"], w["norm_a"], w["norm_b"])
    z, gates = _in_proj(x, mod, w["w_main"], w["w_gate"], tm=tm_in, sub=sub, seq_rows=sample,
                        seq_len=seq_len, z_dtype=F32 if sample else BF16)
    if sample:
        mix, c1, n1, m1, s1 = _mixer(z.reshape(seq_len, n_seq, Z_MAIN), gates.reshape(seq_len, n_seq, LANE),
                                     *mixer_w, n_seq=n_seq, seq_len=seq_len, states=states)
        mix = mix.reshape(n_seq * seq_len, D_MODEL)
    else:
        mix, c1, n1, m1, s1 = _mixer_chunk(z, gates, *mixer_w, n_seq=n_seq, seq_len=seq_len)
    x1, h2 = _res_block(mix, w["w_out"], x, mod, w["ln1_g"], w["ln1_b"], tm=tm_res, bk=D_MODEL, sub=sub,
                        seq_rows=sample, seq_len=seq_len, gate_col=2, emit_h2=True)
    g, conv_out, w_up_bf, w_down_bf = _up_proj(h2, w["w_up"], w["w_down"], w["conv_w"], w["conv_b"], tm=tm_up,
                                               sub=sub, seq_len=seq_len, n_seq=n_seq, cache=cache)
    (x2,) = _res_block(g, w_down_bf, x1, mod, w["ln2_g"], w["ln2_b"], tm=tm_down, bk=D_FF_PAD // 4, sub=sub,
                       seq_rows=sample, seq_len=seq_len, gate_col=5, emit_h2=False)
    return x2, (c1, n1, m1, s1), conv_out, dict(w_up=w_up_bf, w_down=w_down_bf)


def kernel(x_prompt, x_sample, state_mlstm_C, state_mlstm_n, state_mlstm_m, state_hgrn_S, cache_ffn_conv,
           c_prompt, c_sample, hgrn_lb_logits, w_ada, b_ada, w_in, b_gate_a, norm_a, norm_b, w_out,
           ln1_g, ln1_b, w_up, conv_w, conv_b, w_down, ln2_g, ln2_b):
    bp, tp, _ = x_prompt.shape
    bs, ts, _ = x_sample.shape
    pad_ff = D_FF_PAD - D_FF
    w_main, w_gate = _prep_w_in(jnp.transpose(w_in[0]))
    weights = dict(
        w_main=w_main, w_gate=w_gate, w_out=w_out[0].astype(BF16), w_up=w_up[0],
        w_down=w_down[0],
        gbias=jnp.pad(b_gate_a[0].reshape(1, N_GATE), ((0, 0), (0, LANE - N_GATE))),
        lb_logits=hgrn_lb_logits, norm_a=norm_a, norm_b=norm_b,
        ln1_g=ln1_g, ln1_b=ln1_b, ln2_g=ln2_g, ln2_b=ln2_b,
        conv_w=jnp.pad(conv_w[0], ((0, 0), (0, pad_ff))),
        conv_b=jnp.pad(conv_b, ((0, 0), (0, pad_ff))))

    c_p = jnp.pad(c_prompt, ((0, 8 - bp), (0, 0)))
    mod_p, mod_s = _ada(c_p, c_sample, w_ada[0], b_ada)

    yp, st_p, tail_p, w_ffn_bf = _path(x_prompt.reshape(bp * tp, D_MODEL), mod_p, weights, n_seq=bp,
                                       seq_len=tp, tm_in=1024, tm_res=512, tm_up=1024, tm_down=1024)

    xs_t = jnp.swapaxes(x_sample, 0, 1).reshape(ts * bs, D_MODEL)
    cache_t = jnp.swapaxes(cache_ffn_conv[0], 0, 1)
    ys_t, st_s, tail_s, _ = _path(xs_t, mod_s, dict(weights, **w_ffn_bf), n_seq=bs, seq_len=ts,
                                  tm_in=ts * bs, tm_res=ts * bs, tm_up=ts * bs, tm_down=ts * bs,
                                  states=(state_mlstm_C, state_mlstm_n, state_mlstm_m, state_hgrn_S),
                                  cache=cache_t)

    ys = jnp.swapaxes(ys_t.reshape(ts, bs, D_MODEL), 0, 1)
    conv_p = tail_p[None, :, :, :D_FF]
    conv_s = jnp.swapaxes(tail_s[:, :, :D_FF], 0, 1)[None]
    return (yp.reshape(bp, tp, D_MODEL), ys,
            st_p[0], st_p[1], st_p[2], st_p[3], conv_p,
            st_s[0], st_s[1], st_s[2], st_s[3], conv_s)
```

```python
import functools

import numpy as np
import jax
import jax.numpy as jnp
from jax import lax
from jax.experimental import pallas as pl
from jax.experimental.pallas import tpu as pltpu

F32 = jnp.float32
BF16 = jnp.bfloat16

D_MODEL = 2048
N_HEADS_A, DK_A, DV_A = 4, 128, 256
N_HEADS_B, DK_B, DV_B = 8, 128, 128
A_WIDTH = N_HEADS_A * DV_A
B_WIDTH = N_HEADS_B * DV_B
N_GATE = 2 * N_HEADS_A
GATE_COL = 2 * N_HEADS_A * DK_A + A_WIDTH
D_IN = GATE_COL + N_GATE + A_WIDTH + 4 * B_WIDTH
D_FF = 5504
CONV_W = 3
EPS = 1e-5
ALPHA = 2.0 ** 0.25
LOG2_E = 1.4426950408889634
LANE = 128
FF_BLOCK = 512
D_FF_PAD = 5632
Z_MAIN = D_IN - N_GATE
OFF_QA, OFF_KA, OFF_VA, OFF_OA = 0, 512, 1024, 2048
OFF_FB, OFF_QB, OFF_VB, OFF_GB = 3072, 4096, 5120, 6144
VMEM_LIMIT = 56 * 1024 * 1024
PROMPT_CHUNK = 128
PROMPT_SEQS_PER_STEP = 2
SAMPLE_GROUP = 8


def _ln(x):
    mu = jnp.mean(x, axis=-1, keepdims=True)
    xc = x - mu
    var = jnp.mean(xc * xc, axis=-1, keepdims=True)
    return xc * lax.rsqrt(var + EPS)


def _sigmoid(x):
    return 1.0 / (1.0 + jnp.exp(-x))


def _dot(a, b):
    return jnp.dot(a.astype(BF16), b.astype(BF16), preferred_element_type=F32)


def _dot_nt(a, b):
    return lax.dot_general(a.astype(BF16), b.astype(BF16), (((1,), (1,)), ((), ())),
                           preferred_element_type=F32)


def _dot_tn(a, b):
    return lax.dot_general(a.astype(BF16), b.astype(BF16), (((0,), (0,)), ((), ())),
                           preferred_element_type=F32)


def _mod_rows(ref, seq_rows, seq):
    if seq_rows:
        return ref[...]
    return ref[pl.ds(seq, 1), :]


def _ada_kernel(cp_ref, cs_ref, w_ref, b_ref, op_ref, os_ref):
    w = w_ref[...].astype(BF16)
    for c_ref, o_ref in ((cp_ref, op_ref), (cs_ref, os_ref)):
        c = c_ref[...]
        o_ref[...] = _dot(c * _sigmoid(c), w) + b_ref[...]


def _ada(c_p, c_s, w_ada, b_ada):
    n = w_ada.shape[1]
    bn = 1024
    row = lambda r: pl.BlockSpec((r, D_MODEL), lambda j: (0, 0))
    out = lambda r: pl.BlockSpec((r, bn), lambda j: (0, j))
    return pl.pallas_call(
        _ada_kernel,
        grid=(n // bn,),
        in_specs=[row(c_p.shape[0]), row(c_s.shape[0]),
                  pl.BlockSpec((D_MODEL, bn), lambda j: (0, j)),
                  pl.BlockSpec((1, bn), lambda j: (0, j))],
        out_specs=[out(c_p.shape[0]), out(c_s.shape[0])],
        out_shape=[jax.ShapeDtypeStruct((c_p.shape[0], n), F32),
                   jax.ShapeDtypeStruct((c_s.shape[0], n), F32)],
        compiler_params=pltpu.CompilerParams(vmem_limit_bytes=VMEM_LIMIT),
        name="ada",
    )(c_p, c_s, w_ada, b_ada)


def _prep_in_kernel(front_ref, back_ref, gate_ref, o_ref, og_ref, *, first_shifted):
    j = pl.program_id(0)

    @pl.when(j == 0)
    def _():
        og_ref[...] = gate_ref[...].T.astype(BF16)

    def emit(src_ref):
        for r in range(0, src_ref.shape[0], 256):
            o_ref[:, r:r + 256] = src_ref[r:r + 256, :].T.astype(BF16)

    @pl.when(j < first_shifted)
    def _():
        emit(front_ref)

    @pl.when(j >= first_shifted)
    def _():
        emit(back_ref)


def _prep_w_in(w_in_t):
    bn = 1024
    first_shifted = GATE_COL // bn
    kern = functools.partial(_prep_in_kernel, first_shifted=first_shifted)
    return pl.pallas_call(
        kern,
        grid=(Z_MAIN // bn,),
        in_specs=[pl.BlockSpec((bn, D_MODEL), lambda j: (jnp.minimum(j, first_shifted - 1), 0)),
                  pl.BlockSpec((pl.Element(bn), pl.Element(D_MODEL)),
                               lambda j: (pl.multiple_of(jnp.maximum(j, first_shifted) * bn + N_GATE, N_GATE), 0)),
                  pl.BlockSpec((LANE, D_MODEL), lambda j: (GATE_COL // LANE, 0))],
        out_specs=[pl.BlockSpec((D_MODEL, bn), lambda j: (0, j)),
                   pl.BlockSpec((D_MODEL, LANE), lambda j: (0, 0))],
        out_shape=[jax.ShapeDtypeStruct((D_MODEL, Z_MAIN), BF16),
                   jax.ShapeDtypeStruct((D_MODEL, LANE), BF16)],
        compiler_params=pltpu.CompilerParams(vmem_limit_bytes=VMEM_LIMIT),
        name="prep_w_in",
    )(w_in_t, w_in_t, w_in_t)


def _in_kernel(x_ref, sh_ref, sc_ref, w_ref, wg_ref, z_ref, g_ref, h_scr, *, sub, seq_rows, tiles_per_seq):
    i = pl.program_id(0)
    j = pl.program_id(1)

    @pl.when(j == 0)
    def _():
        seq = i // tiles_per_seq
        sh = _mod_rows(sh_ref, seq_rows, seq)
        sc = _mod_rows(sc_ref, seq_rows, seq)
        for r in range(0, x_ref.shape[0], sub):
            rows = slice(r, r + sub)
            h = (_ln(x_ref[rows, :]) * (1.0 + sc) + sh).astype(BF16)
            h_scr[rows, :] = h
            g_ref[rows, :] = _dot(h, wg_ref[...])
            z_ref[rows, :] = _dot(h, w_ref[...]).astype(z_ref.dtype)

    @pl.when(j > 0)
    def _():
        z_ref[...] = _dot(h_scr[...], w_ref[...]).astype(z_ref.dtype)


def _in_proj(x, mod, w_main, w_gate, *, tm, sub, seq_rows, seq_len, z_dtype):
    tokens = x.shape[0]
    bn = 1024
    mod_spec = lambda k: pl.BlockSpec((mod.shape[0], D_MODEL), lambda i, j: (0, k))
    kern = functools.partial(_in_kernel, sub=min(sub, tm), seq_rows=seq_rows, tiles_per_seq=max(seq_len // tm, 1))
    return pl.pallas_call(
        kern,
        grid=(tokens // tm, Z_MAIN // bn),
        in_specs=[pl.BlockSpec((tm, D_MODEL), lambda i, j: (i, 0)),
                  mod_spec(0), mod_spec(1),
                  pl.BlockSpec((D_MODEL, bn), lambda i, j: (0, j)),
                  pl.BlockSpec((D_MODEL, LANE), lambda i, j: (0, 0))],
        out_specs=[pl.BlockSpec((tm, bn), lambda i, j: (i, j)),
                   pl.BlockSpec((tm, LANE), lambda i, j: (i, 0))],
        out_shape=[jax.ShapeDtypeStruct((tokens, Z_MAIN), z_dtype),
                   jax.ShapeDtypeStruct((tokens, LANE), F32)],
        scratch_shapes=[pltpu.VMEM((tm, D_MODEL), BF16)],
        compiler_params=pltpu.CompilerParams(
            dimension_semantics=("arbitrary", "arbitrary"), vmem_limit_bytes=VMEM_LIMIT),
        name="in_proj",
    )(x, mod, mod, w_main, w_gate)


def _mix_consts(L, Tg, t_major):
    G = L // Tg
    r = np.arange(L)
    seq, tim = (r % G, r // G) if t_major else (r // Tg, r % Tg)
    same = seq[:, None] == seq[None, :]
    t = tim[:, None]
    u = tim[None, :]
    mats = [same & (u <= t), same & (u > t)]
    masks = [same & (u <= t), same & (t <= u), same]
    levels = []
    m = Tg // 2
    while m >= 1:
        levels.append(m)
        m //= 2
    for m in levels:
        blk = same & (u // m == t // m)
        odd = (t // m) % 2 == 1
        mats.append(np.where(odd, blk & (u <= t), blk & (u > t)))
        masks.append(same & odd & ((u // m) % 2 == 0) & (t // (2 * m) == u // (2 * m)))
    mats = np.concatenate(mats, axis=0).astype(np.float32)
    masks = np.stack(masks).astype(np.float32)
    last_rows = tuple(int(np.nonzero((seq == g) & (tim == Tg - 1))[0][0]) for g in range(G))
    return mats, masks, tuple(levels), last_rows


def _mix_kernel(z_ref, g_ref, gbias_ref, lbl_ref, na_ref, nb_ref, mall_ref, msk_ref,
                c_in, n_in, m_in, s_in, mix_ref, c_out, n_out, m_out, s_out, e_scr,
                *, L, Tg, levels, last_rows):
    G = L // Tg

    def zs(off, w):
        return z_ref[:, :, off:off + w].reshape(L, w)

    def put_mix(off, w, val):
        mix_ref[:, :, off:off + w] = val.astype(mix_ref.dtype).reshape(Tg, G, w)

    _mix_body(zs, put_mix, g_ref[...].reshape(L, LANE), gbias_ref, lbl_ref, na_ref, nb_ref, mall_ref, msk_ref,
              c_in, n_in, m_in, s_in, c_out, n_out, m_out, s_out, e_scr, pl.program_id(0) * G,
              L=L, Tg=Tg, levels=levels, last_rows=last_rows, t_major=True)


def _mixchunk_kernel(z_ref, g_ref, gbias_ref, lbl_ref, na_ref, nb_ref, mall_ref, msk_ref,
                     mix_ref, c_out, n_out, m_out, s_out, e_scr, *, L, levels, last_rows):
    nb = z_ref.shape[0]
    b0 = pl.program_id(0) * nb

    @pl.when(pl.program_id(1) == 0)
    def _():
        c_out[...] = jnp.zeros(c_out.shape, F32)
        n_out[...] = jnp.zeros(n_out.shape, F32)
        s_out[...] = jnp.zeros(s_out.shape, F32)
        m_out[0, pl.ds(b0, nb), :] = jnp.zeros((nb, N_HEADS_A), F32)

    for k in range(nb):
        def zs(off, w, k=k):
            return z_ref[k, :, off:off + w]

        def put_mix(off, w, val, k=k):
            mix_ref[k, :, off:off + w] = val.astype(mix_ref.dtype)

        st = [r.at[:, pl.ds(k, 1)] for r in (c_out, n_out, s_out)]
        _mix_body(zs, put_mix, g_ref[k], gbias_ref, lbl_ref, na_ref, nb_ref, mall_ref, msk_ref,
                  st[0], st[1], m_out, st[2], st[0], st[1], m_out, st[2], e_scr.at[k], b0 + k,
                  L=L, Tg=L, levels=levels, last_rows=last_rows, t_major=False)


def _mix_body(zs, put_mix, gates, gbias_ref, lbl_ref, na_ref, nb_ref, mall_ref, msk_ref,
              c_in, n_in, m_in, s_in, c_out, n_out, m_out, s_out, e_scr, m_row0,
              *, L, Tg, levels, last_rows, t_major):
    G = L // Tg
    neg_inf = F32(-jnp.inf)
    causal = msk_ref[0] > 0.5
    causal_t = msk_ref[1] > 0.5
    same = msk_ref[2] > 0.5
    row1 = lax.broadcasted_iota(jnp.int32, (L, 1), 0)
    seq_of_row = (row1 & (G - 1)) if t_major else (row1 // Tg)
    in_group = [seq_of_row == g for g in range(G)]
    col1 = lax.broadcasted_iota(jnp.int32, (1, L), 1)
    seq_of_col = (col1 & (G - 1)) if t_major else (col1 // Tg)
    in_group_lane = [seq_of_col == g for g in range(G)]

    def state_updates(kw, v):
        if G == 1:
            return [_dot_tn(kw, v)]
        kw_t = kw.T
        return [_dot(jnp.where(in_group_lane[g], kw_t, 0.0), v) for g in range(G)]

    def by_group(vals):
        if G == 1:
            return vals[0]
        out = jnp.where(in_group[0], vals[0], 0.0)
        for g in range(1, G):
            out = out + jnp.where(in_group[g], vals[g], 0.0)
        return out

    pre = gates + gbias_ref[...]
    lsig = jnp.minimum(pre, 0.0) - jnp.log(1.0 + jnp.exp(-jnp.abs(pre)))
    pre_t = pre.T
    lsig_t = lsig.T
    m_prev = m_in[0, pl.ds(m_row0, G), :]
    scale = DK_A ** -0.5
    m_new_rows = []
    for h in range(N_HEADS_A):
        q = zs(OFF_QA + DK_A * h, DK_A)
        k = zs(OFF_KA + DK_A * h, DK_A)
        v = zs(OFF_VA + DV_A * h, DV_A)
        logi_c = pre[:, h:h + 1]
        logi_r = pre_t[h:h + 1, :]
        lf_c = lsig[:, N_HEADS_A + h:N_HEADS_A + h + 1]
        lf_r = lsig_t[N_HEADS_A + h:N_HEADS_A + h + 1, :]
        b_c = jnp.sum(jnp.where(causal, lf_r, 0.0), axis=1, keepdims=True)
        b_r = jnp.sum(jnp.where(causal_t, lf_c, 0.0), axis=0, keepdims=True)
        tot_c = jnp.sum(jnp.where(same, lf_r, 0.0), axis=1, keepdims=True)
        tot_r = jnp.sum(jnp.where(same, lf_c, 0.0), axis=0, keepdims=True)
        m_col = by_group([m_prev[g:g + 1, h:h + 1] for g in range(G)])
        dmat = jnp.where(causal, b_c - b_r + logi_r, neg_inf)
        inter = b_c + m_col
        m_t = jnp.maximum(inter, jnp.max(dmat, axis=1, keepdims=True))
        w_inter = jnp.exp(inter - m_t)
        smat = _dot_nt(q, k) * (scale * jnp.exp(dmat - m_t))
        q_c = by_group([_dot(q, c_in[0, g, h]) for g in range(G)]) * scale
        n_rows = by_group([n_in[0, g, h:h + 1, :] for g in range(G)])
        q_n = jnp.sum(q.astype(F32) * n_rows, axis=1, keepdims=True) * scale
        num = w_inter * q_c + _dot(smat, v)
        den = w_inter * q_n + jnp.sum(smat, axis=1, keepdims=True)
        hh = num / jnp.maximum(jnp.abs(den), jnp.exp(-m_t))
        ms = jnp.mean(hh * hh, axis=1, keepdims=True)
        oa = zs(OFF_OA + DV_A * h, DV_A).astype(F32)
        ya = hh * lax.rsqrt(ms + EPS) * na_ref[:, DV_A * h:DV_A * (h + 1)] * _sigmoid(oa)
        put_mix(DV_A * h, DV_A, ya)
        dec_c = tot_c - b_c + logi_c
        dec_r = tot_r - b_r + logi_r
        m_new_c = jnp.maximum(tot_c + m_col,
                              jnp.max(jnp.where(same, dec_r, neg_inf), axis=1, keepdims=True))
        wk = jnp.exp(dec_c - m_new_c)
        sc = jnp.exp(tot_c + m_col - m_new_c)
        kw = k.astype(F32) * wk
        upd = state_updates(kw, v)
        m_new_h = []
        for g in range(G):
            last = last_rows[g]
            kg = kw if G == 1 else jnp.where(in_group[g], kw, 0.0)
            sc_g = sc[last:last + 1, :]
            c_new = sc_g * c_in[0, g, h] + upd[g]
            n_new = sc_g * n_in[0, g, h:h + 1, :] + jnp.sum(kg, axis=0, keepdims=True)
            c_out[0, g, h] = c_new
            n_out[0, g, h:h + 1, :] = n_new
            m_new_h.append(m_new_c[last:last + 1, :])
        m_new_rows.append(m_new_h)
    for g in range(G):
        m_out[0, pl.ds(m_row0 + g, 1), :] = jnp.concatenate(
            [m_new_rows[h][g] for h in range(N_HEADS_A)], axis=1)

    l0 = lbl_ref[0:1, :]
    l1 = lbl_ref[1:2, :]
    lmax = jnp.maximum(l0, l1)
    e0 = jnp.exp(l0 - lmax)
    e1 = jnp.exp(l1 - lmax)
    lb = e0 / (e0 + e1)
    fb = zs(OFF_FB, B_WIDTH).astype(F32)
    e = jnp.exp(-jnp.abs(fb))
    r = 1.0 / (1.0 + e)
    pos = fb >= 0.0
    sig = jnp.where(pos, r, e * r)
    nsig = jnp.where(pos, e * r, r)
    logf = jnp.log(lb + (1.0 - lb) * sig) * LOG2_E
    kb = (1.0 - lb) * nsig
    hi = logf.astype(BF16)
    r1 = logf - hi.astype(F32)
    mid = r1.astype(BF16)
    lo = (r1 - mid.astype(F32)).astype(BF16)
    mall = mall_ref[...]
    if L % 16 == 0:
        e_scr[...] = _dot(mall, jnp.concatenate([hi, mid, lo], axis=0))
    else:
        mall = mall[:, :L]
        e_scr[...] = _dot(mall, hi) + _dot(mall, mid) + _dot(mall, lo)

    dec_t = []
    for g in range(G):
        last = last_rows[g]
        bl = e_scr[last:last + 1, :]
        bl8 = jnp.concatenate([bl[:, DK_B * h:DK_B * (h + 1)] for h in range(N_HEADS_B)], axis=0)
        dec_t.append(jnp.exp2(bl8).T)

    for h in range(N_HEADS_B):
        sl = slice(DK_B * h, DK_B * (h + 1))
        q = zs(OFF_QB + DK_B * h, DK_B).astype(BF16)
        k = kb[:, sl]
        kbf = k.astype(BF16)
        v = zs(OFF_VB + DV_B * h, DV_B)
        amat = jnp.zeros((L, L), F32)
        for li in range(len(levels)):
            ex = jnp.exp2(e_scr[(2 + li) * L:(3 + li) * L, sl]).astype(BF16)
            amat = jnp.where(msk_ref[3 + li] > 0.5, _dot_nt(q * ex, kbf * ex), amat)
        qf = q.astype(F32)
        diag = jnp.sum(qf * k, axis=1, keepdims=True)
        qs = qf * jnp.exp2(e_scr[0:L, sl])
        o_inter = by_group([_dot(qs, s_in[0, g, h]) for g in range(G)])
        o = _dot(amat, v) + diag * v.astype(F32) + o_inter
        ms = jnp.mean(o * o, axis=1, keepdims=True)
        gb = zs(OFF_GB + DV_B * h, DV_B).astype(F32)
        yb = o * lax.rsqrt(ms + EPS) * nb_ref[:, sl] * (gb * _sigmoid(gb))
        put_mix(A_WIDTH + DV_B * h, DV_B, yb)
        upd = state_updates(k * jnp.exp2(e_scr[L:2 * L, sl]), v)
        for g in range(G):
            s_out[0, g, h] = dec_t[g][:, h:h + 1] * s_in[0, g, h] + upd[g]


def _mixer_args(L, Tg, t_major, const2, const3, gbias, lb_logits, norm_a, norm_b):
    mats, masks, levels, last_rows = _mix_consts(L, Tg, t_major)
    n_mats = mats.shape[0] // L
    if L % 16 == 0:
        mats = np.concatenate([mats, mats, mats], axis=1)
    specs = [pl.BlockSpec((1, LANE), const2),
             pl.BlockSpec((2, B_WIDTH), const2),
             pl.BlockSpec((1, A_WIDTH), const2),
             pl.BlockSpec((1, B_WIDTH), const2),
             pl.BlockSpec(mats.shape, const2),
             pl.BlockSpec(masks.shape, const3)]
    args = [gbias, lb_logits, norm_a, norm_b, jnp.asarray(mats, BF16), jnp.asarray(masks)]
    return specs, args, levels, last_rows, n_mats


def _state_shapes(n_seq):
    return [jax.ShapeDtypeStruct((1, n_seq, N_HEADS_A, DK_A, DV_A), F32),
            jax.ShapeDtypeStruct((1, n_seq, N_HEADS_A, DK_A), F32),
            jax.ShapeDtypeStruct((1, n_seq, N_HEADS_A), F32),
            jax.ShapeDtypeStruct((1, n_seq, N_HEADS_B, DK_B, DV_B), F32)]


def _state_specs(G, n_seq, seq_of_step):
    return [pl.BlockSpec((1, G, N_HEADS_A, DK_A, DV_A), lambda *ids: (0, seq_of_step(*ids), 0, 0, 0)),
            pl.BlockSpec((1, G, N_HEADS_A, DK_A), lambda *ids: (0, seq_of_step(*ids), 0, 0)),
            pl.BlockSpec((1, n_seq, N_HEADS_A), lambda *ids: (0, 0, 0)),
            pl.BlockSpec((1, G, N_HEADS_B, DK_B, DV_B), lambda *ids: (0, seq_of_step(*ids), 0, 0, 0))]


def _mixer(z, gates, gbias, lb_logits, norm_a, norm_b, *, n_seq, seq_len, states):
    Tg, G = seq_len, SAMPLE_GROUP
    L = Tg * G
    tile = lambda w: pl.BlockSpec((Tg, G, w), lambda i: (0, i, 0))
    const_specs, const_args, levels, last_rows, n_mats = _mixer_args(
        L, Tg, True, lambda i: (0, 0), lambda i: (0, 0, 0), gbias, lb_logits, norm_a, norm_b)
    st_specs = _state_specs(G, n_seq, lambda i: i)
    kern = functools.partial(_mix_kernel, L=L, Tg=Tg, levels=levels, last_rows=last_rows)
    return pl.pallas_call(
        kern,
        grid=(n_seq // G,),
        in_specs=[tile(Z_MAIN), tile(LANE)] + const_specs + st_specs,
        out_specs=[tile(D_MODEL)] + st_specs,
        out_shape=[jax.ShapeDtypeStruct((Tg, n_seq, D_MODEL), F32)] + _state_shapes(n_seq),
        scratch_shapes=[pltpu.VMEM((n_mats * L, B_WIDTH), F32)],
        compiler_params=pltpu.CompilerParams(dimension_semantics=("arbitrary",), vmem_limit_bytes=VMEM_LIMIT),
        name="mixer_state",
    )(z, gates, *const_args, *states)


def _mixer_chunk(z, gates, gbias, lb_logits, norm_a, norm_b, *, n_seq, seq_len):
    L = PROMPT_CHUNK
    nb = PROMPT_SEQS_PER_STEP
    nchunks = seq_len // L
    tile = lambda w: pl.BlockSpec((nb, L, w), lambda b, c: (b, c, 0))
    const_specs, const_args, levels, last_rows, n_mats = _mixer_args(
        L, L, False, lambda b, c: (0, 0), lambda b, c: (0, 0, 0), gbias, lb_logits, norm_a, norm_b)
    st_specs = _state_specs(nb, n_seq, lambda b, c: b)
    kern = functools.partial(_mixchunk_kernel, L=L, levels=levels, last_rows=last_rows)
    outs = pl.pallas_call(
        kern,
        grid=(n_seq // nb, nchunks),
        in_specs=[tile(Z_MAIN), tile(LANE)] + const_specs,
        out_specs=[tile(D_MODEL)] + st_specs,
        out_shape=[jax.ShapeDtypeStruct((n_seq, seq_len, D_MODEL), BF16)] + _state_shapes(n_seq),
        scratch_shapes=[pltpu.VMEM((nb, n_mats * L, B_WIDTH), F32)],
        compiler_params=pltpu.CompilerParams(dimension_semantics=("arbitrary", "arbitrary"),
                                             vmem_limit_bytes=VMEM_LIMIT),
        name="mixer_chunk",
    )(z.reshape(n_seq, seq_len, Z_MAIN), gates.reshape(n_seq, seq_len, LANE), *const_args)
    return [outs[0].reshape(n_seq * seq_len, D_MODEL)] + list(outs[1:])


def _res_kernel(*refs, sub, seq_rows, tiles_per_seq, emit_h2, nk):
    if emit_h2:
        a_ref, w_ref, x_ref, gm_ref, lg_ref, lb_ref, sh_ref, sc_ref, o_ref, h_ref = refs
    else:
        a_ref, w_ref, x_ref, gm_ref, lg_ref, lb_ref, o_ref = refs
    i = pl.program_id(0)
    k = pl.program_id(1)

    if nk > 1:
        @pl.when(k == 0)
        def _():
            o_ref[...] = _dot(a_ref[...], w_ref[...])

    if nk > 2:
        @pl.when((k > 0) & (k < nk - 1))
        def _():
            o_ref[...] += _dot(a_ref[...], w_ref[...])

    @pl.when(k == nk - 1)
    def _():
        seq = i // tiles_per_seq
        gm = _mod_rows(gm_ref, seq_rows, seq)
        if emit_h2:
            sh = _mod_rows(sh_ref, seq_rows, seq)
            sc = _mod_rows(sc_ref, seq_rows, seq)
        for r in range(0, o_ref.shape[0], sub):
            rows = slice(r, r + sub)
            acc = _dot(a_ref[rows, :], w_ref[...])
            if nk > 1:
                acc = acc + o_ref[rows, :]
            x1 = _ln(ALPHA * x_ref[rows, :] + gm * acc) * lg_ref[...] + lb_ref[...]
            o_ref[rows, :] = x1
            if emit_h2:
                h_ref[rows, :] = (_ln(x1) * (1.0 + sc) + sh).astype(BF16)


def _res_block(a, w, x, mod, ln_g, ln_b, *, tm, bk, sub, seq_rows, seq_len, gate_col, emit_h2):
    tokens, kdim = a.shape
    nk = kdim // bk
    mod_spec = lambda c: pl.BlockSpec((mod.shape[0], D_MODEL), lambda i, k: (0, c))
    row_spec = pl.BlockSpec((tm, D_MODEL), lambda i, k: (i, 0))
    vec_spec = pl.BlockSpec((1, D_MODEL), lambda i, k: (0, 0))
    in_specs = [pl.BlockSpec((tm, bk), lambda i, k: (i, k)),
                pl.BlockSpec((bk, D_MODEL), lambda i, k: (k, 0)),
                row_spec, mod_spec(gate_col), vec_spec, vec_spec]
    args = [a, w, x, mod, ln_g, ln_b]
    out_specs = [row_spec]
    out_shape = [jax.ShapeDtypeStruct((tokens, D_MODEL), F32)]
    if emit_h2:
        in_specs += [mod_spec(3), mod_spec(4)]
        args += [mod, mod]
        out_specs.append(row_spec)
        out_shape.append(jax.ShapeDtypeStruct((tokens, D_MODEL), BF16))
    kern = functools.partial(_res_kernel, sub=sub, seq_rows=seq_rows, tiles_per_seq=max(seq_len // tm, 1),
                             emit_h2=emit_h2, nk=nk)
    return pl.pallas_call(
        kern,
        grid=(tokens // tm, nk),
        in_specs=in_specs,
        out_specs=out_specs,
        out_shape=out_shape,
        compiler_params=pltpu.CompilerParams(
            dimension_semantics=("arbitrary", "arbitrary"), vmem_limit_bytes=VMEM_LIMIT),
        name="res_block",
    )(*args)


def _up_kernel(*refs, tm, sub, seq_len, n_seq, t_major, tiles_per_seq):
    if t_major:
        h_ref, wa_scr, wu_scr, cw_ref, cb_ref, cache_ref, g_ref, tail_ref, abuf = refs
    else:
        (h_ref, wa_ref, ulo_ref, uhi_ref, wd_ref, cw_ref, cb_ref,
         g_ref, tail_ref, wa_scr, wu_scr, wd_out, carry) = refs
    j = pl.program_id(0)
    i = pl.program_id(1)
    bn = wa_scr.shape[1]
    valid = (j * bn + lax.broadcasted_iota(jnp.int32, (1, bn), 1)) < D_FF
    hist = (CONV_W - 1) * n_seq

    @pl.when(i == 0)
    def _():
        if t_major:
            abuf[0:hist, :] = jnp.where(valid, cache_ref[...].reshape(hist, bn), 0.0)
        else:
            split = D_FF % bn
            wu = jnp.concatenate([ulo_ref[:, split:], uhi_ref[:, :split]], axis=1)
            wa_scr[...] = jnp.where(valid, wa_ref[...], 0.0).astype(BF16)
            wu_scr[...] = jnp.where(valid, wu, 0.0).astype(BF16)
            wd_rows = j * bn + lax.broadcasted_iota(jnp.int32, (bn, 1), 0)
            wd_out[...] = jnp.where(wd_rows < D_FF, wd_ref[...], 0.0).astype(BF16)

    if not t_major:
        @pl.when(i % tiles_per_seq == 0)
        def _():
            carry[...] = jnp.zeros(carry.shape, F32)

    row8 = lax.broadcasted_iota(jnp.int32, (8, 1), 0)
    for r in range(0, tm, sub):
        h = h_ref[r:r + sub, :]
        a = _dot(h, wa_scr[...])
        u = _dot(h, wu_scr[...])
        if t_major:
            abuf[hist + r:hist + r + sub, :] = a
            a1 = abuf[hist + r - n_seq:hist + r - n_seq + sub, :]
            a2 = abuf[hist + r - 2 * n_seq:hist + r - 2 * n_seq + sub, :]
        else:
            p1 = carry[7:8, :]
            p2 = carry[6:7, :]
            r1 = pltpu.roll(a, 1, axis=0)
            r2 = pltpu.roll(a, 2, axis=0)
            a1 = jnp.concatenate([jnp.where(row8 == 0, p1, r1[0:8, :]), r1[8:, :]], axis=0)
            a2 = jnp.concatenate(
                [jnp.where(row8 == 0, p2, jnp.where(row8 == 1, p1, r2[0:8, :])), r2[8:, :]], axis=0)
            carry[...] = a[sub - 8:sub, :]
        conv = cb_ref[...] + cw_ref[0:1, :] * a2 + cw_ref[1:2, :] * a1 + cw_ref[2:3, :] * a
        gl = 0.5 * conv * (1.0 + lax.erf(conv * (2.0 ** -0.5)))
        g_ref[r:r + sub, :] = (gl * u).astype(BF16)

    if t_major:
        tail_ref[...] = abuf[tm:tm + hist, :].reshape(tail_ref.shape)
    else:
        @pl.when((i + 1) % tiles_per_seq == 0)
        def _():
            tail_ref[0] = carry[8 - (CONV_W - 1):8, :]


def _up_proj(h2, w_up, w_down, conv_w_p, conv_b_p, *, tm, sub, seq_len, n_seq, cache=None):
    tokens = h2.shape[0]
    t_major = cache is not None
    bn = FF_BLOCK
    nj = D_FF_PAD // bn
    u_lo = D_FF // bn
    u_last = (2 * D_FF - 1) // bn
    tiles_per_seq = max(seq_len // tm, 1)
    wspec = lambda f: pl.BlockSpec((D_MODEL, bn), f)
    col_spec = wspec(lambda j, i: (0, j))
    h_spec = pl.BlockSpec((tm, D_MODEL), lambda j, i: (i, 0))
    conv_specs = [pl.BlockSpec((CONV_W, bn), lambda j, i: (0, j)), pl.BlockSpec((1, bn), lambda j, i: (0, j))]
    out_specs = [pl.BlockSpec((tm, bn), lambda j, i: (i, j))]
    out_shape = [jax.ShapeDtypeStruct((tokens, D_FF_PAD), BF16)]
    if t_major:
        hist_spec = pl.BlockSpec((CONV_W - 1, n_seq, bn), lambda j, i: (0, 0, j))
        in_specs = [h_spec, col_spec, col_spec] + conv_specs + [hist_spec]
        args = [h2, w_up[0], w_up[1], conv_w_p, conv_b_p, cache]
        out_specs.append(hist_spec)
        out_shape.append(jax.ShapeDtypeStruct((CONV_W - 1, n_seq, D_FF_PAD), F32))
        scratch_rows = (CONV_W - 1) * n_seq + tm
    else:
        row_spec = pl.BlockSpec((bn, D_MODEL), lambda j, i: (j, 0))
        in_specs = [h_spec, col_spec,
                    wspec(lambda j, i: (0, u_lo + j)),
                    wspec(lambda j, i: (0, jnp.minimum(u_lo + 1 + j, u_last))),
                    row_spec] + conv_specs
        args = [h2, w_up, w_up, w_up, w_down, conv_w_p, conv_b_p]
        out_specs += [pl.BlockSpec((1, CONV_W - 1, bn), lambda j, i: (i // tiles_per_seq, 0, j)),
                      col_spec, col_spec, row_spec]
        out_shape += [jax.ShapeDtypeStruct((n_seq, CONV_W - 1, D_FF_PAD), F32),
                      jax.ShapeDtypeStruct((D_MODEL, D_FF_PAD), BF16),
                      jax.ShapeDtypeStruct((D_MODEL, D_FF_PAD), BF16),
                      jax.ShapeDtypeStruct((D_FF_PAD, D_MODEL), BF16)]
        scratch_rows = 8
    kern = functools.partial(_up_kernel, tm=tm, sub=sub, seq_len=seq_len, n_seq=n_seq, t_major=t_major,
                             tiles_per_seq=tiles_per_seq)
    outs = pl.pallas_call(
        kern,
        grid=(nj, tokens // tm),
        in_specs=in_specs,
        out_specs=out_specs,
        out_shape=out_shape,
        scratch_shapes=[pltpu.VMEM((scratch_rows, bn), F32)],
        compiler_params=pltpu.CompilerParams(
            dimension_semantics=("arbitrary", "arbitrary"), vmem_limit_bytes=VMEM_LIMIT),
        name="up_proj",
    )(*args)
    if t_major:
        return outs[0], outs[1], w_up, w_down
    return outs[0], outs[1], (outs[2], outs[3]), outs[4]


def _path(x, mod, w, *, n_seq, seq_len, tm_in, tm_res, tm_up, tm_down, states=None, cache=None):
    sample = states is not None
    sub = n_seq if sample else 256
    mixer_w = (w["gbias"], w["lb_logits"], w["norm_a"], w["norm_b"])
    z, gates = _in_proj(x, mod, w["w_main"], w["w_gate"], tm=tm_in, sub=sub, seq_rows=sample,
                        seq_len=seq_len, z_dtype=F32 if sample else BF16)
    if sample:
        mix, c1, n1, m1, s1 = _mixer(z.reshape(seq_len, n_seq, Z_MAIN), gates.reshape(seq_len, n_seq, LANE),
                                     *mixer_w, n_seq=n_seq, seq_len=seq_len, states=states)
        mix = mix.reshape(n_seq * seq_len, D_MODEL)
    else:
        mix, c1, n1, m1, s1 = _mixer_chunk(z, gates, *mixer_w, n_seq=n_seq, seq_len=seq_len)
    x1, h2 = _res_block(mix, w["w_out"], x, mod, w["ln1_g"], w["ln1_b"], tm=tm_res, bk=D_MODEL, sub=sub,
                        seq_rows=sample, seq_len=seq_len, gate_col=2, emit_h2=True)
    g, conv_out, w_up_bf, w_down_bf = _up_proj(h2, w["w_up"], w["w_down"], w["conv_w"], w["conv_b"], tm=tm_up,
                                               sub=sub, seq_len=seq_len, n_seq=n_seq, cache=cache)
    (x2,) = _res_block(g, w_down_bf, x1, mod, w["ln2_g"], w["ln2_b"], tm=tm_down, bk=D_FF_PAD // 2, sub=sub,
                       seq_rows=sample, seq_len=seq_len, gate_col=5, emit_h2=False)
    return x2, (c1, n1, m1, s1), conv_out, dict(w_up=w_up_bf, w_down=w_down_bf)


def kernel(x_prompt, x_sample, state_mlstm_C, state_mlstm_n, state_mlstm_m, state_hgrn_S, cache_ffn_conv,
           c_prompt, c_sample, hgrn_lb_logits, w_ada, b_ada, w_in, b_gate_a, norm_a, norm_b, w_out,
           ln1_g, ln1_b, w_up, conv_w, conv_b, w_down, ln2_g, ln2_b):
    bp, tp, _ = x_prompt.shape
    bs, ts, _ = x_sample.shape
    pad_ff = D_FF_PAD - D_FF
    w_main, w_gate = _prep_w_in(jnp.transpose(w_in[0]))
    weights = dict(
        w_main=w_main, w_gate=w_gate, w_out=w_out[0].astype(BF16), w_up=w_up[0],
        w_down=w_down[0],
        gbias=jnp.pad(b_gate_a[0].reshape(1, N_GATE), ((0, 0), (0, LANE - N_GATE))),
        lb_logits=hgrn_lb_logits, norm_a=norm_a, norm_b=norm_b,
        ln1_g=ln1_g, ln1_b=ln1_b, ln2_g=ln2_g, ln2_b=ln2_b,
        conv_w=jnp.pad(conv_w[0], ((0, 0), (0, pad_ff))),
        conv_b=jnp.pad(conv_b, ((0, 0), (0, pad_ff))))

    c_p = jnp.pad(c_prompt, ((0, 8 - bp), (0, 0)))
    mod_p, mod_s = _ada(c_p, c_sample, w_ada[0], b_ada)

    yp, st_p, tail_p, w_ffn_bf = _path(x_prompt.reshape(bp * tp, D_MODEL), mod_p, weights, n_seq=bp,
                                       seq_len=tp, tm_in=1024, tm_res=512, tm_up=1024, tm_down=512)

    xs_t = jnp.swapaxes(x_sample, 0, 1).reshape(ts * bs, D_MODEL)
    cache_t = jnp.swapaxes(cache_ffn_conv[0], 0, 1)
    ys_t, st_s, tail_s, _ = _path(xs_t, mod_s, dict(weights, **w_ffn_bf), n_seq=bs, seq_len=ts,
                                  tm_in=ts * bs, tm_res=ts * bs, tm_up=ts * bs, tm_down=ts * bs,
                                  states=(state_mlstm_C, state_mlstm_n, state_mlstm_m, state_hgrn_S),
                                  cache=cache_t)

    ys = jnp.swapaxes(ys_t.reshape(ts, bs, D_MODEL), 0, 1)
    conv_p = tail_p[None, :, :, :D_FF]
    conv_s = jnp.swapaxes(tail_s[:, :, :D_FF], 0, 1)[None]
    return (yp.reshape(bp, tp, D_MODEL), ys,
            st_p[0], st_p[1], st_p[2], st_p[3], conv_p,
            st_s[0], st_s[1], st_s[2], st_s[3], conv_s)
```

```python
import functools

import numpy as np
import jax
import jax.numpy as jnp
from jax import lax
from jax.experimental import pallas as pl
from jax.experimental.pallas import tpu as pltpu

F32 = jnp.float32
BF16 = jnp.bfloat16

D_MODEL = 2048
N_HEADS_A, DK_A, DV_A = 4, 128, 256
N_HEADS_B, DK_B, DV_B = 8, 128, 128
A_WIDTH = N_HEADS_A * DV_A
B_WIDTH = N_HEADS_B * DV_B
N_GATE = 2 * N_HEADS_A
GATE_COL = 2 * N_HEADS_A * DK_A + A_WIDTH
D_IN = GATE_COL + N_GATE + A_WIDTH + 4 * B_WIDTH
D_FF = 5504
CONV_W = 3
EPS = 1e-5
ALPHA = 2.0 ** 0.25
LOG2_E = 1.4426950408889634
LANE = 128
FF_BLOCK = 512
D_FF_PAD = 5632
Z_MAIN = D_IN - N_GATE
OFF_QA, OFF_KA, OFF_VA, OFF_OA = 0, 512, 1024, 2048
OFF_FB, OFF_QB, OFF_VB, OFF_GB = 3072, 4096, 5120, 6144
VMEM_LIMIT = 56 * 1024 * 1024
PROMPT_CHUNK = 128
PROMPT_SEQS_PER_STEP = 2
SAMPLE_GROUP = 8


def _ln(x):
    mu = jnp.mean(x, axis=-1, keepdims=True)
    xc = x - mu
    var = jnp.mean(xc * xc, axis=-1, keepdims=True)
    return xc * lax.rsqrt(var + EPS)


def _sigmoid(x):
    return 1.0 / (1.0 + jnp.exp(-x))


def _dot(a, b):
    return jnp.dot(a.astype(BF16), b.astype(BF16), preferred_element_type=F32)


def _dot_nt(a, b):
    return lax.dot_general(a.astype(BF16), b.astype(BF16), (((1,), (1,)), ((), ())),
                           preferred_element_type=F32)


def _dot_tn(a, b):
    return lax.dot_general(a.astype(BF16), b.astype(BF16), (((0,), (0,)), ((), ())),
                           preferred_element_type=F32)


def _mod_rows(ref, seq_rows, seq):
    if seq_rows:
        return ref[...]
    return ref[pl.ds(seq, 1), :]


def _ada_kernel(cp_ref, cs_ref, w_ref, b_ref, op_ref, os_ref):
    w = w_ref[...].astype(BF16)
    for c_ref, o_ref in ((cp_ref, op_ref), (cs_ref, os_ref)):
        c = c_ref[...]
        o_ref[...] = _dot(c * _sigmoid(c), w) + b_ref[...]


def _ada(c_p, c_s, w_ada, b_ada):
    n = w_ada.shape[1]
    bn = 1024
    row = lambda r: pl.BlockSpec((r, D_MODEL), lambda j: (0, 0))
    out = lambda r: pl.BlockSpec((r, bn), lambda j: (0, j))
    return pl.pallas_call(
        _ada_kernel,
        grid=(n // bn,),
        in_specs=[row(c_p.shape[0]), row(c_s.shape[0]),
                  pl.BlockSpec((D_MODEL, bn), lambda j: (0, j)),
                  pl.BlockSpec((1, bn), lambda j: (0, j))],
        out_specs=[out(c_p.shape[0]), out(c_s.shape[0])],
        out_shape=[jax.ShapeDtypeStruct((c_p.shape[0], n), F32),
                   jax.ShapeDtypeStruct((c_s.shape[0], n), F32)],
        compiler_params=pltpu.CompilerParams(vmem_limit_bytes=VMEM_LIMIT),
        name="ada",
    )(c_p, c_s, w_ada, b_ada)


def _prep_in_kernel(front_ref, back_ref, gate_ref, o_ref, og_ref, *, first_shifted):
    j = pl.program_id(0)

    @pl.when(j == 0)
    def _():
        og_ref[...] = gate_ref[...].T.astype(BF16)

    def emit(src_ref):
        for r in range(0, src_ref.shape[0], 256):
            o_ref[:, r:r + 256] = src_ref[r:r + 256, :].T.astype(BF16)

    @pl.when(j < first_shifted)
    def _():
        emit(front_ref)

    @pl.when(j >= first_shifted)
    def _():
        emit(back_ref)


def _prep_w_in(w_in_t):
    bn = 1024
    first_shifted = GATE_COL // bn
    kern = functools.partial(_prep_in_kernel, first_shifted=first_shifted)
    return pl.pallas_call(
        kern,
        grid=(Z_MAIN // bn,),
        in_specs=[pl.BlockSpec((bn, D_MODEL), lambda j: (jnp.minimum(j, first_shifted - 1), 0)),
                  pl.BlockSpec((pl.Element(bn), pl.Element(D_MODEL)),
                               lambda j: (pl.multiple_of(jnp.maximum(j, first_shifted) * bn + N_GATE, N_GATE), 0)),
                  pl.BlockSpec((LANE, D_MODEL), lambda j: (GATE_COL // LANE, 0))],
        out_specs=[pl.BlockSpec((D_MODEL, bn), lambda j: (0, j)),
                   pl.BlockSpec((D_MODEL, LANE), lambda j: (0, 0))],
        out_shape=[jax.ShapeDtypeStruct((D_MODEL, Z_MAIN), BF16),
                   jax.ShapeDtypeStruct((D_MODEL, LANE), BF16)],
        compiler_params=pltpu.CompilerParams(vmem_limit_bytes=VMEM_LIMIT),
        name="prep_w_in",
    )(w_in_t, w_in_t, w_in_t)


def _in_kernel(x_ref, sh_ref, sc_ref, w_ref, wg_ref, z_ref, g_ref, h_scr, *, sub, seq_rows, tiles_per_seq):
    i = pl.program_id(0)
    j = pl.program_id(1)

    @pl.when(j == 0)
    def _():
        seq = i // tiles_per_seq
        sh = _mod_rows(sh_ref, seq_rows, seq)
        sc = _mod_rows(sc_ref, seq_rows, seq)
        for r in range(0, x_ref.shape[0], sub):
            rows = slice(r, r + sub)
            h = (_ln(x_ref[rows, :]) * (1.0 + sc) + sh).astype(BF16)
            h_scr[rows, :] = h
            g_ref[rows, :] = _dot(h, wg_ref[...])
            z_ref[rows, :] = _dot(h, w_ref[...]).astype(z_ref.dtype)

    @pl.when(j > 0)
    def _():
        z_ref[...] = _dot(h_scr[...], w_ref[...]).astype(z_ref.dtype)


def _in_proj(x, mod, w_main, w_gate, *, tm, sub, seq_rows, seq_len, z_dtype):
    tokens = x.shape[0]
    bn = 1024
    mod_spec = lambda k: pl.BlockSpec((mod.shape[0], D_MODEL), lambda i, j: (0, k))
    kern = functools.partial(_in_kernel, sub=min(sub, tm), seq_rows=seq_rows, tiles_per_seq=max(seq_len // tm, 1))
    return pl.pallas_call(
        kern,
        grid=(tokens // tm, Z_MAIN // bn),
        in_specs=[pl.BlockSpec((tm, D_MODEL), lambda i, j: (i, 0)),
                  mod_spec(0), mod_spec(1),
                  pl.BlockSpec((D_MODEL, bn), lambda i, j: (0, j)),
                  pl.BlockSpec((D_MODEL, LANE), lambda i, j: (0, 0))],
        out_specs=[pl.BlockSpec((tm, bn), lambda i, j: (i, j)),
                   pl.BlockSpec((tm, LANE), lambda i, j: (i, 0))],
        out_shape=[jax.ShapeDtypeStruct((tokens, Z_MAIN), z_dtype),
                   jax.ShapeDtypeStruct((tokens, LANE), F32)],
        scratch_shapes=[pltpu.VMEM((tm, D_MODEL), BF16)],
        compiler_params=pltpu.CompilerParams(
            dimension_semantics=("arbitrary", "arbitrary"), vmem_limit_bytes=VMEM_LIMIT),
        name="in_proj",
    )(x, mod, mod, w_main, w_gate)


def _mix_consts(L, Tg, t_major):
    G = L // Tg
    r = np.arange(L)
    seq, tim = (r % G, r // G) if t_major else (r // Tg, r % Tg)
    same = seq[:, None] == seq[None, :]
    t = tim[:, None]
    u = tim[None, :]
    mats = [same & (u <= t), same & (u > t)]
    masks = [same & (u <= t), same & (t <= u), same]
    levels = []
    m = Tg // 2
    while m >= 1:
        levels.append(m)
        m //= 2
    for m in levels:
        blk = same & (u // m == t // m)
        odd = (t // m) % 2 == 1
        mats.append(np.where(odd, blk & (u <= t), blk & (u > t)))
        masks.append(same & odd & ((u // m) % 2 == 0) & (t // (2 * m) == u // (2 * m)))
    mats = np.concatenate(mats, axis=0).astype(np.float32)
    masks = np.stack(masks).astype(np.float32)
    last_rows = tuple(int(np.nonzero((seq == g) & (tim == Tg - 1))[0][0]) for g in range(G))
    return mats, masks, tuple(levels), last_rows


def _mix_kernel(z_ref, g_ref, gbias_ref, lbl_ref, na_ref, nb_ref, mall_ref, msk_ref,
                c_in, n_in, m_in, s_in, mix_ref, c_out, n_out, m_out, s_out, e_scr,
                *, L, Tg, levels, last_rows):
    G = L // Tg

    def zs(off, w):
        return z_ref[:, :, off:off + w].reshape(L, w)

    def put_mix(off, w, val):
        mix_ref[:, :, off:off + w] = val.astype(mix_ref.dtype).reshape(Tg, G, w)

    _mix_body(zs, put_mix, g_ref[...].reshape(L, LANE), gbias_ref, lbl_ref, na_ref, nb_ref, mall_ref, msk_ref,
              c_in, n_in, m_in, s_in, c_out, n_out, m_out, s_out, e_scr, pl.program_id(0) * G,
              L=L, Tg=Tg, levels=levels, last_rows=last_rows, t_major=True)


def _mixchunk_kernel(z_ref, g_ref, gbias_ref, lbl_ref, na_ref, nb_ref, mall_ref, msk_ref,
                     mix_ref, c_out, n_out, m_out, s_out, e_scr, *, L, levels, last_rows):
    nb = z_ref.shape[0]
    b0 = pl.program_id(0) * nb

    @pl.when(pl.program_id(1) == 0)
    def _():
        c_out[...] = jnp.zeros(c_out.shape, F32)
        n_out[...] = jnp.zeros(n_out.shape, F32)
        s_out[...] = jnp.zeros(s_out.shape, F32)
        m_out[0, pl.ds(b0, nb), :] = jnp.zeros((nb, N_HEADS_A), F32)

    for k in range(nb):
        def zs(off, w, k=k):
            return z_ref[k, :, off:off + w]

        def put_mix(off, w, val, k=k):
            mix_ref[k, :, off:off + w] = val.astype(mix_ref.dtype)

        st = [r.at[:, pl.ds(k, 1)] for r in (c_out, n_out, s_out)]
        _mix_body(zs, put_mix, g_ref[k], gbias_ref, lbl_ref, na_ref, nb_ref, mall_ref, msk_ref,
                  st[0], st[1], m_out, st[2], st[0], st[1], m_out, st[2], e_scr.at[k], b0 + k,
                  L=L, Tg=L, levels=levels, last_rows=last_rows, t_major=False)


def _mix_body(zs, put_mix, gates, gbias_ref, lbl_ref, na_ref, nb_ref, mall_ref, msk_ref,
              c_in, n_in, m_in, s_in, c_out, n_out, m_out, s_out, e_scr, m_row0,
              *, L, Tg, levels, last_rows, t_major):
    G = L // Tg
    neg_inf = F32(-jnp.inf)
    causal = msk_ref[0] > 0.5
    causal_t = msk_ref[1] > 0.5
    same = msk_ref[2] > 0.5
    row1 = lax.broadcasted_iota(jnp.int32, (L, 1), 0)
    seq_of_row = (row1 & (G - 1)) if t_major else (row1 // Tg)
    in_group = [seq_of_row == g for g in range(G)]
    col1 = lax.broadcasted_iota(jnp.int32, (1, L), 1)
    seq_of_col = (col1 & (G - 1)) if t_major else (col1 // Tg)
    in_group_lane = [seq_of_col == g for g in range(G)]

    def state_updates(kw, v):
        if G == 1:
            return [_dot_tn(kw, v)]
        kw_t = kw.T
        return [_dot(jnp.where(in_group_lane[g], kw_t, 0.0), v) for g in range(G)]

    def by_group(vals):
        if G == 1:
            return vals[0]
        out = jnp.where(in_group[0], vals[0], 0.0)
        for g in range(1, G):
            out = out + jnp.where(in_group[g], vals[g], 0.0)
        return out

    pre = gates + gbias_ref[...]
    lsig = jnp.minimum(pre, 0.0) - jnp.log(1.0 + jnp.exp(-jnp.abs(pre)))
    pre_t = pre.T
    lsig_t = lsig.T
    m_prev = m_in[0, pl.ds(m_row0, G), :]
    scale = DK_A ** -0.5
    m_new_rows = []
    for h in range(N_HEADS_A):
        q = zs(OFF_QA + DK_A * h, DK_A)
        k = zs(OFF_KA + DK_A * h, DK_A)
        v = zs(OFF_VA + DV_A * h, DV_A)
        logi_c = pre[:, h:h + 1]
        logi_r = pre_t[h:h + 1, :]
        lf_c = lsig[:, N_HEADS_A + h:N_HEADS_A + h + 1]
        lf_r = lsig_t[N_HEADS_A + h:N_HEADS_A + h + 1, :]
        b_c = jnp.sum(jnp.where(causal, lf_r, 0.0), axis=1, keepdims=True)
        b_r = jnp.sum(jnp.where(causal_t, lf_c, 0.0), axis=0, keepdims=True)
        tot_c = jnp.sum(jnp.where(same, lf_r, 0.0), axis=1, keepdims=True)
        tot_r = jnp.sum(jnp.where(same, lf_c, 0.0), axis=0, keepdims=True)
        m_col = by_group([m_prev[g:g + 1, h:h + 1] for g in range(G)])
        dmat = jnp.where(causal, b_c - b_r + logi_r, neg_inf)
        inter = b_c + m_col
        m_t = jnp.maximum(inter, jnp.max(dmat, axis=1, keepdims=True))
        w_inter = jnp.exp(inter - m_t)
        smat = _dot_nt(q, k) * (scale * jnp.exp(dmat - m_t))
        q_c = by_group([_dot(q, c_in[0, g, h]) for g in range(G)]) * scale
        n_rows = by_group([n_in[0, g, h:h + 1, :] for g in range(G)])
        q_n = jnp.sum(q.astype(F32) * n_rows, axis=1, keepdims=True) * scale
        num = w_inter * q_c + _dot(smat, v)
        den = w_inter * q_n + jnp.sum(smat, axis=1, keepdims=True)
        hh = num / jnp.maximum(jnp.abs(den), jnp.exp(-m_t))
        ms = jnp.mean(hh * hh, axis=1, keepdims=True)
        oa = zs(OFF_OA + DV_A * h, DV_A).astype(F32)
        ya = hh * lax.rsqrt(ms + EPS) * na_ref[:, DV_A * h:DV_A * (h + 1)] * _sigmoid(oa)
        put_mix(DV_A * h, DV_A, ya)
        dec_c = tot_c - b_c + logi_c
        dec_r = tot_r - b_r + logi_r
        m_new_c = jnp.maximum(tot_c + m_col,
                              jnp.max(jnp.where(same, dec_r, neg_inf), axis=1, keepdims=True))
        wk = jnp.exp(dec_c - m_new_c)
        sc = jnp.exp(tot_c + m_col - m_new_c)
        kw = k.astype(F32) * wk
        upd = state_updates(kw, v)
        m_new_h = []
        for g in range(G):
            last = last_rows[g]
            kg = kw if G == 1 else jnp.where(in_group[g], kw, 0.0)
            sc_g = sc[last:last + 1, :]
            c_new = sc_g * c_in[0, g, h] + upd[g]
            n_new = sc_g * n_in[0, g, h:h + 1, :] + jnp.sum(kg, axis=0, keepdims=True)
            c_out[0, g, h] = c_new
            n_out[0, g, h:h + 1, :] = n_new
            m_new_h.append(m_new_c[last:last + 1, :])
        m_new_rows.append(m_new_h)
    for g in range(G):
        m_out[0, pl.ds(m_row0 + g, 1), :] = jnp.concatenate(
            [m_new_rows[h][g] for h in range(N_HEADS_A)], axis=1)

    l0 = lbl_ref[0:1, :]
    l1 = lbl_ref[1:2, :]
    lmax = jnp.maximum(l0, l1)
    e0 = jnp.exp(l0 - lmax)
    e1 = jnp.exp(l1 - lmax)
    lb = e0 / (e0 + e1)
    fb = zs(OFF_FB, B_WIDTH).astype(F32)
    e = jnp.exp(-jnp.abs(fb))
    r = 1.0 / (1.0 + e)
    pos = fb >= 0.0
    sig = jnp.where(pos, r, e * r)
    nsig = jnp.where(pos, e * r, r)
    logf = jnp.log(lb + (1.0 - lb) * sig) * LOG2_E
    kb = (1.0 - lb) * nsig
    hi = logf.astype(BF16)
    r1 = logf - hi.astype(F32)
    mid = r1.astype(BF16)
    lo = (r1 - mid.astype(F32)).astype(BF16)
    mall = mall_ref[...]
    if L % 16 == 0:
        e_scr[...] = _dot(mall, jnp.concatenate([hi, mid, lo], axis=0))
    else:
        mall = mall[:, :L]
        e_scr[...] = _dot(mall, hi) + _dot(mall, mid) + _dot(mall, lo)

    dec_t = []
    for g in range(G):
        last = last_rows[g]
        bl = e_scr[last:last + 1, :]
        bl8 = jnp.concatenate([bl[:, DK_B * h:DK_B * (h + 1)] for h in range(N_HEADS_B)], axis=0)
        dec_t.append(jnp.exp2(bl8).T)

    for h in range(N_HEADS_B):
        sl = slice(DK_B * h, DK_B * (h + 1))
        q = zs(OFF_QB + DK_B * h, DK_B).astype(BF16)
        k = kb[:, sl]
        kbf = k.astype(BF16)
        v = zs(OFF_VB + DV_B * h, DV_B)
        amat = jnp.zeros((L, L), F32)
        for li in range(len(levels)):
            ex = jnp.exp2(e_scr[(2 + li) * L:(3 + li) * L, sl]).astype(BF16)
            amat = jnp.where(msk_ref[3 + li] > 0.5, _dot_nt(q * ex, kbf * ex), amat)
        qf = q.astype(F32)
        diag = jnp.sum(qf * k, axis=1, keepdims=True)
        qs = qf * jnp.exp2(e_scr[0:L, sl])
        o_inter = by_group([_dot(qs, s_in[0, g, h]) for g in range(G)])
        o = _dot(amat, v) + diag * v.astype(F32) + o_inter
        ms = jnp.mean(o * o, axis=1, keepdims=True)
        gb = zs(OFF_GB + DV_B * h, DV_B).astype(F32)
        yb = o * lax.rsqrt(ms + EPS) * nb_ref[:, sl] * (gb * _sigmoid(gb))
        put_mix(A_WIDTH + DV_B * h, DV_B, yb)
        upd = state_updates(k * jnp.exp2(e_scr[L:2 * L, sl]), v)
        for g in range(G):
            s_out[0, g, h] = dec_t[g][:, h:h + 1] * s_in[0, g, h] + upd[g]


def _mixer_args(L, Tg, t_major, const2, const3, gbias, lb_logits, norm_a, norm_b):
    mats, masks, levels, last_rows = _mix_consts(L, Tg, t_major)
    n_mats = mats.shape[0] // L
    if L % 16 == 0:
        mats = np.concatenate([mats, mats, mats], axis=1)
    specs = [pl.BlockSpec((1, LANE), const2),
             pl.BlockSpec((2, B_WIDTH), const2),
             pl.BlockSpec((1, A_WIDTH), const2),
             pl.BlockSpec((1, B_WIDTH), const2),
             pl.BlockSpec(mats.shape, const2),
             pl.BlockSpec(masks.shape, const3)]
    args = [gbias, lb_logits, norm_a, norm_b, jnp.asarray(mats, BF16), jnp.asarray(masks)]
    return specs, args, levels, last_rows, n_mats


def _state_shapes(n_seq):
    return [jax.ShapeDtypeStruct((1, n_seq, N_HEADS_A, DK_A, DV_A), F32),
            jax.ShapeDtypeStruct((1, n_seq, N_HEADS_A, DK_A), F32),
            jax.ShapeDtypeStruct((1, n_seq, N_HEADS_A), F32),
            jax.ShapeDtypeStruct((1, n_seq, N_HEADS_B, DK_B, DV_B), F32)]


def _state_specs(G, n_seq, seq_of_step):
    return [pl.BlockSpec((1, G, N_HEADS_A, DK_A, DV_A), lambda *ids: (0, seq_of_step(*ids), 0, 0, 0)),
            pl.BlockSpec((1, G, N_HEADS_A, DK_A), lambda *ids: (0, seq_of_step(*ids), 0, 0)),
            pl.BlockSpec((1, n_seq, N_HEADS_A), lambda *ids: (0, 0, 0)),
            pl.BlockSpec((1, G, N_HEADS_B, DK_B, DV_B), lambda *ids: (0, seq_of_step(*ids), 0, 0, 0))]


def _mixer(z, gates, gbias, lb_logits, norm_a, norm_b, *, n_seq, seq_len, states):
    Tg, G = seq_len, SAMPLE_GROUP
    L = Tg * G
    tile = lambda w: pl.BlockSpec((Tg, G, w), lambda i: (0, i, 0))
    const_specs, const_args, levels, last_rows, n_mats = _mixer_args(
        L, Tg, True, lambda i: (0, 0), lambda i: (0, 0, 0), gbias, lb_logits, norm_a, norm_b)
    st_specs = _state_specs(G, n_seq, lambda i: i)
    kern = functools.partial(_mix_kernel, L=L, Tg=Tg, levels=levels, last_rows=last_rows)
    return pl.pallas_call(
        kern,
        grid=(n_seq // G,),
        in_specs=[tile(Z_MAIN), tile(LANE)] + const_specs + st_specs,
        out_specs=[tile(D_MODEL)] + st_specs,
        out_shape=[jax.ShapeDtypeStruct((Tg, n_seq, D_MODEL), F32)] + _state_shapes(n_seq),
        scratch_shapes=[pltpu.VMEM((n_mats * L, B_WIDTH), F32)],
        compiler_params=pltpu.CompilerParams(dimension_semantics=("arbitrary",), vmem_limit_bytes=VMEM_LIMIT),
        name="mixer_state",
    )(z, gates, *const_args, *states)


def _mixer_chunk(z, gates, gbias, lb_logits, norm_a, norm_b, *, n_seq, seq_len):
    L = PROMPT_CHUNK
    nb = PROMPT_SEQS_PER_STEP
    nchunks = seq_len // L
    tile = lambda w: pl.BlockSpec((nb, L, w), lambda b, c: (b, c, 0))
    const_specs, const_args, levels, last_rows, n_mats = _mixer_args(
        L, L, False, lambda b, c: (0, 0), lambda b, c: (0, 0, 0), gbias, lb_logits, norm_a, norm_b)
    st_specs = _state_specs(nb, n_seq, lambda b, c: b)
    kern = functools.partial(_mixchunk_kernel, L=L, levels=levels, last_rows=last_rows)
    outs = pl.pallas_call(
        kern,
        grid=(n_seq // nb, nchunks),
        in_specs=[tile(Z_MAIN), tile(LANE)] + const_specs,
        out_specs=[tile(D_MODEL)] + st_specs,
        out_shape=[jax.ShapeDtypeStruct((n_seq, seq_len, D_MODEL), BF16)] + _state_shapes(n_seq),
        scratch_shapes=[pltpu.VMEM((nb, n_mats * L, B_WIDTH), F32)],
        compiler_params=pltpu.CompilerParams(dimension_semantics=("arbitrary", "arbitrary"),
                                             vmem_limit_bytes=VMEM_LIMIT),
        name="mixer_chunk",
    )(z.reshape(n_seq, seq_len, Z_MAIN), gates.reshape(n_seq, seq_len, LANE), *const_args)
    return [outs[0].reshape(n_seq * seq_len, D_MODEL)] + list(outs[1:])


def _res_kernel(*refs, sub, seq_rows, tiles_per_seq, emit_h2, nk):
    if emit_h2:
        a_ref, w_ref, x_ref, gm_ref, lg_ref, lb_ref, sh_ref, sc_ref, o_ref, h_ref = refs
    else:
        a_ref, w_ref, x_ref, gm_ref, lg_ref, lb_ref, o_ref = refs
    i = pl.program_id(0)
    k = pl.program_id(1)

    if nk > 1:
        @pl.when(k == 0)
        def _():
            o_ref[...] = _dot(a_ref[...], w_ref[...])

    if nk > 2:
        @pl.when((k > 0) & (k < nk - 1))
        def _():
            o_ref[...] += _dot(a_ref[...], w_ref[...])

    @pl.when(k == nk - 1)
    def _():
        seq = i // tiles_per_seq
        gm = _mod_rows(gm_ref, seq_rows, seq)
        if emit_h2:
            sh = _mod_rows(sh_ref, seq_rows, seq)
            sc = _mod_rows(sc_ref, seq_rows, seq)
        for r in range(0, o_ref.shape[0], sub):
            rows = slice(r, r + sub)
            acc = _dot(a_ref[rows, :], w_ref[...])
            if nk > 1:
                acc = acc + o_ref[rows, :]
            x1 = _ln(ALPHA * x_ref[rows, :] + gm * acc) * lg_ref[...] + lb_ref[...]
            o_ref[rows, :] = x1
            if emit_h2:
                h_ref[rows, :] = (_ln(x1) * (1.0 + sc) + sh).astype(BF16)


def _res_block(a, w, x, mod, ln_g, ln_b, *, tm, bk, sub, seq_rows, seq_len, gate_col, emit_h2):
    tokens, kdim = a.shape
    nk = kdim // bk
    mod_spec = lambda c: pl.BlockSpec((mod.shape[0], D_MODEL), lambda i, k: (0, c))
    row_spec = pl.BlockSpec((tm, D_MODEL), lambda i, k: (i, 0))
    vec_spec = pl.BlockSpec((1, D_MODEL), lambda i, k: (0, 0))
    in_specs = [pl.BlockSpec((tm, bk), lambda i, k: (i, k)),
                pl.BlockSpec((bk, D_MODEL), lambda i, k: (k, 0)),
                row_spec, mod_spec(gate_col), vec_spec, vec_spec]
    args = [a, w, x, mod, ln_g, ln_b]
    out_specs = [row_spec]
    out_shape = [jax.ShapeDtypeStruct((tokens, D_MODEL), F32)]
    if emit_h2:
        in_specs += [mod_spec(3), mod_spec(4)]
        args += [mod, mod]
        out_specs.append(row_spec)
        out_shape.append(jax.ShapeDtypeStruct((tokens, D_MODEL), BF16))
    kern = functools.partial(_res_kernel, sub=sub, seq_rows=seq_rows, tiles_per_seq=max(seq_len // tm, 1),
                             emit_h2=emit_h2, nk=nk)
    return pl.pallas_call(
        kern,
        grid=(tokens // tm, nk),
        in_specs=in_specs,
        out_specs=out_specs,
        out_shape=out_shape,
        compiler_params=pltpu.CompilerParams(
            dimension_semantics=("arbitrary", "arbitrary"), vmem_limit_bytes=VMEM_LIMIT),
        name="res_block",
    )(*args)


def _up_kernel(*refs, tm, sub, seq_len, n_seq, t_major, tiles_per_seq):
    if t_major:
        h_ref, wa_scr, wu_scr, cw_ref, cb_ref, cache_ref, g_ref, tail_ref, abuf = refs
    else:
        (h_ref, wa_ref, uu_ref, wd_ref, cw_ref, cb_ref,
         g_ref, tail_ref, wa_scr, wu_scr, wd_out, carry) = refs
    j = pl.program_id(0)
    i = pl.program_id(1)
    bn = wa_scr.shape[1]
    valid = (j * bn + lax.broadcasted_iota(jnp.int32, (1, bn), 1)) < D_FF
    hist = (CONV_W - 1) * n_seq

    @pl.when(i == 0)
    def _():
        if t_major:
            abuf[0:hist, :] = jnp.where(valid, cache_ref[...].reshape(hist, bn), 0.0)
        else:
            wu = uu_ref[...]
            wu_last = jnp.concatenate([wu[:, LANE:], jnp.zeros((D_MODEL, LANE), F32)], axis=1)
            wu = jnp.where(j == pl.num_programs(0) - 1, wu_last, wu)
            wa_scr[...] = jnp.where(valid, wa_ref[...], 0.0).astype(BF16)
            wu_scr[...] = jnp.where(valid, wu, 0.0).astype(BF16)
            wd_rows = j * bn + lax.broadcasted_iota(jnp.int32, (bn, 1), 0)
            wd_out[...] = jnp.where(wd_rows < D_FF, wd_ref[...], 0.0).astype(BF16)

    if not t_major:
        @pl.when(i % tiles_per_seq == 0)
        def _():
            carry[...] = jnp.zeros(carry.shape, F32)

    row8 = lax.broadcasted_iota(jnp.int32, (8, 1), 0)
    for r in range(0, tm, sub):
        h = h_ref[r:r + sub, :]
        a = _dot(h, wa_scr[...])
        u = _dot(h, wu_scr[...])
        if t_major:
            abuf[hist + r:hist + r + sub, :] = a
            a1 = abuf[hist + r - n_seq:hist + r - n_seq + sub, :]
            a2 = abuf[hist + r - 2 * n_seq:hist + r - 2 * n_seq + sub, :]
        else:
            p1 = carry[7:8, :]
            p2 = carry[6:7, :]
            r1 = pltpu.roll(a, 1, axis=0)
            r2 = pltpu.roll(a, 2, axis=0)
            a1 = jnp.concatenate([jnp.where(row8 == 0, p1, r1[0:8, :]), r1[8:, :]], axis=0)
            a2 = jnp.concatenate(
                [jnp.where(row8 == 0, p2, jnp.where(row8 == 1, p1, r2[0:8, :])), r2[8:, :]], axis=0)
            carry[...] = a[sub - 8:sub, :]
        conv = cb_ref[...] + cw_ref[0:1, :] * a2 + cw_ref[1:2, :] * a1 + cw_ref[2:3, :] * a
        gl = 0.5 * conv * (1.0 + lax.erf(conv * (2.0 ** -0.5)))
        g_ref[r:r + sub, :] = (gl * u).astype(BF16)

    if t_major:
        tail_ref[...] = abuf[tm:tm + hist, :].reshape(tail_ref.shape)
    else:
        @pl.when((i + 1) % tiles_per_seq == 0)
        def _():
            tail_ref[0] = carry[8 - (CONV_W - 1):8, :]


def _up_proj(h2, w_up, w_down, conv_w_p, conv_b_p, *, tm, sub, seq_len, n_seq, cache=None):
    tokens = h2.shape[0]
    t_major = cache is not None
    bn = FF_BLOCK
    nj = D_FF_PAD // bn
    assert D_FF_PAD - D_FF == LANE
    tiles_per_seq = max(seq_len // tm, 1)
    wspec = lambda f: pl.BlockSpec((D_MODEL, bn), f)
    col_spec = wspec(lambda j, i: (0, j))
    h_spec = pl.BlockSpec((tm, D_MODEL), lambda j, i: (i, 0))
    conv_specs = [pl.BlockSpec((CONV_W, bn), lambda j, i: (0, j)), pl.BlockSpec((1, bn), lambda j, i: (0, j))]
    out_specs = [pl.BlockSpec((tm, bn), lambda j, i: (i, j))]
    out_shape = [jax.ShapeDtypeStruct((tokens, D_FF_PAD), BF16)]
    if t_major:
        hist_spec = pl.BlockSpec((CONV_W - 1, n_seq, bn), lambda j, i: (0, 0, j))
        in_specs = [h_spec, col_spec, col_spec] + conv_specs + [hist_spec]
        args = [h2, w_up[0], w_up[1], conv_w_p, conv_b_p, cache]
        out_specs.append(hist_spec)
        out_shape.append(jax.ShapeDtypeStruct((CONV_W - 1, n_seq, D_FF_PAD), F32))
        scratch_rows = (CONV_W - 1) * n_seq + tm
    else:
        row_spec = pl.BlockSpec((bn, D_MODEL), lambda j, i: (j, 0))
        u_off = lambda j, i: (0, pl.multiple_of(jnp.minimum(D_FF + bn * j, 2 * D_FF - bn), LANE))
        in_specs = [h_spec, col_spec,
                    pl.BlockSpec((pl.Element(D_MODEL), pl.Element(bn)), u_off),
                    row_spec] + conv_specs
        args = [h2, w_up, w_up, w_down, conv_w_p, conv_b_p]
        out_specs += [pl.BlockSpec((1, CONV_W - 1, bn), lambda j, i: (i // tiles_per_seq, 0, j)),
                      col_spec, col_spec, row_spec]
        out_shape += [jax.ShapeDtypeStruct((n_seq, CONV_W - 1, D_FF_PAD), F32),
                      jax.ShapeDtypeStruct((D_MODEL, D_FF_PAD), BF16),
                      jax.ShapeDtypeStruct((D_MODEL, D_FF_PAD), BF16),
                      jax.ShapeDtypeStruct((D_FF_PAD, D_MODEL), BF16)]
        scratch_rows = 8
    kern = functools.partial(_up_kernel, tm=tm, sub=sub, seq_len=seq_len, n_seq=n_seq, t_major=t_major,
                             tiles_per_seq=tiles_per_seq)
    outs = pl.pallas_call(
        kern,
        grid=(nj, tokens // tm),
        in_specs=in_specs,
        out_specs=out_specs,
        out_shape=out_shape,
        scratch_shapes=[pltpu.VMEM((scratch_rows, bn), F32)],
        compiler_params=pltpu.CompilerParams(
            dimension_semantics=("arbitrary", "arbitrary"), vmem_limit_bytes=VMEM_LIMIT),
        name="up_proj",
    )(*args)
    if t_major:
        return outs[0], outs[1], w_up, w_down
    return outs[0], outs[1], (outs[2], outs[3]), outs[4]


def _path(x, mod, w, *, n_seq, seq_len, tm_in, tm_res, tm_up, tm_down, states=None, cache=None):
    sample = states is not None
    sub = n_seq if sample else 256
    mixer_w = (w["gbias"], w["lb_logits"], w["norm_a"], w["norm_b"])
    z, gates = _in_proj(x, mod, w["w_main"], w["w_gate"], tm=tm_in, sub=sub, seq_rows=sample,
                        seq_len=seq_len, z_dtype=F32 if sample else BF16)
    if sample:
        mix, c1, n1, m1, s1 = _mixer(z.reshape(seq_len, n_seq, Z_MAIN), gates.reshape(seq_len, n_seq, LANE),
                                     *mixer_w, n_seq=n_seq, seq_len=seq_len, states=states)
        mix = mix.reshape(n_seq * seq_len, D_MODEL)
    else:
        mix, c1, n1, m1, s1 = _mixer_chunk(z, gates, *mixer_w, n_seq=n_seq, seq_len=seq_len)
    x1, h2 = _res_block(mix, w["w_out"], x, mod, w["ln1_g"], w["ln1_b"], tm=tm_res, bk=D_MODEL, sub=sub,
                        seq_rows=sample, seq_len=seq_len, gate_col=2, emit_h2=True)
    g, conv_out, w_up_bf, w_down_bf = _up_proj(h2, w["w_up"], w["w_down"], w["conv_w"], w["conv_b"], tm=tm_up,
                                               sub=sub, seq_len=seq_len, n_seq=n_seq, cache=cache)
    (x2,) = _res_block(g, w_down_bf, x1, mod, w["ln2_g"], w["ln2_b"], tm=tm_down, bk=D_FF_PAD // 2, sub=sub,
                       seq_rows=sample, seq_len=seq_len, gate_col=5, emit_h2=False)
    return x2, (c1, n1, m1, s1), conv_out, dict(w_up=w_up_bf, w_down=w_down_bf)


def kernel(x_prompt, x_sample, state_mlstm_C, state_mlstm_n, state_mlstm_m, state_hgrn_S, cache_ffn_conv,
           c_prompt, c_sample, hgrn_lb_logits, w_ada, b_ada, w_in, b_gate_a, norm_a, norm_b, w_out,
           ln1_g, ln1_b, w_up, conv_w, conv_b, w_down, ln2_g, ln2_b):
    bp, tp, _ = x_prompt.shape
    bs, ts, _ = x_sample.shape
    pad_ff = D_FF_PAD - D_FF
    w_main, w_gate = _prep_w_in(jnp.transpose(w_in[0]))
    weights = dict(
        w_main=w_main, w_gate=w_gate, w_out=w_out[0].astype(BF16), w_up=w_up[0],
        w_down=w_down[0],
        gbias=jnp.pad(b_gate_a[0].reshape(1, N_GATE), ((0, 0), (0, LANE - N_GATE))),
        lb_logits=hgrn_lb_logits, norm_a=norm_a, norm_b=norm_b,
        ln1_g=ln1_g, ln1_b=ln1_b, ln2_g=ln2_g, ln2_b=ln2_b,
        conv_w=jnp.pad(conv_w[0], ((0, 0), (0, pad_ff))),
        conv_b=jnp.pad(conv_b, ((0, 0), (0, pad_ff))))

    c_p = jnp.pad(c_prompt, ((0, 8 - bp), (0, 0)))
    mod_p, mod_s = _ada(c_p, c_sample, w_ada[0], b_ada)

    yp, st_p, tail_p, w_ffn_bf = _path(x_prompt.reshape(bp * tp, D_MODEL), mod_p, weights, n_seq=bp,
                                       seq_len=tp, tm_in=1024, tm_res=512, tm_up=1024, tm_down=512)

    xs_t = jnp.swapaxes(x_sample, 0, 1).reshape(ts * bs, D_MODEL)
    cache_t = jnp.swapaxes(cache_ffn_conv[0], 0, 1)
    ys_t, st_s, tail_s, _ = _path(xs_t, mod_s, dict(weights, **w_ffn_bf), n_seq=bs, seq_len=ts,
                                  tm_in=ts * bs, tm_res=ts * bs, tm_up=ts * bs, tm_down=ts * bs,
                                  states=(state_mlstm_C, state_mlstm_n, state_mlstm_m, state_hgrn_S),
                                  cache=cache_t)

    ys = jnp.swapaxes(ys_t.reshape(ts, bs, D_MODEL), 0, 1)
    conv_p = tail_p[None, :, :, :D_FF]
    conv_s = jnp.swapaxes(tail_s[:, :, :D_FF], 0, 1)[None]
    return (yp.reshape(bp, tp, D_MODEL), ys,
            st_p[0], st_p[1], st_p[2], st_p[3], conv_p,
            st_s[0], st_s[1], st_s[2], st_s[3], conv_s)
```

```python
import functools

import numpy as np
import jax
import jax.numpy as jnp
from jax import lax
from jax.experimental import pallas as pl
from jax.experimental.pallas import tpu as pltpu

F32 = jnp.float32
BF16 = jnp.bfloat16

D_MODEL = 2048
N_HEADS_A, DK_A, DV_A = 4, 128, 256
N_HEADS_B, DK_B, DV_B = 8, 128, 128
A_WIDTH = N_HEADS_A * DV_A
B_WIDTH = N_HEADS_B * DV_B
N_GATE = 2 * N_HEADS_A
GATE_COL = 2 * N_HEADS_A * DK_A + A_WIDTH
D_IN = GATE_COL + N_GATE + A_WIDTH + 4 * B_WIDTH
D_FF = 5504
CONV_W = 3
EPS = 1e-5
ALPHA = 2.0 ** 0.25
LOG2_E = 1.4426950408889634
LANE = 128
FF_BLOCK = 512
D_FF_PAD = 5632
Z_MAIN = D_IN - N_GATE
OFF_QA, OFF_KA, OFF_VA, OFF_OA = 0, 512, 1024, 2048
OFF_FB, OFF_QB, OFF_VB, OFF_GB = 3072, 4096, 5120, 6144
VMEM_LIMIT = 56 * 1024 * 1024
PROMPT_CHUNK = 128
PROMPT_SEQS_PER_STEP = 2
SAMPLE_GROUP = 8


def _ln(x):
    mu = jnp.mean(x, axis=-1, keepdims=True)
    xc = x - mu
    var = jnp.mean(xc * xc, axis=-1, keepdims=True)
    return xc * lax.rsqrt(var + EPS)


def _sigmoid(x):
    return 1.0 / (1.0 + jnp.exp(-x))


def _dot(a, b):
    return jnp.dot(a.astype(BF16), b.astype(BF16), preferred_element_type=F32)


def _dot_nt(a, b):
    return lax.dot_general(a.astype(BF16), b.astype(BF16), (((1,), (1,)), ((), ())),
                           preferred_element_type=F32)


def _dot_tn(a, b):
    return lax.dot_general(a.astype(BF16), b.astype(BF16), (((0,), (0,)), ((), ())),
                           preferred_element_type=F32)


def _mod_rows(ref, seq_rows, seq):
    if seq_rows:
        return ref[...]
    return ref[pl.ds(seq, 1), :]


def _ada_kernel(cp_ref, cs_ref, w_ref, b_ref, op_ref, os_ref):
    w = w_ref[...].astype(BF16)
    for c_ref, o_ref in ((cp_ref, op_ref), (cs_ref, os_ref)):
        c = c_ref[...]
        o_ref[...] = _dot(c * _sigmoid(c), w) + b_ref[...]


def _ada(c_p, c_s, w_ada, b_ada):
    n = w_ada.shape[1]
    bn = 1024
    row = lambda r: pl.BlockSpec((r, D_MODEL), lambda j: (0, 0))
    out = lambda r: pl.BlockSpec((r, bn), lambda j: (0, j))
    return pl.pallas_call(
        _ada_kernel,
        grid=(n // bn,),
        in_specs=[row(c_p.shape[0]), row(c_s.shape[0]),
                  pl.BlockSpec((D_MODEL, bn), lambda j: (0, j)),
                  pl.BlockSpec((1, bn), lambda j: (0, j))],
        out_specs=[out(c_p.shape[0]), out(c_s.shape[0])],
        out_shape=[jax.ShapeDtypeStruct((c_p.shape[0], n), F32),
                   jax.ShapeDtypeStruct((c_s.shape[0], n), F32)],
        compiler_params=pltpu.CompilerParams(vmem_limit_bytes=VMEM_LIMIT),
        name="ada",
    )(c_p, c_s, w_ada, b_ada)


def _prep_in_kernel(front_ref, back_ref, gate_ref, o_ref, og_ref, *, first_shifted):
    j = pl.program_id(0)

    @pl.when(j == 0)
    def _():
        og_ref[...] = gate_ref[...].T.astype(BF16)

    def emit(src_ref):
        for r in range(0, src_ref.shape[0], 256):
            o_ref[:, r:r + 256] = src_ref[r:r + 256, :].T.astype(BF16)

    @pl.when(j < first_shifted)
    def _():
        emit(front_ref)

    @pl.when(j >= first_shifted)
    def _():
        emit(back_ref)


def _prep_w_in(w_in_t):
    bn = 1024
    first_shifted = GATE_COL // bn
    kern = functools.partial(_prep_in_kernel, first_shifted=first_shifted)
    return pl.pallas_call(
        kern,
        grid=(Z_MAIN // bn,),
        in_specs=[pl.BlockSpec((bn, D_MODEL), lambda j: (jnp.minimum(j, first_shifted - 1), 0)),
                  pl.BlockSpec((pl.Element(bn), pl.Element(D_MODEL)),
                               lambda j: (pl.multiple_of(jnp.maximum(j, first_shifted) * bn + N_GATE, N_GATE), 0)),
                  pl.BlockSpec((LANE, D_MODEL), lambda j: (GATE_COL // LANE, 0))],
        out_specs=[pl.BlockSpec((D_MODEL, bn), lambda j: (0, j)),
                   pl.BlockSpec((D_MODEL, LANE), lambda j: (0, 0))],
        out_shape=[jax.ShapeDtypeStruct((D_MODEL, Z_MAIN), BF16),
                   jax.ShapeDtypeStruct((D_MODEL, LANE), BF16)],
        compiler_params=pltpu.CompilerParams(vmem_limit_bytes=VMEM_LIMIT),
        name="prep_w_in",
    )(w_in_t, w_in_t, w_in_t)


def _in_kernel(x_ref, sh_ref, sc_ref, w_ref, wg_ref, z_ref, g_ref, h_scr, *, sub, seq_rows, tiles_per_seq):
    i = pl.program_id(0)
    j = pl.program_id(1)

    @pl.when(j == 0)
    def _():
        seq = i // tiles_per_seq
        sh = _mod_rows(sh_ref, seq_rows, seq)
        sc = _mod_rows(sc_ref, seq_rows, seq)
        for r in range(0, x_ref.shape[0], sub):
            rows = slice(r, r + sub)
            h = (_ln(x_ref[rows, :]) * (1.0 + sc) + sh).astype(BF16)
            h_scr[rows, :] = h
            g_ref[rows, :] = _dot(h, wg_ref[...])
            z_ref[rows, :] = _dot(h, w_ref[...]).astype(z_ref.dtype)

    @pl.when(j > 0)
    def _():
        z_ref[...] = _dot(h_scr[...], w_ref[...]).astype(z_ref.dtype)


def _in_proj(x, mod, w_main, w_gate, *, tm, sub, seq_rows, seq_len, z_dtype):
    tokens = x.shape[0]
    bn = 1024
    mod_spec = lambda k: pl.BlockSpec((mod.shape[0], D_MODEL), lambda i, j: (0, k))
    kern = functools.partial(_in_kernel, sub=min(sub, tm), seq_rows=seq_rows, tiles_per_seq=max(seq_len // tm, 1))
    return pl.pallas_call(
        kern,
        grid=(tokens // tm, Z_MAIN // bn),
        in_specs=[pl.BlockSpec((tm, D_MODEL), lambda i, j: (i, 0)),
                  mod_spec(0), mod_spec(1),
                  pl.BlockSpec((D_MODEL, bn), lambda i, j: (0, j)),
                  pl.BlockSpec((D_MODEL, LANE), lambda i, j: (0, 0))],
        out_specs=[pl.BlockSpec((tm, bn), lambda i, j: (i, j)),
                   pl.BlockSpec((tm, LANE), lambda i, j: (i, 0))],
        out_shape=[jax.ShapeDtypeStruct((tokens, Z_MAIN), z_dtype),
                   jax.ShapeDtypeStruct((tokens, LANE), F32)],
        scratch_shapes=[pltpu.VMEM((tm, D_MODEL), BF16)],
        compiler_params=pltpu.CompilerParams(
            dimension_semantics=("arbitrary", "arbitrary"), vmem_limit_bytes=VMEM_LIMIT),
        name="in_proj",
    )(x, mod, mod, w_main, w_gate)


def _mix_consts(L, Tg, t_major):
    G = L // Tg
    r = np.arange(L)
    seq, tim = (r % G, r // G) if t_major else (r // Tg, r % Tg)
    same = seq[:, None] == seq[None, :]
    t = tim[:, None]
    u = tim[None, :]
    mats = [same & (u <= t), same & (u > t)]
    masks = [same & (u <= t), same & (t <= u), same]
    levels = []
    m = Tg // 2
    while m >= 1:
        levels.append(m)
        m //= 2
    for m in levels:
        blk = same & (u // m == t // m)
        odd = (t // m) % 2 == 1
        mats.append(np.where(odd, blk & (u <= t), blk & (u > t)))
        masks.append(same & odd & ((u // m) % 2 == 0) & (t // (2 * m) == u // (2 * m)))
    mats = np.concatenate(mats, axis=0).astype(np.float32)
    masks = np.stack(masks).astype(np.float32)
    last_rows = tuple(int(np.nonzero((seq == g) & (tim == Tg - 1))[0][0]) for g in range(G))
    return mats, masks, tuple(levels), last_rows


def _mix_kernel(z_ref, g_ref, gbias_ref, lbl_ref, na_ref, nb_ref, mall_ref, msk_ref,
                c_in, n_in, m_in, s_in, mix_ref, c_out, n_out, m_out, s_out, e_scr,
                *, L, Tg, levels, last_rows):
    G = L // Tg

    def zs(off, w):
        return z_ref[:, :, off:off + w].reshape(L, w)

    def put_mix(off, w, val):
        mix_ref[:, :, off:off + w] = val.astype(mix_ref.dtype).reshape(Tg, G, w)

    _mix_body(zs, put_mix, g_ref[...].reshape(L, LANE), gbias_ref, lbl_ref, na_ref, nb_ref, mall_ref, msk_ref,
              c_in, n_in, m_in, s_in, c_out, n_out, m_out, s_out, e_scr, pl.program_id(0) * G,
              L=L, Tg=Tg, levels=levels, last_rows=last_rows, t_major=True)


def _mixchunk_kernel(z_ref, g_ref, gbias_ref, lbl_ref, na_ref, nb_ref, mall_ref, msk_ref,
                     mix_ref, c_out, n_out, m_out, s_out, e_scr, *, L, levels, last_rows):
    nb = z_ref.shape[0]
    b0 = pl.program_id(0) * nb

    @pl.when(pl.program_id(1) == 0)
    def _():
        c_out[...] = jnp.zeros(c_out.shape, F32)
        n_out[...] = jnp.zeros(n_out.shape, F32)
        s_out[...] = jnp.zeros(s_out.shape, F32)
        m_out[0, pl.ds(b0, nb), :] = jnp.zeros((nb, N_HEADS_A), F32)

    for k in range(nb):
        def zs(off, w, k=k):
            return z_ref[k, :, off:off + w]

        def put_mix(off, w, val, k=k):
            mix_ref[k, :, off:off + w] = val.astype(mix_ref.dtype)

        st = [r.at[:, pl.ds(k, 1)] for r in (c_out, n_out, s_out)]
        _mix_body(zs, put_mix, g_ref[k], gbias_ref, lbl_ref, na_ref, nb_ref, mall_ref, msk_ref,
                  st[0], st[1], m_out, st[2], st[0], st[1], m_out, st[2], e_scr.at[k], b0 + k,
                  L=L, Tg=L, levels=levels, last_rows=last_rows, t_major=False)


def _mix_body(zs, put_mix, gates, gbias_ref, lbl_ref, na_ref, nb_ref, mall_ref, msk_ref,
              c_in, n_in, m_in, s_in, c_out, n_out, m_out, s_out, e_scr, m_row0,
              *, L, Tg, levels, last_rows, t_major):
    G = L // Tg
    neg_inf = F32(-jnp.inf)
    causal = msk_ref[0] > 0.5
    causal_t = msk_ref[1] > 0.5
    same = msk_ref[2] > 0.5
    row1 = lax.broadcasted_iota(jnp.int32, (L, 1), 0)
    seq_of_row = (row1 & (G - 1)) if t_major else (row1 // Tg)
    in_group = [seq_of_row == g for g in range(G)]
    col1 = lax.broadcasted_iota(jnp.int32, (1, L), 1)
    seq_of_col = (col1 & (G - 1)) if t_major else (col1 // Tg)
    in_group_lane = [seq_of_col == g for g in range(G)]

    def state_updates(kw, v):
        if G == 1:
            return [_dot_tn(kw, v)]
        kw_t = kw.T
        return [_dot(jnp.where(in_group_lane[g], kw_t, 0.0), v) for g in range(G)]

    def by_group(vals):
        if G == 1:
            return vals[0]
        out = jnp.where(in_group[0], vals[0], 0.0)
        for g in range(1, G):
            out = out + jnp.where(in_group[g], vals[g], 0.0)
        return out

    pre = gates + gbias_ref[...]
    lsig = jnp.minimum(pre, 0.0) - jnp.log(1.0 + jnp.exp(-jnp.abs(pre)))
    pre_t = pre.T
    lsig_t = lsig.T
    m_prev = m_in[0, pl.ds(m_row0, G), :]
    scale = DK_A ** -0.5
    m_new_rows = []
    for h in range(N_HEADS_A):
        q = zs(OFF_QA + DK_A * h, DK_A)
        k = zs(OFF_KA + DK_A * h, DK_A)
        v = zs(OFF_VA + DV_A * h, DV_A)
        logi_c = pre[:, h:h + 1]
        logi_r = pre_t[h:h + 1, :]
        lf_c = lsig[:, N_HEADS_A + h:N_HEADS_A + h + 1]
        lf_r = lsig_t[N_HEADS_A + h:N_HEADS_A + h + 1, :]
        b_c = jnp.sum(jnp.where(causal, lf_r, 0.0), axis=1, keepdims=True)
        b_r = jnp.sum(jnp.where(causal_t, lf_c, 0.0), axis=0, keepdims=True)
        tot_c = jnp.sum(jnp.where(same, lf_r, 0.0), axis=1, keepdims=True)
        tot_r = jnp.sum(jnp.where(same, lf_c, 0.0), axis=0, keepdims=True)
        m_col = by_group([m_prev[g:g + 1, h:h + 1] for g in range(G)])
        dmat = jnp.where(causal, b_c - b_r + logi_r, neg_inf)
        inter = b_c + m_col
        m_t = jnp.maximum(inter, jnp.max(dmat, axis=1, keepdims=True))
        w_inter = jnp.exp(inter - m_t)
        smat = _dot_nt(q, k) * (scale * jnp.exp(dmat - m_t))
        q_c = by_group([_dot(q, c_in[0, g, h]) for g in range(G)]) * scale
        n_rows = by_group([n_in[0, g, h:h + 1, :] for g in range(G)])
        q_n = jnp.sum(q.astype(F32) * n_rows, axis=1, keepdims=True) * scale
        num = w_inter * q_c + _dot(smat, v)
        den = w_inter * q_n + jnp.sum(smat, axis=1, keepdims=True)
        hh = num / jnp.maximum(jnp.abs(den), jnp.exp(-m_t))
        ms = jnp.mean(hh * hh, axis=1, keepdims=True)
        oa = zs(OFF_OA + DV_A * h, DV_A).astype(F32)
        ya = hh * lax.rsqrt(ms + EPS) * na_ref[:, DV_A * h:DV_A * (h + 1)] * _sigmoid(oa)
        put_mix(DV_A * h, DV_A, ya)
        dec_c = tot_c - b_c + logi_c
        dec_r = tot_r - b_r + logi_r
        m_new_c = jnp.maximum(tot_c + m_col,
                              jnp.max(jnp.where(same, dec_r, neg_inf), axis=1, keepdims=True))
        wk = jnp.exp(dec_c - m_new_c)
        sc = jnp.exp(tot_c + m_col - m_new_c)
        kw = k.astype(F32) * wk
        upd = state_updates(kw, v)
        m_new_h = []
        for g in range(G):
            last = last_rows[g]
            kg = kw if G == 1 else jnp.where(in_group[g], kw, 0.0)
            sc_g = sc[last:last + 1, :]
            c_new = sc_g * c_in[0, g, h] + upd[g]
            n_new = sc_g * n_in[0, g, h:h + 1, :] + jnp.sum(kg, axis=0, keepdims=True)
            c_out[0, g, h] = c_new
            n_out[0, g, h:h + 1, :] = n_new
            m_new_h.append(m_new_c[last:last + 1, :])
        m_new_rows.append(m_new_h)
    for g in range(G):
        m_out[0, pl.ds(m_row0 + g, 1), :] = jnp.concatenate(
            [m_new_rows[h][g] for h in range(N_HEADS_A)], axis=1)

    l0 = lbl_ref[0:1, :]
    l1 = lbl_ref[1:2, :]
    lmax = jnp.maximum(l0, l1)
    e0 = jnp.exp(l0 - lmax)
    e1 = jnp.exp(l1 - lmax)
    lb = e0 / (e0 + e1)
    fb = zs(OFF_FB, B_WIDTH).astype(F32)
    e = jnp.exp(-jnp.abs(fb))
    r = 1.0 / (1.0 + e)
    pos = fb >= 0.0
    sig = jnp.where(pos, r, e * r)
    nsig = jnp.where(pos, e * r, r)
    logf = jnp.log(lb + (1.0 - lb) * sig) * LOG2_E
    kb = (1.0 - lb) * nsig
    hi = logf.astype(BF16)
    r1 = logf - hi.astype(F32)
    mid = r1.astype(BF16)
    lo = (r1 - mid.astype(F32)).astype(BF16)
    mall = mall_ref[...]
    if L % 16 == 0:
        e_scr[...] = _dot(mall, jnp.concatenate([hi, mid, lo], axis=0))
    else:
        mall = mall[:, :L]
        e_scr[...] = _dot(mall, hi) + _dot(mall, mid) + _dot(mall, lo)

    dec_t = []
    for g in range(G):
        last = last_rows[g]
        bl = e_scr[last:last + 1, :]
        bl8 = jnp.concatenate([bl[:, DK_B * h:DK_B * (h + 1)] for h in range(N_HEADS_B)], axis=0)
        dec_t.append(jnp.exp2(bl8).T)

    for h in range(N_HEADS_B):
        sl = slice(DK_B * h, DK_B * (h + 1))
        q = zs(OFF_QB + DK_B * h, DK_B).astype(BF16)
        k = kb[:, sl]
        kbf = k.astype(BF16)
        v = zs(OFF_VB + DV_B * h, DV_B)
        amat = jnp.zeros((L, L), F32)
        for li in range(len(levels)):
            ex = jnp.exp2(e_scr[(2 + li) * L:(3 + li) * L, sl]).astype(BF16)
            amat = jnp.where(msk_ref[3 + li] > 0.5, _dot_nt(q * ex, kbf * ex), amat)
        qf = q.astype(F32)
        diag = jnp.sum(qf * k, axis=1, keepdims=True)
        qs = qf * jnp.exp2(e_scr[0:L, sl])
        o_inter = by_group([_dot(qs, s_in[0, g, h]) for g in range(G)])
        o = _dot(amat, v) + diag * v.astype(F32) + o_inter
        ms = jnp.mean(o * o, axis=1, keepdims=True)
        gb = zs(OFF_GB + DV_B * h, DV_B).astype(F32)
        yb = o * lax.rsqrt(ms + EPS) * nb_ref[:, sl] * (gb * _sigmoid(gb))
        put_mix(A_WIDTH + DV_B * h, DV_B, yb)
        upd = state_updates(k * jnp.exp2(e_scr[L:2 * L, sl]), v)
        for g in range(G):
            s_out[0, g, h] = dec_t[g][:, h:h + 1] * s_in[0, g, h] + upd[g]


def _mixer_args(L, Tg, t_major, const2, const3, gbias, lb_logits, norm_a, norm_b):
    mats, masks, levels, last_rows = _mix_consts(L, Tg, t_major)
    n_mats = mats.shape[0] // L
    if L % 16 == 0:
        mats = np.concatenate([mats, mats, mats], axis=1)
    specs = [pl.BlockSpec((1, LANE), const2),
             pl.BlockSpec((2, B_WIDTH), const2),
             pl.BlockSpec((1, A_WIDTH), const2),
             pl.BlockSpec((1, B_WIDTH), const2),
             pl.BlockSpec(mats.shape, const2),
             pl.BlockSpec(masks.shape, const3)]
    args = [gbias, lb_logits, norm_a, norm_b, jnp.asarray(mats, BF16), jnp.asarray(masks)]
    return specs, args, levels, last_rows, n_mats


def _state_shapes(n_seq):
    return [jax.ShapeDtypeStruct((1, n_seq, N_HEADS_A, DK_A, DV_A), F32),
            jax.ShapeDtypeStruct((1, n_seq, N_HEADS_A, DK_A), F32),
            jax.ShapeDtypeStruct((1, n_seq, N_HEADS_A), F32),
            jax.ShapeDtypeStruct((1, n_seq, N_HEADS_B, DK_B, DV_B), F32)]


def _state_specs(G, n_seq, seq_of_step):
    return [pl.BlockSpec((1, G, N_HEADS_A, DK_A, DV_A), lambda *ids: (0, seq_of_step(*ids), 0, 0, 0)),
            pl.BlockSpec((1, G, N_HEADS_A, DK_A), lambda *ids: (0, seq_of_step(*ids), 0, 0)),
            pl.BlockSpec((1, n_seq, N_HEADS_A), lambda *ids: (0, 0, 0)),
            pl.BlockSpec((1, G, N_HEADS_B, DK_B, DV_B), lambda *ids: (0, seq_of_step(*ids), 0, 0, 0))]


def _mixer(z, gates, gbias, lb_logits, norm_a, norm_b, *, n_seq, seq_len, states):
    Tg, G = seq_len, SAMPLE_GROUP
    L = Tg * G
    tile = lambda w: pl.BlockSpec((Tg, G, w), lambda i: (0, i, 0))
    const_specs, const_args, levels, last_rows, n_mats = _mixer_args(
        L, Tg, True, lambda i: (0, 0), lambda i: (0, 0, 0), gbias, lb_logits, norm_a, norm_b)
    st_specs = _state_specs(G, n_seq, lambda i: i)
    kern = functools.partial(_mix_kernel, L=L, Tg=Tg, levels=levels, last_rows=last_rows)
    return pl.pallas_call(
        kern,
        grid=(n_seq // G,),
        in_specs=[tile(Z_MAIN), tile(LANE)] + const_specs + st_specs,
        out_specs=[tile(D_MODEL)] + st_specs,
        out_shape=[jax.ShapeDtypeStruct((Tg, n_seq, D_MODEL), F32)] + _state_shapes(n_seq),
        scratch_shapes=[pltpu.VMEM((n_mats * L, B_WIDTH), F32)],
        compiler_params=pltpu.CompilerParams(dimension_semantics=("arbitrary",), vmem_limit_bytes=VMEM_LIMIT),
        name="mixer_state",
    )(z, gates, *const_args, *states)


def _mixer_chunk(z, gates, gbias, lb_logits, norm_a, norm_b, *, n_seq, seq_len):
    L = PROMPT_CHUNK
    nb = PROMPT_SEQS_PER_STEP
    nchunks = seq_len // L
    tile = lambda w: pl.BlockSpec((nb, L, w), lambda b, c: (b, c, 0))
    const_specs, const_args, levels, last_rows, n_mats = _mixer_args(
        L, L, False, lambda b, c: (0, 0), lambda b, c: (0, 0, 0), gbias, lb_logits, norm_a, norm_b)
    st_specs = _state_specs(nb, n_seq, lambda b, c: b)
    kern = functools.partial(_mixchunk_kernel, L=L, levels=levels, last_rows=last_rows)
    outs = pl.pallas_call(
        kern,
        grid=(n_seq // nb, nchunks),
        in_specs=[tile(Z_MAIN), tile(LANE)] + const_specs,
        out_specs=[tile(D_MODEL)] + st_specs,
        out_shape=[jax.ShapeDtypeStruct((n_seq, seq_len, D_MODEL), BF16)] + _state_shapes(n_seq),
        scratch_shapes=[pltpu.VMEM((nb, n_mats * L, B_WIDTH), F32)],
        compiler_params=pltpu.CompilerParams(dimension_semantics=("arbitrary", "arbitrary"),
                                             vmem_limit_bytes=VMEM_LIMIT),
        name="mixer_chunk",
    )(z.reshape(n_seq, seq_len, Z_MAIN), gates.reshape(n_seq, seq_len, LANE), *const_args)
    return [outs[0].reshape(n_seq * seq_len, D_MODEL)] + list(outs[1:])


def _res_kernel(*refs, sub, seq_rows, tiles_per_seq, emit_h2, nk):
    if emit_h2:
        a_ref, w_ref, x_ref, gm_ref, lg_ref, lb_ref, sh_ref, sc_ref, o_ref, h_ref = refs
    else:
        a_ref, w_ref, x_ref, gm_ref, lg_ref, lb_ref, o_ref = refs
    i = pl.program_id(0)
    k = pl.program_id(1)

    if nk > 1:
        @pl.when(k == 0)
        def _():
            o_ref[...] = _dot(a_ref[...], w_ref[...])

    if nk > 2:
        @pl.when((k > 0) & (k < nk - 1))
        def _():
            o_ref[...] += _dot(a_ref[...], w_ref[...])

    @pl.when(k == nk - 1)
    def _():
        seq = i // tiles_per_seq
        gm = _mod_rows(gm_ref, seq_rows, seq)
        if emit_h2:
            sh = _mod_rows(sh_ref, seq_rows, seq)
            sc = _mod_rows(sc_ref, seq_rows, seq)
        for r in range(0, o_ref.shape[0], sub):
            rows = slice(r, r + sub)
            acc = _dot(a_ref[rows, :], w_ref[...])
            if nk > 1:
                acc = acc + o_ref[rows, :]
            x1 = _ln(ALPHA * x_ref[rows, :] + gm * acc) * lg_ref[...] + lb_ref[...]
            o_ref[rows, :] = x1
            if emit_h2:
                h_ref[rows, :] = (_ln(x1) * (1.0 + sc) + sh).astype(BF16)


def _res_block(a, w, x, mod, ln_g, ln_b, *, tm, bk, sub, seq_rows, seq_len, gate_col, emit_h2):
    tokens, kdim = a.shape
    nk = kdim // bk
    mod_spec = lambda c: pl.BlockSpec((mod.shape[0], D_MODEL), lambda i, k: (0, c))
    row_spec = pl.BlockSpec((tm, D_MODEL), lambda i, k: (i, 0))
    vec_spec = pl.BlockSpec((1, D_MODEL), lambda i, k: (0, 0))
    in_specs = [pl.BlockSpec((tm, bk), lambda i, k: (i, k)),
                pl.BlockSpec((bk, D_MODEL), lambda i, k: (k, 0)),
                row_spec, mod_spec(gate_col), vec_spec, vec_spec]
    args = [a, w, x, mod, ln_g, ln_b]
    out_specs = [row_spec]
    out_shape = [jax.ShapeDtypeStruct((tokens, D_MODEL), F32)]
    if emit_h2:
        in_specs += [mod_spec(3), mod_spec(4)]
        args += [mod, mod]
        out_specs.append(row_spec)
        out_shape.append(jax.ShapeDtypeStruct((tokens, D_MODEL), BF16))
    kern = functools.partial(_res_kernel, sub=sub, seq_rows=seq_rows, tiles_per_seq=max(seq_len // tm, 1),
                             emit_h2=emit_h2, nk=nk)
    return pl.pallas_call(
        kern,
        grid=(tokens // tm, nk),
        in_specs=in_specs,
        out_specs=out_specs,
        out_shape=out_shape,
        compiler_params=pltpu.CompilerParams(
            dimension_semantics=("arbitrary", "arbitrary"), vmem_limit_bytes=VMEM_LIMIT),
        name="res_block",
    )(*args)


def _up_kernel(*refs, tm, sub, seq_len, n_seq, t_major, tiles_per_seq):
    if t_major:
        h_ref, wa_scr, wu_scr, cw_ref, cb_ref, cache_ref, g_ref, tail_ref, abuf = refs
    else:
        (h_ref, wa_ref, uu_ref, wd_ref, cw_ref, cb_ref,
         g_ref, tail_ref, wa_scr, wu_scr, wd_out, carry) = refs
    j = pl.program_id(0)
    i = pl.program_id(1)
    bn = wa_scr.shape[1]
    valid = (j * bn + lax.broadcasted_iota(jnp.int32, (1, bn), 1)) < D_FF
    hist = (CONV_W - 1) * n_seq

    if t_major:
        @pl.when(i == 0)
        def _():
            abuf[0:hist, :] = jnp.where(valid, cache_ref[...].reshape(hist, bn), 0.0)
    else:
        last_j = pl.num_programs(0) - 1

        @pl.when((i == 0) & (j < last_j))
        def _():
            wa_scr[...] = wa_ref[...].astype(BF16)
            wu_scr[...] = uu_ref[...].astype(BF16)
            wd_out[...] = wd_ref[...].astype(BF16)

        @pl.when((i == 0) & (j == last_j))
        def _():
            wu = jnp.concatenate([uu_ref[:, LANE:], jnp.zeros((D_MODEL, LANE), F32)], axis=1)
            wa_scr[...] = jnp.where(valid, wa_ref[...], 0.0).astype(BF16)
            wu_scr[...] = jnp.where(valid, wu, 0.0).astype(BF16)
            wd_rows = j * bn + lax.broadcasted_iota(jnp.int32, (bn, 1), 0)
            wd_out[...] = jnp.where(wd_rows < D_FF, wd_ref[...], 0.0).astype(BF16)

    if not t_major:
        @pl.when(i % tiles_per_seq == 0)
        def _():
            carry[...] = jnp.zeros(carry.shape, F32)

    row8 = lax.broadcasted_iota(jnp.int32, (8, 1), 0)
    for r in range(0, tm, sub):
        h = h_ref[r:r + sub, :]
        a = _dot(h, wa_scr[...])
        u = _dot(h, wu_scr[...])
        if t_major:
            abuf[hist + r:hist + r + sub, :] = a
            a1 = abuf[hist + r - n_seq:hist + r - n_seq + sub, :]
            a2 = abuf[hist + r - 2 * n_seq:hist + r - 2 * n_seq + sub, :]
        else:
            p1 = carry[7:8, :]
            p2 = carry[6:7, :]
            r1 = pltpu.roll(a, 1, axis=0)
            r2 = pltpu.roll(a, 2, axis=0)
            a1 = jnp.concatenate([jnp.where(row8 == 0, p1, r1[0:8, :]), r1[8:, :]], axis=0)
            a2 = jnp.concatenate(
                [jnp.where(row8 == 0, p2, jnp.where(row8 == 1, p1, r2[0:8, :])), r2[8:, :]], axis=0)
            carry[...] = a[sub - 8:sub, :]
        conv = cb_ref[...] + cw_ref[0:1, :] * a2 + cw_ref[1:2, :] * a1 + cw_ref[2:3, :] * a
        gl = 0.5 * conv * (1.0 + lax.erf(conv * (2.0 ** -0.5)))
        g_ref[r:r + sub, :] = (gl * u).astype(BF16)

    if t_major:
        tail_ref[...] = abuf[tm:tm + hist, :].reshape(tail_ref.shape)
    else:
        @pl.when((i + 1) % tiles_per_seq == 0)
        def _():
            tail_ref[0] = carry[8 - (CONV_W - 1):8, :]


def _up_proj(h2, w_up, w_down, conv_w_p, conv_b_p, *, tm, sub, seq_len, n_seq, cache=None):
    tokens = h2.shape[0]
    t_major = cache is not None
    bn = FF_BLOCK
    nj = D_FF_PAD // bn
    assert D_FF_PAD - D_FF == LANE
    tiles_per_seq = max(seq_len // tm, 1)
    wspec = lambda f: pl.BlockSpec((D_MODEL, bn), f)
    col_spec = wspec(lambda j, i: (0, j))
    h_spec = pl.BlockSpec((tm, D_MODEL), lambda j, i: (i, 0))
    conv_specs = [pl.BlockSpec((CONV_W, bn), lambda j, i: (0, j)), pl.BlockSpec((1, bn), lambda j, i: (0, j))]
    out_specs = [pl.BlockSpec((tm, bn), lambda j, i: (i, j))]
    out_shape = [jax.ShapeDtypeStruct((tokens, D_FF_PAD), BF16)]
    if t_major:
        hist_spec = pl.BlockSpec((CONV_W - 1, n_seq, bn), lambda j, i: (0, 0, j))
        in_specs = [h_spec, col_spec, col_spec] + conv_specs + [hist_spec]
        args = [h2, w_up[0], w_up[1], conv_w_p, conv_b_p, cache]
        out_specs.append(hist_spec)
        out_shape.append(jax.ShapeDtypeStruct((CONV_W - 1, n_seq, D_FF_PAD), F32))
        scratch_rows = (CONV_W - 1) * n_seq + tm
    else:
        row_spec = pl.BlockSpec((bn, D_MODEL), lambda j, i: (j, 0))
        u_off = lambda j, i: (0, pl.multiple_of(jnp.minimum(D_FF + bn * j, 2 * D_FF - bn), LANE))
        in_specs = [h_spec, col_spec,
                    pl.BlockSpec((pl.Element(D_MODEL), pl.Element(bn)), u_off),
                    row_spec] + conv_specs
        args = [h2, w_up, w_up, w_down, conv_w_p, conv_b_p]
        out_specs += [pl.BlockSpec((1, CONV_W - 1, bn), lambda j, i: (i // tiles_per_seq, 0, j)),
                      col_spec, col_spec, row_spec]
        out_shape += [jax.ShapeDtypeStruct((n_seq, CONV_W - 1, D_FF_PAD), F32),
                      jax.ShapeDtypeStruct((D_MODEL, D_FF_PAD), BF16),
                      jax.ShapeDtypeStruct((D_MODEL, D_FF_PAD), BF16),
                      jax.ShapeDtypeStruct((D_FF_PAD, D_MODEL), BF16)]
        scratch_rows = 8
    kern = functools.partial(_up_kernel, tm=tm, sub=sub, seq_len=seq_len, n_seq=n_seq, t_major=t_major,
                             tiles_per_seq=tiles_per_seq)
    outs = pl.pallas_call(
        kern,
        grid=(nj, tokens // tm),
        in_specs=in_specs,
        out_specs=out_specs,
        out_shape=out_shape,
        scratch_shapes=[pltpu.VMEM((scratch_rows, bn), F32)],
        compiler_params=pltpu.CompilerParams(
            dimension_semantics=("arbitrary", "arbitrary"), vmem_limit_bytes=VMEM_LIMIT),
        name="up_proj",
    )(*args)
    if t_major:
        return outs[0], outs[1], w_up, w_down
    return outs[0], outs[1], (outs[2], outs[3]), outs[4]


def _path(x, mod, w, *, n_seq, seq_len, tm_in, tm_res, tm_up, tm_down, bk_down, states=None, cache=None):
    sample = states is not None
    sub = n_seq if sample else 256
    mixer_w = (w["gbias"], w["lb_logits"], w["norm_a"], w["norm_b"])
    z, gates = _in_proj(x, mod, w["w_main"], w["w_gate"], tm=tm_in, sub=sub, seq_rows=sample,
                        seq_len=seq_len, z_dtype=F32 if sample else BF16)
    if sample:
        mix, c1, n1, m1, s1 = _mixer(z.reshape(seq_len, n_seq, Z_MAIN), gates.reshape(seq_len, n_seq, LANE),
                                     *mixer_w, n_seq=n_seq, seq_len=seq_len, states=states)
        mix = mix.reshape(n_seq * seq_len, D_MODEL)
    else:
        mix, c1, n1, m1, s1 = _mixer_chunk(z, gates, *mixer_w, n_seq=n_seq, seq_len=seq_len)
    x1, h2 = _res_block(mix, w["w_out"], x, mod, w["ln1_g"], w["ln1_b"], tm=tm_res, bk=D_MODEL, sub=sub,
                        seq_rows=sample, seq_len=seq_len, gate_col=2, emit_h2=True)
    g, conv_out, w_up_bf, w_down_bf = _up_proj(h2, w["w_up"], w["w_down"], w["conv_w"], w["conv_b"], tm=tm_up,
                                               sub=sub, seq_len=seq_len, n_seq=n_seq, cache=cache)
    (x2,) = _res_block(g, w_down_bf, x1, mod, w["ln2_g"], w["ln2_b"], tm=tm_down, bk=bk_down, sub=sub,
                       seq_rows=sample, seq_len=seq_len, gate_col=5, emit_h2=False)
    return x2, (c1, n1, m1, s1), conv_out, dict(w_up=w_up_bf, w_down=w_down_bf)


def kernel(x_prompt, x_sample, state_mlstm_C, state_mlstm_n, state_mlstm_m, state_hgrn_S, cache_ffn_conv,
           c_prompt, c_sample, hgrn_lb_logits, w_ada, b_ada, w_in, b_gate_a, norm_a, norm_b, w_out,
           ln1_g, ln1_b, w_up, conv_w, conv_b, w_down, ln2_g, ln2_b):
    bp, tp, _ = x_prompt.shape
    bs, ts, _ = x_sample.shape
    pad_ff = D_FF_PAD - D_FF
    w_main, w_gate = _prep_w_in(jnp.transpose(w_in[0]))
    weights = dict(
        w_main=w_main, w_gate=w_gate, w_out=w_out[0].astype(BF16), w_up=w_up[0],
        w_down=w_down[0],
        gbias=jnp.pad(b_gate_a[0].reshape(1, N_GATE), ((0, 0), (0, LANE - N_GATE))),
        lb_logits=hgrn_lb_logits, norm_a=norm_a, norm_b=norm_b,
        ln1_g=ln1_g, ln1_b=ln1_b, ln2_g=ln2_g, ln2_b=ln2_b,
        conv_w=jnp.pad(conv_w[0], ((0, 0), (0, pad_ff))),
        conv_b=jnp.pad(conv_b, ((0, 0), (0, pad_ff))))

    c_p = jnp.pad(c_prompt, ((0, 8 - bp), (0, 0)))
    mod_p, mod_s = _ada(c_p, c_sample, w_ada[0], b_ada)

    yp, st_p, tail_p, w_ffn_bf = _path(x_prompt.reshape(bp * tp, D_MODEL), mod_p, weights, n_seq=bp,
                                       seq_len=tp, tm_in=1024, tm_res=512, tm_up=1024, tm_down=512,
                                       bk_down=D_FF_PAD // 2)

    xs_t = jnp.swapaxes(x_sample, 0, 1).reshape(ts * bs, D_MODEL)
    cache_t = jnp.swapaxes(cache_ffn_conv[0], 0, 1)
    ys_t, st_s, tail_s, _ = _path(xs_t, mod_s, dict(weights, **w_ffn_bf), n_seq=bs, seq_len=ts,
                                  tm_in=ts * bs, tm_res=bs, tm_up=ts * bs, tm_down=ts * bs, bk_down=FF_BLOCK,
                                  states=(state_mlstm_C, state_mlstm_n, state_mlstm_m, state_hgrn_S),
                                  cache=cache_t)

    ys = jnp.swapaxes(ys_t.reshape(ts, bs, D_MODEL), 0, 1)
    conv_p = tail_p[None, :, :, :D_FF]
    conv_s = jnp.swapaxes(tail_s[:, :, :D_FF], 0, 1)[None]
    return (yp.reshape(bp, tp, D_MODEL), ys,
            st_p[0], st_p[1], st_p[2], st_p[3], conv_p,
            st_s[0], st_s[1], st_s[2], st_s[3], conv_s)
```

```python
import functools

import numpy as np
import jax
import jax.numpy as jnp
from jax import lax
from jax.experimental import pallas as pl
from jax.experimental.pallas import tpu as pltpu

F32 = jnp.float32
BF16 = jnp.bfloat16

D_MODEL = 2048
N_HEADS_A, DK_A, DV_A = 4, 128, 256
N_HEADS_B, DK_B, DV_B = 8, 128, 128
A_WIDTH = N_HEADS_A * DV_A
B_WIDTH = N_HEADS_B * DV_B
N_GATE = 2 * N_HEADS_A
GATE_COL = 2 * N_HEADS_A * DK_A + A_WIDTH
D_IN = GATE_COL + N_GATE + A_WIDTH + 4 * B_WIDTH
D_FF = 5504
CONV_W = 3
EPS = 1e-5
ALPHA = 2.0 ** 0.25
LOG2_E = 1.4426950408889634
LANE = 128
FF_BLOCK = 512
D_FF_PAD = 5632
Z_MAIN = D_IN - N_GATE
OFF_QA, OFF_KA, OFF_VA, OFF_OA = 0, 512, 1024, 2048
OFF_FB, OFF_QB, OFF_VB, OFF_GB = 3072, 4096, 5120, 6144
VMEM_LIMIT = 60 * 1024 * 1024
PROMPT_CHUNK = 128
PROMPT_SEQS_PER_STEP = 2
SAMPLE_GROUP = 8


def _ln(x):
    mu = jnp.mean(x, axis=-1, keepdims=True)
    xc = x - mu
    var = jnp.mean(xc * xc, axis=-1, keepdims=True)
    return xc * lax.rsqrt(var + EPS)


def _sigmoid(x):
    return 1.0 / (1.0 + jnp.exp(-x))


def _dot(a, b):
    return jnp.dot(a.astype(BF16), b.astype(BF16), preferred_element_type=F32)


def _dot_nt(a, b):
    return lax.dot_general(a.astype(BF16), b.astype(BF16), (((1,), (1,)), ((), ())),
                           preferred_element_type=F32)


def _dot_tn(a, b):
    return lax.dot_general(a.astype(BF16), b.astype(BF16), (((0,), (0,)), ((), ())),
                           preferred_element_type=F32)


def _mod_rows(ref, seq_rows, seq):
    if seq_rows:
        return ref[...]
    return ref[pl.ds(seq, 1), :]


def _ada_kernel(cp_ref, cs_ref, w_ref, b_ref, op_ref, os_ref):
    w = w_ref[...].astype(BF16)
    for c_ref, o_ref in ((cp_ref, op_ref), (cs_ref, os_ref)):
        c = c_ref[...]
        o_ref[...] = _dot(c * _sigmoid(c), w) + b_ref[...]


def _ada(c_p, c_s, w_ada, b_ada):
    n = w_ada.shape[1]
    bn = 1024
    row = lambda r: pl.BlockSpec((r, D_MODEL), lambda j: (0, 0))
    out = lambda r: pl.BlockSpec((r, bn), lambda j: (0, j))
    return pl.pallas_call(
        _ada_kernel,
        grid=(n // bn,),
        in_specs=[row(c_p.shape[0]), row(c_s.shape[0]),
                  pl.BlockSpec((D_MODEL, bn), lambda j: (0, j)),
                  pl.BlockSpec((1, bn), lambda j: (0, j))],
        out_specs=[out(c_p.shape[0]), out(c_s.shape[0])],
        out_shape=[jax.ShapeDtypeStruct((c_p.shape[0], n), F32),
                   jax.ShapeDtypeStruct((c_s.shape[0], n), F32)],
        compiler_params=pltpu.CompilerParams(vmem_limit_bytes=VMEM_LIMIT),
        name="ada",
    )(c_p, c_s, w_ada, b_ada)


def _prep_in_kernel(front_ref, back_ref, gate_ref, o_ref, og_ref, *, first_shifted):
    j = pl.program_id(0)

    @pl.when(j == 0)
    def _():
        og_ref[...] = gate_ref[...].T.astype(BF16)

    def emit(src_ref):
        for r in range(0, src_ref.shape[0], 256):
            o_ref[:, r:r + 256] = src_ref[r:r + 256, :].T.astype(BF16)

    @pl.when(j < first_shifted)
    def _():
        emit(front_ref)

    @pl.when(j >= first_shifted)
    def _():
        emit(back_ref)


def _prep_w_in(w_in_t):
    bn = 1024
    first_shifted = GATE_COL // bn
    kern = functools.partial(_prep_in_kernel, first_shifted=first_shifted)
    return pl.pallas_call(
        kern,
        grid=(Z_MAIN // bn,),
        in_specs=[pl.BlockSpec((bn, D_MODEL), lambda j: (jnp.minimum(j, first_shifted - 1), 0)),
                  pl.BlockSpec((pl.Element(bn), pl.Element(D_MODEL)),
                               lambda j: (pl.multiple_of(jnp.maximum(j, first_shifted) * bn + N_GATE, N_GATE), 0)),
                  pl.BlockSpec((LANE, D_MODEL), lambda j: (GATE_COL // LANE, 0))],
        out_specs=[pl.BlockSpec((D_MODEL, bn), lambda j: (0, j)),
                   pl.BlockSpec((D_MODEL, LANE), lambda j: (0, 0))],
        out_shape=[jax.ShapeDtypeStruct((D_MODEL, Z_MAIN), BF16),
                   jax.ShapeDtypeStruct((D_MODEL, LANE), BF16)],
        compiler_params=pltpu.CompilerParams(vmem_limit_bytes=VMEM_LIMIT),
        name="prep_w_in",
    )(w_in_t, w_in_t, w_in_t)


def _in_kernel(x_ref, sh_ref, sc_ref, w_ref, wg_ref, z_ref, g_ref, h_scr, *, sub, seq_rows, tiles_per_seq):
    i = pl.program_id(0)
    j = pl.program_id(1)

    @pl.when(j == 0)
    def _():
        seq = i // tiles_per_seq
        sh = _mod_rows(sh_ref, seq_rows, seq)
        sc = _mod_rows(sc_ref, seq_rows, seq)
        for r in range(0, x_ref.shape[0], sub):
            rows = slice(r, r + sub)
            h = (_ln(x_ref[rows, :]) * (1.0 + sc) + sh).astype(BF16)
            h_scr[rows, :] = h
            g_ref[rows, :] = _dot(h, wg_ref[...])
            z_ref[rows, :] = _dot(h, w_ref[...]).astype(z_ref.dtype)

    @pl.when(j > 0)
    def _():
        z_ref[...] = _dot(h_scr[...], w_ref[...]).astype(z_ref.dtype)


def _in_proj(x, mod, w_main, w_gate, *, tm, sub, seq_rows, seq_len, z_dtype):
    tokens = x.shape[0]
    bn = 1024
    mod_spec = lambda k: pl.BlockSpec((mod.shape[0], D_MODEL), lambda i, j: (0, k))
    kern = functools.partial(_in_kernel, sub=min(sub, tm), seq_rows=seq_rows, tiles_per_seq=max(seq_len // tm, 1))
    return pl.pallas_call(
        kern,
        grid=(tokens // tm, Z_MAIN // bn),
        in_specs=[pl.BlockSpec((tm, D_MODEL), lambda i, j: (i, 0)),
                  mod_spec(0), mod_spec(1),
                  pl.BlockSpec((D_MODEL, bn), lambda i, j: (0, j)),
                  pl.BlockSpec((D_MODEL, LANE), lambda i, j: (0, 0))],
        out_specs=[pl.BlockSpec((tm, bn), lambda i, j: (i, j)),
                   pl.BlockSpec((tm, LANE), lambda i, j: (i, 0))],
        out_shape=[jax.ShapeDtypeStruct((tokens, Z_MAIN), z_dtype),
                   jax.ShapeDtypeStruct((tokens, LANE), F32)],
        scratch_shapes=[pltpu.VMEM((tm, D_MODEL), BF16)],
        compiler_params=pltpu.CompilerParams(
            dimension_semantics=("arbitrary", "arbitrary"), vmem_limit_bytes=VMEM_LIMIT),
        name="in_proj",
    )(x, mod, mod, w_main, w_gate)


def _mix_consts(L, Tg, t_major):
    G = L // Tg
    r = np.arange(L)
    seq, tim = (r % G, r // G) if t_major else (r // Tg, r % Tg)
    same = seq[:, None] == seq[None, :]
    t = tim[:, None]
    u = tim[None, :]
    mats = [same & (u <= t), same & (u > t)]
    masks = [same & (u <= t), same & (t <= u), same]
    levels = []
    m = Tg // 2
    while m >= 1:
        levels.append(m)
        m //= 2
    for m in levels:
        blk = same & (u // m == t // m)
        odd = (t // m) % 2 == 1
        mats.append(np.where(odd, blk & (u <= t), blk & (u > t)))
        masks.append(same & odd & ((u // m) % 2 == 0) & (t // (2 * m) == u // (2 * m)))
    mats = np.concatenate(mats, axis=0).astype(np.float32)
    masks = np.stack(masks).astype(np.float32)
    last_rows = tuple(int(np.nonzero((seq == g) & (tim == Tg - 1))[0][0]) for g in range(G))
    return mats, masks, tuple(levels), last_rows


def _mix_kernel(z_ref, g_ref, gbias_ref, lbl_ref, na_ref, nb_ref, mall_ref, msk_ref,
                c_in, n_in, m_in, s_in, mix_ref, c_out, n_out, m_out, s_out, e_scr,
                *, L, Tg, levels, last_rows):
    G = L // Tg

    def zs(off, w):
        return z_ref[:, :, off:off + w].reshape(L, w)

    def put_mix(off, w, val):
        mix_ref[:, :, off:off + w] = val.astype(mix_ref.dtype).reshape(Tg, G, w)

    _mix_body(zs, put_mix, g_ref[...].reshape(L, LANE), gbias_ref, lbl_ref, na_ref, nb_ref, mall_ref, msk_ref,
              c_in, n_in, m_in, s_in, c_out, n_out, m_out, s_out, e_scr, pl.program_id(0) * G,
              L=L, Tg=Tg, levels=levels, last_rows=last_rows, t_major=True)


def _mixchunk_kernel(z_ref, g_ref, gbias_ref, lbl_ref, na_ref, nb_ref, mall_ref, msk_ref,
                     mix_ref, c_out, n_out, m_out, s_out, e_scr, *, L, levels, last_rows):
    nb = z_ref.shape[0]
    b0 = pl.program_id(0) * nb

    @pl.when(pl.program_id(1) == 0)
    def _():
        c_out[...] = jnp.zeros(c_out.shape, F32)
        n_out[...] = jnp.zeros(n_out.shape, F32)
        s_out[...] = jnp.zeros(s_out.shape, F32)
        m_out[0, pl.ds(b0, nb), :] = jnp.zeros((nb, N_HEADS_A), F32)

    for k in range(nb):
        def zs(off, w, k=k):
            return z_ref[k, :, off:off + w]

        def put_mix(off, w, val, k=k):
            mix_ref[k, :, off:off + w] = val.astype(mix_ref.dtype)

        st = [r.at[:, pl.ds(k, 1)] for r in (c_out, n_out, s_out)]
        _mix_body(zs, put_mix, g_ref[k], gbias_ref, lbl_ref, na_ref, nb_ref, mall_ref, msk_ref,
                  st[0], st[1], m_out, st[2], st[0], st[1], m_out, st[2], e_scr.at[k], b0 + k,
                  L=L, Tg=L, levels=levels, last_rows=last_rows, t_major=False)


def _mix_body(zs, put_mix, gates, gbias_ref, lbl_ref, na_ref, nb_ref, mall_ref, msk_ref,
              c_in, n_in, m_in, s_in, c_out, n_out, m_out, s_out, e_scr, m_row0,
              *, L, Tg, levels, last_rows, t_major):
    G = L // Tg
    neg_inf = F32(-jnp.inf)
    causal = msk_ref[0] > 0.5
    causal_t = msk_ref[1] > 0.5
    same = msk_ref[2] > 0.5
    row1 = lax.broadcasted_iota(jnp.int32, (L, 1), 0)
    seq_of_row = (row1 & (G - 1)) if t_major else (row1 // Tg)
    in_group = [seq_of_row == g for g in range(G)]
    col1 = lax.broadcasted_iota(jnp.int32, (1, L), 1)
    seq_of_col = (col1 & (G - 1)) if t_major else (col1 // Tg)
    in_group_lane = [seq_of_col == g for g in range(G)]

    def state_updates(kw, v):
        if G == 1:
            return [_dot_tn(kw, v)]
        kw_t = kw.T
        return [_dot(jnp.where(in_group_lane[g], kw_t, 0.0), v) for g in range(G)]

    def by_group(vals):
        if G == 1:
            return vals[0]
        out = jnp.where(in_group[0], vals[0], 0.0)
        for g in range(1, G):
            out = out + jnp.where(in_group[g], vals[g], 0.0)
        return out

    pre = gates + gbias_ref[...]
    lsig = jnp.minimum(pre, 0.0) - jnp.log(1.0 + jnp.exp(-jnp.abs(pre)))
    pre_t = pre.T
    lsig_t = lsig.T
    m_prev = m_in[0, pl.ds(m_row0, G), :]
    scale = DK_A ** -0.5
    m_new_rows = []
    for h in range(N_HEADS_A):
        q = zs(OFF_QA + DK_A * h, DK_A)
        k = zs(OFF_KA + DK_A * h, DK_A)
        v = zs(OFF_VA + DV_A * h, DV_A)
        logi_c = pre[:, h:h + 1]
        logi_r = pre_t[h:h + 1, :]
        lf_c = lsig[:, N_HEADS_A + h:N_HEADS_A + h + 1]
        lf_r = lsig_t[N_HEADS_A + h:N_HEADS_A + h + 1, :]
        b_c = jnp.sum(jnp.where(causal, lf_r, 0.0), axis=1, keepdims=True)
        b_r = jnp.sum(jnp.where(causal_t, lf_c, 0.0), axis=0, keepdims=True)
        tot_c = jnp.sum(jnp.where(same, lf_r, 0.0), axis=1, keepdims=True)
        tot_r = jnp.sum(jnp.where(same, lf_c, 0.0), axis=0, keepdims=True)
        m_col = by_group([m_prev[g:g + 1, h:h + 1] for g in range(G)])
        dmat = jnp.where(causal, b_c - b_r + logi_r, neg_inf)
        inter = b_c + m_col
        m_t = jnp.maximum(inter, jnp.max(dmat, axis=1, keepdims=True))
        w_inter = jnp.exp(inter - m_t)
        smat = _dot_nt(q, k) * (scale * jnp.exp(dmat - m_t))
        q_c = by_group([_dot(q, c_in[0, g, h]) for g in range(G)]) * scale
        n_rows = by_group([n_in[0, g, h:h + 1, :] for g in range(G)])
        q_n = jnp.sum(q.astype(F32) * n_rows, axis=1, keepdims=True) * scale
        num = w_inter * q_c + _dot(smat, v)
        den = w_inter * q_n + jnp.sum(smat, axis=1, keepdims=True)
        hh = num / jnp.maximum(jnp.abs(den), jnp.exp(-m_t))
        ms = jnp.mean(hh * hh, axis=1, keepdims=True)
        oa = zs(OFF_OA + DV_A * h, DV_A).astype(F32)
        ya = hh * lax.rsqrt(ms + EPS) * na_ref[:, DV_A * h:DV_A * (h + 1)] * _sigmoid(oa)
        put_mix(DV_A * h, DV_A, ya)
        dec_c = tot_c - b_c + logi_c
        dec_r = tot_r - b_r + logi_r
        m_new_c = jnp.maximum(tot_c + m_col,
                              jnp.max(jnp.where(same, dec_r, neg_inf), axis=1, keepdims=True))
        wk = jnp.exp(dec_c - m_new_c)
        sc = jnp.exp(tot_c + m_col - m_new_c)
        kw = k.astype(F32) * wk
        upd = state_updates(kw, v)
        m_new_h = []
        for g in range(G):
            last = last_rows[g]
            kg = kw if G == 1 else jnp.where(in_group[g], kw, 0.0)
            sc_g = sc[last:last + 1, :]
            c_new = sc_g * c_in[0, g, h] + upd[g]
            n_new = sc_g * n_in[0, g, h:h + 1, :] + jnp.sum(kg, axis=0, keepdims=True)
            c_out[0, g, h] = c_new
            n_out[0, g, h:h + 1, :] = n_new
            m_new_h.append(m_new_c[last:last + 1, :])
        m_new_rows.append(m_new_h)
    for g in range(G):
        m_out[0, pl.ds(m_row0 + g, 1), :] = jnp.concatenate(
            [m_new_rows[h][g] for h in range(N_HEADS_A)], axis=1)

    l0 = lbl_ref[0:1, :]
    l1 = lbl_ref[1:2, :]
    lmax = jnp.maximum(l0, l1)
    e0 = jnp.exp(l0 - lmax)
    e1 = jnp.exp(l1 - lmax)
    lb = e0 / (e0 + e1)
    fb = zs(OFF_FB, B_WIDTH).astype(F32)
    e = jnp.exp(-jnp.abs(fb))
    r = 1.0 / (1.0 + e)
    pos = fb >= 0.0
    sig = jnp.where(pos, r, e * r)
    nsig = jnp.where(pos, e * r, r)
    logf = jnp.log(lb + (1.0 - lb) * sig) * LOG2_E
    kb = (1.0 - lb) * nsig
    hi = logf.astype(BF16)
    r1 = logf - hi.astype(F32)
    mid = r1.astype(BF16)
    lo = (r1 - mid.astype(F32)).astype(BF16)
    mall = mall_ref[...]
    if L % 16 == 0:
        e_scr[...] = _dot(mall, jnp.concatenate([hi, mid, lo], axis=0))
    else:
        mall = mall[:, :L]
        e_scr[...] = _dot(mall, hi) + _dot(mall, mid) + _dot(mall, lo)

    dec_t = []
    for g in range(G):
        last = last_rows[g]
        bl = e_scr[last:last + 1, :]
        bl8 = jnp.concatenate([bl[:, DK_B * h:DK_B * (h + 1)] for h in range(N_HEADS_B)], axis=0)
        dec_t.append(jnp.exp2(bl8).T)

    for h in range(N_HEADS_B):
        sl = slice(DK_B * h, DK_B * (h + 1))
        q = zs(OFF_QB + DK_B * h, DK_B).astype(BF16)
        k = kb[:, sl]
        kbf = k.astype(BF16)
        v = zs(OFF_VB + DV_B * h, DV_B)
        amat = jnp.zeros((L, L), F32)
        for li in range(len(levels)):
            ex = jnp.exp2(e_scr[(2 + li) * L:(3 + li) * L, sl]).astype(BF16)
            amat = jnp.where(msk_ref[3 + li] > 0.5, _dot_nt(q * ex, kbf * ex), amat)
        qf = q.astype(F32)
        diag = jnp.sum(qf * k, axis=1, keepdims=True)
        qs = qf * jnp.exp2(e_scr[0:L, sl])
        o_inter = by_group([_dot(qs, s_in[0, g, h]) for g in range(G)])
        o = _dot(amat, v) + diag * v.astype(F32) + o_inter
        ms = jnp.mean(o * o, axis=1, keepdims=True)
        gb = zs(OFF_GB + DV_B * h, DV_B).astype(F32)
        yb = o * lax.rsqrt(ms + EPS) * nb_ref[:, sl] * (gb * _sigmoid(gb))
        put_mix(A_WIDTH + DV_B * h, DV_B, yb)
        upd = state_updates(k * jnp.exp2(e_scr[L:2 * L, sl]), v)
        for g in range(G):
            s_out[0, g, h] = dec_t[g][:, h:h + 1] * s_in[0, g, h] + upd[g]


def _mixer_args(L, Tg, t_major, const2, const3, gbias, lb_logits, norm_a, norm_b):
    mats, masks, levels, last_rows = _mix_consts(L, Tg, t_major)
    n_mats = mats.shape[0] // L
    if L % 16 == 0:
        mats = np.concatenate([mats, mats, mats], axis=1)
    specs = [pl.BlockSpec((1, LANE), const2),
             pl.BlockSpec((2, B_WIDTH), const2),
             pl.BlockSpec((1, A_WIDTH), const2),
             pl.BlockSpec((1, B_WIDTH), const2),
             pl.BlockSpec(mats.shape, const2),
             pl.BlockSpec(masks.shape, const3)]
    args = [gbias, lb_logits, norm_a, norm_b, jnp.asarray(mats, BF16), jnp.asarray(masks)]
    return specs, args, levels, last_rows, n_mats


def _state_shapes(n_seq):
    return [jax.ShapeDtypeStruct((1, n_seq, N_HEADS_A, DK_A, DV_A), F32),
            jax.ShapeDtypeStruct((1, n_seq, N_HEADS_A, DK_A), F32),
            jax.ShapeDtypeStruct((1, n_seq, N_HEADS_A), F32),
            jax.ShapeDtypeStruct((1, n_seq, N_HEADS_B, DK_B, DV_B), F32)]


def _state_specs(G, n_seq, seq_of_step):
    return [pl.BlockSpec((1, G, N_HEADS_A, DK_A, DV_A), lambda *ids: (0, seq_of_step(*ids), 0, 0, 0)),
            pl.BlockSpec((1, G, N_HEADS_A, DK_A), lambda *ids: (0, seq_of_step(*ids), 0, 0)),
            pl.BlockSpec((1, n_seq, N_HEADS_A), lambda *ids: (0, 0, 0)),
            pl.BlockSpec((1, G, N_HEADS_B, DK_B, DV_B), lambda *ids: (0, seq_of_step(*ids), 0, 0, 0))]


def _mixer(z, gates, gbias, lb_logits, norm_a, norm_b, *, n_seq, seq_len, states):
    Tg, G = seq_len, SAMPLE_GROUP
    L = Tg * G
    tile = lambda w: pl.BlockSpec((Tg, G, w), lambda i: (0, i, 0))
    const_specs, const_args, levels, last_rows, n_mats = _mixer_args(
        L, Tg, True, lambda i: (0, 0), lambda i: (0, 0, 0), gbias, lb_logits, norm_a, norm_b)
    st_specs = _state_specs(G, n_seq, lambda i: i)
    kern = functools.partial(_mix_kernel, L=L, Tg=Tg, levels=levels, last_rows=last_rows)
    return pl.pallas_call(
        kern,
        grid=(n_seq // G,),
        in_specs=[tile(Z_MAIN), tile(LANE)] + const_specs + st_specs,
        out_specs=[tile(D_MODEL)] + st_specs,
        out_shape=[jax.ShapeDtypeStruct((Tg, n_seq, D_MODEL), F32)] + _state_shapes(n_seq),
        scratch_shapes=[pltpu.VMEM((n_mats * L, B_WIDTH), F32)],
        compiler_params=pltpu.CompilerParams(dimension_semantics=("arbitrary",), vmem_limit_bytes=VMEM_LIMIT),
        name="mixer_state",
    )(z, gates, *const_args, *states)


def _mixer_chunk(z, gates, gbias, lb_logits, norm_a, norm_b, *, n_seq, seq_len):
    L = PROMPT_CHUNK
    nb = PROMPT_SEQS_PER_STEP
    nchunks = seq_len // L
    tile = lambda w: pl.BlockSpec((nb, L, w), lambda b, c: (b, c, 0))
    const_specs, const_args, levels, last_rows, n_mats = _mixer_args(
        L, L, False, lambda b, c: (0, 0), lambda b, c: (0, 0, 0), gbias, lb_logits, norm_a, norm_b)
    st_specs = _state_specs(nb, n_seq, lambda b, c: b)
    kern = functools.partial(_mixchunk_kernel, L=L, levels=levels, last_rows=last_rows)
    outs = pl.pallas_call(
        kern,
        grid=(n_seq // nb, nchunks),
        in_specs=[tile(Z_MAIN), tile(LANE)] + const_specs,
        out_specs=[tile(D_MODEL)] + st_specs,
        out_shape=[jax.ShapeDtypeStruct((n_seq, seq_len, D_MODEL), BF16)] + _state_shapes(n_seq),
        scratch_shapes=[pltpu.VMEM((nb, n_mats * L, B_WIDTH), F32)],
        compiler_params=pltpu.CompilerParams(dimension_semantics=("arbitrary", "arbitrary"),
                                             vmem_limit_bytes=VMEM_LIMIT),
        name="mixer_chunk",
    )(z.reshape(n_seq, seq_len, Z_MAIN), gates.reshape(n_seq, seq_len, LANE), *const_args)
    return [outs[0].reshape(n_seq * seq_len, D_MODEL)] + list(outs[1:])


def _res_kernel(*refs, sub, seq_rows, tiles_per_seq, emit_h2, nk):
    if emit_h2:
        a_ref, w_ref, x_ref, gm_ref, lg_ref, lb_ref, sh_ref, sc_ref, o_ref, h_ref = refs
    else:
        a_ref, w_ref, x_ref, gm_ref, lg_ref, lb_ref, o_ref = refs
    i = pl.program_id(0)
    k = pl.program_id(1)

    if nk > 1:
        @pl.when(k == 0)
        def _():
            o_ref[...] = _dot(a_ref[...], w_ref[...])

    if nk > 2:
        @pl.when((k > 0) & (k < nk - 1))
        def _():
            o_ref[...] += _dot(a_ref[...], w_ref[...])

    @pl.when(k == nk - 1)
    def _():
        seq = i // tiles_per_seq
        gm = _mod_rows(gm_ref, seq_rows, seq)
        if emit_h2:
            sh = _mod_rows(sh_ref, seq_rows, seq)
            sc = _mod_rows(sc_ref, seq_rows, seq)
        for r in range(0, o_ref.shape[0], sub):
            rows = slice(r, r + sub)
            acc = _dot(a_ref[rows, :], w_ref[...])
            if nk > 1:
                acc = acc + o_ref[rows, :]
            x1 = _ln(ALPHA * x_ref[rows, :] + gm * acc) * lg_ref[...] + lb_ref[...]
            o_ref[rows, :] = x1
            if emit_h2:
                h_ref[rows, :] = (_ln(x1) * (1.0 + sc) + sh).astype(BF16)


def _res_block(a, w, x, mod, ln_g, ln_b, *, tm, bk, sub, seq_rows, seq_len, gate_col, emit_h2):
    tokens, kdim = a.shape
    nk = kdim // bk
    mod_spec = lambda c: pl.BlockSpec((mod.shape[0], D_MODEL), lambda i, k: (0, c))
    row_spec = pl.BlockSpec((tm, D_MODEL), lambda i, k: (i, 0))
    vec_spec = pl.BlockSpec((1, D_MODEL), lambda i, k: (0, 0))
    in_specs = [pl.BlockSpec((tm, bk), lambda i, k: (i, k)),
                pl.BlockSpec((bk, D_MODEL), lambda i, k: (k, 0)),
                row_spec, mod_spec(gate_col), vec_spec, vec_spec]
    args = [a, w, x, mod, ln_g, ln_b]
    out_specs = [row_spec]
    out_shape = [jax.ShapeDtypeStruct((tokens, D_MODEL), F32)]
    if emit_h2:
        in_specs += [mod_spec(3), mod_spec(4)]
        args += [mod, mod]
        out_specs.append(row_spec)
        out_shape.append(jax.ShapeDtypeStruct((tokens, D_MODEL), BF16))
    kern = functools.partial(_res_kernel, sub=sub, seq_rows=seq_rows, tiles_per_seq=max(seq_len // tm, 1),
                             emit_h2=emit_h2, nk=nk)
    return pl.pallas_call(
        kern,
        grid=(tokens // tm, nk),
        in_specs=in_specs,
        out_specs=out_specs,
        out_shape=out_shape,
        compiler_params=pltpu.CompilerParams(
            dimension_semantics=("arbitrary", "arbitrary"), vmem_limit_bytes=VMEM_LIMIT),
        name="res_block",
    )(*args)


def _up_kernel(*refs, tm, sub, seq_len, n_seq, t_major, tiles_per_seq):
    if t_major:
        h_ref, wa_scr, wu_scr, cw_ref, cb_ref, cache_ref, g_ref, tail_ref, abuf = refs
    else:
        (h_ref, wa_ref, uu_ref, wd_ref, cw_ref, cb_ref,
         g_ref, tail_ref, wa_scr, wu_scr, wd_out, carry) = refs
    j = pl.program_id(0)
    i = pl.program_id(1)
    bn = wa_scr.shape[1]
    valid = (j * bn + lax.broadcasted_iota(jnp.int32, (1, bn), 1)) < D_FF
    hist = (CONV_W - 1) * n_seq

    @pl.when(i == 0)
    def _():
        if t_major:
            abuf[0:hist, :] = jnp.where(valid, cache_ref[...].reshape(hist, bn), 0.0)
        else:
            wu = uu_ref[...]
            wu_last = jnp.concatenate([wu[:, LANE:], jnp.zeros((D_MODEL, LANE), F32)], axis=1)
            wu = jnp.where(j == pl.num_programs(0) - 1, wu_last, wu)
            wa_scr[...] = jnp.where(valid, wa_ref[...], 0.0).astype(BF16)
            wu_scr[...] = jnp.where(valid, wu, 0.0).astype(BF16)
            wd_rows = j * bn + lax.broadcasted_iota(jnp.int32, (bn, 1), 0)
            wd_out[...] = jnp.where(wd_rows < D_FF, wd_ref[...], 0.0).astype(BF16)

    if not t_major:
        @pl.when(i % tiles_per_seq == 0)
        def _():
            carry[...] = jnp.zeros(carry.shape, F32)

    row8 = lax.broadcasted_iota(jnp.int32, (8, 1), 0)
    for r in range(0, tm, sub):
        h = h_ref[r:r + sub, :]
        a = _dot(h, wa_scr[...])
        u = _dot(h, wu_scr[...])
        if t_major:
            abuf[hist + r:hist + r + sub, :] = a
            a1 = abuf[hist + r - n_seq:hist + r - n_seq + sub, :]
            a2 = abuf[hist + r - 2 * n_seq:hist + r - 2 * n_seq + sub, :]
        else:
            p1 = carry[7:8, :]
            p2 = carry[6:7, :]
            r1 = pltpu.roll(a, 1, axis=0)
            r2 = pltpu.roll(a, 2, axis=0)
            a1 = jnp.concatenate([jnp.where(row8 == 0, p1, r1[0:8, :]), r1[8:, :]], axis=0)
            a2 = jnp.concatenate(
                [jnp.where(row8 == 0, p2, jnp.where(row8 == 1, p1, r2[0:8, :])), r2[8:, :]], axis=0)
            carry[...] = a[sub - 8:sub, :]
        conv = cb_ref[...] + cw_ref[0:1, :] * a2 + cw_ref[1:2, :] * a1 + cw_ref[2:3, :] * a
        gl = 0.5 * conv * (1.0 + lax.erf(conv * (2.0 ** -0.5)))
        g_ref[r:r + sub, :] = (gl * u).astype(BF16)

    if t_major:
        tail_ref[...] = abuf[tm:tm + hist, :].reshape(tail_ref.shape)
    else:
        @pl.when((i + 1) % tiles_per_seq == 0)
        def _():
            tail_ref[0] = carry[8 - (CONV_W - 1):8, :]


def _up_proj(h2, w_up, w_down, conv_w_p, conv_b_p, *, tm, sub, seq_len, n_seq, cache=None):
    tokens = h2.shape[0]
    t_major = cache is not None
    bn = FF_BLOCK
    nj = D_FF_PAD // bn
    assert D_FF_PAD - D_FF == LANE
    tiles_per_seq = max(seq_len // tm, 1)
    wspec = lambda f: pl.BlockSpec((D_MODEL, bn), f)
    col_spec = wspec(lambda j, i: (0, j))
    h_spec = pl.BlockSpec((tm, D_MODEL), lambda j, i: (i, 0))
    conv_specs = [pl.BlockSpec((CONV_W, bn), lambda j, i: (0, j)), pl.BlockSpec((1, bn), lambda j, i: (0, j))]
    out_specs = [pl.BlockSpec((tm, bn), lambda j, i: (i, j))]
    out_shape = [jax.ShapeDtypeStruct((tokens, D_FF_PAD), BF16)]
    if t_major:
        hist_spec = pl.BlockSpec((CONV_W - 1, n_seq, bn), lambda j, i: (0, 0, j))
        in_specs = [h_spec, col_spec, col_spec] + conv_specs + [hist_spec]
        args = [h2, w_up[0], w_up[1], conv_w_p, conv_b_p, cache]
        out_specs.append(hist_spec)
        out_shape.append(jax.ShapeDtypeStruct((CONV_W - 1, n_seq, D_FF_PAD), F32))
        scratch_rows = (CONV_W - 1) * n_seq + tm
    else:
        row_spec = pl.BlockSpec((bn, D_MODEL), lambda j, i: (j, 0))
        u_off = lambda j, i: (0, pl.multiple_of(jnp.minimum(D_FF + bn * j, 2 * D_FF - bn), LANE))
        in_specs = [h_spec, col_spec,
                    pl.BlockSpec((pl.Element(D_MODEL), pl.Element(bn)), u_off),
                    row_spec] + conv_specs
        args = [h2, w_up, w_up, w_down, conv_w_p, conv_b_p]
        out_specs += [pl.BlockSpec((1, CONV_W - 1, bn), lambda j, i: (i // tiles_per_seq, 0, j)),
                      col_spec, col_spec, row_spec]
        out_shape += [jax.ShapeDtypeStruct((n_seq, CONV_W - 1, D_FF_PAD), F32),
                      jax.ShapeDtypeStruct((D_MODEL, D_FF_PAD), BF16),
                      jax.ShapeDtypeStruct((D_MODEL, D_FF_PAD), BF16),
                      jax.ShapeDtypeStruct((D_FF_PAD, D_MODEL), BF16)]
        scratch_rows = 8
    kern = functools.partial(_up_kernel, tm=tm, sub=sub, seq_len=seq_len, n_seq=n_seq, t_major=t_major,
                             tiles_per_seq=tiles_per_seq)
    outs = pl.pallas_call(
        kern,
        grid=(nj, tokens // tm),
        in_specs=in_specs,
        out_specs=out_specs,
        out_shape=out_shape,
        scratch_shapes=[pltpu.VMEM((scratch_rows, bn), F32)],
        compiler_params=pltpu.CompilerParams(
            dimension_semantics=("arbitrary", "arbitrary"), vmem_limit_bytes=VMEM_LIMIT),
        name="up_proj",
    )(*args)
    if t_major:
        return outs[0], outs[1], w_up, w_down
    return outs[0], outs[1], (outs[2], outs[3]), outs[4]


def _path(x, mod, w, *, n_seq, seq_len, tm_in, tm_res, tm_up, tm_down, states=None, cache=None):
    sample = states is not None
    sub = n_seq if sample else 256
    mixer_w = (w["gbias"], w["lb_logits"], w["norm_a"], w["norm_b"])
    z, gates = _in_proj(x, mod, w["w_main"], w["w_gate"], tm=tm_in, sub=sub, seq_rows=sample,
                        seq_len=seq_len, z_dtype=F32 if sample else BF16)
    if sample:
        mix, c1, n1, m1, s1 = _mixer(z.reshape(seq_len, n_seq, Z_MAIN), gates.reshape(seq_len, n_seq, LANE),
                                     *mixer_w, n_seq=n_seq, seq_len=seq_len, states=states)
        mix = mix.reshape(n_seq * seq_len, D_MODEL)
    else:
        mix, c1, n1, m1, s1 = _mixer_chunk(z, gates, *mixer_w, n_seq=n_seq, seq_len=seq_len)
    x1, h2 = _res_block(mix, w["w_out"], x, mod, w["ln1_g"], w["ln1_b"], tm=tm_res, bk=D_MODEL, sub=sub,
                        seq_rows=sample, seq_len=seq_len, gate_col=2, emit_h2=True)
    g, conv_out, w_up_bf, w_down_bf = _up_proj(h2, w["w_up"], w["w_down"], w["conv_w"], w["conv_b"], tm=tm_up,
                                               sub=sub, seq_len=seq_len, n_seq=n_seq, cache=cache)
    (x2,) = _res_block(g, w_down_bf, x1, mod, w["ln2_g"], w["ln2_b"], tm=tm_down, bk=D_FF_PAD // 2, sub=sub,
                       seq_rows=sample, seq_len=seq_len, gate_col=5, emit_h2=False)
    return x2, (c1, n1, m1, s1), conv_out, dict(w_up=w_up_bf, w_down=w_down_bf)


def kernel(x_prompt, x_sample, state_mlstm_C, state_mlstm_n, state_mlstm_m, state_hgrn_S, cache_ffn_conv,
           c_prompt, c_sample, hgrn_lb_logits, w_ada, b_ada, w_in, b_gate_a, norm_a, norm_b, w_out,
           ln1_g, ln1_b, w_up, conv_w, conv_b, w_down, ln2_g, ln2_b):
    bp, tp, _ = x_prompt.shape
    bs, ts, _ = x_sample.shape
    pad_ff = D_FF_PAD - D_FF
    w_main, w_gate = _prep_w_in(jnp.transpose(w_in[0]))
    weights = dict(
        w_main=w_main, w_gate=w_gate, w_out=w_out[0].astype(BF16), w_up=w_up[0],
        w_down=w_down[0],
        gbias=jnp.pad(b_gate_a[0].reshape(1, N_GATE), ((0, 0), (0, LANE - N_GATE))),
        lb_logits=hgrn_lb_logits, norm_a=norm_a, norm_b=norm_b,
        ln1_g=ln1_g, ln1_b=ln1_b, ln2_g=ln2_g, ln2_b=ln2_b,
        conv_w=jnp.pad(conv_w[0], ((0, 0), (0, pad_ff))),
        conv_b=jnp.pad(conv_b, ((0, 0), (0, pad_ff))))

    c_p = jnp.pad(c_prompt, ((0, 8 - bp), (0, 0)))
    mod_p, mod_s = _ada(c_p, c_sample, w_ada[0], b_ada)

    yp, st_p, tail_p, w_ffn_bf = _path(x_prompt.reshape(bp * tp, D_MODEL), mod_p, weights, n_seq=bp,
                                       seq_len=tp, tm_in=1024, tm_res=512, tm_up=2048, tm_down=512)

    xs_t = jnp.swapaxes(x_sample, 0, 1).reshape(ts * bs, D_MODEL)
    cache_t = jnp.swapaxes(cache_ffn_conv[0], 0, 1)
    ys_t, st_s, tail_s, _ = _path(xs_t, mod_s, dict(weights, **w_ffn_bf), n_seq=bs, seq_len=ts,
                                  tm_in=ts * bs, tm_res=ts * bs, tm_up=ts * bs, tm_down=ts * bs,
                                  states=(state_mlstm_C, state_mlstm_n, state_mlstm_m, state_hgrn_S),
                                  cache=cache_t)

    ys = jnp.swapaxes(ys_t.reshape(ts, bs, D_MODEL), 0, 1)
    conv_p = tail_p[None, :, :, :D_FF]
    conv_s = jnp.swapaxes(tail_s[:, :, :D_FF], 0, 1)[None]
    return (yp.reshape(bp, tp, D_MODEL), ys,
            st_p[0], st_p[1], st_p[2], st_p[3], conv_p,
            st_s[0], st_s[1], st_s[2], st_s[3], conv_s)
```

```python
import functools

import numpy as np
import jax
import jax.numpy as jnp
from jax import lax
from jax.experimental import pallas as pl
from jax.experimental.pallas import tpu as pltpu

F32 = jnp.float32
BF16 = jnp.bfloat16

D_MODEL = 2048
N_HEADS_A, DK_A, DV_A = 4, 128, 256
N_HEADS_B, DK_B, DV_B = 8, 128, 128
A_WIDTH = N_HEADS_A * DV_A
B_WIDTH = N_HEADS_B * DV_B
N_GATE = 2 * N_HEADS_A
GATE_COL = 2 * N_HEADS_A * DK_A + A_WIDTH
D_IN = GATE_COL + N_GATE + A_WIDTH + 4 * B_WIDTH
D_FF = 5504
CONV_W = 3
EPS = 1e-5
ALPHA = 2.0 ** 0.25
LOG2_E = 1.4426950408889634
LANE = 128
FF_BLOCK = 512
D_FF_PAD = 5632
Z_MAIN = D_IN - N_GATE
OFF_QA, OFF_KA, OFF_VA, OFF_OA = 0, 512, 1024, 2048
OFF_FB, OFF_QB, OFF_VB, OFF_GB = 3072, 4096, 5120, 6144
VMEM_LIMIT = 60 * 1024 * 1024
PROMPT_CHUNK = 128
PROMPT_SEQS_PER_STEP = 2
SAMPLE_GROUP = 8


def _ln(x):
    mu = jnp.mean(x, axis=-1, keepdims=True)
    xc = x - mu
    var = jnp.mean(xc * xc, axis=-1, keepdims=True)
    return xc * lax.rsqrt(var + EPS)


def _sigmoid(x):
    return 1.0 / (1.0 + jnp.exp(-x))


def _dot(a, b):
    return jnp.dot(a.astype(BF16), b.astype(BF16), preferred_element_type=F32)


def _dot_nt(a, b):
    return lax.dot_general(a.astype(BF16), b.astype(BF16), (((1,), (1,)), ((), ())),
                           preferred_element_type=F32)


def _dot_tn(a, b):
    return lax.dot_general(a.astype(BF16), b.astype(BF16), (((0,), (0,)), ((), ())),
                           preferred_element_type=F32)


def _mod_rows(ref, seq_rows, seq):
    if seq_rows:
        return ref[...]
    return ref[pl.ds(seq, 1), :]


def _ada_kernel(cp_ref, cs_ref, w_ref, b_ref, op_ref, os_ref):
    w = w_ref[...].astype(BF16)
    for c_ref, o_ref in ((cp_ref, op_ref), (cs_ref, os_ref)):
        c = c_ref[...]
        o_ref[...] = _dot(c * _sigmoid(c), w) + b_ref[...]


def _ada(c_p, c_s, w_ada, b_ada):
    n = w_ada.shape[1]
    bn = 1024
    row = lambda r: pl.BlockSpec((r, D_MODEL), lambda j: (0, 0))
    out = lambda r: pl.BlockSpec((r, bn), lambda j: (0, j))
    return pl.pallas_call(
        _ada_kernel,
        grid=(n // bn,),
        in_specs=[row(c_p.shape[0]), row(c_s.shape[0]),
                  pl.BlockSpec((D_MODEL, bn), lambda j: (0, j)),
                  pl.BlockSpec((1, bn), lambda j: (0, j))],
        out_specs=[out(c_p.shape[0]), out(c_s.shape[0])],
        out_shape=[jax.ShapeDtypeStruct((c_p.shape[0], n), F32),
                   jax.ShapeDtypeStruct((c_s.shape[0], n), F32)],
        compiler_params=pltpu.CompilerParams(vmem_limit_bytes=VMEM_LIMIT),
        name="ada",
    )(c_p, c_s, w_ada, b_ada)


def _prep_in_kernel(front_ref, back_ref, gate_ref, o_ref, og_ref, *, first_shifted):
    j = pl.program_id(0)

    @pl.when(j == 0)
    def _():
        og_ref[...] = gate_ref[...].T.astype(BF16)

    def emit(src_ref):
        for r in range(0, src_ref.shape[0], 256):
            o_ref[:, r:r + 256] = src_ref[r:r + 256, :].T.astype(BF16)

    @pl.when(j < first_shifted)
    def _():
        emit(front_ref)

    @pl.when(j >= first_shifted)
    def _():
        emit(back_ref)


def _prep_w_in(w_in_t):
    bn = 1024
    first_shifted = GATE_COL // bn
    kern = functools.partial(_prep_in_kernel, first_shifted=first_shifted)
    return pl.pallas_call(
        kern,
        grid=(Z_MAIN // bn,),
        in_specs=[pl.BlockSpec((bn, D_MODEL), lambda j: (jnp.minimum(j, first_shifted - 1), 0)),
                  pl.BlockSpec((pl.Element(bn), pl.Element(D_MODEL)),
                               lambda j: (pl.multiple_of(jnp.maximum(j, first_shifted) * bn + N_GATE, N_GATE), 0)),
                  pl.BlockSpec((LANE, D_MODEL), lambda j: (GATE_COL // LANE, 0))],
        out_specs=[pl.BlockSpec((D_MODEL, bn), lambda j: (0, j)),
                   pl.BlockSpec((D_MODEL, LANE), lambda j: (0, 0))],
        out_shape=[jax.ShapeDtypeStruct((D_MODEL, Z_MAIN), BF16),
                   jax.ShapeDtypeStruct((D_MODEL, LANE), BF16)],
        compiler_params=pltpu.CompilerParams(vmem_limit_bytes=VMEM_LIMIT),
        name="prep_w_in",
    )(w_in_t, w_in_t, w_in_t)


def _in_kernel(x_ref, sh_ref, sc_ref, w_ref, wg_ref, z_ref, g_ref, h_scr, *, sub, seq_rows, tiles_per_seq):
    i = pl.program_id(0)
    j = pl.program_id(1)

    @pl.when(j == 0)
    def _():
        seq = i // tiles_per_seq
        sh = _mod_rows(sh_ref, seq_rows, seq)
        sc = _mod_rows(sc_ref, seq_rows, seq)
        for r in range(0, x_ref.shape[0], sub):
            rows = slice(r, r + sub)
            h = (_ln(x_ref[rows, :]) * (1.0 + sc) + sh).astype(BF16)
            h_scr[rows, :] = h
            g_ref[rows, :] = _dot(h, wg_ref[...])
            z_ref[rows, :] = _dot(h, w_ref[...]).astype(z_ref.dtype)

    @pl.when(j > 0)
    def _():
        z_ref[...] = _dot(h_scr[...], w_ref[...]).astype(z_ref.dtype)


def _in_proj(x, mod, w_main, w_gate, *, tm, sub, seq_rows, seq_len, z_dtype):
    tokens = x.shape[0]
    bn = 1024
    mod_spec = lambda k: pl.BlockSpec((mod.shape[0], D_MODEL), lambda i, j: (0, k))
    kern = functools.partial(_in_kernel, sub=min(sub, tm), seq_rows=seq_rows, tiles_per_seq=max(seq_len // tm, 1))
    return pl.pallas_call(
        kern,
        grid=(tokens // tm, Z_MAIN // bn),
        in_specs=[pl.BlockSpec((tm, D_MODEL), lambda i, j: (i, 0)),
                  mod_spec(0), mod_spec(1),
                  pl.BlockSpec((D_MODEL, bn), lambda i, j: (0, j)),
                  pl.BlockSpec((D_MODEL, LANE), lambda i, j: (0, 0))],
        out_specs=[pl.BlockSpec((tm, bn), lambda i, j: (i, j)),
                   pl.BlockSpec((tm, LANE), lambda i, j: (i, 0))],
        out_shape=[jax.ShapeDtypeStruct((tokens, Z_MAIN), z_dtype),
                   jax.ShapeDtypeStruct((tokens, LANE), F32)],
        scratch_shapes=[pltpu.VMEM((tm, D_MODEL), BF16)],
        compiler_params=pltpu.CompilerParams(
            dimension_semantics=("arbitrary", "arbitrary"), vmem_limit_bytes=VMEM_LIMIT),
        name="in_proj",
    )(x, mod, mod, w_main, w_gate)


def _mix_consts(L, Tg, t_major):
    G = L // Tg
    r = np.arange(L)
    seq, tim = (r % G, r // G) if t_major else (r // Tg, r % Tg)
    same = seq[:, None] == seq[None, :]
    t = tim[:, None]
    u = tim[None, :]
    mats = [same & (u <= t), same & (u > t)]
    masks = [same & (u <= t), same & (t <= u), same]
    levels = []
    m = Tg // 2
    while m >= 1:
        levels.append(m)
        m //= 2
    for m in levels:
        blk = same & (u // m == t // m)
        odd = (t // m) % 2 == 1
        mats.append(np.where(odd, blk & (u <= t), blk & (u > t)))
        masks.append(same & odd & ((u // m) % 2 == 0) & (t // (2 * m) == u // (2 * m)))
    mats = np.concatenate(mats, axis=0).astype(np.float32)
    masks = np.stack(masks).astype(np.float32)
    last_rows = tuple(int(np.nonzero((seq == g) & (tim == Tg - 1))[0][0]) for g in range(G))
    return mats, masks, tuple(levels), last_rows


def _mix_kernel(z_ref, g_ref, gbias_ref, lbl_ref, na_ref, nb_ref, mall_ref, msk_ref,
                c_in, n_in, m_in, s_in, mix_ref, c_out, n_out, m_out, s_out, e_scr,
                *, L, Tg, levels, last_rows):
    G = L // Tg

    def zs(off, w):
        return z_ref[:, :, off:off + w].reshape(L, w)

    def put_mix(off, w, val):
        mix_ref[:, :, off:off + w] = val.astype(mix_ref.dtype).reshape(Tg, G, w)

    _mix_body(zs, put_mix, g_ref[...].reshape(L, LANE), gbias_ref, lbl_ref, na_ref, nb_ref, mall_ref, msk_ref,
              c_in, n_in, m_in, s_in, c_out, n_out, m_out, s_out, e_scr, pl.program_id(0) * G,
              L=L, Tg=Tg, levels=levels, last_rows=last_rows, t_major=True)


def _mixchunk_kernel(z_ref, g_ref, gbias_ref, lbl_ref, na_ref, nb_ref, mall_ref, msk_ref,
                     mix_ref, c_out, n_out, m_out, s_out, e_scr, *, L, levels, last_rows):
    nb = z_ref.shape[0]
    b0 = pl.program_id(0) * nb

    @pl.when(pl.program_id(1) == 0)
    def _():
        c_out[...] = jnp.zeros(c_out.shape, F32)
        n_out[...] = jnp.zeros(n_out.shape, F32)
        s_out[...] = jnp.zeros(s_out.shape, F32)
        m_out[0, pl.ds(b0, nb), :] = jnp.zeros((nb, N_HEADS_A), F32)

    for k in range(nb):
        def zs(off, w, k=k):
            return z_ref[k, :, off:off + w]

        def put_mix(off, w, val, k=k):
            mix_ref[k, :, off:off + w] = val.astype(mix_ref.dtype)

        st = [r.at[:, pl.ds(k, 1)] for r in (c_out, n_out, s_out)]
        _mix_body(zs, put_mix, g_ref[k], gbias_ref, lbl_ref, na_ref, nb_ref, mall_ref, msk_ref,
                  st[0], st[1], m_out, st[2], st[0], st[1], m_out, st[2], e_scr.at[k], b0 + k,
                  L=L, Tg=L, levels=levels, last_rows=last_rows, t_major=False)


def _mix_body(zs, put_mix, gates, gbias_ref, lbl_ref, na_ref, nb_ref, mall_ref, msk_ref,
              c_in, n_in, m_in, s_in, c_out, n_out, m_out, s_out, e_scr, m_row0,
              *, L, Tg, levels, last_rows, t_major):
    G = L // Tg
    neg_inf = F32(-jnp.inf)
    causal = msk_ref[0] > 0.5
    causal_t = msk_ref[1] > 0.5
    same = msk_ref[2] > 0.5
    row1 = lax.broadcasted_iota(jnp.int32, (L, 1), 0)
    seq_of_row = (row1 & (G - 1)) if t_major else (row1 // Tg)
    in_group = [seq_of_row == g for g in range(G)]
    col1 = lax.broadcasted_iota(jnp.int32, (1, L), 1)
    seq_of_col = (col1 & (G - 1)) if t_major else (col1 // Tg)
    in_group_lane = [seq_of_col == g for g in range(G)]

    def state_updates(kw, v):
        if G == 1:
            return [_dot_tn(kw, v)]
        kw_t = kw.T
        return [_dot(jnp.where(in_group_lane[g], kw_t, 0.0), v) for g in range(G)]

    def by_group(vals):
        if G == 1:
            return vals[0]
        out = jnp.where(in_group[0], vals[0], 0.0)
        for g in range(1, G):
            out = out + jnp.where(in_group[g], vals[g], 0.0)
        return out

    pre = gates + gbias_ref[...]
    lsig = jnp.minimum(pre, 0.0) - jnp.log(1.0 + jnp.exp(-jnp.abs(pre)))
    pre_t = pre.T
    lsig_t = lsig.T
    m_prev = m_in[0, pl.ds(m_row0, G), :]
    scale = DK_A ** -0.5
    m_new_rows = []
    for h in range(N_HEADS_A):
        q = zs(OFF_QA + DK_A * h, DK_A)
        k = zs(OFF_KA + DK_A * h, DK_A)
        v = zs(OFF_VA + DV_A * h, DV_A)
        logi_c = pre[:, h:h + 1]
        logi_r = pre_t[h:h + 1, :]
        lf_c = lsig[:, N_HEADS_A + h:N_HEADS_A + h + 1]
        lf_r = lsig_t[N_HEADS_A + h:N_HEADS_A + h + 1, :]
        b_c = jnp.sum(jnp.where(causal, lf_r, 0.0), axis=1, keepdims=True)
        b_r = jnp.sum(jnp.where(causal_t, lf_c, 0.0), axis=0, keepdims=True)
        tots = [b_c[last_rows[g]:last_rows[g] + 1, :] for g in range(G)]
        tot_c = by_group(tots)
        tot_r = tots[0] if G == 1 else sum(jnp.where(in_group_lane[g], tots[g], 0.0) for g in range(G))
        m_col = by_group([m_prev[g:g + 1, h:h + 1] for g in range(G)])
        dmat = jnp.where(causal, b_c - b_r + logi_r, neg_inf)
        inter = b_c + m_col
        m_t = jnp.maximum(inter, jnp.max(dmat, axis=1, keepdims=True))
        w_inter = jnp.exp(inter - m_t)
        smat = _dot_nt(q, k) * (scale * jnp.exp(dmat - m_t))
        q_c = by_group([_dot(q, c_in[0, g, h]) for g in range(G)]) * scale
        n_rows = by_group([n_in[0, g, h:h + 1, :] for g in range(G)])
        q_n = jnp.sum(q.astype(F32) * n_rows, axis=1, keepdims=True) * scale
        num = w_inter * q_c + _dot(smat, v)
        den = w_inter * q_n + jnp.sum(smat, axis=1, keepdims=True)
        hh = num / jnp.maximum(jnp.abs(den), jnp.exp(-m_t))
        ms = jnp.mean(hh * hh, axis=1, keepdims=True)
        oa = zs(OFF_OA + DV_A * h, DV_A).astype(F32)
        ya = hh * lax.rsqrt(ms + EPS) * na_ref[:, DV_A * h:DV_A * (h + 1)] * _sigmoid(oa)
        put_mix(DV_A * h, DV_A, ya)
        dec_c = tot_c - b_c + logi_c
        dec_r = tot_r - b_r + logi_r
        if G == 1:
            dec_max = jnp.max(dec_r, axis=1, keepdims=True)
        else:
            dec_max = jnp.max(jnp.where(same, dec_r, neg_inf), axis=1, keepdims=True)
        m_new_c = jnp.maximum(tot_c + m_col, dec_max)
        wk = jnp.exp(dec_c - m_new_c)
        sc = jnp.exp(tot_c + m_col - m_new_c)
        kw = k.astype(F32) * wk
        upd = state_updates(kw, v)
        m_new_h = []
        for g in range(G):
            last = last_rows[g]
            kg = kw if G == 1 else jnp.where(in_group[g], kw, 0.0)
            sc_g = sc if G == 1 else sc[last:last + 1, :]
            c_new = sc_g * c_in[0, g, h] + upd[g]
            n_new = sc_g * n_in[0, g, h:h + 1, :] + jnp.sum(kg, axis=0, keepdims=True)
            c_out[0, g, h] = c_new
            n_out[0, g, h:h + 1, :] = n_new
            m_new_h.append(m_new_c if G == 1 else m_new_c[last:last + 1, :])
        m_new_rows.append(m_new_h)
    for g in range(G):
        m_out[0, pl.ds(m_row0 + g, 1), :] = jnp.concatenate(
            [m_new_rows[h][g] for h in range(N_HEADS_A)], axis=1)

    l0 = lbl_ref[0:1, :]
    l1 = lbl_ref[1:2, :]
    lmax = jnp.maximum(l0, l1)
    e0 = jnp.exp(l0 - lmax)
    e1 = jnp.exp(l1 - lmax)
    lb = e0 / (e0 + e1)
    fb = zs(OFF_FB, B_WIDTH).astype(F32)
    e = jnp.exp(-jnp.abs(fb))
    r = 1.0 / (1.0 + e)
    pos = fb >= 0.0
    sig = jnp.where(pos, r, e * r)
    nsig = jnp.where(pos, e * r, r)
    logf = jnp.log(lb + (1.0 - lb) * sig) * LOG2_E
    kb = (1.0 - lb) * nsig
    hi = logf.astype(BF16)
    r1 = logf - hi.astype(F32)
    mid = r1.astype(BF16)
    lo = (r1 - mid.astype(F32)).astype(BF16)
    mall = mall_ref[...]
    if L % 16 == 0:
        e_scr[...] = _dot(mall, jnp.concatenate([hi, mid, lo], axis=0))
    else:
        mall = mall[:, :L]
        e_scr[...] = _dot(mall, hi) + _dot(mall, mid) + _dot(mall, lo)

    dec_t = []
    for g in range(G):
        last = last_rows[g]
        bl = e_scr[last:last + 1, :]
        bl8 = jnp.concatenate([bl[:, DK_B * h:DK_B * (h + 1)] for h in range(N_HEADS_B)], axis=0)
        dec_t.append(jnp.exp2(bl8).T)

    for h in range(N_HEADS_B):
        sl = slice(DK_B * h, DK_B * (h + 1))
        q = zs(OFF_QB + DK_B * h, DK_B).astype(BF16)
        k = kb[:, sl]
        kbf = k.astype(BF16)
        v = zs(OFF_VB + DV_B * h, DV_B)
        amat = jnp.zeros((L, L), F32)
        for li in range(len(levels)):
            ex = jnp.exp2(e_scr[(2 + li) * L:(3 + li) * L, sl]).astype(BF16)
            amat = jnp.where(msk_ref[3 + li] > 0.5, _dot_nt(q * ex, kbf * ex), amat)
        qf = q.astype(F32)
        diag = jnp.sum(qf * k, axis=1, keepdims=True)
        qs = qf * jnp.exp2(e_scr[0:L, sl])
        o_inter = by_group([_dot(qs, s_in[0, g, h]) for g in range(G)])
        o = _dot(amat, v) + diag * v.astype(F32) + o_inter
        ms = jnp.mean(o * o, axis=1, keepdims=True)
        gb = zs(OFF_GB + DV_B * h, DV_B).astype(F32)
        yb = o * lax.rsqrt(ms + EPS) * nb_ref[:, sl] * (gb * _sigmoid(gb))
        put_mix(A_WIDTH + DV_B * h, DV_B, yb)
        upd = state_updates(k * jnp.exp2(e_scr[L:2 * L, sl]), v)
        for g in range(G):
            s_out[0, g, h] = dec_t[g][:, h:h + 1] * s_in[0, g, h] + upd[g]


def _mixer_args(L, Tg, t_major, const2, const3, gbias, lb_logits, norm_a, norm_b):
    mats, masks, levels, last_rows = _mix_consts(L, Tg, t_major)
    n_mats = mats.shape[0] // L
    if L % 16 == 0:
        mats = np.concatenate([mats, mats, mats], axis=1)
    specs = [pl.BlockSpec((1, LANE), const2),
             pl.BlockSpec((2, B_WIDTH), const2),
             pl.BlockSpec((1, A_WIDTH), const2),
             pl.BlockSpec((1, B_WIDTH), const2),
             pl.BlockSpec(mats.shape, const2),
             pl.BlockSpec(masks.shape, const3)]
    args = [gbias, lb_logits, norm_a, norm_b, jnp.asarray(mats, BF16), jnp.asarray(masks)]
    return specs, args, levels, last_rows, n_mats


def _state_shapes(n_seq):
    return [jax.ShapeDtypeStruct((1, n_seq, N_HEADS_A, DK_A, DV_A), F32),
            jax.ShapeDtypeStruct((1, n_seq, N_HEADS_A, DK_A), F32),
            jax.ShapeDtypeStruct((1, n_seq, N_HEADS_A), F32),
            jax.ShapeDtypeStruct((1, n_seq, N_HEADS_B, DK_B, DV_B), F32)]


def _state_specs(G, n_seq, seq_of_step):
    return [pl.BlockSpec((1, G, N_HEADS_A, DK_A, DV_A), lambda *ids: (0, seq_of_step(*ids), 0, 0, 0)),
            pl.BlockSpec((1, G, N_HEADS_A, DK_A), lambda *ids: (0, seq_of_step(*ids), 0, 0)),
            pl.BlockSpec((1, n_seq, N_HEADS_A), lambda *ids: (0, 0, 0)),
            pl.BlockSpec((1, G, N_HEADS_B, DK_B, DV_B), lambda *ids: (0, seq_of_step(*ids), 0, 0, 0))]


def _mixer(z, gates, gbias, lb_logits, norm_a, norm_b, *, n_seq, seq_len, states):
    Tg, G = seq_len, SAMPLE_GROUP
    L = Tg * G
    tile = lambda w: pl.BlockSpec((Tg, G, w), lambda i: (0, i, 0))
    const_specs, const_args, levels, last_rows, n_mats = _mixer_args(
        L, Tg, True, lambda i: (0, 0), lambda i: (0, 0, 0), gbias, lb_logits, norm_a, norm_b)
    st_specs = _state_specs(G, n_seq, lambda i: i)
    kern = functools.partial(_mix_kernel, L=L, Tg=Tg, levels=levels, last_rows=last_rows)
    return pl.pallas_call(
        kern,
        grid=(n_seq // G,),
        in_specs=[tile(Z_MAIN), tile(LANE)] + const_specs + st_specs,
        out_specs=[tile(D_MODEL)] + st_specs,
        out_shape=[jax.ShapeDtypeStruct((Tg, n_seq, D_MODEL), F32)] + _state_shapes(n_seq),
        scratch_shapes=[pltpu.VMEM((n_mats * L, B_WIDTH), F32)],
        compiler_params=pltpu.CompilerParams(dimension_semantics=("arbitrary",), vmem_limit_bytes=VMEM_LIMIT),
        name="mixer_state",
    )(z, gates, *const_args, *states)


def _mixer_chunk(z, gates, gbias, lb_logits, norm_a, norm_b, *, n_seq, seq_len):
    L = PROMPT_CHUNK
    nb = PROMPT_SEQS_PER_STEP
    nchunks = seq_len // L
    tile = lambda w: pl.BlockSpec((nb, L, w), lambda b, c: (b, c, 0))
    const_specs, const_args, levels, last_rows, n_mats = _mixer_args(
        L, L, False, lambda b, c: (0, 0), lambda b, c: (0, 0, 0), gbias, lb_logits, norm_a, norm_b)
    st_specs = _state_specs(nb, n_seq, lambda b, c: b)
    kern = functools.partial(_mixchunk_kernel, L=L, levels=levels, last_rows=last_rows)
    outs = pl.pallas_call(
        kern,
        grid=(n_seq // nb, nchunks),
        in_specs=[tile(Z_MAIN), tile(LANE)] + const_specs,
        out_specs=[tile(D_MODEL)] + st_specs,
        out_shape=[jax.ShapeDtypeStruct((n_seq, seq_len, D_MODEL), BF16)] + _state_shapes(n_seq),
        scratch_shapes=[pltpu.VMEM((nb, n_mats * L, B_WIDTH), F32)],
        compiler_params=pltpu.CompilerParams(dimension_semantics=("arbitrary", "arbitrary"),
                                             vmem_limit_bytes=VMEM_LIMIT),
        name="mixer_chunk",
    )(z.reshape(n_seq, seq_len, Z_MAIN), gates.reshape(n_seq, seq_len, LANE), *const_args)
    return [outs[0].reshape(n_seq * seq_len, D_MODEL)] + list(outs[1:])


def _res_kernel(*refs, sub, seq_rows, tiles_per_seq, emit_h2, nk):
    if emit_h2:
        a_ref, w_ref, x_ref, gm_ref, lg_ref, lb_ref, sh_ref, sc_ref, o_ref, h_ref = refs
    else:
        a_ref, w_ref, x_ref, gm_ref, lg_ref, lb_ref, o_ref = refs
    i = pl.program_id(0)
    k = pl.program_id(1)

    if nk > 1:
        @pl.when(k == 0)
        def _():
            o_ref[...] = _dot(a_ref[...], w_ref[...])

    if nk > 2:
        @pl.when((k > 0) & (k < nk - 1))
        def _():
            o_ref[...] += _dot(a_ref[...], w_ref[...])

    @pl.when(k == nk - 1)
    def _():
        seq = i // tiles_per_seq
        gm = _mod_rows(gm_ref, seq_rows, seq)
        if emit_h2:
            sh = _mod_rows(sh_ref, seq_rows, seq)
            sc = _mod_rows(sc_ref, seq_rows, seq)
        for r in range(0, o_ref.shape[0], sub):
            rows = slice(r, r + sub)
            acc = _dot(a_ref[rows, :], w_ref[...])
            if nk > 1:
                acc = acc + o_ref[rows, :]
            x1 = _ln(ALPHA * x_ref[rows, :] + gm * acc) * lg_ref[...] + lb_ref[...]
            o_ref[rows, :] = x1
            if emit_h2:
                h_ref[rows, :] = (_ln(x1) * (1.0 + sc) + sh).astype(BF16)


def _res_block(a, w, x, mod, ln_g, ln_b, *, tm, bk, sub, seq_rows, seq_len, gate_col, emit_h2):
    tokens, kdim = a.shape
    nk = kdim // bk
    mod_spec = lambda c: pl.BlockSpec((mod.shape[0], D_MODEL), lambda i, k: (0, c))
    row_spec = pl.BlockSpec((tm, D_MODEL), lambda i, k: (i, 0))
    vec_spec = pl.BlockSpec((1, D_MODEL), lambda i, k: (0, 0))
    in_specs = [pl.BlockSpec((tm, bk), lambda i, k: (i, k)),
                pl.BlockSpec((bk, D_MODEL), lambda i, k: (k, 0)),
                row_spec, mod_spec(gate_col), vec_spec, vec_spec]
    args = [a, w, x, mod, ln_g, ln_b]
    out_specs = [row_spec]
    out_shape = [jax.ShapeDtypeStruct((tokens, D_MODEL), F32)]
    if emit_h2:
        in_specs += [mod_spec(3), mod_spec(4)]
        args += [mod, mod]
        out_specs.append(row_spec)
        out_shape.append(jax.ShapeDtypeStruct((tokens, D_MODEL), BF16))
    kern = functools.partial(_res_kernel, sub=sub, seq_rows=seq_rows, tiles_per_seq=max(seq_len // tm, 1),
                             emit_h2=emit_h2, nk=nk)
    return pl.pallas_call(
        kern,
        grid=(tokens // tm, nk),
        in_specs=in_specs,
        out_specs=out_specs,
        out_shape=out_shape,
        compiler_params=pltpu.CompilerParams(
            dimension_semantics=("arbitrary", "arbitrary"), vmem_limit_bytes=VMEM_LIMIT),
        name="res_block",
    )(*args)


def _up_kernel(*refs, tm, sub, seq_len, n_seq, t_major, tiles_per_seq):
    if t_major:
        h_ref, wa_scr, wu_scr, cw_ref, cb_ref, cache_ref, g_ref, tail_ref, abuf = refs
    else:
        (h_ref, wa_ref, uu_ref, wd_ref, cw_ref, cb_ref,
         g_ref, tail_ref, wa_scr, wu_scr, wd_out, carry) = refs
    j = pl.program_id(0)
    i = pl.program_id(1)
    bn = wa_scr.shape[1]
    valid = (j * bn + lax.broadcasted_iota(jnp.int32, (1, bn), 1)) < D_FF
    hist = (CONV_W - 1) * n_seq

    @pl.when(i == 0)
    def _():
        if t_major:
            abuf[0:hist, :] = jnp.where(valid, cache_ref[...].reshape(hist, bn), 0.0)
        else:
            wu = uu_ref[...]
            wu_last = jnp.concatenate([wu[:, LANE:], jnp.zeros((D_MODEL, LANE), F32)], axis=1)
            wu = jnp.where(j == pl.num_programs(0) - 1, wu_last, wu)
            wa_scr[...] = jnp.where(valid, wa_ref[...], 0.0).astype(BF16)
            wu_scr[...] = jnp.where(valid, wu, 0.0).astype(BF16)
            wd_rows = j * bn + lax.broadcasted_iota(jnp.int32, (bn, 1), 0)
            wd_out[...] = jnp.where(wd_rows < D_FF, wd_ref[...], 0.0).astype(BF16)

    if not t_major:
        @pl.when(i % tiles_per_seq == 0)
        def _():
            carry[...] = jnp.zeros(carry.shape, F32)

    row8 = lax.broadcasted_iota(jnp.int32, (8, 1), 0)
    for r in range(0, tm, sub):
        h = h_ref[r:r + sub, :]
        a = _dot(h, wa_scr[...])
        u = _dot(h, wu_scr[...])
        if t_major:
            abuf[hist + r:hist + r + sub, :] = a
            a1 = abuf[hist + r - n_seq:hist + r - n_seq + sub, :]
            a2 = abuf[hist + r - 2 * n_seq:hist + r - 2 * n_seq + sub, :]
        else:
            p1 = carry[7:8, :]
            p2 = carry[6:7, :]
            r1 = pltpu.roll(a, 1, axis=0)
            r2 = pltpu.roll(a, 2, axis=0)
            a1 = jnp.concatenate([jnp.where(row8 == 0, p1, r1[0:8, :]), r1[8:, :]], axis=0)
            a2 = jnp.concatenate(
                [jnp.where(row8 == 0, p2, jnp.where(row8 == 1, p1, r2[0:8, :])), r2[8:, :]], axis=0)
            carry[...] = a[sub - 8:sub, :]
        conv = cb_ref[...] + cw_ref[0:1, :] * a2 + cw_ref[1:2, :] * a1 + cw_ref[2:3, :] * a
        gl = 0.5 * conv * (1.0 + lax.erf(conv * (2.0 ** -0.5)))
        g_ref[r:r + sub, :] = (gl * u).astype(BF16)

    if t_major:
        tail_ref[...] = abuf[tm:tm + hist, :].reshape(tail_ref.shape)
    else:
        @pl.when((i + 1) % tiles_per_seq == 0)
        def _():
            tail_ref[0] = carry[8 - (CONV_W - 1):8, :]


def _up_proj(h2, w_up, w_down, conv_w_p, conv_b_p, *, tm, sub, seq_len, n_seq, cache=None):
    tokens = h2.shape[0]
    t_major = cache is not None
    bn = FF_BLOCK
    nj = D_FF_PAD // bn
    assert D_FF_PAD - D_FF == LANE
    tiles_per_seq = max(seq_len // tm, 1)
    wspec = lambda f: pl.BlockSpec((D_MODEL, bn), f)
    col_spec = wspec(lambda j, i: (0, j))
    h_spec = pl.BlockSpec((tm, D_MODEL), lambda j, i: (i, 0))
    conv_specs = [pl.BlockSpec((CONV_W, bn), lambda j, i: (0, j)), pl.BlockSpec((1, bn), lambda j, i: (0, j))]
    out_specs = [pl.BlockSpec((tm, bn), lambda j, i: (i, j))]
    out_shape = [jax.ShapeDtypeStruct((tokens, D_FF_PAD), BF16)]
    if t_major:
        hist_spec = pl.BlockSpec((CONV_W - 1, n_seq, bn), lambda j, i: (0, 0, j))
        in_specs = [h_spec, col_spec, col_spec] + conv_specs + [hist_spec]
        args = [h2, w_up[0], w_up[1], conv_w_p, conv_b_p, cache]
        out_specs.append(hist_spec)
        out_shape.append(jax.ShapeDtypeStruct((CONV_W - 1, n_seq, D_FF_PAD), F32))
        scratch_rows = (CONV_W - 1) * n_seq + tm
    else:
        row_spec = pl.BlockSpec((bn, D_MODEL), lambda j, i: (j, 0))
        u_off = lambda j, i: (0, pl.multiple_of(jnp.minimum(D_FF + bn * j, 2 * D_FF - bn), LANE))
        in_specs = [h_spec, col_spec,
                    pl.BlockSpec((pl.Element(D_MODEL), pl.Element(bn)), u_off),
                    row_spec] + conv_specs
        args = [h2, w_up, w_up, w_down, conv_w_p, conv_b_p]
        out_specs += [pl.BlockSpec((1, CONV_W - 1, bn), lambda j, i: (i // tiles_per_seq, 0, j)),
                      col_spec, col_spec, row_spec]
        out_shape += [jax.ShapeDtypeStruct((n_seq, CONV_W - 1, D_FF_PAD), F32),
                      jax.ShapeDtypeStruct((D_MODEL, D_FF_PAD), BF16),
                      jax.ShapeDtypeStruct((D_MODEL, D_FF_PAD), BF16),
                      jax.ShapeDtypeStruct((D_FF_PAD, D_MODEL), BF16)]
        scratch_rows = 8
    kern = functools.partial(_up_kernel, tm=tm, sub=sub, seq_len=seq_len, n_seq=n_seq, t_major=t_major,
                             tiles_per_seq=tiles_per_seq)
    outs = pl.pallas_call(
        kern,
        grid=(nj, tokens // tm),
        in_specs=in_specs,
        out_specs=out_specs,
        out_shape=out_shape,
        scratch_shapes=[pltpu.VMEM((scratch_rows, bn), F32)],
        compiler_params=pltpu.CompilerParams(
            dimension_semantics=("arbitrary", "arbitrary"), vmem_limit_bytes=VMEM_LIMIT),
        name="up_proj",
    )(*args)
    if t_major:
        return outs[0], outs[1], w_up, w_down
    return outs[0], outs[1], (outs[2], outs[3]), outs[4]


def _path(x, mod, w, *, n_seq, seq_len, tm_in, tm_res, tm_up, tm_down, states=None, cache=None):
    sample = states is not None
    sub = n_seq if sample else 256
    mixer_w = (w["gbias"], w["lb_logits"], w["norm_a"], w["norm_b"])
    z, gates = _in_proj(x, mod, w["w_main"], w["w_gate"], tm=tm_in, sub=sub, seq_rows=sample,
                        seq_len=seq_len, z_dtype=F32 if sample else BF16)
    if sample:
        mix, c1, n1, m1, s1 = _mixer(z.reshape(seq_len, n_seq, Z_MAIN), gates.reshape(seq_len, n_seq, LANE),
                                     *mixer_w, n_seq=n_seq, seq_len=seq_len, states=states)
        mix = mix.reshape(n_seq * seq_len, D_MODEL)
    else:
        mix, c1, n1, m1, s1 = _mixer_chunk(z, gates, *mixer_w, n_seq=n_seq, seq_len=seq_len)
    x1, h2 = _res_block(mix, w["w_out"], x, mod, w["ln1_g"], w["ln1_b"], tm=tm_res, bk=D_MODEL, sub=sub,
                        seq_rows=sample, seq_len=seq_len, gate_col=2, emit_h2=True)
    g, conv_out, w_up_bf, w_down_bf = _up_proj(h2, w["w_up"], w["w_down"], w["conv_w"], w["conv_b"], tm=tm_up,
                                               sub=sub, seq_len=seq_len, n_seq=n_seq, cache=cache)
    (x2,) = _res_block(g, w_down_bf, x1, mod, w["ln2_g"], w["ln2_b"], tm=tm_down, bk=D_FF_PAD // 2, sub=sub,
                       seq_rows=sample, seq_len=seq_len, gate_col=5, emit_h2=False)
    return x2, (c1, n1, m1, s1), conv_out, dict(w_up=w_up_bf, w_down=w_down_bf)


def kernel(x_prompt, x_sample, state_mlstm_C, state_mlstm_n, state_mlstm_m, state_hgrn_S, cache_ffn_conv,
           c_prompt, c_sample, hgrn_lb_logits, w_ada, b_ada, w_in, b_gate_a, norm_a, norm_b, w_out,
           ln1_g, ln1_b, w_up, conv_w, conv_b, w_down, ln2_g, ln2_b):
    bp, tp, _ = x_prompt.shape
    bs, ts, _ = x_sample.shape
    pad_ff = D_FF_PAD - D_FF
    w_main, w_gate = _prep_w_in(jnp.transpose(w_in[0]))
    weights = dict(
        w_main=w_main, w_gate=w_gate, w_out=w_out[0].astype(BF16), w_up=w_up[0],
        w_down=w_down[0],
        gbias=jnp.pad(b_gate_a[0].reshape(1, N_GATE), ((0, 0), (0, LANE - N_GATE))),
        lb_logits=hgrn_lb_logits, norm_a=norm_a, norm_b=norm_b,
        ln1_g=ln1_g, ln1_b=ln1_b, ln2_g=ln2_g, ln2_b=ln2_b,
        conv_w=jnp.pad(conv_w[0], ((0, 0), (0, pad_ff))),
        conv_b=jnp.pad(conv_b, ((0, 0), (0, pad_ff))))

    c_p = jnp.pad(c_prompt, ((0, 8 - bp), (0, 0)))
    mod_p, mod_s = _ada(c_p, c_sample, w_ada[0], b_ada)

    yp, st_p, tail_p, w_ffn_bf = _path(x_prompt.reshape(bp * tp, D_MODEL), mod_p, weights, n_seq=bp,
                                       seq_len=tp, tm_in=1024, tm_res=512, tm_up=2048, tm_down=512)

    xs_t = jnp.swapaxes(x_sample, 0, 1).reshape(ts * bs, D_MODEL)
    cache_t = jnp.swapaxes(cache_ffn_conv[0], 0, 1)
    ys_t, st_s, tail_s, _ = _path(xs_t, mod_s, dict(weights, **w_ffn_bf), n_seq=bs, seq_len=ts,
                                  tm_in=ts * bs, tm_res=ts * bs, tm_up=ts * bs, tm_down=ts * bs,
                                  states=(state_mlstm_C, state_mlstm_n, state_mlstm_m, state_hgrn_S),
                                  cache=cache_t)

    ys = jnp.swapaxes(ys_t.reshape(ts, bs, D_MODEL), 0, 1)
    conv_p = tail_p[None, :, :, :D_FF]
    conv_s = jnp.swapaxes(tail_s[:, :, :D_FF], 0, 1)[None]
    return (yp.reshape(bp, tp, D_MODEL), ys,
            st_p[0], st_p[1], st_p[2], st_p[3], conv_p,
            st_s[0], st_s[1], st_s[2], st_s[3], conv_s)
```

```python
import functools

import numpy as np
import jax
import jax.numpy as jnp
from jax import lax
from jax.experimental import pallas as pl
from jax.experimental.pallas import tpu as pltpu

F32 = jnp.float32
BF16 = jnp.bfloat16

D_MODEL = 2048
N_HEADS_A, DK_A, DV_A = 4, 128, 256
N_HEADS_B, DK_B, DV_B = 8, 128, 128
A_WIDTH = N_HEADS_A * DV_A
B_WIDTH = N_HEADS_B * DV_B
N_GATE = 2 * N_HEADS_A
GATE_COL = 2 * N_HEADS_A * DK_A + A_WIDTH
D_IN = GATE_COL + N_GATE + A_WIDTH + 4 * B_WIDTH
D_FF = 5504
CONV_W = 3
EPS = 1e-5
ALPHA = 2.0 ** 0.25
LOG2_E = 1.4426950408889634
LOG_SCALE_A = -0.5 * float(np.log(DK_A))
LANE = 128
FF_BLOCK = 512
D_FF_PAD = 5632
Z_MAIN = D_IN - N_GATE
OFF_QA, OFF_KA, OFF_VA, OFF_OA = 0, 512, 1024, 2048
OFF_FB, OFF_QB, OFF_VB, OFF_GB = 3072, 4096, 5120, 6144
VMEM_LIMIT = 60 * 1024 * 1024
PROMPT_CHUNK = 128
PROMPT_SEQS_PER_STEP = 2
SAMPLE_GROUP = 8


def _ln(x):
    mu = jnp.mean(x, axis=-1, keepdims=True)
    xc = x - mu
    var = jnp.mean(xc * xc, axis=-1, keepdims=True)
    return xc * lax.rsqrt(var + EPS)


def _sigmoid(x):
    return 1.0 / (1.0 + jnp.exp(-x))


def _dot(a, b):
    return jnp.dot(a.astype(BF16), b.astype(BF16), preferred_element_type=F32)


def _dot_nt(a, b):
    return lax.dot_general(a.astype(BF16), b.astype(BF16), (((1,), (1,)), ((), ())),
                           preferred_element_type=F32)


def _dot_tn(a, b):
    return lax.dot_general(a.astype(BF16), b.astype(BF16), (((0,), (0,)), ((), ())),
                           preferred_element_type=F32)


def _mod_rows(ref, seq_rows, seq):
    if seq_rows:
        return ref[...]
    return ref[pl.ds(seq, 1), :]


def _ada_kernel(cp_ref, cs_ref, w_ref, b_ref, op_ref, os_ref):
    w = w_ref[...].astype(BF16)
    for c_ref, o_ref in ((cp_ref, op_ref), (cs_ref, os_ref)):
        c = c_ref[...]
        o_ref[...] = _dot(c * _sigmoid(c), w) + b_ref[...]


def _ada(c_p, c_s, w_ada, b_ada):
    n = w_ada.shape[1]
    bn = 1024
    row = lambda r: pl.BlockSpec((r, D_MODEL), lambda j: (0, 0))
    out = lambda r: pl.BlockSpec((r, bn), lambda j: (0, j))
    return pl.pallas_call(
        _ada_kernel,
        grid=(n // bn,),
        in_specs=[row(c_p.shape[0]), row(c_s.shape[0]),
                  pl.BlockSpec((D_MODEL, bn), lambda j: (0, j)),
                  pl.BlockSpec((1, bn), lambda j: (0, j))],
        out_specs=[out(c_p.shape[0]), out(c_s.shape[0])],
        out_shape=[jax.ShapeDtypeStruct((c_p.shape[0], n), F32),
                   jax.ShapeDtypeStruct((c_s.shape[0], n), F32)],
        compiler_params=pltpu.CompilerParams(vmem_limit_bytes=VMEM_LIMIT),
        name="ada",
    )(c_p, c_s, w_ada, b_ada)


def _prep_in_kernel(front_ref, back_ref, gate_ref, o_ref, og_ref, *, first_shifted):
    j = pl.program_id(0)

    @pl.when(j == 0)
    def _():
        og_ref[...] = gate_ref[...].T.astype(BF16)

    def emit(src_ref):
        for r in range(0, src_ref.shape[0], 256):
            o_ref[:, r:r + 256] = src_ref[r:r + 256, :].T.astype(BF16)

    @pl.when(j < first_shifted)
    def _():
        emit(front_ref)

    @pl.when(j >= first_shifted)
    def _():
        emit(back_ref)


def _prep_w_in(w_in_t):
    bn = 1024
    first_shifted = GATE_COL // bn
    kern = functools.partial(_prep_in_kernel, first_shifted=first_shifted)
    return pl.pallas_call(
        kern,
        grid=(Z_MAIN // bn,),
        in_specs=[pl.BlockSpec((bn, D_MODEL), lambda j: (jnp.minimum(j, first_shifted - 1), 0)),
                  pl.BlockSpec((pl.Element(bn), pl.Element(D_MODEL)),
                               lambda j: (pl.multiple_of(jnp.maximum(j, first_shifted) * bn + N_GATE, N_GATE), 0)),
                  pl.BlockSpec((LANE, D_MODEL), lambda j: (GATE_COL // LANE, 0))],
        out_specs=[pl.BlockSpec((D_MODEL, bn), lambda j: (0, j)),
                   pl.BlockSpec((D_MODEL, LANE), lambda j: (0, 0))],
        out_shape=[jax.ShapeDtypeStruct((D_MODEL, Z_MAIN), BF16),
                   jax.ShapeDtypeStruct((D_MODEL, LANE), BF16)],
        compiler_params=pltpu.CompilerParams(vmem_limit_bytes=VMEM_LIMIT),
        name="prep_w_in",
    )(w_in_t, w_in_t, w_in_t)


def _in_kernel(x_ref, sh_ref, sc_ref, w_ref, wg_ref, z_ref, g_ref, h_scr, *, sub, seq_rows, tiles_per_seq):
    i = pl.program_id(0)
    j = pl.program_id(1)

    @pl.when(j == 0)
    def _():
        seq = i // tiles_per_seq
        sh = _mod_rows(sh_ref, seq_rows, seq)
        sc = _mod_rows(sc_ref, seq_rows, seq)
        for r in range(0, x_ref.shape[0], sub):
            rows = slice(r, r + sub)
            h = (_ln(x_ref[rows, :]) * (1.0 + sc) + sh).astype(BF16)
            h_scr[rows, :] = h
            g_ref[rows, :] = _dot(h, wg_ref[...])
            z_ref[rows, :] = _dot(h, w_ref[...]).astype(z_ref.dtype)

    @pl.when(j > 0)
    def _():
        z_ref[...] = _dot(h_scr[...], w_ref[...]).astype(z_ref.dtype)


def _in_proj(x, mod, w_main, w_gate, *, tm, sub, seq_rows, seq_len, z_dtype):
    tokens = x.shape[0]
    bn = 1024
    mod_spec = lambda k: pl.BlockSpec((mod.shape[0], D_MODEL), lambda i, j: (0, k))
    kern = functools.partial(_in_kernel, sub=min(sub, tm), seq_rows=seq_rows, tiles_per_seq=max(seq_len // tm, 1))
    return pl.pallas_call(
        kern,
        grid=(tokens // tm, Z_MAIN // bn),
        in_specs=[pl.BlockSpec((tm, D_MODEL), lambda i, j: (i, 0)),
                  mod_spec(0), mod_spec(1),
                  pl.BlockSpec((D_MODEL, bn), lambda i, j: (0, j)),
                  pl.BlockSpec((D_MODEL, LANE), lambda i, j: (0, 0))],
        out_specs=[pl.BlockSpec((tm, bn), lambda i, j: (i, j)),
                   pl.BlockSpec((tm, LANE), lambda i, j: (i, 0))],
        out_shape=[jax.ShapeDtypeStruct((tokens, Z_MAIN), z_dtype),
                   jax.ShapeDtypeStruct((tokens, LANE), F32)],
        scratch_shapes=[pltpu.VMEM((tm, D_MODEL), BF16)],
        compiler_params=pltpu.CompilerParams(
            dimension_semantics=("arbitrary", "arbitrary"), vmem_limit_bytes=VMEM_LIMIT),
        name="in_proj",
    )(x, mod, mod, w_main, w_gate)


def _mix_consts(L, Tg, t_major):
    G = L // Tg
    r = np.arange(L)
    seq, tim = (r % G, r // G) if t_major else (r // Tg, r % Tg)
    same = seq[:, None] == seq[None, :]
    t = tim[:, None]
    u = tim[None, :]
    mats = [same & (u <= t), same & (u > t)]
    masks = [same & (u <= t), same & (t <= u), same]
    levels = []
    m = Tg // 2
    while m >= 1:
        levels.append(m)
        m //= 2
    for m in levels:
        blk = same & (u // m == t // m)
        odd = (t // m) % 2 == 1
        mats.append(np.where(odd, blk & (u <= t), blk & (u > t)))
        masks.append(same & odd & ((u // m) % 2 == 0) & (t // (2 * m) == u // (2 * m)))
    masks.append(np.where(masks[0], 0.0, -np.inf))
    mats = np.concatenate(mats, axis=0).astype(np.float32)
    masks = np.stack(masks).astype(np.float32)
    last_rows = tuple(int(np.nonzero((seq == g) & (tim == Tg - 1))[0][0]) for g in range(G))
    return mats, masks, tuple(levels), last_rows


def _mix_kernel(z_ref, g_ref, gbias_ref, lbl_ref, na_ref, nb_ref, mall_ref, msk_ref,
                c_in, n_in, m_in, s_in, mix_ref, c_out, n_out, m_out, s_out, e_scr,
                *, L, Tg, levels, last_rows):
    G = L // Tg

    def zs(off, w):
        return z_ref[:, :, off:off + w].reshape(L, w)

    def put_mix(off, w, val):
        mix_ref[:, :, off:off + w] = val.astype(mix_ref.dtype).reshape(Tg, G, w)

    _mix_body(zs, put_mix, g_ref[...].reshape(L, LANE), gbias_ref, lbl_ref, na_ref, nb_ref, mall_ref, msk_ref,
              c_in, n_in, m_in, s_in, c_out, n_out, m_out, s_out, e_scr, pl.program_id(0) * G,
              L=L, Tg=Tg, levels=levels, last_rows=last_rows, t_major=True)


def _mixchunk_kernel(z_ref, g_ref, gbias_ref, lbl_ref, na_ref, nb_ref, mall_ref, msk_ref,
                     mix_ref, c_out, n_out, m_out, s_out, e_scr, *, L, levels, last_rows):
    nb = z_ref.shape[0]
    b0 = pl.program_id(0) * nb

    @pl.when(pl.program_id(1) == 0)
    def _():
        c_out[...] = jnp.zeros(c_out.shape, F32)
        n_out[...] = jnp.zeros(n_out.shape, F32)
        s_out[...] = jnp.zeros(s_out.shape, F32)
        m_out[0, pl.ds(b0, nb), :] = jnp.zeros((nb, N_HEADS_A), F32)

    for k in range(nb):
        def zs(off, w, k=k):
            return z_ref[k, :, off:off + w]

        def put_mix(off, w, val, k=k):
            mix_ref[k, :, off:off + w] = val.astype(mix_ref.dtype)

        st = [r.at[:, pl.ds(k, 1)] for r in (c_out, n_out, s_out)]
        _mix_body(zs, put_mix, g_ref[k], gbias_ref, lbl_ref, na_ref, nb_ref, mall_ref, msk_ref,
                  st[0], st[1], m_out, st[2], st[0], st[1], m_out, st[2], e_scr.at[k], b0 + k,
                  L=L, Tg=L, levels=levels, last_rows=last_rows, t_major=False)


def _mix_body(zs, put_mix, gates, gbias_ref, lbl_ref, na_ref, nb_ref, mall_ref, msk_ref,
              c_in, n_in, m_in, s_in, c_out, n_out, m_out, s_out, e_scr, m_row0,
              *, L, Tg, levels, last_rows, t_major):
    G = L // Tg
    neg_inf = F32(-jnp.inf)
    same = msk_ref[2] > 0.5
    causal_neg = msk_ref[3 + len(levels)]
    row1 = lax.broadcasted_iota(jnp.int32, (L, 1), 0)
    seq_of_row = (row1 & (G - 1)) if t_major else (row1 // Tg)
    in_group = [seq_of_row == g for g in range(G)]
    col1 = lax.broadcasted_iota(jnp.int32, (1, L), 1)
    seq_of_col = (col1 & (G - 1)) if t_major else (col1 // Tg)
    in_group_lane = [seq_of_col == g for g in range(G)]

    def state_updates(kw, v):
        if G == 1:
            return [_dot_tn(kw, v)]
        kw_t = kw.T
        return [_dot(jnp.where(in_group_lane[g], kw_t, 0.0), v) for g in range(G)]

    def by_group(vals):
        if G == 1:
            return vals[0]
        out = jnp.where(in_group[0], vals[0], 0.0)
        for g in range(1, G):
            out = out + jnp.where(in_group[g], vals[g], 0.0)
        return out

    pre = gates + gbias_ref[...]
    lsig = jnp.minimum(pre, 0.0) - jnp.log(1.0 + jnp.exp(-jnp.abs(pre)))
    pre_t = pre.T
    lsig_t = lsig.T
    m_prev = m_in[0, pl.ds(m_row0, G), :]
    scale = DK_A ** -0.5
    m_new_rows = []
    for h in range(N_HEADS_A):
        q = zs(OFF_QA + DK_A * h, DK_A)
        k = zs(OFF_KA + DK_A * h, DK_A)
        v = zs(OFF_VA + DV_A * h, DV_A)
        logi_c = pre[:, h:h + 1]
        logi_r = pre_t[h:h + 1, :]
        lf_c = lsig[:, N_HEADS_A + h:N_HEADS_A + h + 1]
        lf_r = lsig_t[N_HEADS_A + h:N_HEADS_A + h + 1, :]
        b_c = jnp.sum(msk_ref[0] * lf_r, axis=1, keepdims=True)
        b_r = jnp.sum(msk_ref[1] * lf_c, axis=0, keepdims=True)
        tots = [b_c[last_rows[g]:last_rows[g] + 1, :] for g in range(G)]
        tot_c = by_group(tots)
        tot_r = tots[0] if G == 1 else sum(jnp.where(in_group_lane[g], tots[g], 0.0) for g in range(G))
        m_col = by_group([m_prev[g:g + 1, h:h + 1] for g in range(G)])
        dmat = (b_c - b_r + logi_r) + causal_neg
        inter = b_c + m_col
        m_t = jnp.maximum(inter, jnp.max(dmat, axis=1, keepdims=True))
        w_inter = jnp.exp(inter - m_t)
        smat = _dot_nt(q, k) * jnp.exp(dmat - (m_t - LOG_SCALE_A))
        q_c = by_group([_dot(q, c_in[0, g, h]) for g in range(G)]) * scale
        n_rows = by_group([n_in[0, g, h:h + 1, :] for g in range(G)])
        q_n = jnp.sum(q.astype(F32) * n_rows, axis=1, keepdims=True) * scale
        num = w_inter * q_c + _dot(smat, v)
        den = w_inter * q_n + jnp.sum(smat, axis=1, keepdims=True)
        hh = num / jnp.maximum(jnp.abs(den), jnp.exp(-m_t))
        ms = jnp.mean(hh * hh, axis=1, keepdims=True)
        oa = zs(OFF_OA + DV_A * h, DV_A).astype(F32)
        ya = hh * lax.rsqrt(ms + EPS) * na_ref[:, DV_A * h:DV_A * (h + 1)] * _sigmoid(oa)
        put_mix(DV_A * h, DV_A, ya)
        dec_c = tot_c - b_c + logi_c
        dec_r = tot_r - b_r + logi_r
        if G == 1:
            dec_max = jnp.max(dec_r, axis=1, keepdims=True)
        else:
            dec_max = jnp.max(jnp.where(same, dec_r, neg_inf), axis=1, keepdims=True)
        m_new_c = jnp.maximum(tot_c + m_col, dec_max)
        wk = jnp.exp(dec_c - m_new_c)
        sc = jnp.exp(tot_c + m_col - m_new_c)
        kw = k.astype(F32) * wk
        upd = state_updates(kw, v)
        m_new_h = []
        for g in range(G):
            last = last_rows[g]
            kg = kw if G == 1 else jnp.where(in_group[g], kw, 0.0)
            sc_g = sc if G == 1 else sc[last:last + 1, :]
            c_new = sc_g * c_in[0, g, h] + upd[g]
            n_new = sc_g * n_in[0, g, h:h + 1, :] + jnp.sum(kg, axis=0, keepdims=True)
            c_out[0, g, h] = c_new
            n_out[0, g, h:h + 1, :] = n_new
            m_new_h.append(m_new_c if G == 1 else m_new_c[last:last + 1, :])
        m_new_rows.append(m_new_h)
    for g in range(G):
        m_out[0, pl.ds(m_row0 + g, 1), :] = jnp.concatenate(
            [m_new_rows[h][g] for h in range(N_HEADS_A)], axis=1)

    l0 = lbl_ref[0:1, :]
    l1 = lbl_ref[1:2, :]
    lmax = jnp.maximum(l0, l1)
    e0 = jnp.exp(l0 - lmax)
    e1 = jnp.exp(l1 - lmax)
    lb = e0 / (e0 + e1)
    fb = zs(OFF_FB, B_WIDTH).astype(F32)
    e = jnp.exp(-jnp.abs(fb))
    r = 1.0 / (1.0 + e)
    pos = fb >= 0.0
    sig = jnp.where(pos, r, e * r)
    nsig = jnp.where(pos, e * r, r)
    logf = jnp.log(lb + (1.0 - lb) * sig) * LOG2_E
    kb = (1.0 - lb) * nsig
    hi = logf.astype(BF16)
    r1 = logf - hi.astype(F32)
    mid = r1.astype(BF16)
    lo = (r1 - mid.astype(F32)).astype(BF16)
    mall = mall_ref[...]
    if L % 16 == 0:
        e_scr[...] = _dot(mall, jnp.concatenate([hi, mid, lo], axis=0))
    else:
        mall = mall[:, :L]
        e_scr[...] = _dot(mall, hi) + _dot(mall, mid) + _dot(mall, lo)

    dec_t = []
    for g in range(G):
        last = last_rows[g]
        bl = e_scr[last:last + 1, :]
        bl8 = jnp.concatenate([bl[:, DK_B * h:DK_B * (h + 1)] for h in range(N_HEADS_B)], axis=0)
        dec_t.append(jnp.exp2(bl8).T)

    for h in range(N_HEADS_B):
        sl = slice(DK_B * h, DK_B * (h + 1))
        q = zs(OFF_QB + DK_B * h, DK_B).astype(BF16)
        k = kb[:, sl]
        kbf = k.astype(BF16)
        v = zs(OFF_VB + DV_B * h, DV_B)
        amat = jnp.zeros((L, L), F32)
        for li in range(len(levels)):
            ex = jnp.exp2(e_scr[(2 + li) * L:(3 + li) * L, sl]).astype(BF16)
            amat = jnp.where(msk_ref[3 + li] > 0.5, _dot_nt(q * ex, kbf * ex), amat)
        qf = q.astype(F32)
        diag = jnp.sum(qf * k, axis=1, keepdims=True)
        qs = qf * jnp.exp2(e_scr[0:L, sl])
        o_inter = by_group([_dot(qs, s_in[0, g, h]) for g in range(G)])
        o = _dot(amat, v) + diag * v.astype(F32) + o_inter
        ms = jnp.mean(o * o, axis=1, keepdims=True)
        gb = zs(OFF_GB + DV_B * h, DV_B).astype(F32)
        yb = o * lax.rsqrt(ms + EPS) * nb_ref[:, sl] * (gb * _sigmoid(gb))
        put_mix(A_WIDTH + DV_B * h, DV_B, yb)
        upd = state_updates(k * jnp.exp2(e_scr[L:2 * L, sl]), v)
        for g in range(G):
            s_out[0, g, h] = dec_t[g][:, h:h + 1] * s_in[0, g, h] + upd[g]


def _mixer_args(L, Tg, t_major, const2, const3, gbias, lb_logits, norm_a, norm_b):
    mats, masks, levels, last_rows = _mix_consts(L, Tg, t_major)
    n_mats = mats.shape[0] // L
    if L % 16 == 0:
        mats = np.concatenate([mats, mats, mats], axis=1)
    specs = [pl.BlockSpec((1, LANE), const2),
             pl.BlockSpec((2, B_WIDTH), const2),
             pl.BlockSpec((1, A_WIDTH), const2),
             pl.BlockSpec((1, B_WIDTH), const2),
             pl.BlockSpec(mats.shape, const2),
             pl.BlockSpec(masks.shape, const3)]
    args = [gbias, lb_logits, norm_a, norm_b, jnp.asarray(mats, BF16), jnp.asarray(masks)]
    return specs, args, levels, last_rows, n_mats


def _state_shapes(n_seq):
    return [jax.ShapeDtypeStruct((1, n_seq, N_HEADS_A, DK_A, DV_A), F32),
            jax.ShapeDtypeStruct((1, n_seq, N_HEADS_A, DK_A), F32),
            jax.ShapeDtypeStruct((1, n_seq, N_HEADS_A), F32),
            jax.ShapeDtypeStruct((1, n_seq, N_HEADS_B, DK_B, DV_B), F32)]


def _state_specs(G, n_seq, seq_of_step):
    return [pl.BlockSpec((1, G, N_HEADS_A, DK_A, DV_A), lambda *ids: (0, seq_of_step(*ids), 0, 0, 0)),
            pl.BlockSpec((1, G, N_HEADS_A, DK_A), lambda *ids: (0, seq_of_step(*ids), 0, 0)),
            pl.BlockSpec((1, n_seq, N_HEADS_A), lambda *ids: (0, 0, 0)),
            pl.BlockSpec((1, G, N_HEADS_B, DK_B, DV_B), lambda *ids: (0, seq_of_step(*ids), 0, 0, 0))]


def _mixer(z, gates, gbias, lb_logits, norm_a, norm_b, *, n_seq, seq_len, states):
    Tg, G = seq_len, SAMPLE_GROUP
    L = Tg * G
    tile = lambda w: pl.BlockSpec((Tg, G, w), lambda i: (0, i, 0))
    const_specs, const_args, levels, last_rows, n_mats = _mixer_args(
        L, Tg, True, lambda i: (0, 0), lambda i: (0, 0, 0), gbias, lb_logits, norm_a, norm_b)
    st_specs = _state_specs(G, n_seq, lambda i: i)
    kern = functools.partial(_mix_kernel, L=L, Tg=Tg, levels=levels, last_rows=last_rows)
    return pl.pallas_call(
        kern,
        grid=(n_seq // G,),
        in_specs=[tile(Z_MAIN), tile(LANE)] + const_specs + st_specs,
        out_specs=[tile(D_MODEL)] + st_specs,
        out_shape=[jax.ShapeDtypeStruct((Tg, n_seq, D_MODEL), F32)] + _state_shapes(n_seq),
        scratch_shapes=[pltpu.VMEM((n_mats * L, B_WIDTH), F32)],
        compiler_params=pltpu.CompilerParams(dimension_semantics=("arbitrary",), vmem_limit_bytes=VMEM_LIMIT),
        name="mixer_state",
    )(z, gates, *const_args, *states)


def _mixer_chunk(z, gates, gbias, lb_logits, norm_a, norm_b, *, n_seq, seq_len):
    L = PROMPT_CHUNK
    nb = PROMPT_SEQS_PER_STEP
    nchunks = seq_len // L
    tile = lambda w: pl.BlockSpec((nb, L, w), lambda b, c: (b, c, 0))
    const_specs, const_args, levels, last_rows, n_mats = _mixer_args(
        L, L, False, lambda b, c: (0, 0), lambda b, c: (0, 0, 0), gbias, lb_logits, norm_a, norm_b)
    st_specs = _state_specs(nb, n_seq, lambda b, c: b)
    kern = functools.partial(_mixchunk_kernel, L=L, levels=levels, last_rows=last_rows)
    outs = pl.pallas_call(
        kern,
        grid=(n_seq // nb, nchunks),
        in_specs=[tile(Z_MAIN), tile(LANE)] + const_specs,
        out_specs=[tile(D_MODEL)] + st_specs,
        out_shape=[jax.ShapeDtypeStruct((n_seq, seq_len, D_MODEL), BF16)] + _state_shapes(n_seq),
        scratch_shapes=[pltpu.VMEM((nb, n_mats * L, B_WIDTH), F32)],
        compiler_params=pltpu.CompilerParams(dimension_semantics=("arbitrary", "arbitrary"),
                                             vmem_limit_bytes=VMEM_LIMIT),
        name="mixer_chunk",
    )(z.reshape(n_seq, seq_len, Z_MAIN), gates.reshape(n_seq, seq_len, LANE), *const_args)
    return [outs[0].reshape(n_seq * seq_len, D_MODEL)] + list(outs[1:])


def _res_kernel(*refs, sub, seq_rows, tiles_per_seq, emit_h2, nk):
    if emit_h2:
        a_ref, w_ref, x_ref, gm_ref, lg_ref, lb_ref, sh_ref, sc_ref, o_ref, h_ref = refs
    else:
        a_ref, w_ref, x_ref, gm_ref, lg_ref, lb_ref, o_ref = refs
    i = pl.program_id(0)
    k = pl.program_id(1)

    if nk > 1:
        @pl.when(k == 0)
        def _():
            o_ref[...] = _dot(a_ref[...], w_ref[...])

    if nk > 2:
        @pl.when((k > 0) & (k < nk - 1))
        def _():
            o_ref[...] += _dot(a_ref[...], w_ref[...])

    @pl.when(k == nk - 1)
    def _():
        seq = i // tiles_per_seq
        gm = _mod_rows(gm_ref, seq_rows, seq)
        if emit_h2:
            sh = _mod_rows(sh_ref, seq_rows, seq)
            sc = _mod_rows(sc_ref, seq_rows, seq)
        for r in range(0, o_ref.shape[0], sub):
            rows = slice(r, r + sub)
            acc = _dot(a_ref[rows, :], w_ref[...])
            if nk > 1:
                acc = acc + o_ref[rows, :]
            x1 = _ln(ALPHA * x_ref[rows, :] + gm * acc) * lg_ref[...] + lb_ref[...]
            o_ref[rows, :] = x1
            if emit_h2:
                h_ref[rows, :] = (_ln(x1) * (1.0 + sc) + sh).astype(BF16)


def _res_block(a, w, x, mod, ln_g, ln_b, *, tm, bk, sub, seq_rows, seq_len, gate_col, emit_h2):
    tokens, kdim = a.shape
    nk = kdim // bk
    mod_spec = lambda c: pl.BlockSpec((mod.shape[0], D_MODEL), lambda i, k: (0, c))
    row_spec = pl.BlockSpec((tm, D_MODEL), lambda i, k: (i, 0))
    vec_spec = pl.BlockSpec((1, D_MODEL), lambda i, k: (0, 0))
    in_specs = [pl.BlockSpec((tm, bk), lambda i, k: (i, k)),
                pl.BlockSpec((bk, D_MODEL), lambda i, k: (k, 0)),
                row_spec, mod_spec(gate_col), vec_spec, vec_spec]
    args = [a, w, x, mod, ln_g, ln_b]
    out_specs = [row_spec]
    out_shape = [jax.ShapeDtypeStruct((tokens, D_MODEL), F32)]
    if emit_h2:
        in_specs += [mod_spec(3), mod_spec(4)]
        args += [mod, mod]
        out_specs.append(row_spec)
        out_shape.append(jax.ShapeDtypeStruct((tokens, D_MODEL), BF16))
    kern = functools.partial(_res_kernel, sub=sub, seq_rows=seq_rows, tiles_per_seq=max(seq_len // tm, 1),
                             emit_h2=emit_h2, nk=nk)
    return pl.pallas_call(
        kern,
        grid=(tokens // tm, nk),
        in_specs=in_specs,
        out_specs=out_specs,
        out_shape=out_shape,
        compiler_params=pltpu.CompilerParams(
            dimension_semantics=("arbitrary", "arbitrary"), vmem_limit_bytes=VMEM_LIMIT),
        name="res_block",
    )(*args)


def _up_kernel(*refs, tm, sub, seq_len, n_seq, t_major, tiles_per_seq):
    if t_major:
        h_ref, wa_scr, wu_scr, cw_ref, cb_ref, cache_ref, g_ref, tail_ref, abuf = refs
    else:
        (h_ref, wa_ref, uu_ref, wd_ref, cw_ref, cb_ref,
         g_ref, tail_ref, wa_scr, wu_scr, wd_out, carry) = refs
    j = pl.program_id(0)
    i = pl.program_id(1)
    bn = wa_scr.shape[1]
    valid = (j * bn + lax.broadcasted_iota(jnp.int32, (1, bn), 1)) < D_FF
    hist = (CONV_W - 1) * n_seq

    @pl.when(i == 0)
    def _():
        if t_major:
            abuf[0:hist, :] = jnp.where(valid, cache_ref[...].reshape(hist, bn), 0.0)
        else:
            wu = uu_ref[...]
            wu_last = jnp.concatenate([wu[:, LANE:], jnp.zeros((D_MODEL, LANE), F32)], axis=1)
            wu = jnp.where(j == pl.num_programs(0) - 1, wu_last, wu)
            wa_scr[...] = jnp.where(valid, wa_ref[...], 0.0).astype(BF16)
            wu_scr[...] = jnp.where(valid, wu, 0.0).astype(BF16)
            wd_rows = j * bn + lax.broadcasted_iota(jnp.int32, (bn, 1), 0)
            wd_out[...] = jnp.where(wd_rows < D_FF, wd_ref[...], 0.0).astype(BF16)

    if not t_major:
        @pl.when(i % tiles_per_seq == 0)
        def _():
            carry[...] = jnp.zeros(carry.shape, F32)

    row8 = lax.broadcasted_iota(jnp.int32, (8, 1), 0)
    for r in range(0, tm, sub):
        h = h_ref[r:r + sub, :]
        a = _dot(h, wa_scr[...])
        u = _dot(h, wu_scr[...])
        if t_major:
            abuf[hist + r:hist + r + sub, :] = a
            a1 = abuf[hist + r - n_seq:hist + r - n_seq + sub, :]
            a2 = abuf[hist + r - 2 * n_seq:hist + r - 2 * n_seq + sub, :]
        else:
            p1 = carry[7:8, :]
            p2 = carry[6:7, :]
            r1 = pltpu.roll(a, 1, axis=0)
            r2 = pltpu.roll(a, 2, axis=0)
            a1 = jnp.concatenate([jnp.where(row8 == 0, p1, r1[0:8, :]), r1[8:, :]], axis=0)
            a2 = jnp.concatenate(
                [jnp.where(row8 == 0, p2, jnp.where(row8 == 1, p1, r2[0:8, :])), r2[8:, :]], axis=0)
            carry[...] = a[sub - 8:sub, :]
        conv = cb_ref[...] + cw_ref[0:1, :] * a2 + cw_ref[1:2, :] * a1 + cw_ref[2:3, :] * a
        gl = 0.5 * conv * (1.0 + lax.erf(conv * (2.0 ** -0.5)))
        g_ref[r:r + sub, :] = (gl * u).astype(BF16)

    if t_major:
        tail_ref[...] = abuf[tm:tm + hist, :].reshape(tail_ref.shape)
    else:
        @pl.when((i + 1) % tiles_per_seq == 0)
        def _():
            tail_ref[0] = carry[8 - (CONV_W - 1):8, :]


def _up_proj(h2, w_up, w_down, conv_w_p, conv_b_p, *, tm, sub, seq_len, n_seq, cache=None):
    tokens = h2.shape[0]
    t_major = cache is not None
    bn = FF_BLOCK
    nj = D_FF_PAD // bn
    assert D_FF_PAD - D_FF == LANE
    tiles_per_seq = max(seq_len // tm, 1)
    wspec = lambda f: pl.BlockSpec((D_MODEL, bn), f)
    col_spec = wspec(lambda j, i: (0, j))
    h_spec = pl.BlockSpec((tm, D_MODEL), lambda j, i: (i, 0))
    conv_specs = [pl.BlockSpec((CONV_W, bn), lambda j, i: (0, j)), pl.BlockSpec((1, bn), lambda j, i: (0, j))]
    out_specs = [pl.BlockSpec((tm, bn), lambda j, i: (i, j))]
    out_shape = [jax.ShapeDtypeStruct((tokens, D_FF_PAD), BF16)]
    if t_major:
        hist_spec = pl.BlockSpec((CONV_W - 1, n_seq, bn), lambda j, i: (0, 0, j))
        in_specs = [h_spec, col_spec, col_spec] + conv_specs + [hist_spec]
        args = [h2, w_up[0], w_up[1], conv_w_p, conv_b_p, cache]
        out_specs.append(hist_spec)
        out_shape.append(jax.ShapeDtypeStruct((CONV_W - 1, n_seq, D_FF_PAD), F32))
        scratch_rows = (CONV_W - 1) * n_seq + tm
    else:
        row_spec = pl.BlockSpec((bn, D_MODEL), lambda j, i: (j, 0))
        u_off = lambda j, i: (0, pl.multiple_of(jnp.minimum(D_FF + bn * j, 2 * D_FF - bn), LANE))
        in_specs = [h_spec, col_spec,
                    pl.BlockSpec((pl.Element(D_MODEL), pl.Element(bn)), u_off),
                    row_spec] + conv_specs
        args = [h2, w_up, w_up, w_down, conv_w_p, conv_b_p]
        out_specs += [pl.BlockSpec((1, CONV_W - 1, bn), lambda j, i: (i // tiles_per_seq, 0, j)),
                      col_spec, col_spec, row_spec]
        out_shape += [jax.ShapeDtypeStruct((n_seq, CONV_W - 1, D_FF_PAD), F32),
                      jax.ShapeDtypeStruct((D_MODEL, D_FF_PAD), BF16),
                      jax.ShapeDtypeStruct((D_MODEL, D_FF_PAD), BF16),
                      jax.ShapeDtypeStruct((D_FF_PAD, D_MODEL), BF16)]
        scratch_rows = 8
    kern = functools.partial(_up_kernel, tm=tm, sub=sub, seq_len=seq_len, n_seq=n_seq, t_major=t_major,
                             tiles_per_seq=tiles_per_seq)
    outs = pl.pallas_call(
        kern,
        grid=(nj, tokens // tm),
        in_specs=in_specs,
        out_specs=out_specs,
        out_shape=out_shape,
        scratch_shapes=[pltpu.VMEM((scratch_rows, bn), F32)],
        compiler_params=pltpu.CompilerParams(
            dimension_semantics=("arbitrary", "arbitrary"), vmem_limit_bytes=VMEM_LIMIT),
        name="up_proj",
    )(*args)
    if t_major:
        return outs[0], outs[1], w_up, w_down
    return outs[0], outs[1], (outs[2], outs[3]), outs[4]


def _path(x, mod, w, *, n_seq, seq_len, tm_in, tm_res, tm_up, tm_down, states=None, cache=None):
    sample = states is not None
    sub = n_seq if sample else 256
    mixer_w = (w["gbias"], w["lb_logits"], w["norm_a"], w["norm_b"])
    z, gates = _in_proj(x, mod, w["w_main"], w["w_gate"], tm=tm_in, sub=sub, seq_rows=sample,
                        seq_len=seq_len, z_dtype=F32 if sample else BF16)
    if sample:
        mix, c1, n1, m1, s1 = _mixer(z.reshape(seq_len, n_seq, Z_MAIN), gates.reshape(seq_len, n_seq, LANE),
                                     *mixer_w, n_seq=n_seq, seq_len=seq_len, states=states)
        mix = mix.reshape(n_seq * seq_len, D_MODEL)
    else:
        mix, c1, n1, m1, s1 = _mixer_chunk(z, gates, *mixer_w, n_seq=n_seq, seq_len=seq_len)
    x1, h2 = _res_block(mix, w["w_out"], x, mod, w["ln1_g"], w["ln1_b"], tm=tm_res, bk=D_MODEL, sub=sub,
                        seq_rows=sample, seq_len=seq_len, gate_col=2, emit_h2=True)
    g, conv_out, w_up_bf, w_down_bf = _up_proj(h2, w["w_up"], w["w_down"], w["conv_w"], w["conv_b"], tm=tm_up,
                                               sub=sub, seq_len=seq_len, n_seq=n_seq, cache=cache)
    (x2,) = _res_block(g, w_down_bf, x1, mod, w["ln2_g"], w["ln2_b"], tm=tm_down, bk=D_FF_PAD // 2, sub=sub,
                       seq_rows=sample, seq_len=seq_len, gate_col=5, emit_h2=False)
    return x2, (c1, n1, m1, s1), conv_out, dict(w_up=w_up_bf, w_down=w_down_bf)


def kernel(x_prompt, x_sample, state_mlstm_C, state_mlstm_n, state_mlstm_m, state_hgrn_S, cache_ffn_conv,
           c_prompt, c_sample, hgrn_lb_logits, w_ada, b_ada, w_in, b_gate_a, norm_a, norm_b, w_out,
           ln1_g, ln1_b, w_up, conv_w, conv_b, w_down, ln2_g, ln2_b):
    bp, tp, _ = x_prompt.shape
    bs, ts, _ = x_sample.shape
    pad_ff = D_FF_PAD - D_FF
    w_main, w_gate = _prep_w_in(jnp.transpose(w_in[0]))
    weights = dict(
        w_main=w_main, w_gate=w_gate, w_out=w_out[0].astype(BF16), w_up=w_up[0],
        w_down=w_down[0],
        gbias=jnp.pad(b_gate_a[0].reshape(1, N_GATE), ((0, 0), (0, LANE - N_GATE))),
        lb_logits=hgrn_lb_logits, norm_a=norm_a, norm_b=norm_b,
        ln1_g=ln1_g, ln1_b=ln1_b, ln2_g=ln2_g, ln2_b=ln2_b,
        conv_w=jnp.pad(conv_w[0], ((0, 0), (0, pad_ff))),
        conv_b=jnp.pad(conv_b, ((0, 0), (0, pad_ff))))

    c_p = jnp.pad(c_prompt, ((0, 8 - bp), (0, 0)))
    mod_p, mod_s = _ada(c_p, c_sample, w_ada[0], b_ada)

    yp, st_p, tail_p, w_ffn_bf = _path(x_prompt.reshape(bp * tp, D_MODEL), mod_p, weights, n_seq=bp,
                                       seq_len=tp, tm_in=1024, tm_res=512, tm_up=2048, tm_down=512)

    xs_t = jnp.swapaxes(x_sample, 0, 1).reshape(ts * bs, D_MODEL)
    cache_t = jnp.swapaxes(cache_ffn_conv[0], 0, 1)
    ys_t, st_s, tail_s, _ = _path(xs_t, mod_s, dict(weights, **w_ffn_bf), n_seq=bs, seq_len=ts,
                                  tm_in=ts * bs, tm_res=ts * bs, tm_up=ts * bs, tm_down=ts * bs,
                                  states=(state_mlstm_C, state_mlstm_n, state_mlstm_m, state_hgrn_S),
                                  cache=cache_t)

    ys = jnp.swapaxes(ys_t.reshape(ts, bs, D_MODEL), 0, 1)
    conv_p = tail_p[None, :, :, :D_FF]
    conv_s = jnp.swapaxes(tail_s[:, :, :D_FF], 0, 1)[None]
    return (yp.reshape(bp, tp, D_MODEL), ys,
            st_p[0], st_p[1], st_p[2], st_p[3], conv_p,
            st_s[0], st_s[1], st_s[2], st_s[3], conv_s)
```

```python
import functools

import numpy as np
import jax
import jax.numpy as jnp
from jax import lax
from jax.experimental import pallas as pl
from jax.experimental.pallas import tpu as pltpu

F32 = jnp.float32
BF16 = jnp.bfloat16

D_MODEL = 2048
N_HEADS_A, DK_A, DV_A = 4, 128, 256
N_HEADS_B, DK_B, DV_B = 8, 128, 128
A_WIDTH = N_HEADS_A * DV_A
B_WIDTH = N_HEADS_B * DV_B
N_GATE = 2 * N_HEADS_A
GATE_COL = 2 * N_HEADS_A * DK_A + A_WIDTH
D_IN = GATE_COL + N_GATE + A_WIDTH + 4 * B_WIDTH
D_FF = 5504
CONV_W = 3
EPS = 1e-5
ALPHA = 2.0 ** 0.25
LOG2_E = 1.4426950408889634
LANE = 128
FF_BLOCK = 512
D_FF_PAD = 5632
Z_MAIN = D_IN - N_GATE
OFF_QA, OFF_KA, OFF_VA, OFF_OA = 0, 512, 1024, 2048
OFF_FB, OFF_QB, OFF_VB, OFF_GB = 3072, 4096, 5120, 6144
VMEM_LIMIT = 60 * 1024 * 1024
PROMPT_CHUNK = 128
PROMPT_SEQS_PER_STEP = 2
SAMPLE_GROUP = 8


def _ln(x):
    mu = jnp.mean(x, axis=-1, keepdims=True)
    xc = x - mu
    var = jnp.mean(xc * xc, axis=-1, keepdims=True)
    return xc * lax.rsqrt(var + EPS)


def _sigmoid(x):
    return 1.0 / (1.0 + jnp.exp(-x))


def _dot(a, b):
    return jnp.dot(a.astype(BF16), b.astype(BF16), preferred_element_type=F32)


def _dot_nt(a, b):
    return lax.dot_general(a.astype(BF16), b.astype(BF16), (((1,), (1,)), ((), ())),
                           preferred_element_type=F32)


def _dot_tn(a, b):
    return lax.dot_general(a.astype(BF16), b.astype(BF16), (((0,), (0,)), ((), ())),
                           preferred_element_type=F32)


def _mod_rows(ref, seq_rows, seq):
    if seq_rows:
        return ref[...]
    return ref[pl.ds(seq, 1), :]


def _ada_kernel(cp_ref, cs_ref, w_ref, b_ref, op_ref, os_ref):
    w = w_ref[...].astype(BF16)
    for c_ref, o_ref in ((cp_ref, op_ref), (cs_ref, os_ref)):
        c = c_ref[...]
        o_ref[...] = _dot(c * _sigmoid(c), w) + b_ref[...]


def _ada(c_p, c_s, w_ada, b_ada):
    n = w_ada.shape[1]
    bn = 1024
    row = lambda r: pl.BlockSpec((r, D_MODEL), lambda j: (0, 0))
    out = lambda r: pl.BlockSpec((r, bn), lambda j: (0, j))
    return pl.pallas_call(
        _ada_kernel,
        grid=(n // bn,),
        in_specs=[row(c_p.shape[0]), row(c_s.shape[0]),
                  pl.BlockSpec((D_MODEL, bn), lambda j: (0, j)),
                  pl.BlockSpec((1, bn), lambda j: (0, j))],
        out_specs=[out(c_p.shape[0]), out(c_s.shape[0])],
        out_shape=[jax.ShapeDtypeStruct((c_p.shape[0], n), F32),
                   jax.ShapeDtypeStruct((c_s.shape[0], n), F32)],
        compiler_params=pltpu.CompilerParams(vmem_limit_bytes=VMEM_LIMIT),
        name="ada",
    )(c_p, c_s, w_ada, b_ada)


def _prep_in_kernel(front_ref, back_ref, gate_ref, o_ref, og_ref, *, first_shifted):
    j = pl.program_id(0)

    @pl.when(j == 0)
    def _():
        og_ref[...] = gate_ref[...].T.astype(BF16)

    def emit(src_ref):
        for r in range(0, src_ref.shape[0], 256):
            o_ref[:, r:r + 256] = src_ref[r:r + 256, :].T.astype(BF16)

    @pl.when(j < first_shifted)
    def _():
        emit(front_ref)

    @pl.when(j >= first_shifted)
    def _():
        emit(back_ref)


def _prep_w_in(w_in_t):
    bn = 1024
    first_shifted = GATE_COL // bn
    kern = functools.partial(_prep_in_kernel, first_shifted=first_shifted)
    return pl.pallas_call(
        kern,
        grid=(Z_MAIN // bn,),
        in_specs=[pl.BlockSpec((bn, D_MODEL), lambda j: (jnp.minimum(j, first_shifted - 1), 0)),
                  pl.BlockSpec((pl.Element(bn), pl.Element(D_MODEL)),
                               lambda j: (pl.multiple_of(jnp.maximum(j, first_shifted) * bn + N_GATE, N_GATE), 0)),
                  pl.BlockSpec((LANE, D_MODEL), lambda j: (GATE_COL // LANE, 0))],
        out_specs=[pl.BlockSpec((D_MODEL, bn), lambda j: (0, j)),
                   pl.BlockSpec((D_MODEL, LANE), lambda j: (0, 0))],
        out_shape=[jax.ShapeDtypeStruct((D_MODEL, Z_MAIN), BF16),
                   jax.ShapeDtypeStruct((D_MODEL, LANE), BF16)],
        compiler_params=pltpu.CompilerParams(vmem_limit_bytes=VMEM_LIMIT),
        name="prep_w_in",
    )(w_in_t, w_in_t, w_in_t)


def _in_kernel(x_ref, sh_ref, sc_ref, w_ref, wg_ref, z_ref, g_ref, h_scr, *, sub, seq_rows, tiles_per_seq):
    i = pl.program_id(0)
    j = pl.program_id(1)

    @pl.when(j == 0)
    def _():
        seq = i // tiles_per_seq
        sh = _mod_rows(sh_ref, seq_rows, seq)
        sc = _mod_rows(sc_ref, seq_rows, seq)
        for r in range(0, x_ref.shape[0], sub):
            rows = slice(r, r + sub)
            h = (_ln(x_ref[rows, :]) * (1.0 + sc) + sh).astype(BF16)
            h_scr[rows, :] = h
            g_ref[rows, :] = _dot(h, wg_ref[...])
            z_ref[rows, :] = _dot(h, w_ref[...]).astype(z_ref.dtype)

    @pl.when(j > 0)
    def _():
        z_ref[...] = _dot(h_scr[...], w_ref[...]).astype(z_ref.dtype)


def _in_proj(x, mod, w_main, w_gate, *, tm, sub, seq_rows, seq_len, z_dtype):
    tokens = x.shape[0]
    bn = 1024
    mod_spec = lambda k: pl.BlockSpec((mod.shape[0], D_MODEL), lambda i, j: (0, k))
    kern = functools.partial(_in_kernel, sub=min(sub, tm), seq_rows=seq_rows, tiles_per_seq=max(seq_len // tm, 1))
    return pl.pallas_call(
        kern,
        grid=(tokens // tm, Z_MAIN // bn),
        in_specs=[pl.BlockSpec((tm, D_MODEL), lambda i, j: (i, 0)),
                  mod_spec(0), mod_spec(1),
                  pl.BlockSpec((D_MODEL, bn), lambda i, j: (0, j)),
                  pl.BlockSpec((D_MODEL, LANE), lambda i, j: (0, 0))],
        out_specs=[pl.BlockSpec((tm, bn), lambda i, j: (i, j)),
                   pl.BlockSpec((tm, LANE), lambda i, j: (i, 0))],
        out_shape=[jax.ShapeDtypeStruct((tokens, Z_MAIN), z_dtype),
                   jax.ShapeDtypeStruct((tokens, LANE), F32)],
        scratch_shapes=[pltpu.VMEM((tm, D_MODEL), BF16)],
        compiler_params=pltpu.CompilerParams(
            dimension_semantics=("arbitrary", "arbitrary"), vmem_limit_bytes=VMEM_LIMIT),
        name="in_proj",
    )(x, mod, mod, w_main, w_gate)


def _mix_consts(L, Tg, t_major):
    G = L // Tg
    r = np.arange(L)
    seq, tim = (r % G, r // G) if t_major else (r // Tg, r % Tg)
    same = seq[:, None] == seq[None, :]
    t = tim[:, None]
    u = tim[None, :]
    mats = [same & (u <= t), same & (u > t)]
    masks = [same & (u <= t), same & (t <= u), same]
    levels = []
    m = Tg // 2
    while m >= 1:
        levels.append(m)
        m //= 2
    for m in levels:
        blk = same & (u // m == t // m)
        odd = (t // m) % 2 == 1
        mats.append(np.where(odd, blk & (u <= t), blk & (u > t)))
        masks.append(same & odd & ((u // m) % 2 == 0) & (t // (2 * m) == u // (2 * m)))
    mats = np.concatenate(mats, axis=0).astype(np.float32)
    masks = np.stack(masks).astype(np.float32)
    last_rows = tuple(int(np.nonzero((seq == g) & (tim == Tg - 1))[0][0]) for g in range(G))
    return mats, masks, tuple(levels), last_rows


def _mix_kernel(z_ref, g_ref, gbias_ref, lbl_ref, na_ref, nb_ref, mall_ref, msk_ref,
                c_in, n_in, m_in, s_in, mix_ref, c_out, n_out, m_out, s_out, e_scr,
                *, L, Tg, levels, last_rows):
    G = L // Tg

    def zs(off, w):
        return z_ref[:, :, off:off + w].reshape(L, w)

    def put_mix(off, w, val):
        mix_ref[:, :, off:off + w] = val.astype(mix_ref.dtype).reshape(Tg, G, w)

    _mix_body(zs, put_mix, g_ref[...].reshape(L, LANE), gbias_ref, lbl_ref, na_ref, nb_ref, mall_ref, msk_ref,
              c_in, n_in, m_in, s_in, c_out, n_out, m_out, s_out, e_scr, pl.program_id(0) * G,
              L=L, Tg=Tg, levels=levels, last_rows=last_rows, t_major=True)


def _mixchunk_kernel(z_ref, g_ref, gbias_ref, lbl_ref, na_ref, nb_ref, mall_ref, msk_ref,
                     mix_ref, c_out, n_out, m_out, s_out, e_scr, *, L, levels, last_rows):
    nb = z_ref.shape[0]
    b0 = pl.program_id(0) * nb

    @pl.when(pl.program_id(1) == 0)
    def _():
        c_out[...] = jnp.zeros(c_out.shape, F32)
        n_out[...] = jnp.zeros(n_out.shape, F32)
        s_out[...] = jnp.zeros(s_out.shape, F32)
        m_out[0, pl.ds(b0, nb), :] = jnp.zeros((nb, N_HEADS_A), F32)

    for k in range(nb):
        def zs(off, w, k=k):
            return z_ref[k, :, off:off + w]

        def put_mix(off, w, val, k=k):
            mix_ref[k, :, off:off + w] = val.astype(mix_ref.dtype)

        st = [r.at[:, pl.ds(k, 1)] for r in (c_out, n_out, s_out)]
        _mix_body(zs, put_mix, g_ref[k], gbias_ref, lbl_ref, na_ref, nb_ref, mall_ref, msk_ref,
                  st[0], st[1], m_out, st[2], st[0], st[1], m_out, st[2], e_scr.at[k], b0 + k,
                  L=L, Tg=L, levels=levels, last_rows=last_rows, t_major=False)


def _mix_body(zs, put_mix, gates, gbias_ref, lbl_ref, na_ref, nb_ref, mall_ref, msk_ref,
              c_in, n_in, m_in, s_in, c_out, n_out, m_out, s_out, e_scr, m_row0,
              *, L, Tg, levels, last_rows, t_major):
    G = L // Tg
    neg_inf = F32(-jnp.inf)
    causal = msk_ref[0] > 0.5
    causal_t = msk_ref[1] > 0.5
    same = msk_ref[2] > 0.5
    row1 = lax.broadcasted_iota(jnp.int32, (L, 1), 0)
    seq_of_row = (row1 & (G - 1)) if t_major else (row1 // Tg)
    in_group = [seq_of_row == g for g in range(G)]
    col1 = lax.broadcasted_iota(jnp.int32, (1, L), 1)
    seq_of_col = (col1 & (G - 1)) if t_major else (col1 // Tg)
    in_group_lane = [seq_of_col == g for g in range(G)]

    def state_updates(kw, v):
        if G == 1:
            return [_dot_tn(kw, v)]
        kw_t = kw.T
        return [_dot(jnp.where(in_group_lane[g], kw_t, 0.0), v) for g in range(G)]

    def by_group(vals):
        if G == 1:
            return vals[0]
        out = jnp.where(in_group[0], vals[0], 0.0)
        for g in range(1, G):
            out = out + jnp.where(in_group[g], vals[g], 0.0)
        return out

    pre = gates + gbias_ref[...]
    lsig = jnp.minimum(pre, 0.0) - jnp.log(1.0 + jnp.exp(-jnp.abs(pre)))
    pre_t = pre.T
    lsig_t = lsig.T
    m_prev = m_in[0, pl.ds(m_row0, G), :]
    scale = DK_A ** -0.5
    m_new_rows = []
    for h in range(N_HEADS_A):
        q = zs(OFF_QA + DK_A * h, DK_A)
        k = zs(OFF_KA + DK_A * h, DK_A)
        v = zs(OFF_VA + DV_A * h, DV_A)
        logi_c = pre[:, h:h + 1]
        logi_r = pre_t[h:h + 1, :]
        lf_c = lsig[:, N_HEADS_A + h:N_HEADS_A + h + 1]
        lf_r = lsig_t[N_HEADS_A + h:N_HEADS_A + h + 1, :]
        b_c = jnp.sum(jnp.where(causal, lf_r, 0.0), axis=1, keepdims=True)
        b_r = jnp.sum(jnp.where(causal_t, lf_c, 0.0), axis=0, keepdims=True)
        tots = [b_c[last_rows[g]:last_rows[g] + 1, :] for g in range(G)]
        tot_c = by_group(tots)
        tot_r = tots[0] if G == 1 else sum(jnp.where(in_group_lane[g], tots[g], 0.0) for g in range(G))
        m_col = by_group([m_prev[g:g + 1, h:h + 1] for g in range(G)])
        dmat = jnp.where(causal, b_c - b_r + logi_r, neg_inf)
        inter = b_c + m_col
        m_t = jnp.maximum(inter, jnp.max(dmat, axis=1, keepdims=True))
        w_inter = jnp.exp(inter - m_t)
        smat = _dot_nt(q, k) * (scale * jnp.exp(dmat - m_t))
        q_c = by_group([_dot(q, c_in[0, g, h]) for g in range(G)]) * scale
        n_rows = by_group([n_in[0, g, h:h + 1, :] for g in range(G)])
        q_n = jnp.sum(q.astype(F32) * n_rows, axis=1, keepdims=True) * scale
        num = w_inter * q_c + _dot(smat, v)
        den = w_inter * q_n + jnp.sum(smat, axis=1, keepdims=True)
        hh = num / jnp.maximum(jnp.abs(den), jnp.exp(-m_t))
        ms = jnp.mean(hh * hh, axis=1, keepdims=True)
        oa = zs(OFF_OA + DV_A * h, DV_A).astype(F32)
        ya = hh * lax.rsqrt(ms + EPS) * na_ref[:, DV_A * h:DV_A * (h + 1)] * _sigmoid(oa)
        put_mix(DV_A * h, DV_A, ya)
        dec_c = tot_c - b_c + logi_c
        dec_r = tot_r - b_r + logi_r
        if G == 1:
            dec_max = jnp.max(dec_r, axis=1, keepdims=True)
        else:
            dec_max = jnp.max(jnp.where(same, dec_r, neg_inf), axis=1, keepdims=True)
        m_new_c = jnp.maximum(tot_c + m_col, dec_max)
        wk = jnp.exp(dec_c - m_new_c)
        sc = jnp.exp(tot_c + m_col - m_new_c)
        kw = k.astype(F32) * wk
        upd = state_updates(kw, v)
        m_new_h = []
        for g in range(G):
            last = last_rows[g]
            kg = kw if G == 1 else jnp.where(in_group[g], kw, 0.0)
            sc_g = sc if G == 1 else sc[last:last + 1, :]
            c_new = sc_g * c_in[0, g, h] + upd[g]
            n_new = sc_g * n_in[0, g, h:h + 1, :] + jnp.sum(kg, axis=0, keepdims=True)
            c_out[0, g, h] = c_new
            n_out[0, g, h:h + 1, :] = n_new
            m_new_h.append(m_new_c if G == 1 else m_new_c[last:last + 1, :])
        m_new_rows.append(m_new_h)
    for g in range(G):
        m_out[0, pl.ds(m_row0 + g, 1), :] = jnp.concatenate(
            [m_new_rows[h][g] for h in range(N_HEADS_A)], axis=1)

    l0 = lbl_ref[0:1, :]
    l1 = lbl_ref[1:2, :]
    lmax = jnp.maximum(l0, l1)
    e0 = jnp.exp(l0 - lmax)
    e1 = jnp.exp(l1 - lmax)
    lb = e0 / (e0 + e1)
    fb = zs(OFF_FB, B_WIDTH).astype(F32)
    e = jnp.exp(-jnp.abs(fb))
    r = 1.0 / (1.0 + e)
    pos = fb >= 0.0
    sig = jnp.where(pos, r, e * r)
    nsig = jnp.where(pos, e * r, r)
    logf = jnp.log(lb + (1.0 - lb) * sig) * LOG2_E
    kb = (1.0 - lb) * nsig
    hi = logf.astype(BF16)
    r1 = logf - hi.astype(F32)
    mid = r1.astype(BF16)
    lo = (r1 - mid.astype(F32)).astype(BF16)
    mall = mall_ref[...]
    if L % 16 == 0:
        e_scr[...] = _dot(mall, jnp.concatenate([hi, mid, lo], axis=0))
    else:
        mall = mall[:, :L]
        e_scr[...] = _dot(mall, hi) + _dot(mall, mid) + _dot(mall, lo)

    dec_t = []
    for g in range(G):
        last = last_rows[g]
        bl = e_scr[last:last + 1, :]
        bl8 = jnp.concatenate([bl[:, DK_B * h:DK_B * (h + 1)] for h in range(N_HEADS_B)], axis=0)
        dec_t.append(jnp.exp2(bl8).T)

    for h in range(N_HEADS_B):
        sl = slice(DK_B * h, DK_B * (h + 1))
        q = zs(OFF_QB + DK_B * h, DK_B).astype(BF16)
        k = kb[:, sl]
        kbf = k.astype(BF16)
        v = zs(OFF_VB + DV_B * h, DV_B)
        amat = jnp.zeros((L, L), F32)
        for li in range(len(levels)):
            ex = jnp.exp2(e_scr[(2 + li) * L:(3 + li) * L, sl]).astype(BF16)
            amat = jnp.where(msk_ref[3 + li] > 0.5, _dot_nt(q * ex, kbf * ex), amat)
        qf = q.astype(F32)
        diag = jnp.sum(qf * k, axis=1, keepdims=True)
        qs = qf * jnp.exp2(e_scr[0:L, sl])
        o_inter = by_group([_dot(qs, s_in[0, g, h]) for g in range(G)])
        o = _dot(amat, v) + diag * v.astype(F32) + o_inter
        ms = jnp.mean(o * o, axis=1, keepdims=True)
        gb = zs(OFF_GB + DV_B * h, DV_B).astype(F32)
        yb = o * lax.rsqrt(ms + EPS) * nb_ref[:, sl] * (gb * _sigmoid(gb))
        put_mix(A_WIDTH + DV_B * h, DV_B, yb)
        upd = state_updates(k * jnp.exp2(e_scr[L:2 * L, sl]), v)
        for g in range(G):
            s_out[0, g, h] = dec_t[g][:, h:h + 1] * s_in[0, g, h] + upd[g]


def _mixer_args(L, Tg, t_major, const2, const3, gbias, lb_logits, norm_a, norm_b):
    mats, masks, levels, last_rows = _mix_consts(L, Tg, t_major)
    n_mats = mats.shape[0] // L
    if L % 16 == 0:
        mats = np.concatenate([mats, mats, mats], axis=1)
    specs = [pl.BlockSpec((1, LANE), const2),
             pl.BlockSpec((2, B_WIDTH), const2),
             pl.BlockSpec((1, A_WIDTH), const2),
             pl.BlockSpec((1, B_WIDTH), const2),
             pl.BlockSpec(mats.shape, const2),
             pl.BlockSpec(masks.shape, const3)]
    args = [gbias, lb_logits, norm_a, norm_b, jnp.asarray(mats, BF16), jnp.asarray(masks)]
    return specs, args, levels, last_rows, n_mats


def _state_shapes(n_seq):
    return [jax.ShapeDtypeStruct((1, n_seq, N_HEADS_A, DK_A, DV_A), F32),
            jax.ShapeDtypeStruct((1, n_seq, N_HEADS_A, DK_A), F32),
            jax.ShapeDtypeStruct((1, n_seq, N_HEADS_A), F32),
            jax.ShapeDtypeStruct((1, n_seq, N_HEADS_B, DK_B, DV_B), F32)]


def _state_specs(G, n_seq, seq_of_step):
    return [pl.BlockSpec((1, G, N_HEADS_A, DK_A, DV_A), lambda *ids: (0, seq_of_step(*ids), 0, 0, 0)),
            pl.BlockSpec((1, G, N_HEADS_A, DK_A), lambda *ids: (0, seq_of_step(*ids), 0, 0)),
            pl.BlockSpec((1, n_seq, N_HEADS_A), lambda *ids: (0, 0, 0)),
            pl.BlockSpec((1, G, N_HEADS_B, DK_B, DV_B), lambda *ids: (0, seq_of_step(*ids), 0, 0, 0))]


def _mixer(z, gates, gbias, lb_logits, norm_a, norm_b, *, n_seq, seq_len, states):
    Tg, G = seq_len, SAMPLE_GROUP
    L = Tg * G
    tile = lambda w: pl.BlockSpec((Tg, G, w), lambda i: (0, i, 0))
    const_specs, const_args, levels, last_rows, n_mats = _mixer_args(
        L, Tg, True, lambda i: (0, 0), lambda i: (0, 0, 0), gbias, lb_logits, norm_a, norm_b)
    st_specs = _state_specs(G, n_seq, lambda i: i)
    kern = functools.partial(_mix_kernel, L=L, Tg=Tg, levels=levels, last_rows=last_rows)
    return pl.pallas_call(
        kern,
        grid=(n_seq // G,),
        in_specs=[tile(Z_MAIN), tile(LANE)] + const_specs + st_specs,
        out_specs=[tile(D_MODEL)] + st_specs,
        out_shape=[jax.ShapeDtypeStruct((Tg, n_seq, D_MODEL), F32)] + _state_shapes(n_seq),
        scratch_shapes=[pltpu.VMEM((n_mats * L, B_WIDTH), F32)],
        compiler_params=pltpu.CompilerParams(dimension_semantics=("arbitrary",), vmem_limit_bytes=VMEM_LIMIT),
        name="mixer_state",
    )(z, gates, *const_args, *states)


def _mixer_chunk(z, gates, gbias, lb_logits, norm_a, norm_b, *, n_seq, seq_len):
    L = PROMPT_CHUNK
    nb = PROMPT_SEQS_PER_STEP
    nchunks = seq_len // L
    tile = lambda w: pl.BlockSpec((nb, L, w), lambda b, c: (b, c, 0))
    const_specs, const_args, levels, last_rows, n_mats = _mixer_args(
        L, L, False, lambda b, c: (0, 0), lambda b, c: (0, 0, 0), gbias, lb_logits, norm_a, norm_b)
    st_specs = _state_specs(nb, n_seq, lambda b, c: b)
    kern = functools.partial(_mixchunk_kernel, L=L, levels=levels, last_rows=last_rows)
    outs = pl.pallas_call(
        kern,
        grid=(n_seq // nb, nchunks),
        in_specs=[tile(Z_MAIN), tile(LANE)] + const_specs,
        out_specs=[tile(D_MODEL)] + st_specs,
        out_shape=[jax.ShapeDtypeStruct((n_seq, seq_len, D_MODEL), BF16)] + _state_shapes(n_seq),
        scratch_shapes=[pltpu.VMEM((nb, n_mats * L, B_WIDTH), F32)],
        compiler_params=pltpu.CompilerParams(dimension_semantics=("arbitrary", "arbitrary"),
                                             vmem_limit_bytes=VMEM_LIMIT),
        name="mixer_chunk",
    )(z.reshape(n_seq, seq_len, Z_MAIN), gates.reshape(n_seq, seq_len, LANE), *const_args)
    return [outs[0].reshape(n_seq * seq_len, D_MODEL)] + list(outs[1:])


def _res_kernel(*refs, sub, seq_rows, tiles_per_seq, emit_h2, nk):
    if emit_h2:
        a_ref, w_ref, x_ref, gm_ref, lg_ref, lb_ref, sh_ref, sc_ref, o_ref, h_ref = refs
    else:
        a_ref, w_ref, x_ref, gm_ref, lg_ref, lb_ref, o_ref = refs
    i = pl.program_id(0)
    k = pl.program_id(1)

    if nk > 1:
        @pl.when(k == 0)
        def _():
            o_ref[...] = _dot(a_ref[...], w_ref[...])

    if nk > 2:
        @pl.when((k > 0) & (k < nk - 1))
        def _():
            o_ref[...] += _dot(a_ref[...], w_ref[...])

    @pl.when(k == nk - 1)
    def _():
        seq = i // tiles_per_seq
        gm = _mod_rows(gm_ref, seq_rows, seq)
        if emit_h2:
            sh = _mod_rows(sh_ref, seq_rows, seq)
            sc = _mod_rows(sc_ref, seq_rows, seq)
        for r in range(0, o_ref.shape[0], sub):
            rows = slice(r, r + sub)
            acc = _dot(a_ref[rows, :], w_ref[...])
            if nk > 1:
                acc = acc + o_ref[rows, :]
            x1 = _ln(ALPHA * x_ref[rows, :] + gm * acc) * lg_ref[...] + lb_ref[...]
            o_ref[rows, :] = x1
            if emit_h2:
                h_ref[rows, :] = (_ln(x1) * (1.0 + sc) + sh).astype(BF16)


def _res_block(a, w, x, mod, ln_g, ln_b, *, tm, bk, sub, seq_rows, seq_len, gate_col, emit_h2):
    tokens, kdim = a.shape
    nk = kdim // bk
    mod_spec = lambda c: pl.BlockSpec((mod.shape[0], D_MODEL), lambda i, k: (0, c))
    row_spec = pl.BlockSpec((tm, D_MODEL), lambda i, k: (i, 0))
    vec_spec = pl.BlockSpec((1, D_MODEL), lambda i, k: (0, 0))
    in_specs = [pl.BlockSpec((tm, bk), lambda i, k: (i, k)),
                pl.BlockSpec((bk, D_MODEL), lambda i, k: (k, 0)),
                row_spec, mod_spec(gate_col), vec_spec, vec_spec]
    args = [a, w, x, mod, ln_g, ln_b]
    out_specs = [row_spec]
    out_shape = [jax.ShapeDtypeStruct((tokens, D_MODEL), F32)]
    if emit_h2:
        in_specs += [mod_spec(3), mod_spec(4)]
        args += [mod, mod]
        out_specs.append(row_spec)
        out_shape.append(jax.ShapeDtypeStruct((tokens, D_MODEL), BF16))
    kern = functools.partial(_res_kernel, sub=sub, seq_rows=seq_rows, tiles_per_seq=max(seq_len // tm, 1),
                             emit_h2=emit_h2, nk=nk)
    return pl.pallas_call(
        kern,
        grid=(tokens // tm, nk),
        in_specs=in_specs,
        out_specs=out_specs,
        out_shape=out_shape,
        compiler_params=pltpu.CompilerParams(
            dimension_semantics=("arbitrary", "arbitrary"), vmem_limit_bytes=VMEM_LIMIT),
        name="res_block",
    )(*args)


def _up_kernel(*refs, tm, sub, seq_len, n_seq, t_major, tiles_per_seq):
    if t_major:
        h_ref, wa_scr, wu_scr, cw_ref, cb_ref, cache_ref, g_ref, tail_ref, abuf = refs
    else:
        (h_ref, wa_ref, uu_ref, wd_ref, cw_ref, cb_ref,
         g_ref, tail_ref, wa_scr, wu_scr, wd_out, carry) = refs
    j = pl.program_id(0)
    i = pl.program_id(1)
    bn = wa_scr.shape[1]
    valid = (j * bn + lax.broadcasted_iota(jnp.int32, (1, bn), 1)) < D_FF
    hist = (CONV_W - 1) * n_seq

    @pl.when(i == 0)
    def _():
        if t_major:
            abuf[0:hist, :] = jnp.where(valid, cache_ref[...].reshape(hist, bn), 0.0)
        else:
            wu = uu_ref[...]
            wu_last = jnp.concatenate([wu[:, LANE:], jnp.zeros((D_MODEL, LANE), F32)], axis=1)
            wu = jnp.where(j == pl.num_programs(0) - 1, wu_last, wu)
            wa_scr[...] = jnp.where(valid, wa_ref[...], 0.0).astype(BF16)
            wu_scr[...] = jnp.where(valid, wu, 0.0).astype(BF16)
            wd_rows = j * bn + lax.broadcasted_iota(jnp.int32, (bn, 1), 0)
            wd_out[...] = jnp.where(wd_rows < D_FF, wd_ref[...], 0.0).astype(BF16)

    if not t_major:
        @pl.when(i % tiles_per_seq == 0)
        def _():
            carry[...] = jnp.zeros(carry.shape, F32)

    row8 = lax.broadcasted_iota(jnp.int32, (8, 1), 0)
    for r in range(0, tm, sub):
        h = h_ref[r:r + sub, :]
        a = _dot(h, wa_scr[...])
        u = _dot(h, wu_scr[...])
        if t_major:
            abuf[hist + r:hist + r + sub, :] = a
            a1 = abuf[hist + r - n_seq:hist + r - n_seq + sub, :]
            a2 = abuf[hist + r - 2 * n_seq:hist + r - 2 * n_seq + sub, :]
        else:
            p1 = carry[7:8, :]
            p2 = carry[6:7, :]
            r1 = pltpu.roll(a, 1, axis=0)
            r2 = pltpu.roll(a, 2, axis=0)
            a1 = jnp.concatenate([jnp.where(row8 == 0, p1, r1[0:8, :]), r1[8:, :]], axis=0)
            a2 = jnp.concatenate(
                [jnp.where(row8 == 0, p2, jnp.where(row8 == 1, p1, r2[0:8, :])), r2[8:, :]], axis=0)
            carry[...] = a[sub - 8:sub, :]
        conv = cb_ref[...] + cw_ref[0:1, :] * a2 + cw_ref[1:2, :] * a1 + cw_ref[2:3, :] * a
        gl = 0.5 * conv * (1.0 + lax.erf(conv * (2.0 ** -0.5)))
        g_ref[r:r + sub, :] = (gl * u).astype(BF16)

    if t_major:
        tail_ref[...] = abuf[tm:tm + hist, :].reshape(tail_ref.shape)
    else:
        @pl.when((i + 1) % tiles_per_seq == 0)
        def _():
            tail_ref[0] = carry[8 - (CONV_W - 1):8, :]


def _up_proj(h2, w_up, w_down, conv_w_p, conv_b_p, *, tm, sub, seq_len, n_seq, cache=None):
    tokens = h2.shape[0]
    t_major = cache is not None
    bn = FF_BLOCK
    nj = D_FF_PAD // bn
    assert D_FF_PAD - D_FF == LANE
    tiles_per_seq = max(seq_len // tm, 1)
    wspec = lambda f: pl.BlockSpec((D_MODEL, bn), f)
    col_spec = wspec(lambda j, i: (0, j))
    h_spec = pl.BlockSpec((tm, D_MODEL), lambda j, i: (i, 0))
    conv_specs = [pl.BlockSpec((CONV_W, bn), lambda j, i: (0, j)), pl.BlockSpec((1, bn), lambda j, i: (0, j))]
    out_specs = [pl.BlockSpec((tm, bn), lambda j, i: (i, j))]
    out_shape = [jax.ShapeDtypeStruct((tokens, D_FF_PAD), BF16)]
    if t_major:
        hist_spec = pl.BlockSpec((CONV_W - 1, n_seq, bn), lambda j, i: (0, 0, j))
        in_specs = [h_spec, col_spec, col_spec] + conv_specs + [hist_spec]
        args = [h2, w_up[0], w_up[1], conv_w_p, conv_b_p, cache]
        out_specs.append(hist_spec)
        out_shape.append(jax.ShapeDtypeStruct((CONV_W - 1, n_seq, D_FF_PAD), F32))
        scratch_rows = (CONV_W - 1) * n_seq + tm
    else:
        row_spec = pl.BlockSpec((bn, D_MODEL), lambda j, i: (j, 0))
        u_off = lambda j, i: (0, pl.multiple_of(jnp.minimum(D_FF + bn * j, 2 * D_FF - bn), LANE))
        in_specs = [h_spec, col_spec,
                    pl.BlockSpec((pl.Element(D_MODEL), pl.Element(bn)), u_off),
                    row_spec] + conv_specs
        args = [h2, w_up, w_up, w_down, conv_w_p, conv_b_p]
        out_specs += [pl.BlockSpec((1, CONV_W - 1, bn), lambda j, i: (i // tiles_per_seq, 0, j)),
                      col_spec, col_spec, row_spec]
        out_shape += [jax.ShapeDtypeStruct((n_seq, CONV_W - 1, D_FF_PAD), F32),
                      jax.ShapeDtypeStruct((D_MODEL, D_FF_PAD), BF16),
                      jax.ShapeDtypeStruct((D_MODEL, D_FF_PAD), BF16),
                      jax.ShapeDtypeStruct((D_FF_PAD, D_MODEL), BF16)]
        scratch_rows = 8
    kern = functools.partial(_up_kernel, tm=tm, sub=sub, seq_len=seq_len, n_seq=n_seq, t_major=t_major,
                             tiles_per_seq=tiles_per_seq)
    outs = pl.pallas_call(
        kern,
        grid=(nj, tokens // tm),
        in_specs=in_specs,
        out_specs=out_specs,
        out_shape=out_shape,
        scratch_shapes=[pltpu.VMEM((scratch_rows, bn), F32)],
        compiler_params=pltpu.CompilerParams(
            dimension_semantics=("arbitrary", "arbitrary"), vmem_limit_bytes=VMEM_LIMIT),
        name="up_proj",
    )(*args)
    if t_major:
        return outs[0], outs[1], w_up, w_down
    return outs[0], outs[1], (outs[2], outs[3]), outs[4]


def _path(x, mod, w, *, n_seq, seq_len, tm_in, tm_res, tm_up, tm_down, states=None, cache=None):
    sample = states is not None
    sub = n_seq if sample else 256
    sub_up = n_seq if sample else 512
    mixer_w = (w["gbias"], w["lb_logits"], w["norm_a"], w["norm_b"])
    z, gates = _in_proj(x, mod, w["w_main"], w["w_gate"], tm=tm_in, sub=sub, seq_rows=sample,
                        seq_len=seq_len, z_dtype=F32 if sample else BF16)
    if sample:
        mix, c1, n1, m1, s1 = _mixer(z.reshape(seq_len, n_seq, Z_MAIN), gates.reshape(seq_len, n_seq, LANE),
                                     *mixer_w, n_seq=n_seq, seq_len=seq_len, states=states)
        mix = mix.reshape(n_seq * seq_len, D_MODEL)
    else:
        mix, c1, n1, m1, s1 = _mixer_chunk(z, gates, *mixer_w, n_seq=n_seq, seq_len=seq_len)
    x1, h2 = _res_block(mix, w["w_out"], x, mod, w["ln1_g"], w["ln1_b"], tm=tm_res, bk=D_MODEL, sub=sub,
                        seq_rows=sample, seq_len=seq_len, gate_col=2, emit_h2=True)
    g, conv_out, w_up_bf, w_down_bf = _up_proj(h2, w["w_up"], w["w_down"], w["conv_w"], w["conv_b"], tm=tm_up,
                                               sub=sub_up, seq_len=seq_len, n_seq=n_seq, cache=cache)
    (x2,) = _res_block(g, w_down_bf, x1, mod, w["ln2_g"], w["ln2_b"], tm=tm_down, bk=D_FF_PAD // 2, sub=sub,
                       seq_rows=sample, seq_len=seq_len, gate_col=5, emit_h2=False)
    return x2, (c1, n1, m1, s1), conv_out, dict(w_up=w_up_bf, w_down=w_down_bf)


def kernel(x_prompt, x_sample, state_mlstm_C, state_mlstm_n, state_mlstm_m, state_hgrn_S, cache_ffn_conv,
           c_prompt, c_sample, hgrn_lb_logits, w_ada, b_ada, w_in, b_gate_a, norm_a, norm_b, w_out,
           ln1_g, ln1_b, w_up, conv_w, conv_b, w_down, ln2_g, ln2_b):
    bp, tp, _ = x_prompt.shape
    bs, ts, _ = x_sample.shape
    pad_ff = D_FF_PAD - D_FF
    w_main, w_gate = _prep_w_in(jnp.transpose(w_in[0]))
    weights = dict(
        w_main=w_main, w_gate=w_gate, w_out=w_out[0].astype(BF16), w_up=w_up[0],
        w_down=w_down[0],
        gbias=jnp.pad(b_gate_a[0].reshape(1, N_GATE), ((0, 0), (0, LANE - N_GATE))),
        lb_logits=hgrn_lb_logits, norm_a=norm_a, norm_b=norm_b,
        ln1_g=ln1_g, ln1_b=ln1_b, ln2_g=ln2_g, ln2_b=ln2_b,
        conv_w=jnp.pad(conv_w[0], ((0, 0), (0, pad_ff))),
        conv_b=jnp.pad(conv_b, ((0, 0), (0, pad_ff))))

    c_p = jnp.pad(c_prompt, ((0, 8 - bp), (0, 0)))
    mod_p, mod_s = _ada(c_p, c_sample, w_ada[0], b_ada)

    yp, st_p, tail_p, w_ffn_bf = _path(x_prompt.reshape(bp * tp, D_MODEL), mod_p, weights, n_seq=bp,
                                       seq_len=tp, tm_in=1024, tm_res=512, tm_up=2048, tm_down=512)

    xs_t = jnp.swapaxes(x_sample, 0, 1).reshape(ts * bs, D_MODEL)
    cache_t = jnp.swapaxes(cache_ffn_conv[0], 0, 1)
    ys_t, st_s, tail_s, _ = _path(xs_t, mod_s, dict(weights, **w_ffn_bf), n_seq=bs, seq_len=ts,
                                  tm_in=ts * bs, tm_res=ts * bs, tm_up=ts * bs, tm_down=ts * bs,
                                  states=(state_mlstm_C, state_mlstm_n, state_mlstm_m, state_hgrn_S),
                                  cache=cache_t)

    ys = jnp.swapaxes(ys_t.reshape(ts, bs, D_MODEL), 0, 1)
    conv_p = tail_p[None, :, :, :D_FF]
    conv_s = jnp.swapaxes(tail_s[:, :, :D_FF], 0, 1)[None]
    return (yp.reshape(bp, tp, D_MODEL), ys,
            st_p[0], st_p[1], st_p[2], st_p[3], conv_p,
            st_s[0], st_s[1], st_s[2], st_s[3], conv_s)
```

```python
import functools

import numpy as np
import jax
import jax.numpy as jnp
from jax import lax
from jax.experimental import pallas as pl
from jax.experimental.pallas import tpu as pltpu

F32 = jnp.float32
BF16 = jnp.bfloat16

D_MODEL = 2048
N_HEADS_A, DK_A, DV_A = 4, 128, 256
N_HEADS_B, DK_B, DV_B = 8, 128, 128
A_WIDTH = N_HEADS_A * DV_A
B_WIDTH = N_HEADS_B * DV_B
N_GATE = 2 * N_HEADS_A
GATE_COL = 2 * N_HEADS_A * DK_A + A_WIDTH
D_IN = GATE_COL + N_GATE + A_WIDTH + 4 * B_WIDTH
D_FF = 5504
CONV_W = 3
EPS = 1e-5
ALPHA = 2.0 ** 0.25
LOG2_E = 1.4426950408889634
LANE = 128
FF_BLOCK = 512
D_FF_PAD = 5632
Z_MAIN = D_IN - N_GATE
OFF_QA, OFF_KA, OFF_VA, OFF_OA = 0, 512, 1024, 2048
OFF_FB, OFF_QB, OFF_VB, OFF_GB = 3072, 4096, 5120, 6144
VMEM_LIMIT = 60 * 1024 * 1024
PROMPT_CHUNK = 128
PROMPT_SEQS_PER_STEP = 2
SAMPLE_GROUP = 8


def _ln(x):
    mu = jnp.mean(x, axis=-1, keepdims=True)
    xc = x - mu
    var = jnp.mean(xc * xc, axis=-1, keepdims=True)
    return xc * lax.rsqrt(var + EPS)


def _sigmoid(x):
    return 1.0 / (1.0 + jnp.exp(-x))


def _dot(a, b):
    return jnp.dot(a.astype(BF16), b.astype(BF16), preferred_element_type=F32)


def _dot_nt(a, b):
    return lax.dot_general(a.astype(BF16), b.astype(BF16), (((1,), (1,)), ((), ())),
                           preferred_element_type=F32)


def _dot_tn(a, b):
    return lax.dot_general(a.astype(BF16), b.astype(BF16), (((0,), (0,)), ((), ())),
                           preferred_element_type=F32)


def _mod_rows(ref, seq_rows, seq):
    if seq_rows:
        return ref[...]
    return ref[pl.ds(seq, 1), :]


def _ada_kernel(cp_ref, cs_ref, w_ref, b_ref, op_ref, os_ref):
    w = w_ref[...].astype(BF16)
    for c_ref, o_ref in ((cp_ref, op_ref), (cs_ref, os_ref)):
        c = c_ref[...]
        o_ref[...] = _dot(c * _sigmoid(c), w) + b_ref[...]


def _ada(c_p, c_s, w_ada, b_ada):
    n = w_ada.shape[1]
    bn = 2048
    row = lambda r: pl.BlockSpec((r, D_MODEL), lambda j: (0, 0))
    out = lambda r: pl.BlockSpec((r, bn), lambda j: (0, j))
    return pl.pallas_call(
        _ada_kernel,
        grid=(n // bn,),
        in_specs=[row(c_p.shape[0]), row(c_s.shape[0]),
                  pl.BlockSpec((D_MODEL, bn), lambda j: (0, j)),
                  pl.BlockSpec((1, bn), lambda j: (0, j))],
        out_specs=[out(c_p.shape[0]), out(c_s.shape[0])],
        out_shape=[jax.ShapeDtypeStruct((c_p.shape[0], n), F32),
                   jax.ShapeDtypeStruct((c_s.shape[0], n), F32)],
        compiler_params=pltpu.CompilerParams(vmem_limit_bytes=VMEM_LIMIT),
        name="ada",
    )(c_p, c_s, w_ada, b_ada)


def _prep_in_kernel(front_ref, back_ref, gate_ref, o_ref, og_ref, *, first_shifted):
    j = pl.program_id(0)

    @pl.when(j == 0)
    def _():
        og_ref[...] = gate_ref[...].T.astype(BF16)

    def emit(src_ref):
        for r in range(0, src_ref.shape[0], 256):
            o_ref[:, r:r + 256] = src_ref[r:r + 256, :].T.astype(BF16)

    @pl.when(j < first_shifted)
    def _():
        emit(front_ref)

    @pl.when(j >= first_shifted)
    def _():
        emit(back_ref)


def _prep_w_in(w_in_t):
    bn = 1024
    first_shifted = GATE_COL // bn
    kern = functools.partial(_prep_in_kernel, first_shifted=first_shifted)
    return pl.pallas_call(
        kern,
        grid=(Z_MAIN // bn,),
        in_specs=[pl.BlockSpec((bn, D_MODEL), lambda j: (jnp.minimum(j, first_shifted - 1), 0)),
                  pl.BlockSpec((pl.Element(bn), pl.Element(D_MODEL)),
                               lambda j: (pl.multiple_of(jnp.maximum(j, first_shifted) * bn + N_GATE, N_GATE), 0)),
                  pl.BlockSpec((LANE, D_MODEL), lambda j: (GATE_COL // LANE, 0))],
        out_specs=[pl.BlockSpec((D_MODEL, bn), lambda j: (0, j)),
                   pl.BlockSpec((D_MODEL, LANE), lambda j: (0, 0))],
        out_shape=[jax.ShapeDtypeStruct((D_MODEL, Z_MAIN), BF16),
                   jax.ShapeDtypeStruct((D_MODEL, LANE), BF16)],
        compiler_params=pltpu.CompilerParams(vmem_limit_bytes=VMEM_LIMIT),
        name="prep_w_in",
    )(w_in_t, w_in_t, w_in_t)


def _in_kernel(x_ref, sh_ref, sc_ref, w_ref, wg_ref, z_ref, g_ref, h_scr, *, sub, seq_rows, tiles_per_seq):
    i = pl.program_id(0)
    j = pl.program_id(1)

    @pl.when(j == 0)
    def _():
        seq = i // tiles_per_seq
        sh = _mod_rows(sh_ref, seq_rows, seq)
        sc = _mod_rows(sc_ref, seq_rows, seq)
        for r in range(0, x_ref.shape[0], sub):
            rows = slice(r, r + sub)
            h = (_ln(x_ref[rows, :]) * (1.0 + sc) + sh).astype(BF16)
            h_scr[rows, :] = h
            g_ref[rows, :] = _dot(h, wg_ref[...])
            z_ref[rows, :] = _dot(h, w_ref[...]).astype(z_ref.dtype)

    @pl.when(j > 0)
    def _():
        z_ref[...] = _dot(h_scr[...], w_ref[...]).astype(z_ref.dtype)


def _in_proj(x, mod, w_main, w_gate, *, tm, sub, seq_rows, seq_len, z_dtype):
    tokens = x.shape[0]
    bn = 1792 if z_dtype == BF16 else 1024
    mod_spec = lambda k: pl.BlockSpec((mod.shape[0], D_MODEL), lambda i, j: (0, k))
    kern = functools.partial(_in_kernel, sub=min(sub, tm), seq_rows=seq_rows, tiles_per_seq=max(seq_len // tm, 1))
    return pl.pallas_call(
        kern,
        grid=(tokens // tm, Z_MAIN // bn),
        in_specs=[pl.BlockSpec((tm, D_MODEL), lambda i, j: (i, 0)),
                  mod_spec(0), mod_spec(1),
                  pl.BlockSpec((D_MODEL, bn), lambda i, j: (0, j)),
                  pl.BlockSpec((D_MODEL, LANE), lambda i, j: (0, 0))],
        out_specs=[pl.BlockSpec((tm, bn), lambda i, j: (i, j)),
                   pl.BlockSpec((tm, LANE), lambda i, j: (i, 0))],
        out_shape=[jax.ShapeDtypeStruct((tokens, Z_MAIN), z_dtype),
                   jax.ShapeDtypeStruct((tokens, LANE), F32)],
        scratch_shapes=[pltpu.VMEM((tm, D_MODEL), BF16)],
        compiler_params=pltpu.CompilerParams(
            dimension_semantics=("arbitrary", "arbitrary"), vmem_limit_bytes=VMEM_LIMIT),
        name="in_proj",
    )(x, mod, mod, w_main, w_gate)


def _mix_consts(L, Tg, t_major):
    G = L // Tg
    r = np.arange(L)
    seq, tim = (r % G, r // G) if t_major else (r // Tg, r % Tg)
    same = seq[:, None] == seq[None, :]
    t = tim[:, None]
    u = tim[None, :]
    mats = [same & (u <= t), same & (u > t)]
    masks = [same & (u <= t), same & (t <= u), same]
    levels = []
    m = Tg // 2
    while m >= 1:
        levels.append(m)
        m //= 2
    for m in levels:
        blk = same & (u // m == t // m)
        odd = (t // m) % 2 == 1
        mats.append(np.where(odd, blk & (u <= t), blk & (u > t)))
        masks.append(same & odd & ((u // m) % 2 == 0) & (t // (2 * m) == u // (2 * m)))
    mats = np.concatenate(mats, axis=0).astype(np.float32)
    masks = np.stack(masks).astype(np.float32)
    last_rows = tuple(int(np.nonzero((seq == g) & (tim == Tg - 1))[0][0]) for g in range(G))
    return mats, masks, tuple(levels), last_rows


def _mix_kernel(z_ref, g_ref, gbias_ref, lbl_ref, na_ref, nb_ref, mall_ref, msk_ref,
                c_in, n_in, m_in, s_in, mix_ref, c_out, n_out, m_out, s_out, e_scr,
                *, L, Tg, levels, last_rows):
    G = L // Tg

    def zs(off, w):
        return z_ref[:, :, off:off + w].reshape(L, w)

    def put_mix(off, w, val):
        mix_ref[:, :, off:off + w] = val.astype(mix_ref.dtype).reshape(Tg, G, w)

    _mix_body(zs, put_mix, g_ref[...].reshape(L, LANE), gbias_ref, lbl_ref, na_ref, nb_ref, mall_ref, msk_ref,
              c_in, n_in, m_in, s_in, c_out, n_out, m_out, s_out, e_scr, pl.program_id(0) * G,
              L=L, Tg=Tg, levels=levels, last_rows=last_rows, t_major=True)


def _mixchunk_kernel(z_ref, g_ref, gbias_ref, lbl_ref, na_ref, nb_ref, mall_ref, msk_ref,
                     mix_ref, c_out, n_out, m_out, s_out, e_scr, *, L, levels, last_rows):
    nb = z_ref.shape[0]
    b0 = pl.program_id(0) * nb

    @pl.when(pl.program_id(1) == 0)
    def _():
        c_out[...] = jnp.zeros(c_out.shape, F32)
        n_out[...] = jnp.zeros(n_out.shape, F32)
        s_out[...] = jnp.zeros(s_out.shape, F32)
        m_out[0, pl.ds(b0, nb), :] = jnp.zeros((nb, N_HEADS_A), F32)

    for k in range(nb):
        def zs(off, w, k=k):
            return z_ref[k, :, off:off + w]

        def put_mix(off, w, val, k=k):
            mix_ref[k, :, off:off + w] = val.astype(mix_ref.dtype)

        st = [r.at[:, pl.ds(k, 1)] for r in (c_out, n_out, s_out)]
        _mix_body(zs, put_mix, g_ref[k], gbias_ref, lbl_ref, na_ref, nb_ref, mall_ref, msk_ref,
                  st[0], st[1], m_out, st[2], st[0], st[1], m_out, st[2], e_scr.at[k], b0 + k,
                  L=L, Tg=L, levels=levels, last_rows=last_rows, t_major=False)


def _mix_body(zs, put_mix, gates, gbias_ref, lbl_ref, na_ref, nb_ref, mall_ref, msk_ref,
              c_in, n_in, m_in, s_in, c_out, n_out, m_out, s_out, e_scr, m_row0,
              *, L, Tg, levels, last_rows, t_major):
    G = L // Tg
    neg_inf = F32(-jnp.inf)
    causal = msk_ref[0] > 0.5
    causal_t = msk_ref[1] > 0.5
    same = msk_ref[2] > 0.5
    row1 = lax.broadcasted_iota(jnp.int32, (L, 1), 0)
    seq_of_row = (row1 & (G - 1)) if t_major else (row1 // Tg)
    in_group = [seq_of_row == g for g in range(G)]
    col1 = lax.broadcasted_iota(jnp.int32, (1, L), 1)
    seq_of_col = (col1 & (G - 1)) if t_major else (col1 // Tg)
    in_group_lane = [seq_of_col == g for g in range(G)]

    def state_updates(kw, v):
        if G == 1:
            return [_dot_tn(kw, v)]
        kw_t = kw.T
        return [_dot(jnp.where(in_group_lane[g], kw_t, 0.0), v) for g in range(G)]

    def by_group(vals):
        if G == 1:
            return vals[0]
        out = jnp.where(in_group[0], vals[0], 0.0)
        for g in range(1, G):
            out = out + jnp.where(in_group[g], vals[g], 0.0)
        return out

    pre = gates + gbias_ref[...]
    lsig = jnp.minimum(pre, 0.0) - jnp.log(1.0 + jnp.exp(-jnp.abs(pre)))
    pre_t = pre.T
    lsig_t = lsig.T
    m_prev = m_in[0, pl.ds(m_row0, G), :]
    scale = DK_A ** -0.5
    m_new_rows = []
    for h in range(N_HEADS_A):
        q = zs(OFF_QA + DK_A * h, DK_A)
        k = zs(OFF_KA + DK_A * h, DK_A)
        v = zs(OFF_VA + DV_A * h, DV_A)
        logi_c = pre[:, h:h + 1]
        logi_r = pre_t[h:h + 1, :]
        lf_c = lsig[:, N_HEADS_A + h:N_HEADS_A + h + 1]
        lf_r = lsig_t[N_HEADS_A + h:N_HEADS_A + h + 1, :]
        b_c = jnp.sum(jnp.where(causal, lf_r, 0.0), axis=1, keepdims=True)
        b_r = jnp.sum(jnp.where(causal_t, lf_c, 0.0), axis=0, keepdims=True)
        tots = [b_c[last_rows[g]:last_rows[g] + 1, :] for g in range(G)]
        tot_c = by_group(tots)
        tot_r = tots[0] if G == 1 else sum(jnp.where(in_group_lane[g], tots[g], 0.0) for g in range(G))
        m_col = by_group([m_prev[g:g + 1, h:h + 1] for g in range(G)])
        dmat = jnp.where(causal, b_c - b_r + logi_r, neg_inf)
        inter = b_c + m_col
        m_t = jnp.maximum(inter, jnp.max(dmat, axis=1, keepdims=True))
        w_inter = jnp.exp(inter - m_t)
        smat = _dot_nt(q, k) * (scale * jnp.exp(dmat - m_t))
        q_c = by_group([_dot(q, c_in[0, g, h]) for g in range(G)]) * scale
        n_rows = by_group([n_in[0, g, h:h + 1, :] for g in range(G)])
        q_n = jnp.sum(q.astype(F32) * n_rows, axis=1, keepdims=True) * scale
        num = w_inter * q_c + _dot(smat, v)
        den = w_inter * q_n + jnp.sum(smat, axis=1, keepdims=True)
        hh = num / jnp.maximum(jnp.abs(den), jnp.exp(-m_t))
        ms = jnp.mean(hh * hh, axis=1, keepdims=True)
        oa = zs(OFF_OA + DV_A * h, DV_A).astype(F32)
        ya = hh * lax.rsqrt(ms + EPS) * na_ref[:, DV_A * h:DV_A * (h + 1)] * _sigmoid(oa)
        put_mix(DV_A * h, DV_A, ya)
        dec_c = tot_c - b_c + logi_c
        dec_r = tot_r - b_r + logi_r
        if G == 1:
            dec_max = jnp.max(dec_r, axis=1, keepdims=True)
        else:
            dec_max = jnp.max(jnp.where(same, dec_r, neg_inf), axis=1, keepdims=True)
        m_new_c = jnp.maximum(tot_c + m_col, dec_max)
        wk = jnp.exp(dec_c - m_new_c)
        sc = jnp.exp(tot_c + m_col - m_new_c)
        kw = k.astype(F32) * wk
        upd = state_updates(kw, v)
        m_new_h = []
        for g in range(G):
            last = last_rows[g]
            kg = kw if G == 1 else jnp.where(in_group[g], kw, 0.0)
            sc_g = sc if G == 1 else sc[last:last + 1, :]
            c_new = sc_g * c_in[0, g, h] + upd[g]
            n_new = sc_g * n_in[0, g, h:h + 1, :] + jnp.sum(kg, axis=0, keepdims=True)
            c_out[0, g, h] = c_new
            n_out[0, g, h:h + 1, :] = n_new
            m_new_h.append(m_new_c if G == 1 else m_new_c[last:last + 1, :])
        m_new_rows.append(m_new_h)
    for g in range(G):
        m_out[0, pl.ds(m_row0 + g, 1), :] = jnp.concatenate(
            [m_new_rows[h][g] for h in range(N_HEADS_A)], axis=1)

    l0 = lbl_ref[0:1, :]
    l1 = lbl_ref[1:2, :]
    lmax = jnp.maximum(l0, l1)
    e0 = jnp.exp(l0 - lmax)
    e1 = jnp.exp(l1 - lmax)
    lb = e0 / (e0 + e1)
    fb = zs(OFF_FB, B_WIDTH).astype(F32)
    e = jnp.exp(-jnp.abs(fb))
    r = 1.0 / (1.0 + e)
    pos = fb >= 0.0
    sig = jnp.where(pos, r, e * r)
    nsig = jnp.where(pos, e * r, r)
    logf = jnp.log(lb + (1.0 - lb) * sig) * LOG2_E
    kb = (1.0 - lb) * nsig
    hi = logf.astype(BF16)
    r1 = logf - hi.astype(F32)
    mid = r1.astype(BF16)
    lo = (r1 - mid.astype(F32)).astype(BF16)
    mall = mall_ref[...]
    if L % 16 == 0:
        e_scr[...] = _dot(mall, jnp.concatenate([hi, mid, lo], axis=0))
    else:
        mall = mall[:, :L]
        e_scr[...] = _dot(mall, hi) + _dot(mall, mid) + _dot(mall, lo)

    dec_t = []
    for g in range(G):
        last = last_rows[g]
        bl = e_scr[last:last + 1, :]
        bl8 = jnp.concatenate([bl[:, DK_B * h:DK_B * (h + 1)] for h in range(N_HEADS_B)], axis=0)
        dec_t.append(jnp.exp2(bl8).T)

    for h in range(N_HEADS_B):
        sl = slice(DK_B * h, DK_B * (h + 1))
        q = zs(OFF_QB + DK_B * h, DK_B).astype(BF16)
        k = kb[:, sl]
        kbf = k.astype(BF16)
        v = zs(OFF_VB + DV_B * h, DV_B)
        amat = jnp.zeros((L, L), F32)
        for li in range(len(levels)):
            ex = jnp.exp2(e_scr[(2 + li) * L:(3 + li) * L, sl]).astype(BF16)
            amat = jnp.where(msk_ref[3 + li] > 0.5, _dot_nt(q * ex, kbf * ex), amat)
        qf = q.astype(F32)
        diag = jnp.sum(qf * k, axis=1, keepdims=True)
        qs = qf * jnp.exp2(e_scr[0:L, sl])
        o_inter = by_group([_dot(qs, s_in[0, g, h]) for g in range(G)])
        o = _dot(amat, v) + diag * v.astype(F32) + o_inter
        ms = jnp.mean(o * o, axis=1, keepdims=True)
        gb = zs(OFF_GB + DV_B * h, DV_B).astype(F32)
        yb = o * lax.rsqrt(ms + EPS) * nb_ref[:, sl] * (gb * _sigmoid(gb))
        put_mix(A_WIDTH + DV_B * h, DV_B, yb)
        upd = state_updates(k * jnp.exp2(e_scr[L:2 * L, sl]), v)
        for g in range(G):
            s_out[0, g, h] = dec_t[g][:, h:h + 1] * s_in[0, g, h] + upd[g]


def _mixer_args(L, Tg, t_major, const2, const3, gbias, lb_logits, norm_a, norm_b):
    mats, masks, levels, last_rows = _mix_consts(L, Tg, t_major)
    n_mats = mats.shape[0] // L
    if L % 16 == 0:
        mats = np.concatenate([mats, mats, mats], axis=1)
    specs = [pl.BlockSpec((1, LANE), const2),
             pl.BlockSpec((2, B_WIDTH), const2),
             pl.BlockSpec((1, A_WIDTH), const2),
             pl.BlockSpec((1, B_WIDTH), const2),
             pl.BlockSpec(mats.shape, const2),
             pl.BlockSpec(masks.shape, const3)]
    args = [gbias, lb_logits, norm_a, norm_b, jnp.asarray(mats, BF16), jnp.asarray(masks)]
    return specs, args, levels, last_rows, n_mats


def _state_shapes(n_seq):
    return [jax.ShapeDtypeStruct((1, n_seq, N_HEADS_A, DK_A, DV_A), F32),
            jax.ShapeDtypeStruct((1, n_seq, N_HEADS_A, DK_A), F32),
            jax.ShapeDtypeStruct((1, n_seq, N_HEADS_A), F32),
            jax.ShapeDtypeStruct((1, n_seq, N_HEADS_B, DK_B, DV_B), F32)]


def _state_specs(G, n_seq, seq_of_step):
    return [pl.BlockSpec((1, G, N_HEADS_A, DK_A, DV_A), lambda *ids: (0, seq_of_step(*ids), 0, 0, 0)),
            pl.BlockSpec((1, G, N_HEADS_A, DK_A), lambda *ids: (0, seq_of_step(*ids), 0, 0)),
            pl.BlockSpec((1, n_seq, N_HEADS_A), lambda *ids: (0, 0, 0)),
            pl.BlockSpec((1, G, N_HEADS_B, DK_B, DV_B), lambda *ids: (0, seq_of_step(*ids), 0, 0, 0))]


def _mixer(z, gates, gbias, lb_logits, norm_a, norm_b, *, n_seq, seq_len, states):
    Tg, G = seq_len, SAMPLE_GROUP
    L = Tg * G
    tile = lambda w: pl.BlockSpec((Tg, G, w), lambda i: (0, i, 0))
    const_specs, const_args, levels, last_rows, n_mats = _mixer_args(
        L, Tg, True, lambda i: (0, 0), lambda i: (0, 0, 0), gbias, lb_logits, norm_a, norm_b)
    st_specs = _state_specs(G, n_seq, lambda i: i)
    kern = functools.partial(_mix_kernel, L=L, Tg=Tg, levels=levels, last_rows=last_rows)
    return pl.pallas_call(
        kern,
        grid=(n_seq // G,),
        in_specs=[tile(Z_MAIN), tile(LANE)] + const_specs + st_specs,
        out_specs=[tile(D_MODEL)] + st_specs,
        out_shape=[jax.ShapeDtypeStruct((Tg, n_seq, D_MODEL), F32)] + _state_shapes(n_seq),
        scratch_shapes=[pltpu.VMEM((n_mats * L, B_WIDTH), F32)],
        compiler_params=pltpu.CompilerParams(dimension_semantics=("arbitrary",), vmem_limit_bytes=VMEM_LIMIT),
        name="mixer_state",
    )(z, gates, *const_args, *states)


def _mixer_chunk(z, gates, gbias, lb_logits, norm_a, norm_b, *, n_seq, seq_len):
    L = PROMPT_CHUNK
    nb = PROMPT_SEQS_PER_STEP
    nchunks = seq_len // L
    tile = lambda w: pl.BlockSpec((nb, L, w), lambda b, c: (b, c, 0))
    const_specs, const_args, levels, last_rows, n_mats = _mixer_args(
        L, L, False, lambda b, c: (0, 0), lambda b, c: (0, 0, 0), gbias, lb_logits, norm_a, norm_b)
    st_specs = _state_specs(nb, n_seq, lambda b, c: b)
    kern = functools.partial(_mixchunk_kernel, L=L, levels=levels, last_rows=last_rows)
    outs = pl.pallas_call(
        kern,
        grid=(n_seq // nb, nchunks),
        in_specs=[tile(Z_MAIN), tile(LANE)] + const_specs,
        out_specs=[tile(D_MODEL)] + st_specs,
        out_shape=[jax.ShapeDtypeStruct((n_seq, seq_len, D_MODEL), BF16)] + _state_shapes(n_seq),
        scratch_shapes=[pltpu.VMEM((nb, n_mats * L, B_WIDTH), F32)],
        compiler_params=pltpu.CompilerParams(dimension_semantics=("arbitrary", "arbitrary"),
                                             vmem_limit_bytes=VMEM_LIMIT),
        name="mixer_chunk",
    )(z.reshape(n_seq, seq_len, Z_MAIN), gates.reshape(n_seq, seq_len, LANE), *const_args)
    return [outs[0].reshape(n_seq * seq_len, D_MODEL)] + list(outs[1:])


def _res_kernel(*refs, sub, seq_rows, tiles_per_seq, emit_h2, nk):
    if emit_h2:
        a_ref, w_ref, x_ref, gm_ref, lg_ref, lb_ref, sh_ref, sc_ref, o_ref, h_ref = refs
    else:
        a_ref, w_ref, x_ref, gm_ref, lg_ref, lb_ref, o_ref = refs
    i = pl.program_id(0)
    k = pl.program_id(1)

    if nk > 1:
        @pl.when(k == 0)
        def _():
            o_ref[...] = _dot(a_ref[...], w_ref[...])

    if nk > 2:
        @pl.when((k > 0) & (k < nk - 1))
        def _():
            o_ref[...] += _dot(a_ref[...], w_ref[...])

    @pl.when(k == nk - 1)
    def _():
        seq = i // tiles_per_seq
        gm = _mod_rows(gm_ref, seq_rows, seq)
        if emit_h2:
            sh = _mod_rows(sh_ref, seq_rows, seq)
            sc = _mod_rows(sc_ref, seq_rows, seq)
        for r in range(0, o_ref.shape[0], sub):
            rows = slice(r, r + sub)
            acc = _dot(a_ref[rows, :], w_ref[...])
            if nk > 1:
                acc = acc + o_ref[rows, :]
            x1 = _ln(ALPHA * x_ref[rows, :] + gm * acc) * lg_ref[...] + lb_ref[...]
            o_ref[rows, :] = x1
            if emit_h2:
                h_ref[rows, :] = (_ln(x1) * (1.0 + sc) + sh).astype(BF16)


def _res_block(a, w, x, mod, ln_g, ln_b, *, tm, bk, sub, seq_rows, seq_len, gate_col, emit_h2):
    tokens, kdim = a.shape
    nk = kdim // bk
    mod_spec = lambda c: pl.BlockSpec((mod.shape[0], D_MODEL), lambda i, k: (0, c))
    row_spec = pl.BlockSpec((tm, D_MODEL), lambda i, k: (i, 0))
    vec_spec = pl.BlockSpec((1, D_MODEL), lambda i, k: (0, 0))
    in_specs = [pl.BlockSpec((tm, bk), lambda i, k: (i, k)),
                pl.BlockSpec((bk, D_MODEL), lambda i, k: (k, 0)),
                row_spec, mod_spec(gate_col), vec_spec, vec_spec]
    args = [a, w, x, mod, ln_g, ln_b]
    out_specs = [row_spec]
    out_shape = [jax.ShapeDtypeStruct((tokens, D_MODEL), F32)]
    if emit_h2:
        in_specs += [mod_spec(3), mod_spec(4)]
        args += [mod, mod]
        out_specs.append(row_spec)
        out_shape.append(jax.ShapeDtypeStruct((tokens, D_MODEL), BF16))
    kern = functools.partial(_res_kernel, sub=sub, seq_rows=seq_rows, tiles_per_seq=max(seq_len // tm, 1),
                             emit_h2=emit_h2, nk=nk)
    return pl.pallas_call(
        kern,
        grid=(tokens // tm, nk),
        in_specs=in_specs,
        out_specs=out_specs,
        out_shape=out_shape,
        compiler_params=pltpu.CompilerParams(
            dimension_semantics=("arbitrary", "arbitrary"), vmem_limit_bytes=VMEM_LIMIT),
        name="res_block",
    )(*args)


def _up_kernel(*refs, tm, sub, seq_len, n_seq, t_major, tiles_per_seq):
    if t_major:
        h_ref, wa_scr, wu_scr, cw_ref, cb_ref, cache_ref, g_ref, tail_ref, abuf = refs
    else:
        (h_ref, wa_ref, uu_ref, wd_ref, cw_ref, cb_ref,
         g_ref, tail_ref, wa_scr, wu_scr, wd_out, carry) = refs
    j = pl.program_id(0)
    i = pl.program_id(1)
    bn = wa_scr.shape[1]
    valid = (j * bn + lax.broadcasted_iota(jnp.int32, (1, bn), 1)) < D_FF
    hist = (CONV_W - 1) * n_seq

    @pl.when(i == 0)
    def _():
        if t_major:
            abuf[0:hist, :] = jnp.where(valid, cache_ref[...].reshape(hist, bn), 0.0)
        else:
            wu = uu_ref[...]
            wu_last = jnp.concatenate([wu[:, LANE:], jnp.zeros((D_MODEL, LANE), F32)], axis=1)
            wu = jnp.where(j == pl.num_programs(0) - 1, wu_last, wu)
            wa_scr[...] = jnp.where(valid, wa_ref[...], 0.0).astype(BF16)
            wu_scr[...] = jnp.where(valid, wu, 0.0).astype(BF16)
            wd_rows = j * bn + lax.broadcasted_iota(jnp.int32, (bn, 1), 0)
            wd_out[...] = jnp.where(wd_rows < D_FF, wd_ref[...], 0.0).astype(BF16)

    if not t_major:
        @pl.when(i % tiles_per_seq == 0)
        def _():
            carry[...] = jnp.zeros(carry.shape, F32)

    row8 = lax.broadcasted_iota(jnp.int32, (8, 1), 0)
    for r in range(0, tm, sub):
        h = h_ref[r:r + sub, :]
        a = _dot(h, wa_scr[...])
        u = _dot(h, wu_scr[...])
        if t_major:
            abuf[hist + r:hist + r + sub, :] = a
            a1 = abuf[hist + r - n_seq:hist + r - n_seq + sub, :]
            a2 = abuf[hist + r - 2 * n_seq:hist + r - 2 * n_seq + sub, :]
        else:
            p1 = carry[7:8, :]
            p2 = carry[6:7, :]
            r1 = pltpu.roll(a, 1, axis=0)
            r2 = pltpu.roll(a, 2, axis=0)
            a1 = jnp.concatenate([jnp.where(row8 == 0, p1, r1[0:8, :]), r1[8:, :]], axis=0)
            a2 = jnp.concatenate(
                [jnp.where(row8 == 0, p2, jnp.where(row8 == 1, p1, r2[0:8, :])), r2[8:, :]], axis=0)
            carry[...] = a[sub - 8:sub, :]
        conv = cb_ref[...] + cw_ref[0:1, :] * a2 + cw_ref[1:2, :] * a1 + cw_ref[2:3, :] * a
        gl = 0.5 * conv * (1.0 + lax.erf(conv * (2.0 ** -0.5)))
        g_ref[r:r + sub, :] = (gl * u).astype(BF16)

    if t_major:
        tail_ref[...] = abuf[tm:tm + hist, :].reshape(tail_ref.shape)
    else:
        @pl.when((i + 1) % tiles_per_seq == 0)
        def _():
            tail_ref[0] = carry[8 - (CONV_W - 1):8, :]


def _up_proj(h2, w_up, w_down, conv_w_p, conv_b_p, *, tm, sub, seq_len, n_seq, cache=None):
    tokens = h2.shape[0]
    t_major = cache is not None
    bn = FF_BLOCK
    nj = D_FF_PAD // bn
    assert D_FF_PAD - D_FF == LANE
    tiles_per_seq = max(seq_len // tm, 1)
    wspec = lambda f: pl.BlockSpec((D_MODEL, bn), f)
    col_spec = wspec(lambda j, i: (0, j))
    h_spec = pl.BlockSpec((tm, D_MODEL), lambda j, i: (i, 0))
    conv_specs = [pl.BlockSpec((CONV_W, bn), lambda j, i: (0, j)), pl.BlockSpec((1, bn), lambda j, i: (0, j))]
    out_specs = [pl.BlockSpec((tm, bn), lambda j, i: (i, j))]
    out_shape = [jax.ShapeDtypeStruct((tokens, D_FF_PAD), BF16)]
    if t_major:
        hist_spec = pl.BlockSpec((CONV_W - 1, n_seq, bn), lambda j, i: (0, 0, j))
        in_specs = [h_spec, col_spec, col_spec] + conv_specs + [hist_spec]
        args = [h2, w_up[0], w_up[1], conv_w_p, conv_b_p, cache]
        out_specs.append(hist_spec)
        out_shape.append(jax.ShapeDtypeStruct((CONV_W - 1, n_seq, D_FF_PAD), F32))
        scratch_rows = (CONV_W - 1) * n_seq + tm
    else:
        row_spec = pl.BlockSpec((bn, D_MODEL), lambda j, i: (j, 0))
        u_off = lambda j, i: (0, pl.multiple_of(jnp.minimum(D_FF + bn * j, 2 * D_FF - bn), LANE))
        in_specs = [h_spec, col_spec,
                    pl.BlockSpec((pl.Element(D_MODEL), pl.Element(bn)), u_off),
                    row_spec] + conv_specs
        args = [h2, w_up, w_up, w_down, conv_w_p, conv_b_p]
        out_specs += [pl.BlockSpec((1, CONV_W - 1, bn), lambda j, i: (i // tiles_per_seq, 0, j)),
                      col_spec, col_spec, row_spec]
        out_shape += [jax.ShapeDtypeStruct((n_seq, CONV_W - 1, D_FF_PAD), F32),
                      jax.ShapeDtypeStruct((D_MODEL, D_FF_PAD), BF16),
                      jax.ShapeDtypeStruct((D_MODEL, D_FF_PAD), BF16),
                      jax.ShapeDtypeStruct((D_FF_PAD, D_MODEL), BF16)]
        scratch_rows = 8
    kern = functools.partial(_up_kernel, tm=tm, sub=sub, seq_len=seq_len, n_seq=n_seq, t_major=t_major,
                             tiles_per_seq=tiles_per_seq)
    outs = pl.pallas_call(
        kern,
        grid=(nj, tokens // tm),
        in_specs=in_specs,
        out_specs=out_specs,
        out_shape=out_shape,
        scratch_shapes=[pltpu.VMEM((scratch_rows, bn), F32)],
        compiler_params=pltpu.CompilerParams(
            dimension_semantics=("arbitrary", "arbitrary"), vmem_limit_bytes=VMEM_LIMIT),
        name="up_proj",
    )(*args)
    if t_major:
        return outs[0], outs[1], w_up, w_down
    return outs[0], outs[1], (outs[2], outs[3]), outs[4]


def _path(x, mod, w, *, n_seq, seq_len, tm_in, tm_res, tm_up, tm_down, states=None, cache=None):
    sample = states is not None
    sub = n_seq if sample else 256
    sub_up = n_seq if sample else 512
    mixer_w = (w["gbias"], w["lb_logits"], w["norm_a"], w["norm_b"])
    z, gates = _in_proj(x, mod, w["w_main"], w["w_gate"], tm=tm_in, sub=sub, seq_rows=sample,
                        seq_len=seq_len, z_dtype=F32 if sample else BF16)
    if sample:
        mix, c1, n1, m1, s1 = _mixer(z.reshape(seq_len, n_seq, Z_MAIN), gates.reshape(seq_len, n_seq, LANE),
                                     *mixer_w, n_seq=n_seq, seq_len=seq_len, states=states)
        mix = mix.reshape(n_seq * seq_len, D_MODEL)
    else:
        mix, c1, n1, m1, s1 = _mixer_chunk(z, gates, *mixer_w, n_seq=n_seq, seq_len=seq_len)
    x1, h2 = _res_block(mix, w["w_out"], x, mod, w["ln1_g"], w["ln1_b"], tm=tm_res, bk=D_MODEL, sub=sub,
                        seq_rows=sample, seq_len=seq_len, gate_col=2, emit_h2=True)
    g, conv_out, w_up_bf, w_down_bf = _up_proj(h2, w["w_up"], w["w_down"], w["conv_w"], w["conv_b"], tm=tm_up,
                                               sub=sub_up, seq_len=seq_len, n_seq=n_seq, cache=cache)
    (x2,) = _res_block(g, w_down_bf, x1, mod, w["ln2_g"], w["ln2_b"], tm=tm_down, bk=D_FF_PAD // 2, sub=sub,
                       seq_rows=sample, seq_len=seq_len, gate_col=5, emit_h2=False)
    return x2, (c1, n1, m1, s1), conv_out, dict(w_up=w_up_bf, w_down=w_down_bf)


def kernel(x_prompt, x_sample, state_mlstm_C, state_mlstm_n, state_mlstm_m, state_hgrn_S, cache_ffn_conv,
           c_prompt, c_sample, hgrn_lb_logits, w_ada, b_ada, w_in, b_gate_a, norm_a, norm_b, w_out,
           ln1_g, ln1_b, w_up, conv_w, conv_b, w_down, ln2_g, ln2_b):
    bp, tp, _ = x_prompt.shape
    bs, ts, _ = x_sample.shape
    pad_ff = D_FF_PAD - D_FF
    w_main, w_gate = _prep_w_in(jnp.transpose(w_in[0]))
    weights = dict(
        w_main=w_main, w_gate=w_gate, w_out=w_out[0].astype(BF16), w_up=w_up[0],
        w_down=w_down[0],
        gbias=jnp.pad(b_gate_a[0].reshape(1, N_GATE), ((0, 0), (0, LANE - N_GATE))),
        lb_logits=hgrn_lb_logits, norm_a=norm_a, norm_b=norm_b,
        ln1_g=ln1_g, ln1_b=ln1_b, ln2_g=ln2_g, ln2_b=ln2_b,
        conv_w=jnp.pad(conv_w[0], ((0, 0), (0, pad_ff))),
        conv_b=jnp.pad(conv_b, ((0, 0), (0, pad_ff))))

    c_p = jnp.pad(c_prompt, ((0, 8 - bp), (0, 0)))
    mod_p, mod_s = _ada(c_p, c_sample, w_ada[0], b_ada)

    yp, st_p, tail_p, w_ffn_bf = _path(x_prompt.reshape(bp * tp, D_MODEL), mod_p, weights, n_seq=bp,
                                       seq_len=tp, tm_in=1024, tm_res=512, tm_up=2048, tm_down=512)

    xs_t = jnp.swapaxes(x_sample, 0, 1).reshape(ts * bs, D_MODEL)
    cache_t = jnp.swapaxes(cache_ffn_conv[0], 0, 1)
    ys_t, st_s, tail_s, _ = _path(xs_t, mod_s, dict(weights, **w_ffn_bf), n_seq=bs, seq_len=ts,
                                  tm_in=ts * bs, tm_res=ts * bs, tm_up=ts * bs, tm_down=ts * bs,
                                  states=(state_mlstm_C, state_mlstm_n, state_mlstm_m, state_hgrn_S),
                                  cache=cache_t)

    ys = jnp.swapaxes(ys_t.reshape(ts, bs, D_MODEL), 0, 1)
    conv_p = tail_p[None, :, :, :D_FF]
    conv_s = jnp.swapaxes(tail_s[:, :, :D_FF], 0, 1)[None]
    return (yp.reshape(bp, tp, D_MODEL), ys,
            st_p[0], st_p[1], st_p[2], st_p[3], conv_p,
            st_s[0], st_s[1], st_s[2], st_s[3], conv_s)
```

```python
import functools

import numpy as np
import jax
import jax.numpy as jnp
from jax import lax
from jax.experimental import pallas as pl
from jax.experimental.pallas import tpu as pltpu

F32 = jnp.float32
BF16 = jnp.bfloat16

D_MODEL = 2048
N_HEADS_A, DK_A, DV_A = 4, 128, 256
N_HEADS_B, DK_B, DV_B = 8, 128, 128
A_WIDTH = N_HEADS_A * DV_A
B_WIDTH = N_HEADS_B * DV_B
N_GATE = 2 * N_HEADS_A
GATE_COL = 2 * N_HEADS_A * DK_A + A_WIDTH
D_IN = GATE_COL + N_GATE + A_WIDTH + 4 * B_WIDTH
D_FF = 5504
CONV_W = 3
EPS = 1e-5
ALPHA = 2.0 ** 0.25
LOG2_E = 1.4426950408889634
LANE = 128
FF_BLOCK = 512
D_FF_PAD = 5632
Z_MAIN = D_IN - N_GATE
OFF_QA, OFF_KA, OFF_VA, OFF_OA = 0, 512, 1024, 2048
OFF_FB, OFF_QB, OFF_VB, OFF_GB = 3072, 4096, 5120, 6144
VMEM_LIMIT = 60 * 1024 * 1024
PROMPT_CHUNK = 128
PROMPT_SEQS_PER_STEP = 2
SAMPLE_GROUP = 8


def _ln(x):
    mu = jnp.mean(x, axis=-1, keepdims=True)
    xc = x - mu
    var = jnp.mean(xc * xc, axis=-1, keepdims=True)
    return xc * lax.rsqrt(var + EPS)


def _sigmoid(x):
    return 1.0 / (1.0 + jnp.exp(-x))


def _dot(a, b):
    return jnp.dot(a.astype(BF16), b.astype(BF16), preferred_element_type=F32)


def _dot_nt(a, b):
    return lax.dot_general(a.astype(BF16), b.astype(BF16), (((1,), (1,)), ((), ())),
                           preferred_element_type=F32)


def _dot_tn(a, b):
    return lax.dot_general(a.astype(BF16), b.astype(BF16), (((0,), (0,)), ((), ())),
                           preferred_element_type=F32)


def _mod_rows(ref, seq_rows, seq):
    if seq_rows:
        return ref[...]
    return ref[pl.ds(seq, 1), :]


def _ada_kernel(cp_ref, cs_ref, w_ref, b_ref, op_ref, os_ref):
    w = w_ref[...].astype(BF16)
    for c_ref, o_ref in ((cp_ref, op_ref), (cs_ref, os_ref)):
        c = c_ref[...]
        o_ref[...] = _dot(c * _sigmoid(c), w) + b_ref[...]


def _ada(c_p, c_s, w_ada, b_ada):
    n = w_ada.shape[1]
    bn = 1024
    row = lambda r: pl.BlockSpec((r, D_MODEL), lambda j: (0, 0))
    out = lambda r: pl.BlockSpec((r, bn), lambda j: (0, j))
    return pl.pallas_call(
        _ada_kernel,
        grid=(n // bn,),
        in_specs=[row(c_p.shape[0]), row(c_s.shape[0]),
                  pl.BlockSpec((D_MODEL, bn), lambda j: (0, j)),
                  pl.BlockSpec((1, bn), lambda j: (0, j))],
        out_specs=[out(c_p.shape[0]), out(c_s.shape[0])],
        out_shape=[jax.ShapeDtypeStruct((c_p.shape[0], n), F32),
                   jax.ShapeDtypeStruct((c_s.shape[0], n), F32)],
        compiler_params=pltpu.CompilerParams(vmem_limit_bytes=VMEM_LIMIT),
        name="ada",
    )(c_p, c_s, w_ada, b_ada)


def _prep_in_kernel(front_ref, back_ref, gate_ref, o_ref, og_ref, *, first_shifted):
    j = pl.program_id(0)

    @pl.when(j == 0)
    def _():
        og_ref[...] = gate_ref[...].T.astype(BF16)

    def emit(src_ref):
        for r in range(0, src_ref.shape[0], 256):
            o_ref[:, r:r + 256] = src_ref[r:r + 256, :].T.astype(BF16)

    @pl.when(j < first_shifted)
    def _():
        emit(front_ref)

    @pl.when(j >= first_shifted)
    def _():
        emit(back_ref)


def _prep_w_in(w_in_t):
    bn = 1024
    first_shifted = GATE_COL // bn
    kern = functools.partial(_prep_in_kernel, first_shifted=first_shifted)
    return pl.pallas_call(
        kern,
        grid=(Z_MAIN // bn,),
        in_specs=[pl.BlockSpec((bn, D_MODEL), lambda j: (jnp.minimum(j, first_shifted - 1), 0)),
                  pl.BlockSpec((pl.Element(bn), pl.Element(D_MODEL)),
                               lambda j: (pl.multiple_of(jnp.maximum(j, first_shifted) * bn + N_GATE, N_GATE), 0)),
                  pl.BlockSpec((LANE, D_MODEL), lambda j: (GATE_COL // LANE, 0))],
        out_specs=[pl.BlockSpec((D_MODEL, bn), lambda j: (0, j)),
                   pl.BlockSpec((D_MODEL, LANE), lambda j: (0, 0))],
        out_shape=[jax.ShapeDtypeStruct((D_MODEL, Z_MAIN), BF16),
                   jax.ShapeDtypeStruct((D_MODEL, LANE), BF16)],
        compiler_params=pltpu.CompilerParams(vmem_limit_bytes=VMEM_LIMIT),
        name="prep_w_in",
    )(w_in_t, w_in_t, w_in_t)


def _in_kernel(x_ref, sh_ref, sc_ref, w_ref, wg_ref, z_ref, g_ref, h_scr, *, sub, seq_rows, tiles_per_seq):
    i = pl.program_id(0)
    j = pl.program_id(1)

    @pl.when(j == 0)
    def _():
        seq = i // tiles_per_seq
        sh = _mod_rows(sh_ref, seq_rows, seq)
        sc = _mod_rows(sc_ref, seq_rows, seq)
        for r in range(0, x_ref.shape[0], sub):
            rows = slice(r, r + sub)
            h = (_ln(x_ref[rows, :]) * (1.0 + sc) + sh).astype(BF16)
            h_scr[rows, :] = h
            g_ref[rows, :] = _dot(h, wg_ref[...])
            z_ref[rows, :] = _dot(h, w_ref[...]).astype(z_ref.dtype)

    @pl.when(j > 0)
    def _():
        z_ref[...] = _dot(h_scr[...], w_ref[...]).astype(z_ref.dtype)


def _in_proj(x, mod, w_main, w_gate, *, tm, sub, seq_rows, seq_len, z_dtype):
    tokens = x.shape[0]
    bn = 1792
    mod_spec = lambda k: pl.BlockSpec((mod.shape[0], D_MODEL), lambda i, j: (0, k))
    kern = functools.partial(_in_kernel, sub=min(sub, tm), seq_rows=seq_rows, tiles_per_seq=max(seq_len // tm, 1))
    return pl.pallas_call(
        kern,
        grid=(tokens // tm, Z_MAIN // bn),
        in_specs=[pl.BlockSpec((tm, D_MODEL), lambda i, j: (i, 0)),
                  mod_spec(0), mod_spec(1),
                  pl.BlockSpec((D_MODEL, bn), lambda i, j: (0, j)),
                  pl.BlockSpec((D_MODEL, LANE), lambda i, j: (0, 0))],
        out_specs=[pl.BlockSpec((tm, bn), lambda i, j: (i, j)),
                   pl.BlockSpec((tm, LANE), lambda i, j: (i, 0))],
        out_shape=[jax.ShapeDtypeStruct((tokens, Z_MAIN), z_dtype),
                   jax.ShapeDtypeStruct((tokens, LANE), F32)],
        scratch_shapes=[pltpu.VMEM((tm, D_MODEL), BF16)],
        compiler_params=pltpu.CompilerParams(
            dimension_semantics=("arbitrary", "arbitrary"), vmem_limit_bytes=VMEM_LIMIT),
        name="in_proj",
    )(x, mod, mod, w_main, w_gate)


def _mix_consts(L, Tg, t_major):
    G = L // Tg
    r = np.arange(L)
    seq, tim = (r % G, r // G) if t_major else (r // Tg, r % Tg)
    same = seq[:, None] == seq[None, :]
    t = tim[:, None]
    u = tim[None, :]
    mats = [same & (u <= t), same & (u > t)]
    masks = [same & (u <= t), same & (t <= u), same]
    levels = []
    m = Tg // 2
    while m >= 1:
        levels.append(m)
        m //= 2
    for m in levels:
        blk = same & (u // m == t // m)
        odd = (t // m) % 2 == 1
        mats.append(np.where(odd, blk & (u <= t), blk & (u > t)))
        masks.append(same & odd & ((u // m) % 2 == 0) & (t // (2 * m) == u // (2 * m)))
    mats = np.concatenate(mats, axis=0).astype(np.float32)
    masks = np.stack(masks).astype(np.float32)
    last_rows = tuple(int(np.nonzero((seq == g) & (tim == Tg - 1))[0][0]) for g in range(G))
    return mats, masks, tuple(levels), last_rows


def _mix_kernel(z_ref, g_ref, gbias_ref, lbl_ref, na_ref, nb_ref, mall_ref, msk_ref,
                c_in, n_in, m_in, s_in, mix_ref, c_out, n_out, m_out, s_out, e_scr,
                *, L, Tg, levels, last_rows):
    G = L // Tg

    def zs(off, w):
        return z_ref[:, :, off:off + w].reshape(L, w)

    def put_mix(off, w, val):
        mix_ref[:, :, off:off + w] = val.astype(mix_ref.dtype).reshape(Tg, G, w)

    _mix_body(zs, put_mix, g_ref[...].reshape(L, LANE), gbias_ref, lbl_ref, na_ref, nb_ref, mall_ref, msk_ref,
              c_in, n_in, m_in, s_in, c_out, n_out, m_out, s_out, e_scr, pl.program_id(0) * G,
              L=L, Tg=Tg, levels=levels, last_rows=last_rows, t_major=True)


def _mixchunk_kernel(z_ref, g_ref, gbias_ref, lbl_ref, na_ref, nb_ref, mall_ref, msk_ref,
                     mix_ref, c_out, n_out, m_out, s_out, e_scr, *, L, levels, last_rows):
    nb = z_ref.shape[0]
    b0 = pl.program_id(0) * nb

    @pl.when(pl.program_id(1) == 0)
    def _():
        c_out[...] = jnp.zeros(c_out.shape, F32)
        n_out[...] = jnp.zeros(n_out.shape, F32)
        s_out[...] = jnp.zeros(s_out.shape, F32)
        m_out[0, pl.ds(b0, nb), :] = jnp.zeros((nb, N_HEADS_A), F32)

    for k in range(nb):
        def zs(off, w, k=k):
            return z_ref[k, :, off:off + w]

        def put_mix(off, w, val, k=k):
            mix_ref[k, :, off:off + w] = val.astype(mix_ref.dtype)

        st = [r.at[:, pl.ds(k, 1)] for r in (c_out, n_out, s_out)]
        _mix_body(zs, put_mix, g_ref[k], gbias_ref, lbl_ref, na_ref, nb_ref, mall_ref, msk_ref,
                  st[0], st[1], m_out, st[2], st[0], st[1], m_out, st[2], e_scr.at[k], b0 + k,
                  L=L, Tg=L, levels=levels, last_rows=last_rows, t_major=False)


def _mix_body(zs, put_mix, gates, gbias_ref, lbl_ref, na_ref, nb_ref, mall_ref, msk_ref,
              c_in, n_in, m_in, s_in, c_out, n_out, m_out, s_out, e_scr, m_row0,
              *, L, Tg, levels, last_rows, t_major):
    G = L // Tg
    neg_inf = F32(-jnp.inf)
    causal = msk_ref[0] > 0.5
    causal_t = msk_ref[1] > 0.5
    same = msk_ref[2] > 0.5
    row1 = lax.broadcasted_iota(jnp.int32, (L, 1), 0)
    seq_of_row = (row1 & (G - 1)) if t_major else (row1 // Tg)
    in_group = [seq_of_row == g for g in range(G)]
    col1 = lax.broadcasted_iota(jnp.int32, (1, L), 1)
    seq_of_col = (col1 & (G - 1)) if t_major else (col1 // Tg)
    in_group_lane = [seq_of_col == g for g in range(G)]

    def state_updates(kw, v):
        if G == 1:
            return [_dot_tn(kw, v)]
        kw_t = kw.T
        return [_dot(jnp.where(in_group_lane[g], kw_t, 0.0), v) for g in range(G)]

    def by_group(vals):
        if G == 1:
            return vals[0]
        out = jnp.where(in_group[0], vals[0], 0.0)
        for g in range(1, G):
            out = out + jnp.where(in_group[g], vals[g], 0.0)
        return out

    pre = gates + gbias_ref[...]
    lsig = jnp.minimum(pre, 0.0) - jnp.log(1.0 + jnp.exp(-jnp.abs(pre)))
    pre_t = pre.T
    lsig_t = lsig.T
    m_prev = m_in[0, pl.ds(m_row0, G), :]
    scale = DK_A ** -0.5
    m_new_rows = []
    for h in range(N_HEADS_A):
        q = zs(OFF_QA + DK_A * h, DK_A)
        k = zs(OFF_KA + DK_A * h, DK_A)
        v = zs(OFF_VA + DV_A * h, DV_A)
        logi_c = pre[:, h:h + 1]
        logi_r = pre_t[h:h + 1, :]
        lf_c = lsig[:, N_HEADS_A + h:N_HEADS_A + h + 1]
        lf_r = lsig_t[N_HEADS_A + h:N_HEADS_A + h + 1, :]
        b_c = jnp.sum(jnp.where(causal, lf_r, 0.0), axis=1, keepdims=True)
        b_r = jnp.sum(jnp.where(causal_t, lf_c, 0.0), axis=0, keepdims=True)
        tots = [b_c[last_rows[g]:last_rows[g] + 1, :] for g in range(G)]
        tot_c = by_group(tots)
        tot_r = tots[0] if G == 1 else sum(jnp.where(in_group_lane[g], tots[g], 0.0) for g in range(G))
        m_col = by_group([m_prev[g:g + 1, h:h + 1] for g in range(G)])
        dmat = jnp.where(causal, b_c - b_r + logi_r, neg_inf)
        inter = b_c + m_col
        m_t = jnp.maximum(inter, jnp.max(dmat, axis=1, keepdims=True))
        w_inter = jnp.exp(inter - m_t)
        smat = _dot_nt(q, k) * (scale * jnp.exp(dmat - m_t))
        q_c = by_group([_dot(q, c_in[0, g, h]) for g in range(G)]) * scale
        n_rows = by_group([n_in[0, g, h:h + 1, :] for g in range(G)])
        q_n = jnp.sum(q.astype(F32) * n_rows, axis=1, keepdims=True) * scale
        num = w_inter * q_c + _dot(smat, v)
        den = w_inter * q_n + jnp.sum(smat, axis=1, keepdims=True)
        hh = num / jnp.maximum(jnp.abs(den), jnp.exp(-m_t))
        ms = jnp.mean(hh * hh, axis=1, keepdims=True)
        oa = zs(OFF_OA + DV_A * h, DV_A).astype(F32)
        ya = hh * lax.rsqrt(ms + EPS) * na_ref[:, DV_A * h:DV_A * (h + 1)] * _sigmoid(oa)
        put_mix(DV_A * h, DV_A, ya)
        dec_c = tot_c - b_c + logi_c
        dec_r = tot_r - b_r + logi_r
        if G == 1:
            dec_max = jnp.max(dec_r, axis=1, keepdims=True)
        else:
            dec_max = jnp.max(jnp.where(same, dec_r, neg_inf), axis=1, keepdims=True)
        m_new_c = jnp.maximum(tot_c + m_col, dec_max)
        wk = jnp.exp(dec_c - m_new_c)
        sc = jnp.exp(tot_c + m_col - m_new_c)
        kw = k.astype(F32) * wk
        upd = state_updates(kw, v)
        m_new_h = []
        for g in range(G):
            last = last_rows[g]
            kg = kw if G == 1 else jnp.where(in_group[g], kw, 0.0)
            sc_g = sc if G == 1 else sc[last:last + 1, :]
            c_new = sc_g * c_in[0, g, h] + upd[g]
            n_new = sc_g * n_in[0, g, h:h + 1, :] + jnp.sum(kg, axis=0, keepdims=True)
            c_out[0, g, h] = c_new
            n_out[0, g, h:h + 1, :] = n_new
            m_new_h.append(m_new_c if G == 1 else m_new_c[last:last + 1, :])
        m_new_rows.append(m_new_h)
    for g in range(G):
        m_out[0, pl.ds(m_row0 + g, 1), :] = jnp.concatenate(
            [m_new_rows[h][g] for h in range(N_HEADS_A)], axis=1)

    l0 = lbl_ref[0:1, :]
    l1 = lbl_ref[1:2, :]
    lmax = jnp.maximum(l0, l1)
    e0 = jnp.exp(l0 - lmax)
    e1 = jnp.exp(l1 - lmax)
    lb = e0 / (e0 + e1)
    fb = zs(OFF_FB, B_WIDTH).astype(F32)
    e = jnp.exp(-jnp.abs(fb))
    r = 1.0 / (1.0 + e)
    pos = fb >= 0.0
    sig = jnp.where(pos, r, e * r)
    nsig = jnp.where(pos, e * r, r)
    logf = jnp.log(lb + (1.0 - lb) * sig) * LOG2_E
    kb = (1.0 - lb) * nsig
    hi = logf.astype(BF16)
    r1 = logf - hi.astype(F32)
    mid = r1.astype(BF16)
    lo = (r1 - mid.astype(F32)).astype(BF16)
    mall = mall_ref[...]
    if L % 16 == 0:
        e_scr[...] = _dot(mall, jnp.concatenate([hi, mid, lo], axis=0))
    else:
        mall = mall[:, :L]
        e_scr[...] = _dot(mall, hi) + _dot(mall, mid) + _dot(mall, lo)

    dec_t = []
    for g in range(G):
        last = last_rows[g]
        bl = e_scr[last:last + 1, :]
        bl8 = jnp.concatenate([bl[:, DK_B * h:DK_B * (h + 1)] for h in range(N_HEADS_B)], axis=0)
        dec_t.append(jnp.exp2(bl8).T)

    for h in range(N_HEADS_B):
        sl = slice(DK_B * h, DK_B * (h + 1))
        q = zs(OFF_QB + DK_B * h, DK_B).astype(BF16)
        k = kb[:, sl]
        kbf = k.astype(BF16)
        v = zs(OFF_VB + DV_B * h, DV_B)
        amat = jnp.zeros((L, L), F32)
        for li in range(len(levels)):
            ex = jnp.exp2(e_scr[(2 + li) * L:(3 + li) * L, sl]).astype(BF16)
            amat = jnp.where(msk_ref[3 + li] > 0.5, _dot_nt(q * ex, kbf * ex), amat)
        qf = q.astype(F32)
        diag = jnp.sum(qf * k, axis=1, keepdims=True)
        qs = qf * jnp.exp2(e_scr[0:L, sl])
        o_inter = by_group([_dot(qs, s_in[0, g, h]) for g in range(G)])
        o = _dot(amat, v) + diag * v.astype(F32) + o_inter
        ms = jnp.mean(o * o, axis=1, keepdims=True)
        gb = zs(OFF_GB + DV_B * h, DV_B).astype(F32)
        yb = o * lax.rsqrt(ms + EPS) * nb_ref[:, sl] * (gb * _sigmoid(gb))
        put_mix(A_WIDTH + DV_B * h, DV_B, yb)
        upd = state_updates(k * jnp.exp2(e_scr[L:2 * L, sl]), v)
        for g in range(G):
            s_out[0, g, h] = dec_t[g][:, h:h + 1] * s_in[0, g, h] + upd[g]


def _mixer_args(L, Tg, t_major, const2, const3, gbias, lb_logits, norm_a, norm_b):
    mats, masks, levels, last_rows = _mix_consts(L, Tg, t_major)
    n_mats = mats.shape[0] // L
    if L % 16 == 0:
        mats = np.concatenate([mats, mats, mats], axis=1)
    specs = [pl.BlockSpec((1, LANE), const2),
             pl.BlockSpec((2, B_WIDTH), const2),
             pl.BlockSpec((1, A_WIDTH), const2),
             pl.BlockSpec((1, B_WIDTH), const2),
             pl.BlockSpec(mats.shape, const2),
             pl.BlockSpec(masks.shape, const3)]
    args = [gbias, lb_logits, norm_a, norm_b, jnp.asarray(mats, BF16), jnp.asarray(masks)]
    return specs, args, levels, last_rows, n_mats


def _state_shapes(n_seq):
    return [jax.ShapeDtypeStruct((1, n_seq, N_HEADS_A, DK_A, DV_A), F32),
            jax.ShapeDtypeStruct((1, n_seq, N_HEADS_A, DK_A), F32),
            jax.ShapeDtypeStruct((1, n_seq, N_HEADS_A), F32),
            jax.ShapeDtypeStruct((1, n_seq, N_HEADS_B, DK_B, DV_B), F32)]


def _state_specs(G, n_seq, seq_of_step):
    return [pl.BlockSpec((1, G, N_HEADS_A, DK_A, DV_A), lambda *ids: (0, seq_of_step(*ids), 0, 0, 0)),
            pl.BlockSpec((1, G, N_HEADS_A, DK_A), lambda *ids: (0, seq_of_step(*ids), 0, 0)),
            pl.BlockSpec((1, n_seq, N_HEADS_A), lambda *ids: (0, 0, 0)),
            pl.BlockSpec((1, G, N_HEADS_B, DK_B, DV_B), lambda *ids: (0, seq_of_step(*ids), 0, 0, 0))]


def _mixer(z, gates, gbias, lb_logits, norm_a, norm_b, *, n_seq, seq_len, states):
    Tg, G = seq_len, SAMPLE_GROUP
    L = Tg * G
    tile = lambda w: pl.BlockSpec((Tg, G, w), lambda i: (0, i, 0))
    const_specs, const_args, levels, last_rows, n_mats = _mixer_args(
        L, Tg, True, lambda i: (0, 0), lambda i: (0, 0, 0), gbias, lb_logits, norm_a, norm_b)
    st_specs = _state_specs(G, n_seq, lambda i: i)
    kern = functools.partial(_mix_kernel, L=L, Tg=Tg, levels=levels, last_rows=last_rows)
    return pl.pallas_call(
        kern,
        grid=(n_seq // G,),
        in_specs=[tile(Z_MAIN), tile(LANE)] + const_specs + st_specs,
        out_specs=[tile(D_MODEL)] + st_specs,
        out_shape=[jax.ShapeDtypeStruct((Tg, n_seq, D_MODEL), F32)] + _state_shapes(n_seq),
        scratch_shapes=[pltpu.VMEM((n_mats * L, B_WIDTH), F32)],
        compiler_params=pltpu.CompilerParams(dimension_semantics=("arbitrary",), vmem_limit_bytes=VMEM_LIMIT),
        name="mixer_state",
    )(z, gates, *const_args, *states)


def _mixer_chunk(z, gates, gbias, lb_logits, norm_a, norm_b, *, n_seq, seq_len):
    L = PROMPT_CHUNK
    nb = PROMPT_SEQS_PER_STEP
    nchunks = seq_len // L
    tile = lambda w: pl.BlockSpec((nb, L, w), lambda b, c: (b, c, 0))
    const_specs, const_args, levels, last_rows, n_mats = _mixer_args(
        L, L, False, lambda b, c: (0, 0), lambda b, c: (0, 0, 0), gbias, lb_logits, norm_a, norm_b)
    st_specs = _state_specs(nb, n_seq, lambda b, c: b)
    kern = functools.partial(_mixchunk_kernel, L=L, levels=levels, last_rows=last_rows)
    outs = pl.pallas_call(
        kern,
        grid=(n_seq // nb, nchunks),
        in_specs=[tile(Z_MAIN), tile(LANE)] + const_specs,
        out_specs=[tile(D_MODEL)] + st_specs,
        out_shape=[jax.ShapeDtypeStruct((n_seq, seq_len, D_MODEL), BF16)] + _state_shapes(n_seq),
        scratch_shapes=[pltpu.VMEM((nb, n_mats * L, B_WIDTH), F32)],
        compiler_params=pltpu.CompilerParams(dimension_semantics=("arbitrary", "arbitrary"),
                                             vmem_limit_bytes=VMEM_LIMIT),
        name="mixer_chunk",
    )(z.reshape(n_seq, seq_len, Z_MAIN), gates.reshape(n_seq, seq_len, LANE), *const_args)
    return [outs[0].reshape(n_seq * seq_len, D_MODEL)] + list(outs[1:])


def _res_kernel(*refs, sub, seq_rows, tiles_per_seq, emit_h2, nk):
    if emit_h2:
        a_ref, w_ref, x_ref, gm_ref, lg_ref, lb_ref, sh_ref, sc_ref, o_ref, h_ref = refs
    else:
        a_ref, w_ref, x_ref, gm_ref, lg_ref, lb_ref, o_ref = refs
    i = pl.program_id(0)
    k = pl.program_id(1)

    if nk > 1:
        @pl.when(k == 0)
        def _():
            o_ref[...] = _dot(a_ref[...], w_ref[...])

    if nk > 2:
        @pl.when((k > 0) & (k < nk - 1))
        def _():
            o_ref[...] += _dot(a_ref[...], w_ref[...])

    @pl.when(k == nk - 1)
    def _():
        seq = i // tiles_per_seq
        gm = _mod_rows(gm_ref, seq_rows, seq)
        if emit_h2:
            sh = _mod_rows(sh_ref, seq_rows, seq)
            sc = _mod_rows(sc_ref, seq_rows, seq)
        for r in range(0, o_ref.shape[0], sub):
            rows = slice(r, r + sub)
            acc = _dot(a_ref[rows, :], w_ref[...])
            if nk > 1:
                acc = acc + o_ref[rows, :]
            x1 = _ln(ALPHA * x_ref[rows, :] + gm * acc) * lg_ref[...] + lb_ref[...]
            o_ref[rows, :] = x1
            if emit_h2:
                h_ref[rows, :] = (_ln(x1) * (1.0 + sc) + sh).astype(BF16)


def _res_block(a, w, x, mod, ln_g, ln_b, *, tm, bk, sub, seq_rows, seq_len, gate_col, emit_h2):
    tokens, kdim = a.shape
    nk = kdim // bk
    mod_spec = lambda c: pl.BlockSpec((mod.shape[0], D_MODEL), lambda i, k: (0, c))
    row_spec = pl.BlockSpec((tm, D_MODEL), lambda i, k: (i, 0))
    vec_spec = pl.BlockSpec((1, D_MODEL), lambda i, k: (0, 0))
    in_specs = [pl.BlockSpec((tm, bk), lambda i, k: (i, k)),
                pl.BlockSpec((bk, D_MODEL), lambda i, k: (k, 0)),
                row_spec, mod_spec(gate_col), vec_spec, vec_spec]
    args = [a, w, x, mod, ln_g, ln_b]
    out_specs = [row_spec]
    out_shape = [jax.ShapeDtypeStruct((tokens, D_MODEL), F32)]
    if emit_h2:
        in_specs += [mod_spec(3), mod_spec(4)]
        args += [mod, mod]
        out_specs.append(row_spec)
        out_shape.append(jax.ShapeDtypeStruct((tokens, D_MODEL), BF16))
    kern = functools.partial(_res_kernel, sub=sub, seq_rows=seq_rows, tiles_per_seq=max(seq_len // tm, 1),
                             emit_h2=emit_h2, nk=nk)
    return pl.pallas_call(
        kern,
        grid=(tokens // tm, nk),
        in_specs=in_specs,
        out_specs=out_specs,
        out_shape=out_shape,
        compiler_params=pltpu.CompilerParams(
            dimension_semantics=("arbitrary", "arbitrary"), vmem_limit_bytes=VMEM_LIMIT),
        name="res_block",
    )(*args)


def _up_kernel(*refs, tm, sub, seq_len, n_seq, t_major, tiles_per_seq):
    if t_major:
        h_ref, wa_scr, wu_scr, cw_ref, cb_ref, cache_ref, g_ref, tail_ref, abuf = refs
    else:
        (h_ref, wa_ref, uu_ref, wd_ref, cw_ref, cb_ref,
         g_ref, tail_ref, wa_scr, wu_scr, wd_out, carry) = refs
    j = pl.program_id(0)
    i = pl.program_id(1)
    bn = wa_scr.shape[1]
    valid = (j * bn + lax.broadcasted_iota(jnp.int32, (1, bn), 1)) < D_FF
    hist = (CONV_W - 1) * n_seq

    if t_major:
        @pl.when(i == 0)
        def _():
            abuf[0:hist, :] = jnp.where(valid, cache_ref[...].reshape(hist, bn), 0.0)
    else:
        last_j = pl.num_programs(0) - 1

        @pl.when((i == 0) & (j < last_j))
        def _():
            wa_scr[...] = wa_ref[...].astype(BF16)
            wu_scr[...] = uu_ref[...].astype(BF16)
            wd_out[...] = wd_ref[...].astype(BF16)

        @pl.when((i == 0) & (j == last_j))
        def _():
            wu = jnp.concatenate([uu_ref[:, LANE:], jnp.zeros((D_MODEL, LANE), F32)], axis=1)
            wa_scr[...] = jnp.where(valid, wa_ref[...], 0.0).astype(BF16)
            wu_scr[...] = jnp.where(valid, wu, 0.0).astype(BF16)
            wd_rows = j * bn + lax.broadcasted_iota(jnp.int32, (bn, 1), 0)
            wd_out[...] = jnp.where(wd_rows < D_FF, wd_ref[...], 0.0).astype(BF16)

    if not t_major:
        @pl.when(i % tiles_per_seq == 0)
        def _():
            carry[...] = jnp.zeros(carry.shape, F32)

    row8 = lax.broadcasted_iota(jnp.int32, (8, 1), 0)
    for r in range(0, tm, sub):
        h = h_ref[r:r + sub, :]
        a = _dot(h, wa_scr[...])
        u = _dot(h, wu_scr[...])
        if t_major:
            abuf[hist + r:hist + r + sub, :] = a
            a1 = abuf[hist + r - n_seq:hist + r - n_seq + sub, :]
            a2 = abuf[hist + r - 2 * n_seq:hist + r - 2 * n_seq + sub, :]
        else:
            p1 = carry[7:8, :]
            p2 = carry[6:7, :]
            r1 = pltpu.roll(a, 1, axis=0)
            r2 = pltpu.roll(a, 2, axis=0)
            a1 = jnp.concatenate([jnp.where(row8 == 0, p1, r1[0:8, :]), r1[8:, :]], axis=0)
            a2 = jnp.concatenate(
                [jnp.where(row8 == 0, p2, jnp.where(row8 == 1, p1, r2[0:8, :])), r2[8:, :]], axis=0)
            carry[...] = a[sub - 8:sub, :]
        conv = cb_ref[...] + cw_ref[0:1, :] * a2 + cw_ref[1:2, :] * a1 + cw_ref[2:3, :] * a
        gl = 0.5 * conv * (1.0 + lax.erf(conv * (2.0 ** -0.5)))
        g_ref[r:r + sub, :] = (gl * u).astype(BF16)

    if t_major:
        tail_ref[...] = abuf[tm:tm + hist, :].reshape(tail_ref.shape)
    else:
        @pl.when((i + 1) % tiles_per_seq == 0)
        def _():
            tail_ref[0] = carry[8 - (CONV_W - 1):8, :]


def _up_proj(h2, w_up, w_down, conv_w_p, conv_b_p, *, tm, sub, seq_len, n_seq, cache=None):
    tokens = h2.shape[0]
    t_major = cache is not None
    bn = FF_BLOCK
    nj = D_FF_PAD // bn
    assert D_FF_PAD - D_FF == LANE
    tiles_per_seq = max(seq_len // tm, 1)
    wspec = lambda f: pl.BlockSpec((D_MODEL, bn), f)
    col_spec = wspec(lambda j, i: (0, j))
    h_spec = pl.BlockSpec((tm, D_MODEL), lambda j, i: (i, 0))
    conv_specs = [pl.BlockSpec((CONV_W, bn), lambda j, i: (0, j)), pl.BlockSpec((1, bn), lambda j, i: (0, j))]
    out_specs = [pl.BlockSpec((tm, bn), lambda j, i: (i, j))]
    out_shape = [jax.ShapeDtypeStruct((tokens, D_FF_PAD), BF16)]
    if t_major:
        hist_spec = pl.BlockSpec((CONV_W - 1, n_seq, bn), lambda j, i: (0, 0, j))
        in_specs = [h_spec, col_spec, col_spec] + conv_specs + [hist_spec]
        args = [h2, w_up[0], w_up[1], conv_w_p, conv_b_p, cache]
        out_specs.append(hist_spec)
        out_shape.append(jax.ShapeDtypeStruct((CONV_W - 1, n_seq, D_FF_PAD), F32))
        scratch_rows = (CONV_W - 1) * n_seq + tm
    else:
        row_spec = pl.BlockSpec((bn, D_MODEL), lambda j, i: (j, 0))
        u_off = lambda j, i: (0, pl.multiple_of(jnp.minimum(D_FF + bn * j, 2 * D_FF - bn), LANE))
        in_specs = [h_spec, col_spec,
                    pl.BlockSpec((pl.Element(D_MODEL), pl.Element(bn)), u_off),
                    row_spec] + conv_specs
        args = [h2, w_up, w_up, w_down, conv_w_p, conv_b_p]
        out_specs += [pl.BlockSpec((1, CONV_W - 1, bn), lambda j, i: (i // tiles_per_seq, 0, j)),
                      col_spec, col_spec, row_spec]
        out_shape += [jax.ShapeDtypeStruct((n_seq, CONV_W - 1, D_FF_PAD), F32),
                      jax.ShapeDtypeStruct((D_MODEL, D_FF_PAD), BF16),
                      jax.ShapeDtypeStruct((D_MODEL, D_FF_PAD), BF16),
                      jax.ShapeDtypeStruct((D_FF_PAD, D_MODEL), BF16)]
        scratch_rows = 8
    kern = functools.partial(_up_kernel, tm=tm, sub=sub, seq_len=seq_len, n_seq=n_seq, t_major=t_major,
                             tiles_per_seq=tiles_per_seq)
    outs = pl.pallas_call(
        kern,
        grid=(nj, tokens // tm),
        in_specs=in_specs,
        out_specs=out_specs,
        out_shape=out_shape,
        scratch_shapes=[pltpu.VMEM((scratch_rows, bn), F32)],
        compiler_params=pltpu.CompilerParams(
            dimension_semantics=("arbitrary", "arbitrary"), vmem_limit_bytes=VMEM_LIMIT),
        name="up_proj",
    )(*args)
    if t_major:
        return outs[0], outs[1], w_up, w_down
    return outs[0], outs[1], (outs[2], outs[3]), outs[4]


def _path(x, mod, w, *, n_seq, seq_len, tm_in, tm_res, tm_up, tm_down, states=None, cache=None):
    sample = states is not None
    sub = n_seq if sample else 256
    sub_up = n_seq if sample else 512
    mixer_w = (w["gbias"], w["lb_logits"], w["norm_a"], w["norm_b"])
    z, gates = _in_proj(x, mod, w["w_main"], w["w_gate"], tm=tm_in, sub=sub, seq_rows=sample,
                        seq_len=seq_len, z_dtype=F32 if sample else BF16)
    if sample:
        mix, c1, n1, m1, s1 = _mixer(z.reshape(seq_len, n_seq, Z_MAIN), gates.reshape(seq_len, n_seq, LANE),
                                     *mixer_w, n_seq=n_seq, seq_len=seq_len, states=states)
        mix = mix.reshape(n_seq * seq_len, D_MODEL)
    else:
        mix, c1, n1, m1, s1 = _mixer_chunk(z, gates, *mixer_w, n_seq=n_seq, seq_len=seq_len)
    x1, h2 = _res_block(mix, w["w_out"], x, mod, w["ln1_g"], w["ln1_b"], tm=tm_res, bk=D_MODEL, sub=sub,
                        seq_rows=sample, seq_len=seq_len, gate_col=2, emit_h2=True)
    g, conv_out, w_up_bf, w_down_bf = _up_proj(h2, w["w_up"], w["w_down"], w["conv_w"], w["conv_b"], tm=tm_up,
                                               sub=sub_up, seq_len=seq_len, n_seq=n_seq, cache=cache)
    (x2,) = _res_block(g, w_down_bf, x1, mod, w["ln2_g"], w["ln2_b"], tm=tm_down, bk=D_FF_PAD // 2, sub=sub,
                       seq_rows=sample, seq_len=seq_len, gate_col=5, emit_h2=False)
    return x2, (c1, n1, m1, s1), conv_out, dict(w_up=w_up_bf, w_down=w_down_bf)


def kernel(x_prompt, x_sample, state_mlstm_C, state_mlstm_n, state_mlstm_m, state_hgrn_S, cache_ffn_conv,
           c_prompt, c_sample, hgrn_lb_logits, w_ada, b_ada, w_in, b_gate_a, norm_a, norm_b, w_out,
           ln1_g, ln1_b, w_up, conv_w, conv_b, w_down, ln2_g, ln2_b):
    bp, tp, _ = x_prompt.shape
    bs, ts, _ = x_sample.shape
    pad_ff = D_FF_PAD - D_FF
    w_main, w_gate = _prep_w_in(jnp.transpose(w_in[0]))
    weights = dict(
        w_main=w_main, w_gate=w_gate, w_out=w_out[0].astype(BF16), w_up=w_up[0],
        w_down=w_down[0],
        gbias=jnp.pad(b_gate_a[0].reshape(1, N_GATE), ((0, 0), (0, LANE - N_GATE))),
        lb_logits=hgrn_lb_logits, norm_a=norm_a, norm_b=norm_b,
        ln1_g=ln1_g, ln1_b=ln1_b, ln2_g=ln2_g, ln2_b=ln2_b,
        conv_w=jnp.pad(conv_w[0], ((0, 0), (0, pad_ff))),
        conv_b=jnp.pad(conv_b, ((0, 0), (0, pad_ff))))

    c_p = jnp.pad(c_prompt, ((0, 8 - bp), (0, 0)))
    mod_p, mod_s = _ada(c_p, c_sample, w_ada[0], b_ada)

    yp, st_p, tail_p, w_ffn_bf = _path(x_prompt.reshape(bp * tp, D_MODEL), mod_p, weights, n_seq=bp,
                                       seq_len=tp, tm_in=1024, tm_res=512, tm_up=2048, tm_down=512)

    xs_t = jnp.swapaxes(x_sample, 0, 1).reshape(ts * bs, D_MODEL)
    cache_t = jnp.swapaxes(cache_ffn_conv[0], 0, 1)
    ys_t, st_s, tail_s, _ = _path(xs_t, mod_s, dict(weights, **w_ffn_bf), n_seq=bs, seq_len=ts,
                                  tm_in=ts * bs, tm_res=ts * bs, tm_up=ts * bs, tm_down=ts * bs,
                                  states=(state_mlstm_C, state_mlstm_n, state_mlstm_m, state_hgrn_S),
                                  cache=cache_t)

    ys = jnp.swapaxes(ys_t.reshape(ts, bs, D_MODEL), 0, 1)
    conv_p = tail_p[None, :, :, :D_FF]
    conv_s = jnp.swapaxes(tail_s[:, :, :D_FF], 0, 1)[None]
    return (yp.reshape(bp, tp, D_MODEL), ys,
            st_p[0], st_p[1], st_p[2], st_p[3], conv_p,
            st_s[0], st_s[1], st_s[2], st_s[3], conv_s)
```

```python
import functools

import numpy as np
import jax
import jax.numpy as jnp
from jax import lax
from jax.experimental import pallas as pl
from jax.experimental.pallas import tpu as pltpu

F32 = jnp.float32
BF16 = jnp.bfloat16

D_MODEL = 2048
N_HEADS_A, DK_A, DV_A = 4, 128, 256
N_HEADS_B, DK_B, DV_B = 8, 128, 128
A_WIDTH = N_HEADS_A * DV_A
B_WIDTH = N_HEADS_B * DV_B
N_GATE = 2 * N_HEADS_A
GATE_COL = 2 * N_HEADS_A * DK_A + A_WIDTH
D_IN = GATE_COL + N_GATE + A_WIDTH + 4 * B_WIDTH
D_FF = 5504
CONV_W = 3
EPS = 1e-5
ALPHA = 2.0 ** 0.25
LOG2_E = 1.4426950408889634
LANE = 128
FF_BLOCK = 512
D_FF_PAD = 5632
Z_MAIN = D_IN - N_GATE
OFF_QA, OFF_KA, OFF_VA, OFF_OA = 0, 512, 1024, 2048
OFF_FB, OFF_QB, OFF_VB, OFF_GB = 3072, 4096, 5120, 6144
VMEM_LIMIT = 60 * 1024 * 1024
PROMPT_CHUNK = 128
PROMPT_SEQS_PER_STEP = 2
SAMPLE_GROUP = 8


def _ln(x):
    mu = jnp.mean(x, axis=-1, keepdims=True)
    xc = x - mu
    var = jnp.mean(xc * xc, axis=-1, keepdims=True)
    return xc * lax.rsqrt(var + EPS)


def _sigmoid(x):
    return 1.0 / (1.0 + jnp.exp(-x))


def _dot(a, b):
    return jnp.dot(a.astype(BF16), b.astype(BF16), preferred_element_type=F32)


def _dot_nt(a, b):
    return lax.dot_general(a.astype(BF16), b.astype(BF16), (((1,), (1,)), ((), ())),
                           preferred_element_type=F32)


def _dot_tn(a, b):
    return lax.dot_general(a.astype(BF16), b.astype(BF16), (((0,), (0,)), ((), ())),
                           preferred_element_type=F32)


def _mod_rows(ref, seq_rows, seq):
    if seq_rows:
        return ref[...]
    return ref[pl.ds(seq, 1), :]


def _ada_kernel(cp_ref, cs_ref, w_ref, b_ref, op_ref, os_ref):
    w = w_ref[...].astype(BF16)
    for c_ref, o_ref in ((cp_ref, op_ref), (cs_ref, os_ref)):
        c = c_ref[...]
        o_ref[...] = _dot(c * _sigmoid(c), w) + b_ref[...]


def _ada(c_p, c_s, w_ada, b_ada):
    n = w_ada.shape[1]
    bn = 1024
    row = lambda r: pl.BlockSpec((r, D_MODEL), lambda j: (0, 0))
    out = lambda r: pl.BlockSpec((r, bn), lambda j: (0, j))
    return pl.pallas_call(
        _ada_kernel,
        grid=(n // bn,),
        in_specs=[row(c_p.shape[0]), row(c_s.shape[0]),
                  pl.BlockSpec((D_MODEL, bn), lambda j: (0, j)),
                  pl.BlockSpec((1, bn), lambda j: (0, j))],
        out_specs=[out(c_p.shape[0]), out(c_s.shape[0])],
        out_shape=[jax.ShapeDtypeStruct((c_p.shape[0], n), F32),
                   jax.ShapeDtypeStruct((c_s.shape[0], n), F32)],
        compiler_params=pltpu.CompilerParams(vmem_limit_bytes=VMEM_LIMIT),
        name="ada",
    )(c_p, c_s, w_ada, b_ada)


def _prep_in_kernel(front_ref, back_ref, gate_ref, o_ref, og_ref, *, first_shifted):
    j = pl.program_id(0)

    @pl.when(j == 0)
    def _():
        og_ref[...] = gate_ref[...].T.astype(BF16)

    def emit(src_ref):
        for r in range(0, src_ref.shape[0], 256):
            o_ref[:, r:r + 256] = src_ref[r:r + 256, :].T.astype(BF16)

    @pl.when(j < first_shifted)
    def _():
        emit(front_ref)

    @pl.when(j >= first_shifted)
    def _():
        emit(back_ref)


def _prep_w_in(w_in_t):
    bn = 1024
    first_shifted = GATE_COL // bn
    kern = functools.partial(_prep_in_kernel, first_shifted=first_shifted)
    return pl.pallas_call(
        kern,
        grid=(Z_MAIN // bn,),
        in_specs=[pl.BlockSpec((bn, D_MODEL), lambda j: (jnp.minimum(j, first_shifted - 1), 0)),
                  pl.BlockSpec((pl.Element(bn), pl.Element(D_MODEL)),
                               lambda j: (pl.multiple_of(jnp.maximum(j, first_shifted) * bn + N_GATE, N_GATE), 0)),
                  pl.BlockSpec((LANE, D_MODEL), lambda j: (GATE_COL // LANE, 0))],
        out_specs=[pl.BlockSpec((D_MODEL, bn), lambda j: (0, j)),
                   pl.BlockSpec((D_MODEL, LANE), lambda j: (0, 0))],
        out_shape=[jax.ShapeDtypeStruct((D_MODEL, Z_MAIN), BF16),
                   jax.ShapeDtypeStruct((D_MODEL, LANE), BF16)],
        compiler_params=pltpu.CompilerParams(vmem_limit_bytes=VMEM_LIMIT),
        name="prep_w_in",
    )(w_in_t, w_in_t, w_in_t)


def _in_kernel(x_ref, sh_ref, sc_ref, w_ref, wg_ref, z_ref, g_ref, h_scr, *, sub, seq_rows, tiles_per_seq):
    i = pl.program_id(0)
    j = pl.program_id(1)

    @pl.when(j == 0)
    def _():
        seq = i // tiles_per_seq
        sh = _mod_rows(sh_ref, seq_rows, seq)
        sc = _mod_rows(sc_ref, seq_rows, seq)
        for r in range(0, x_ref.shape[0], sub):
            rows = slice(r, r + sub)
            h = (_ln(x_ref[rows, :]) * (1.0 + sc) + sh).astype(BF16)
            h_scr[rows, :] = h
            g_ref[rows, :] = _dot(h, wg_ref[...])
            z_ref[rows, :] = _dot(h, w_ref[...]).astype(z_ref.dtype)

    @pl.when(j > 0)
    def _():
        z_ref[...] = _dot(h_scr[...], w_ref[...]).astype(z_ref.dtype)


def _in_proj(x, mod, w_main, w_gate, *, tm, sub, seq_rows, seq_len, z_dtype):
    tokens = x.shape[0]
    bn = 1792
    mod_spec = lambda k: pl.BlockSpec((mod.shape[0], D_MODEL), lambda i, j: (0, k))
    kern = functools.partial(_in_kernel, sub=min(sub, tm), seq_rows=seq_rows, tiles_per_seq=max(seq_len // tm, 1))
    return pl.pallas_call(
        kern,
        grid=(tokens // tm, Z_MAIN // bn),
        in_specs=[pl.BlockSpec((tm, D_MODEL), lambda i, j: (i, 0)),
                  mod_spec(0), mod_spec(1),
                  pl.BlockSpec((D_MODEL, bn), lambda i, j: (0, j)),
                  pl.BlockSpec((D_MODEL, LANE), lambda i, j: (0, 0))],
        out_specs=[pl.BlockSpec((tm, bn), lambda i, j: (i, j)),
                   pl.BlockSpec((tm, LANE), lambda i, j: (i, 0))],
        out_shape=[jax.ShapeDtypeStruct((tokens, Z_MAIN), z_dtype),
                   jax.ShapeDtypeStruct((tokens, LANE), F32)],
        scratch_shapes=[pltpu.VMEM((tm, D_MODEL), BF16)],
        compiler_params=pltpu.CompilerParams(
            dimension_semantics=("arbitrary", "arbitrary"), vmem_limit_bytes=VMEM_LIMIT),
        name="in_proj",
    )(x, mod, mod, w_main, w_gate)


def _mix_consts(L, Tg, t_major):
    G = L // Tg
    r = np.arange(L)
    seq, tim = (r % G, r // G) if t_major else (r // Tg, r % Tg)
    same = seq[:, None] == seq[None, :]
    t = tim[:, None]
    u = tim[None, :]
    mats = [same & (u <= t), same & (u > t)]
    masks = [same & (u <= t), same & (t <= u), same]
    levels = []
    m = Tg // 2
    while m >= 1:
        levels.append(m)
        m //= 2
    for m in levels:
        blk = same & (u // m == t // m)
        odd = (t // m) % 2 == 1
        mats.append(np.where(odd, blk & (u <= t), blk & (u > t)))
        masks.append(same & odd & ((u // m) % 2 == 0) & (t // (2 * m) == u // (2 * m)))
    mats = np.concatenate(mats, axis=0).astype(np.float32)
    masks = np.stack(masks).astype(np.float32)
    last_rows = tuple(int(np.nonzero((seq == g) & (tim == Tg - 1))[0][0]) for g in range(G))
    return mats, masks, tuple(levels), last_rows


def _mix_kernel(z_ref, g_ref, gbias_ref, lbl_ref, na_ref, nb_ref, mall_ref, msk_ref,
                c_in, n_in, m_in, s_in, mix_ref, c_out, n_out, m_out, s_out, e_scr,
                *, L, Tg, levels, last_rows):
    G = L // Tg

    def zs(off, w):
        return z_ref[:, :, off:off + w].reshape(L, w)

    def put_mix(off, w, val):
        mix_ref[:, :, off:off + w] = val.astype(mix_ref.dtype).reshape(Tg, G, w)

    _mix_body(zs, put_mix, g_ref[...].reshape(L, LANE), gbias_ref, lbl_ref, na_ref, nb_ref, mall_ref, msk_ref,
              c_in, n_in, m_in, s_in, c_out, n_out, m_out, s_out, e_scr, pl.program_id(0) * G,
              L=L, Tg=Tg, levels=levels, last_rows=last_rows, t_major=True)


def _mixchunk_kernel(z_ref, g_ref, gbias_ref, lbl_ref, na_ref, nb_ref, mall_ref, msk_ref,
                     mix_ref, c_out, n_out, m_out, s_out, e_scr, *, L, levels, last_rows):
    nb = z_ref.shape[0]
    b0 = pl.program_id(0) * nb

    @pl.when(pl.program_id(1) == 0)
    def _():
        c_out[...] = jnp.zeros(c_out.shape, F32)
        n_out[...] = jnp.zeros(n_out.shape, F32)
        s_out[...] = jnp.zeros(s_out.shape, F32)
        m_out[0, pl.ds(b0, nb), :] = jnp.zeros((nb, N_HEADS_A), F32)

    for k in range(nb):
        def zs(off, w, k=k):
            return z_ref[k, :, off:off + w]

        def put_mix(off, w, val, k=k):
            mix_ref[k, :, off:off + w] = val.astype(mix_ref.dtype)

        st = [r.at[:, pl.ds(k, 1)] for r in (c_out, n_out, s_out)]
        _mix_body(zs, put_mix, g_ref[k], gbias_ref, lbl_ref, na_ref, nb_ref, mall_ref, msk_ref,
                  st[0], st[1], m_out, st[2], st[0], st[1], m_out, st[2], e_scr.at[k], b0 + k,
                  L=L, Tg=L, levels=levels, last_rows=last_rows, t_major=False)


def _mix_body(zs, put_mix, gates, gbias_ref, lbl_ref, na_ref, nb_ref, mall_ref, msk_ref,
              c_in, n_in, m_in, s_in, c_out, n_out, m_out, s_out, e_scr, m_row0,
              *, L, Tg, levels, last_rows, t_major):
    G = L // Tg
    neg_inf = F32(-jnp.inf)
    causal = msk_ref[0] > 0.5
    causal_t = msk_ref[1] > 0.5
    same = msk_ref[2] > 0.5
    row1 = lax.broadcasted_iota(jnp.int32, (L, 1), 0)
    seq_of_row = (row1 & (G - 1)) if t_major else (row1 // Tg)
    in_group = [seq_of_row == g for g in range(G)]
    col1 = lax.broadcasted_iota(jnp.int32, (1, L), 1)
    seq_of_col = (col1 & (G - 1)) if t_major else (col1 // Tg)
    in_group_lane = [seq_of_col == g for g in range(G)]

    def state_updates(kw, v):
        if G == 1:
            return [_dot_tn(kw, v)]
        kw_t = kw.T
        return [_dot(jnp.where(in_group_lane[g], kw_t, 0.0), v) for g in range(G)]

    def by_group(vals):
        if G == 1:
            return vals[0]
        out = jnp.where(in_group[0], vals[0], 0.0)
        for g in range(1, G):
            out = out + jnp.where(in_group[g], vals[g], 0.0)
        return out

    pre = gates + gbias_ref[...]
    lsig = jnp.minimum(pre, 0.0) - jnp.log(1.0 + jnp.exp(-jnp.abs(pre)))
    pre_t = pre.T
    lsig_t = lsig.T
    m_prev = m_in[0, pl.ds(m_row0, G), :]
    scale = DK_A ** -0.5
    m_new_rows = []
    for h in range(N_HEADS_A):
        q = zs(OFF_QA + DK_A * h, DK_A)
        k = zs(OFF_KA + DK_A * h, DK_A)
        v = zs(OFF_VA + DV_A * h, DV_A)
        logi_c = pre[:, h:h + 1]
        logi_r = pre_t[h:h + 1, :]
        lf_c = lsig[:, N_HEADS_A + h:N_HEADS_A + h + 1]
        lf_r = lsig_t[N_HEADS_A + h:N_HEADS_A + h + 1, :]
        b_c = jnp.sum(jnp.where(causal, lf_r, 0.0), axis=1, keepdims=True)
        b_r = jnp.sum(jnp.where(causal_t, lf_c, 0.0), axis=0, keepdims=True)
        tots = [b_c[last_rows[g]:last_rows[g] + 1, :] for g in range(G)]
        tot_c = by_group(tots)
        tot_r = tots[0] if G == 1 else sum(jnp.where(in_group_lane[g], tots[g], 0.0) for g in range(G))
        m_col = by_group([m_prev[g:g + 1, h:h + 1] for g in range(G)])
        dmat = jnp.where(causal, b_c - b_r + logi_r, neg_inf)
        inter = b_c + m_col
        m_t = jnp.maximum(inter, jnp.max(dmat, axis=1, keepdims=True))
        w_inter = jnp.exp(inter - m_t)
        smat = _dot_nt(q, k) * (scale * jnp.exp(dmat - m_t))
        q_c = by_group([_dot(q, c_in[0, g, h]) for g in range(G)]) * scale
        n_rows = by_group([n_in[0, g, h:h + 1, :] for g in range(G)])
        q_n = jnp.sum(q.astype(F32) * n_rows, axis=1, keepdims=True) * scale
        num = w_inter * q_c + _dot(smat, v)
        den = w_inter * q_n + jnp.sum(smat, axis=1, keepdims=True)
        hh = num / jnp.maximum(jnp.abs(den), jnp.exp(-m_t))
        ms = jnp.mean(hh * hh, axis=1, keepdims=True)
        oa = zs(OFF_OA + DV_A * h, DV_A).astype(F32)
        ya = hh * lax.rsqrt(ms + EPS) * na_ref[:, DV_A * h:DV_A * (h + 1)] * _sigmoid(oa)
        put_mix(DV_A * h, DV_A, ya)
        dec_c = tot_c - b_c + logi_c
        dec_r = tot_r - b_r + logi_r
        if G == 1:
            dec_max = jnp.max(dec_r, axis=1, keepdims=True)
        else:
            dec_max = jnp.max(jnp.where(same, dec_r, neg_inf), axis=1, keepdims=True)
        m_new_c = jnp.maximum(tot_c + m_col, dec_max)
        wk = jnp.exp(dec_c - m_new_c)
        sc = jnp.exp(tot_c + m_col - m_new_c)
        kw = k.astype(F32) * wk
        upd = state_updates(kw, v)
        m_new_h = []
        for g in range(G):
            last = last_rows[g]
            kg = kw if G == 1 else jnp.where(in_group[g], kw, 0.0)
            sc_g = sc if G == 1 else sc[last:last + 1, :]
            c_new = sc_g * c_in[0, g, h] + upd[g]
            n_new = sc_g * n_in[0, g, h:h + 1, :] + jnp.sum(kg, axis=0, keepdims=True)
            c_out[0, g, h] = c_new
            n_out[0, g, h:h + 1, :] = n_new
            m_new_h.append(m_new_c if G == 1 else m_new_c[last:last + 1, :])
        m_new_rows.append(m_new_h)
    for g in range(G):
        m_out[0, pl.ds(m_row0 + g, 1), :] = jnp.concatenate(
            [m_new_rows[h][g] for h in range(N_HEADS_A)], axis=1)

    l0 = lbl_ref[0:1, :]
    l1 = lbl_ref[1:2, :]
    lmax = jnp.maximum(l0, l1)
    e0 = jnp.exp(l0 - lmax)
    e1 = jnp.exp(l1 - lmax)
    lb = e0 / (e0 + e1)
    fb = zs(OFF_FB, B_WIDTH).astype(F32)
    e = jnp.exp(-jnp.abs(fb))
    r = 1.0 / (1.0 + e)
    pos = fb >= 0.0
    sig = jnp.where(pos, r, e * r)
    nsig = jnp.where(pos, e * r, r)
    logf = jnp.log(lb + (1.0 - lb) * sig) * LOG2_E
    kb = (1.0 - lb) * nsig
    hi = logf.astype(BF16)
    r1 = logf - hi.astype(F32)
    mid = r1.astype(BF16)
    lo = (r1 - mid.astype(F32)).astype(BF16)
    mall = mall_ref[...]
    if L % 16 == 0:
        e_scr[...] = _dot(mall, jnp.concatenate([hi, mid, lo], axis=0))
    else:
        mall = mall[:, :L]
        e_scr[...] = _dot(mall, hi) + _dot(mall, mid) + _dot(mall, lo)

    dec_t = []
    for g in range(G):
        last = last_rows[g]
        bl = e_scr[last:last + 1, :]
        bl8 = jnp.concatenate([bl[:, DK_B * h:DK_B * (h + 1)] for h in range(N_HEADS_B)], axis=0)
        dec_t.append(jnp.exp2(bl8).T)

    for h in range(N_HEADS_B):
        sl = slice(DK_B * h, DK_B * (h + 1))
        q = zs(OFF_QB + DK_B * h, DK_B).astype(BF16)
        k = kb[:, sl]
        kbf = k.astype(BF16)
        v = zs(OFF_VB + DV_B * h, DV_B)
        amat = jnp.zeros((L, L), F32)
        for li in range(len(levels)):
            ex = jnp.exp2(e_scr[(2 + li) * L:(3 + li) * L, sl]).astype(BF16)
            amat = jnp.where(msk_ref[3 + li] > 0.5, _dot_nt(q * ex, kbf * ex), amat)
        qf = q.astype(F32)
        diag = jnp.sum(qf * k, axis=1, keepdims=True)
        qs = qf * jnp.exp2(e_scr[0:L, sl])
        o_inter = by_group([_dot(qs, s_in[0, g, h]) for g in range(G)])
        o = _dot(amat, v) + diag * v.astype(F32) + o_inter
        ms = jnp.mean(o * o, axis=1, keepdims=True)
        gb = zs(OFF_GB + DV_B * h, DV_B).astype(F32)
        yb = o * lax.rsqrt(ms + EPS) * nb_ref[:, sl] * (gb * _sigmoid(gb))
        put_mix(A_WIDTH + DV_B * h, DV_B, yb)
        upd = state_updates(k * jnp.exp2(e_scr[L:2 * L, sl]), v)
        for g in range(G):
            s_out[0, g, h] = dec_t[g][:, h:h + 1] * s_in[0, g, h] + upd[g]


def _mixer_args(L, Tg, t_major, const2, const3, gbias, lb_logits, norm_a, norm_b):
    mats, masks, levels, last_rows = _mix_consts(L, Tg, t_major)
    n_mats = mats.shape[0] // L
    if L % 16 == 0:
        mats = np.concatenate([mats, mats, mats], axis=1)
    specs = [pl.BlockSpec((1, LANE), const2),
             pl.BlockSpec((2, B_WIDTH), const2),
             pl.BlockSpec((1, A_WIDTH), const2),
             pl.BlockSpec((1, B_WIDTH), const2),
             pl.BlockSpec(mats.shape, const2),
             pl.BlockSpec(masks.shape, const3)]
    args = [gbias, lb_logits, norm_a, norm_b, jnp.asarray(mats, BF16), jnp.asarray(masks)]
    return specs, args, levels, last_rows, n_mats


def _state_shapes(n_seq):
    return [jax.ShapeDtypeStruct((1, n_seq, N_HEADS_A, DK_A, DV_A), F32),
            jax.ShapeDtypeStruct((1, n_seq, N_HEADS_A, DK_A), F32),
            jax.ShapeDtypeStruct((1, n_seq, N_HEADS_A), F32),
            jax.ShapeDtypeStruct((1, n_seq, N_HEADS_B, DK_B, DV_B), F32)]


def _state_specs(G, n_seq, seq_of_step):
    return [pl.BlockSpec((1, G, N_HEADS_A, DK_A, DV_A), lambda *ids: (0, seq_of_step(*ids), 0, 0, 0)),
            pl.BlockSpec((1, G, N_HEADS_A, DK_A), lambda *ids: (0, seq_of_step(*ids), 0, 0)),
            pl.BlockSpec((1, n_seq, N_HEADS_A), lambda *ids: (0, 0, 0)),
            pl.BlockSpec((1, G, N_HEADS_B, DK_B, DV_B), lambda *ids: (0, seq_of_step(*ids), 0, 0, 0))]


def _mixer(z, gates, gbias, lb_logits, norm_a, norm_b, *, n_seq, seq_len, states):
    Tg, G = seq_len, SAMPLE_GROUP
    L = Tg * G
    tile = lambda w: pl.BlockSpec((Tg, G, w), lambda i: (0, i, 0))
    const_specs, const_args, levels, last_rows, n_mats = _mixer_args(
        L, Tg, True, lambda i: (0, 0), lambda i: (0, 0, 0), gbias, lb_logits, norm_a, norm_b)
    st_specs = _state_specs(G, n_seq, lambda i: i)
    kern = functools.partial(_mix_kernel, L=L, Tg=Tg, levels=levels, last_rows=last_rows)
    return pl.pallas_call(
        kern,
        grid=(n_seq // G,),
        in_specs=[tile(Z_MAIN), tile(LANE)] + const_specs + st_specs,
        out_specs=[tile(D_MODEL)] + st_specs,
        out_shape=[jax.ShapeDtypeStruct((Tg, n_seq, D_MODEL), F32)] + _state_shapes(n_seq),
        scratch_shapes=[pltpu.VMEM((n_mats * L, B_WIDTH), F32)],
        compiler_params=pltpu.CompilerParams(dimension_semantics=("arbitrary",), vmem_limit_bytes=VMEM_LIMIT),
        name="mixer_state",
    )(z, gates, *const_args, *states)


def _mixer_chunk(z, gates, gbias, lb_logits, norm_a, norm_b, *, n_seq, seq_len):
    L = PROMPT_CHUNK
    nb = PROMPT_SEQS_PER_STEP
    nchunks = seq_len // L
    tile = lambda w: pl.BlockSpec((nb, L, w), lambda b, c: (b, c, 0))
    const_specs, const_args, levels, last_rows, n_mats = _mixer_args(
        L, L, False, lambda b, c: (0, 0), lambda b, c: (0, 0, 0), gbias, lb_logits, norm_a, norm_b)
    st_specs = _state_specs(nb, n_seq, lambda b, c: b)
    kern = functools.partial(_mixchunk_kernel, L=L, levels=levels, last_rows=last_rows)
    outs = pl.pallas_call(
        kern,
        grid=(n_seq // nb, nchunks),
        in_specs=[tile(Z_MAIN), tile(LANE)] + const_specs,
        out_specs=[tile(D_MODEL)] + st_specs,
        out_shape=[jax.ShapeDtypeStruct((n_seq, seq_len, D_MODEL), BF16)] + _state_shapes(n_seq),
        scratch_shapes=[pltpu.VMEM((nb, n_mats * L, B_WIDTH), F32)],
        compiler_params=pltpu.CompilerParams(dimension_semantics=("arbitrary", "arbitrary"),
                                             vmem_limit_bytes=VMEM_LIMIT),
        name="mixer_chunk",
    )(z.reshape(n_seq, seq_len, Z_MAIN), gates.reshape(n_seq, seq_len, LANE), *const_args)
    return [outs[0].reshape(n_seq * seq_len, D_MODEL)] + list(outs[1:])


def _res_kernel(*refs, sub, seq_rows, tiles_per_seq, emit_h2, nk, cast_w):
    if cast_w:
        refs, wbf_ref = refs[:-1], refs[-1]
    if emit_h2:
        a_ref, w_ref, x_ref, gm_ref, lg_ref, lb_ref, sh_ref, sc_ref, o_ref, h_ref = refs
    else:
        a_ref, w_ref, x_ref, gm_ref, lg_ref, lb_ref, o_ref = refs
    i = pl.program_id(0)
    k = pl.program_id(1)

    if cast_w:
        @pl.when((i == 0) & (k == 0))
        def _():
            wbf_ref[...] = w_ref[...].astype(BF16)
        w_ref = wbf_ref

    if nk > 1:
        @pl.when(k == 0)
        def _():
            o_ref[...] = _dot(a_ref[...], w_ref[...])

    if nk > 2:
        @pl.when((k > 0) & (k < nk - 1))
        def _():
            o_ref[...] += _dot(a_ref[...], w_ref[...])

    @pl.when(k == nk - 1)
    def _():
        seq = i // tiles_per_seq
        gm = _mod_rows(gm_ref, seq_rows, seq)
        if emit_h2:
            sh = _mod_rows(sh_ref, seq_rows, seq)
            sc = _mod_rows(sc_ref, seq_rows, seq)
        for r in range(0, o_ref.shape[0], sub):
            rows = slice(r, r + sub)
            acc = _dot(a_ref[rows, :], w_ref[...])
            if nk > 1:
                acc = acc + o_ref[rows, :]
            x1 = _ln(ALPHA * x_ref[rows, :] + gm * acc) * lg_ref[...] + lb_ref[...]
            o_ref[rows, :] = x1
            if emit_h2:
                h_ref[rows, :] = (_ln(x1) * (1.0 + sc) + sh).astype(BF16)


def _res_block(a, w, x, mod, ln_g, ln_b, *, tm, bk, sub, seq_rows, seq_len, gate_col, emit_h2):
    tokens, kdim = a.shape
    nk = kdim // bk
    cast_w = w.dtype == F32
    assert not cast_w or nk == 1
    mod_spec = lambda c: pl.BlockSpec((mod.shape[0], D_MODEL), lambda i, k: (0, c))
    row_spec = pl.BlockSpec((tm, D_MODEL), lambda i, k: (i, 0))
    vec_spec = pl.BlockSpec((1, D_MODEL), lambda i, k: (0, 0))
    w_spec = pl.BlockSpec((bk, D_MODEL), lambda i, k: (k, 0),
                          **(dict(pipeline_mode=pl.Buffered(1)) if cast_w else {}))
    in_specs = [pl.BlockSpec((tm, bk), lambda i, k: (i, k)), w_spec,
                row_spec, mod_spec(gate_col), vec_spec, vec_spec]
    args = [a, w, x, mod, ln_g, ln_b]
    out_specs = [row_spec]
    out_shape = [jax.ShapeDtypeStruct((tokens, D_MODEL), F32)]
    if emit_h2:
        in_specs += [mod_spec(3), mod_spec(4)]
        args += [mod, mod]
        out_specs.append(row_spec)
        out_shape.append(jax.ShapeDtypeStruct((tokens, D_MODEL), BF16))
    if cast_w:
        out_specs.append(pl.BlockSpec((kdim, D_MODEL), lambda i, k: (0, 0)))
        out_shape.append(jax.ShapeDtypeStruct((kdim, D_MODEL), BF16))
    kern = functools.partial(_res_kernel, sub=sub, seq_rows=seq_rows, tiles_per_seq=max(seq_len // tm, 1),
                             emit_h2=emit_h2, nk=nk, cast_w=cast_w)
    return pl.pallas_call(
        kern,
        grid=(tokens // tm, nk),
        in_specs=in_specs,
        out_specs=out_specs,
        out_shape=out_shape,
        compiler_params=pltpu.CompilerParams(
            dimension_semantics=("arbitrary", "arbitrary"), vmem_limit_bytes=VMEM_LIMIT),
        name="res_block",
    )(*args)


def _up_kernel(*refs, tm, sub, seq_len, n_seq, t_major, tiles_per_seq):
    if t_major:
        h_ref, wa_scr, wu_scr, cw_ref, cb_ref, cache_ref, g_ref, tail_ref, abuf = refs
    else:
        (h_ref, wa_ref, uu_ref, wd_ref, cw_ref, cb_ref,
         g_ref, tail_ref, wa_scr, wu_scr, wd_out, carry) = refs
    j = pl.program_id(0)
    i = pl.program_id(1)
    bn = wa_scr.shape[1]
    valid = (j * bn + lax.broadcasted_iota(jnp.int32, (1, bn), 1)) < D_FF
    hist = (CONV_W - 1) * n_seq

    if t_major:
        @pl.when(i == 0)
        def _():
            abuf[0:hist, :] = jnp.where(valid, cache_ref[...].reshape(hist, bn), 0.0)
    else:
        last_j = pl.num_programs(0) - 1

        @pl.when((i == 0) & (j < last_j))
        def _():
            wa_scr[...] = wa_ref[...].astype(BF16)
            wu_scr[...] = uu_ref[...].astype(BF16)
            wd_out[...] = wd_ref[...].astype(BF16)

        @pl.when((i == 0) & (j == last_j))
        def _():
            wu = jnp.concatenate([uu_ref[:, LANE:], jnp.zeros((D_MODEL, LANE), F32)], axis=1)
            wa_scr[...] = jnp.where(valid, wa_ref[...], 0.0).astype(BF16)
            wu_scr[...] = jnp.where(valid, wu, 0.0).astype(BF16)
            wd_rows = j * bn + lax.broadcasted_iota(jnp.int32, (bn, 1), 0)
            wd_out[...] = jnp.where(wd_rows < D_FF, wd_ref[...], 0.0).astype(BF16)

    if not t_major:
        @pl.when(i % tiles_per_seq == 0)
        def _():
            carry[...] = jnp.zeros(carry.shape, F32)

    row8 = lax.broadcasted_iota(jnp.int32, (8, 1), 0)
    for r in range(0, tm, sub):
        h = h_ref[r:r + sub, :]
        a = _dot(h, wa_scr[...])
        u = _dot(h, wu_scr[...])
        if t_major:
            abuf[hist + r:hist + r + sub, :] = a
            a1 = abuf[hist + r - n_seq:hist + r - n_seq + sub, :]
            a2 = abuf[hist + r - 2 * n_seq:hist + r - 2 * n_seq + sub, :]
        else:
            p1 = carry[7:8, :]
            p2 = carry[6:7, :]
            r1 = pltpu.roll(a, 1, axis=0)
            r2 = pltpu.roll(a, 2, axis=0)
            a1 = jnp.concatenate([jnp.where(row8 == 0, p1, r1[0:8, :]), r1[8:, :]], axis=0)
            a2 = jnp.concatenate(
                [jnp.where(row8 == 0, p2, jnp.where(row8 == 1, p1, r2[0:8, :])), r2[8:, :]], axis=0)
            carry[...] = a[sub - 8:sub, :]
        conv = cb_ref[...] + cw_ref[0:1, :] * a2 + cw_ref[1:2, :] * a1 + cw_ref[2:3, :] * a
        gl = 0.5 * conv * (1.0 + lax.erf(conv * (2.0 ** -0.5)))
        g_ref[r:r + sub, :] = (gl * u).astype(BF16)

    if t_major:
        tail_ref[...] = abuf[tm:tm + hist, :].reshape(tail_ref.shape)
    else:
        @pl.when((i + 1) % tiles_per_seq == 0)
        def _():
            tail_ref[0] = carry[8 - (CONV_W - 1):8, :]


def _up_proj(h2, w_up, w_down, conv_w_p, conv_b_p, *, tm, sub, seq_len, n_seq, cache=None):
    tokens = h2.shape[0]
    t_major = cache is not None
    bn = FF_BLOCK
    nj = D_FF_PAD // bn
    assert D_FF_PAD - D_FF == LANE
    tiles_per_seq = max(seq_len // tm, 1)
    wspec = lambda f: pl.BlockSpec((D_MODEL, bn), f)
    col_spec = wspec(lambda j, i: (0, j))
    h_spec = pl.BlockSpec((tm, D_MODEL), lambda j, i: (i, 0))
    conv_specs = [pl.BlockSpec((CONV_W, bn), lambda j, i: (0, j)), pl.BlockSpec((1, bn), lambda j, i: (0, j))]
    out_specs = [pl.BlockSpec((tm, bn), lambda j, i: (i, j))]
    out_shape = [jax.ShapeDtypeStruct((tokens, D_FF_PAD), BF16)]
    if t_major:
        hist_spec = pl.BlockSpec((CONV_W - 1, n_seq, bn), lambda j, i: (0, 0, j))
        in_specs = [h_spec, col_spec, col_spec] + conv_specs + [hist_spec]
        args = [h2, w_up[0], w_up[1], conv_w_p, conv_b_p, cache]
        out_specs.append(hist_spec)
        out_shape.append(jax.ShapeDtypeStruct((CONV_W - 1, n_seq, D_FF_PAD), F32))
        scratch_rows = (CONV_W - 1) * n_seq + tm
    else:
        row_spec = pl.BlockSpec((bn, D_MODEL), lambda j, i: (j, 0))
        u_off = lambda j, i: (0, pl.multiple_of(jnp.minimum(D_FF + bn * j, 2 * D_FF - bn), LANE))
        in_specs = [h_spec, col_spec,
                    pl.BlockSpec((pl.Element(D_MODEL), pl.Element(bn)), u_off),
                    row_spec] + conv_specs
        args = [h2, w_up, w_up, w_down, conv_w_p, conv_b_p]
        out_specs += [pl.BlockSpec((1, CONV_W - 1, bn), lambda j, i: (i // tiles_per_seq, 0, j)),
                      col_spec, col_spec, row_spec]
        out_shape += [jax.ShapeDtypeStruct((n_seq, CONV_W - 1, D_FF_PAD), F32),
                      jax.ShapeDtypeStruct((D_MODEL, D_FF_PAD), BF16),
                      jax.ShapeDtypeStruct((D_MODEL, D_FF_PAD), BF16),
                      jax.ShapeDtypeStruct((D_FF_PAD, D_MODEL), BF16)]
        scratch_rows = 8
    kern = functools.partial(_up_kernel, tm=tm, sub=sub, seq_len=seq_len, n_seq=n_seq, t_major=t_major,
                             tiles_per_seq=tiles_per_seq)
    outs = pl.pallas_call(
        kern,
        grid=(nj, tokens // tm),
        in_specs=in_specs,
        out_specs=out_specs,
        out_shape=out_shape,
        scratch_shapes=[pltpu.VMEM((scratch_rows, bn), F32)],
        compiler_params=pltpu.CompilerParams(
            dimension_semantics=("arbitrary", "arbitrary"), vmem_limit_bytes=VMEM_LIMIT),
        name="up_proj",
    )(*args)
    if t_major:
        return outs[0], outs[1], w_up, w_down
    return outs[0], outs[1], (outs[2], outs[3]), outs[4]


def _path(x, mod, w, *, n_seq, seq_len, tm_in, tm_res, tm_up, tm_down, states=None, cache=None):
    sample = states is not None
    sub = n_seq if sample else 256
    sub_up = n_seq if sample else 512
    mixer_w = (w["gbias"], w["lb_logits"], w["norm_a"], w["norm_b"])
    z, gates = _in_proj(x, mod, w["w_main"], w["w_gate"], tm=tm_in, sub=sub, seq_rows=sample,
                        seq_len=seq_len, z_dtype=F32 if sample else BF16)
    if sample:
        mix, c1, n1, m1, s1 = _mixer(z.reshape(seq_len, n_seq, Z_MAIN), gates.reshape(seq_len, n_seq, LANE),
                                     *mixer_w, n_seq=n_seq, seq_len=seq_len, states=states)
        mix = mix.reshape(n_seq * seq_len, D_MODEL)
    else:
        mix, c1, n1, m1, s1 = _mixer_chunk(z, gates, *mixer_w, n_seq=n_seq, seq_len=seq_len)
    x1, h2, *w_out_bf = _res_block(mix, w["w_out"], x, mod, w["ln1_g"], w["ln1_b"], tm=tm_res, bk=D_MODEL,
                                   sub=sub, seq_rows=sample, seq_len=seq_len, gate_col=2, emit_h2=True)
    w_out_bf = w_out_bf[0] if w_out_bf else w["w_out"]
    g, conv_out, w_up_bf, w_down_bf = _up_proj(h2, w["w_up"], w["w_down"], w["conv_w"], w["conv_b"], tm=tm_up,
                                               sub=sub_up, seq_len=seq_len, n_seq=n_seq, cache=cache)
    (x2,) = _res_block(g, w_down_bf, x1, mod, w["ln2_g"], w["ln2_b"], tm=tm_down, bk=D_FF_PAD // 2, sub=sub,
                       seq_rows=sample, seq_len=seq_len, gate_col=5, emit_h2=False)
    return x2, (c1, n1, m1, s1), conv_out, dict(w_out=w_out_bf, w_up=w_up_bf, w_down=w_down_bf)


def kernel(x_prompt, x_sample, state_mlstm_C, state_mlstm_n, state_mlstm_m, state_hgrn_S, cache_ffn_conv,
           c_prompt, c_sample, hgrn_lb_logits, w_ada, b_ada, w_in, b_gate_a, norm_a, norm_b, w_out,
           ln1_g, ln1_b, w_up, conv_w, conv_b, w_down, ln2_g, ln2_b):
    bp, tp, _ = x_prompt.shape
    bs, ts, _ = x_sample.shape
    pad_ff = D_FF_PAD - D_FF
    w_main, w_gate = _prep_w_in(jnp.transpose(w_in[0]))
    weights = dict(
        w_main=w_main, w_gate=w_gate, w_out=w_out[0], w_up=w_up[0],
        w_down=w_down[0],
        gbias=jnp.pad(b_gate_a[0].reshape(1, N_GATE), ((0, 0), (0, LANE - N_GATE))),
        lb_logits=hgrn_lb_logits, norm_a=norm_a, norm_b=norm_b,
        ln1_g=ln1_g, ln1_b=ln1_b, ln2_g=ln2_g, ln2_b=ln2_b,
        conv_w=jnp.pad(conv_w[0], ((0, 0), (0, pad_ff))),
        conv_b=jnp.pad(conv_b, ((0, 0), (0, pad_ff))))

    c_p = jnp.pad(c_prompt, ((0, 8 - bp), (0, 0)))
    mod_p, mod_s = _ada(c_p, c_sample, w_ada[0], b_ada)

    yp, st_p, tail_p, w_ffn_bf = _path(x_prompt.reshape(bp * tp, D_MODEL), mod_p, weights, n_seq=bp,
                                       seq_len=tp, tm_in=1024, tm_res=512, tm_up=2048, tm_down=512)

    xs_t = jnp.swapaxes(x_sample, 0, 1).reshape(ts * bs, D_MODEL)
    cache_t = jnp.swapaxes(cache_ffn_conv[0], 0, 1)
    ys_t, st_s, tail_s, _ = _path(xs_t, mod_s, dict(weights, **w_ffn_bf), n_seq=bs, seq_len=ts,
                                  tm_in=ts * bs, tm_res=ts * bs, tm_up=ts * bs, tm_down=ts * bs,
                                  states=(state_mlstm_C, state_mlstm_n, state_mlstm_m, state_hgrn_S),
                                  cache=cache_t)

    ys = jnp.swapaxes(ys_t.reshape(ts, bs, D_MODEL), 0, 1)
    conv_p = tail_p[None, :, :, :D_FF]
    conv_s = jnp.swapaxes(tail_s[:, :, :D_FF], 0, 1)[None]
    return (yp.reshape(bp, tp, D_MODEL), ys,
            st_p[0], st_p[1], st_p[2], st_p[3], conv_p,
            st_s[0], st_s[1], st_s[2], st_s[3], conv_s)
```

```python
import functools

import numpy as np
import jax
import jax.numpy as jnp
from jax import lax
from jax.experimental import pallas as pl
from jax.experimental.pallas import tpu as pltpu

F32 = jnp.float32
BF16 = jnp.bfloat16

D_MODEL = 2048
N_HEADS_A, DK_A, DV_A = 4, 128, 256
N_HEADS_B, DK_B, DV_B = 8, 128, 128
A_WIDTH = N_HEADS_A * DV_A
B_WIDTH = N_HEADS_B * DV_B
N_GATE = 2 * N_HEADS_A
GATE_COL = 2 * N_HEADS_A * DK_A + A_WIDTH
D_IN = GATE_COL + N_GATE + A_WIDTH + 4 * B_WIDTH
D_FF = 5504
CONV_W = 3
EPS = 1e-5
ALPHA = 2.0 ** 0.25
LOG2_E = 1.4426950408889634
LANE = 128
FF_BLOCK = 512
RES_CHUNK = 512
D_FF_PAD = 5632
Z_MAIN = D_IN - N_GATE
OFF_QA, OFF_KA, OFF_VA, OFF_OA = 0, 512, 1024, 2048
OFF_FB, OFF_QB, OFF_VB, OFF_GB = 3072, 4096, 5120, 6144
VMEM_LIMIT = 60 * 1024 * 1024
PROMPT_CHUNK = 128
PROMPT_SEQS_PER_STEP = 2
SAMPLE_GROUP = 8


def _ln(x):
    mu = jnp.mean(x, axis=-1, keepdims=True)
    xc = x - mu
    var = jnp.mean(xc * xc, axis=-1, keepdims=True)
    return xc * lax.rsqrt(var + EPS)


def _sigmoid(x):
    return 1.0 / (1.0 + jnp.exp(-x))


def _dot(a, b):
    return jnp.dot(a.astype(BF16), b.astype(BF16), preferred_element_type=F32)


def _dot_nt(a, b):
    return lax.dot_general(a.astype(BF16), b.astype(BF16), (((1,), (1,)), ((), ())),
                           preferred_element_type=F32)


def _dot_tn(a, b):
    return lax.dot_general(a.astype(BF16), b.astype(BF16), (((0,), (0,)), ((), ())),
                           preferred_element_type=F32)


def _mod_rows(ref, seq_rows, seq):
    if seq_rows:
        return ref[...]
    return ref[pl.ds(seq, 1), :]


def _ada_kernel(cp_ref, cs_ref, w_ref, b_ref, op_ref, os_ref):
    w = w_ref[...].astype(BF16)
    for c_ref, o_ref in ((cp_ref, op_ref), (cs_ref, os_ref)):
        c = c_ref[...]
        o_ref[...] = _dot(c * _sigmoid(c), w) + b_ref[...]


def _ada(c_p, c_s, w_ada, b_ada):
    n = w_ada.shape[1]
    bn = 1024
    row = lambda r: pl.BlockSpec((r, D_MODEL), lambda j: (0, 0))
    out = lambda r: pl.BlockSpec((r, bn), lambda j: (0, j))
    return pl.pallas_call(
        _ada_kernel,
        grid=(n // bn,),
        in_specs=[row(c_p.shape[0]), row(c_s.shape[0]),
                  pl.BlockSpec((D_MODEL, bn), lambda j: (0, j)),
                  pl.BlockSpec((1, bn), lambda j: (0, j))],
        out_specs=[out(c_p.shape[0]), out(c_s.shape[0])],
        out_shape=[jax.ShapeDtypeStruct((c_p.shape[0], n), F32),
                   jax.ShapeDtypeStruct((c_s.shape[0], n), F32)],
        compiler_params=pltpu.CompilerParams(vmem_limit_bytes=VMEM_LIMIT),
        name="ada",
    )(c_p, c_s, w_ada, b_ada)


def _prep_in_kernel(front_ref, back_ref, gate_ref, o_ref, og_ref, *, first_shifted):
    j = pl.program_id(0)

    @pl.when(j == 0)
    def _():
        og_ref[...] = gate_ref[...].T.astype(BF16)

    def emit(src_ref):
        for r in range(0, src_ref.shape[0], 256):
            o_ref[:, r:r + 256] = src_ref[r:r + 256, :].T.astype(BF16)

    @pl.when(j < first_shifted)
    def _():
        emit(front_ref)

    @pl.when(j >= first_shifted)
    def _():
        emit(back_ref)


def _prep_w_in(w_in_t):
    bn = 1024
    first_shifted = GATE_COL // bn
    kern = functools.partial(_prep_in_kernel, first_shifted=first_shifted)
    return pl.pallas_call(
        kern,
        grid=(Z_MAIN // bn,),
        in_specs=[pl.BlockSpec((bn, D_MODEL), lambda j: (jnp.minimum(j, first_shifted - 1), 0)),
                  pl.BlockSpec((pl.Element(bn), pl.Element(D_MODEL)),
                               lambda j: (pl.multiple_of(jnp.maximum(j, first_shifted) * bn + N_GATE, N_GATE), 0)),
                  pl.BlockSpec((LANE, D_MODEL), lambda j: (GATE_COL // LANE, 0))],
        out_specs=[pl.BlockSpec((D_MODEL, bn), lambda j: (0, j)),
                   pl.BlockSpec((D_MODEL, LANE), lambda j: (0, 0))],
        out_shape=[jax.ShapeDtypeStruct((D_MODEL, Z_MAIN), BF16),
                   jax.ShapeDtypeStruct((D_MODEL, LANE), BF16)],
        compiler_params=pltpu.CompilerParams(vmem_limit_bytes=VMEM_LIMIT),
        name="prep_w_in",
    )(w_in_t, w_in_t, w_in_t)


def _in_kernel(x_ref, sh_ref, sc_ref, w_ref, wg_ref, z_ref, g_ref, h_scr, *, sub, seq_rows, tiles_per_seq):
    i = pl.program_id(0)
    j = pl.program_id(1)

    @pl.when(j == 0)
    def _():
        seq = i // tiles_per_seq
        sh = _mod_rows(sh_ref, seq_rows, seq)
        sc = _mod_rows(sc_ref, seq_rows, seq)
        for r in range(0, x_ref.shape[0], sub):
            rows = slice(r, r + sub)
            h = (_ln(x_ref[rows, :]) * (1.0 + sc) + sh).astype(BF16)
            h_scr[rows, :] = h
            g_ref[rows, :] = _dot(h, wg_ref[...])
            z_ref[rows, :] = _dot(h, w_ref[...]).astype(z_ref.dtype)

    @pl.when(j > 0)
    def _():
        z_ref[...] = _dot(h_scr[...], w_ref[...]).astype(z_ref.dtype)


def _in_proj(x, mod, w_main, w_gate, *, tm, sub, seq_rows, seq_len, z_dtype):
    tokens = x.shape[0]
    bn = 1792
    mod_spec = lambda k: pl.BlockSpec((mod.shape[0], D_MODEL), lambda i, j: (0, k))
    kern = functools.partial(_in_kernel, sub=min(sub, tm), seq_rows=seq_rows, tiles_per_seq=max(seq_len // tm, 1))
    return pl.pallas_call(
        kern,
        grid=(tokens // tm, Z_MAIN // bn),
        in_specs=[pl.BlockSpec((tm, D_MODEL), lambda i, j: (i, 0)),
                  mod_spec(0), mod_spec(1),
                  pl.BlockSpec((D_MODEL, bn), lambda i, j: (0, j)),
                  pl.BlockSpec((D_MODEL, LANE), lambda i, j: (0, 0))],
        out_specs=[pl.BlockSpec((tm, bn), lambda i, j: (i, j)),
                   pl.BlockSpec((tm, LANE), lambda i, j: (i, 0))],
        out_shape=[jax.ShapeDtypeStruct((tokens, Z_MAIN), z_dtype),
                   jax.ShapeDtypeStruct((tokens, LANE), F32)],
        scratch_shapes=[pltpu.VMEM((tm, D_MODEL), BF16)],
        compiler_params=pltpu.CompilerParams(
            dimension_semantics=("arbitrary", "arbitrary"), vmem_limit_bytes=VMEM_LIMIT),
        name="in_proj",
    )(x, mod, mod, w_main, w_gate)


def _mix_consts(L, Tg, t_major):
    G = L // Tg
    r = np.arange(L)
    seq, tim = (r % G, r // G) if t_major else (r // Tg, r % Tg)
    same = seq[:, None] == seq[None, :]
    t = tim[:, None]
    u = tim[None, :]
    mats = [same & (u <= t), same & (u > t)]
    masks = [same & (u <= t), same & (t <= u), same]
    levels = []
    m = Tg // 2
    while m >= 1:
        levels.append(m)
        m //= 2
    for m in levels:
        blk = same & (u // m == t // m)
        odd = (t // m) % 2 == 1
        mats.append(np.where(odd, blk & (u <= t), blk & (u > t)))
        masks.append(same & odd & ((u // m) % 2 == 0) & (t // (2 * m) == u // (2 * m)))
    mats = np.concatenate(mats, axis=0).astype(np.float32)
    masks = np.stack(masks).astype(np.float32)
    last_rows = tuple(int(np.nonzero((seq == g) & (tim == Tg - 1))[0][0]) for g in range(G))
    return mats, masks, tuple(levels), last_rows


def _mix_kernel(z_ref, g_ref, gbias_ref, lbl_ref, na_ref, nb_ref, mall_ref, msk_ref,
                c_in, n_in, m_in, s_in, mix_ref, c_out, n_out, m_out, s_out, e_scr,
                *, L, Tg, levels, last_rows):
    G = L // Tg

    def zs(off, w):
        return z_ref[:, :, off:off + w].reshape(L, w)

    def put_mix(off, w, val):
        mix_ref[:, :, off:off + w] = val.astype(mix_ref.dtype).reshape(Tg, G, w)

    _mix_body(zs, put_mix, g_ref[...].reshape(L, LANE), gbias_ref, lbl_ref, na_ref, nb_ref, mall_ref, msk_ref,
              c_in, n_in, m_in, s_in, c_out, n_out, m_out, s_out, e_scr, pl.program_id(0) * G,
              L=L, Tg=Tg, levels=levels, last_rows=last_rows, t_major=True)


def _mixchunk_kernel(z_ref, g_ref, gbias_ref, lbl_ref, na_ref, nb_ref, mall_ref, msk_ref,
                     mix_ref, c_out, n_out, m_out, s_out, e_scr, *, L, levels, last_rows):
    nb = z_ref.shape[0]
    b0 = pl.program_id(0) * nb

    @pl.when(pl.program_id(1) == 0)
    def _():
        c_out[...] = jnp.zeros(c_out.shape, F32)
        n_out[...] = jnp.zeros(n_out.shape, F32)
        s_out[...] = jnp.zeros(s_out.shape, F32)
        m_out[0, pl.ds(b0, nb), :] = jnp.zeros((nb, N_HEADS_A), F32)

    for k in range(nb):
        def zs(off, w, k=k):
            return z_ref[k, :, off:off + w]

        def put_mix(off, w, val, k=k):
            mix_ref[k, :, off:off + w] = val.astype(mix_ref.dtype)

        st = [r.at[:, pl.ds(k, 1)] for r in (c_out, n_out, s_out)]
        _mix_body(zs, put_mix, g_ref[k], gbias_ref, lbl_ref, na_ref, nb_ref, mall_ref, msk_ref,
                  st[0], st[1], m_out, st[2], st[0], st[1], m_out, st[2], e_scr.at[k], b0 + k,
                  L=L, Tg=L, levels=levels, last_rows=last_rows, t_major=False)


def _mix_body(zs, put_mix, gates, gbias_ref, lbl_ref, na_ref, nb_ref, mall_ref, msk_ref,
              c_in, n_in, m_in, s_in, c_out, n_out, m_out, s_out, e_scr, m_row0,
              *, L, Tg, levels, last_rows, t_major):
    G = L // Tg
    neg_inf = F32(-jnp.inf)
    causal = msk_ref[0] > 0.5
    causal_t = msk_ref[1] > 0.5
    same = msk_ref[2] > 0.5
    row1 = lax.broadcasted_iota(jnp.int32, (L, 1), 0)
    seq_of_row = (row1 & (G - 1)) if t_major else (row1 // Tg)
    in_group = [seq_of_row == g for g in range(G)]
    col1 = lax.broadcasted_iota(jnp.int32, (1, L), 1)
    seq_of_col = (col1 & (G - 1)) if t_major else (col1 // Tg)
    in_group_lane = [seq_of_col == g for g in range(G)]

    def state_updates(kw, v):
        if G == 1:
            return [_dot_tn(kw, v)]
        kw_t = kw.T
        return [_dot(jnp.where(in_group_lane[g], kw_t, 0.0), v) for g in range(G)]

    def by_group(vals):
        if G == 1:
            return vals[0]
        out = jnp.where(in_group[0], vals[0], 0.0)
        for g in range(1, G):
            out = out + jnp.where(in_group[g], vals[g], 0.0)
        return out

    pre = gates + gbias_ref[...]
    lsig = jnp.minimum(pre, 0.0) - jnp.log(1.0 + jnp.exp(-jnp.abs(pre)))
    pre_t = pre.T
    lsig_t = lsig.T
    m_prev = m_in[0, pl.ds(m_row0, G), :]
    scale = DK_A ** -0.5
    m_new_rows = []
    for h in range(N_HEADS_A):
        q = zs(OFF_QA + DK_A * h, DK_A)
        k = zs(OFF_KA + DK_A * h, DK_A)
        v = zs(OFF_VA + DV_A * h, DV_A)
        logi_c = pre[:, h:h + 1]
        logi_r = pre_t[h:h + 1, :]
        lf_c = lsig[:, N_HEADS_A + h:N_HEADS_A + h + 1]
        lf_r = lsig_t[N_HEADS_A + h:N_HEADS_A + h + 1, :]
        b_c = jnp.sum(jnp.where(causal, lf_r, 0.0), axis=1, keepdims=True)
        b_r = jnp.sum(jnp.where(causal_t, lf_c, 0.0), axis=0, keepdims=True)
        tots = [b_c[last_rows[g]:last_rows[g] + 1, :] for g in range(G)]
        tot_c = by_group(tots)
        tot_r = tots[0] if G == 1 else sum(jnp.where(in_group_lane[g], tots[g], 0.0) for g in range(G))
        m_col = by_group([m_prev[g:g + 1, h:h + 1] for g in range(G)])
        dmat = jnp.where(causal, b_c - b_r + logi_r, neg_inf)
        inter = b_c + m_col
        m_t = jnp.maximum(inter, jnp.max(dmat, axis=1, keepdims=True))
        w_inter = jnp.exp(inter - m_t)
        smat = _dot_nt(q, k) * (scale * jnp.exp(dmat - m_t))
        q_c = by_group([_dot(q, c_in[0, g, h]) for g in range(G)]) * scale
        n_rows = by_group([n_in[0, g, h:h + 1, :] for g in range(G)])
        q_n = jnp.sum(q.astype(F32) * n_rows, axis=1, keepdims=True) * scale
        num = w_inter * q_c + _dot(smat, v)
        den = w_inter * q_n + jnp.sum(smat, axis=1, keepdims=True)
        hh = num / jnp.maximum(jnp.abs(den), jnp.exp(-m_t))
        ms = jnp.mean(hh * hh, axis=1, keepdims=True)
        oa = zs(OFF_OA + DV_A * h, DV_A).astype(F32)
        ya = hh * lax.rsqrt(ms + EPS) * na_ref[:, DV_A * h:DV_A * (h + 1)] * _sigmoid(oa)
        put_mix(DV_A * h, DV_A, ya)
        dec_c = tot_c - b_c + logi_c
        dec_r = tot_r - b_r + logi_r
        if G == 1:
            dec_max = jnp.max(dec_r, axis=1, keepdims=True)
        else:
            dec_max = jnp.max(jnp.where(same, dec_r, neg_inf), axis=1, keepdims=True)
        m_new_c = jnp.maximum(tot_c + m_col, dec_max)
        wk = jnp.exp(dec_c - m_new_c)
        sc = jnp.exp(tot_c + m_col - m_new_c)
        kw = k.astype(F32) * wk
        upd = state_updates(kw, v)
        m_new_h = []
        for g in range(G):
            last = last_rows[g]
            kg = kw if G == 1 else jnp.where(in_group[g], kw, 0.0)
            sc_g = sc if G == 1 else sc[last:last + 1, :]
            c_new = sc_g * c_in[0, g, h] + upd[g]
            n_new = sc_g * n_in[0, g, h:h + 1, :] + jnp.sum(kg, axis=0, keepdims=True)
            c_out[0, g, h] = c_new
            n_out[0, g, h:h + 1, :] = n_new
            m_new_h.append(m_new_c if G == 1 else m_new_c[last:last + 1, :])
        m_new_rows.append(m_new_h)
    for g in range(G):
        m_out[0, pl.ds(m_row0 + g, 1), :] = jnp.concatenate(
            [m_new_rows[h][g] for h in range(N_HEADS_A)], axis=1)

    l0 = lbl_ref[0:1, :]
    l1 = lbl_ref[1:2, :]
    lmax = jnp.maximum(l0, l1)
    e0 = jnp.exp(l0 - lmax)
    e1 = jnp.exp(l1 - lmax)
    lb = e0 / (e0 + e1)
    fb = zs(OFF_FB, B_WIDTH).astype(F32)
    e = jnp.exp(-jnp.abs(fb))
    r = 1.0 / (1.0 + e)
    pos = fb >= 0.0
    sig = jnp.where(pos, r, e * r)
    nsig = jnp.where(pos, e * r, r)
    logf = jnp.log(lb + (1.0 - lb) * sig) * LOG2_E
    kb = (1.0 - lb) * nsig
    hi = logf.astype(BF16)
    r1 = logf - hi.astype(F32)
    mid = r1.astype(BF16)
    lo = (r1 - mid.astype(F32)).astype(BF16)
    mall = mall_ref[...]
    if L % 16 == 0:
        e_scr[...] = _dot(mall, jnp.concatenate([hi, mid, lo], axis=0))
    else:
        mall = mall[:, :L]
        e_scr[...] = _dot(mall, hi) + _dot(mall, mid) + _dot(mall, lo)

    dec_t = []
    for g in range(G):
        last = last_rows[g]
        bl = e_scr[last:last + 1, :]
        bl8 = jnp.concatenate([bl[:, DK_B * h:DK_B * (h + 1)] for h in range(N_HEADS_B)], axis=0)
        dec_t.append(jnp.exp2(bl8).T)

    for h in range(N_HEADS_B):
        sl = slice(DK_B * h, DK_B * (h + 1))
        q = zs(OFF_QB + DK_B * h, DK_B).astype(BF16)
        k = kb[:, sl]
        kbf = k.astype(BF16)
        v = zs(OFF_VB + DV_B * h, DV_B)
        amat = jnp.zeros((L, L), F32)
        for li in range(len(levels)):
            ex = jnp.exp2(e_scr[(2 + li) * L:(3 + li) * L, sl]).astype(BF16)
            amat = jnp.where(msk_ref[3 + li] > 0.5, _dot_nt(q * ex, kbf * ex), amat)
        qf = q.astype(F32)
        diag = jnp.sum(qf * k, axis=1, keepdims=True)
        qs = qf * jnp.exp2(e_scr[0:L, sl])
        o_inter = by_group([_dot(qs, s_in[0, g, h]) for g in range(G)])
        o = _dot(amat, v) + diag * v.astype(F32) + o_inter
        ms = jnp.mean(o * o, axis=1, keepdims=True)
        gb = zs(OFF_GB + DV_B * h, DV_B).astype(F32)
        yb = o * lax.rsqrt(ms + EPS) * nb_ref[:, sl] * (gb * _sigmoid(gb))
        put_mix(A_WIDTH + DV_B * h, DV_B, yb)
        upd = state_updates(k * jnp.exp2(e_scr[L:2 * L, sl]), v)
        for g in range(G):
            s_out[0, g, h] = dec_t[g][:, h:h + 1] * s_in[0, g, h] + upd[g]


def _mixer_args(L, Tg, t_major, const2, const3, gbias, lb_logits, norm_a, norm_b):
    mats, masks, levels, last_rows = _mix_consts(L, Tg, t_major)
    n_mats = mats.shape[0] // L
    if L % 16 == 0:
        mats = np.concatenate([mats, mats, mats], axis=1)
    specs = [pl.BlockSpec((1, LANE), const2),
             pl.BlockSpec((2, B_WIDTH), const2),
             pl.BlockSpec((1, A_WIDTH), const2),
             pl.BlockSpec((1, B_WIDTH), const2),
             pl.BlockSpec(mats.shape, const2),
             pl.BlockSpec(masks.shape, const3)]
    args = [gbias, lb_logits, norm_a, norm_b, jnp.asarray(mats, BF16), jnp.asarray(masks)]
    return specs, args, levels, last_rows, n_mats


def _state_shapes(n_seq):
    return [jax.ShapeDtypeStruct((1, n_seq, N_HEADS_A, DK_A, DV_A), F32),
            jax.ShapeDtypeStruct((1, n_seq, N_HEADS_A, DK_A), F32),
            jax.ShapeDtypeStruct((1, n_seq, N_HEADS_A), F32),
            jax.ShapeDtypeStruct((1, n_seq, N_HEADS_B, DK_B, DV_B), F32)]


def _state_specs(G, n_seq, seq_of_step):
    return [pl.BlockSpec((1, G, N_HEADS_A, DK_A, DV_A), lambda *ids: (0, seq_of_step(*ids), 0, 0, 0)),
            pl.BlockSpec((1, G, N_HEADS_A, DK_A), lambda *ids: (0, seq_of_step(*ids), 0, 0)),
            pl.BlockSpec((1, n_seq, N_HEADS_A), lambda *ids: (0, 0, 0)),
            pl.BlockSpec((1, G, N_HEADS_B, DK_B, DV_B), lambda *ids: (0, seq_of_step(*ids), 0, 0, 0))]


def _mixer(z, gates, gbias, lb_logits, norm_a, norm_b, *, n_seq, seq_len, states):
    Tg, G = seq_len, SAMPLE_GROUP
    L = Tg * G
    tile = lambda w: pl.BlockSpec((Tg, G, w), lambda i: (0, i, 0))
    const_specs, const_args, levels, last_rows, n_mats = _mixer_args(
        L, Tg, True, lambda i: (0, 0), lambda i: (0, 0, 0), gbias, lb_logits, norm_a, norm_b)
    st_specs = _state_specs(G, n_seq, lambda i: i)
    kern = functools.partial(_mix_kernel, L=L, Tg=Tg, levels=levels, last_rows=last_rows)
    return pl.pallas_call(
        kern,
        grid=(n_seq // G,),
        in_specs=[tile(Z_MAIN), tile(LANE)] + const_specs + st_specs,
        out_specs=[tile(D_MODEL)] + st_specs,
        out_shape=[jax.ShapeDtypeStruct((Tg, n_seq, D_MODEL), F32)] + _state_shapes(n_seq),
        scratch_shapes=[pltpu.VMEM((n_mats * L, B_WIDTH), F32)],
        compiler_params=pltpu.CompilerParams(dimension_semantics=("arbitrary",), vmem_limit_bytes=VMEM_LIMIT),
        name="mixer_state",
    )(z, gates, *const_args, *states)


def _mixer_chunk(z, gates, gbias, lb_logits, norm_a, norm_b, *, n_seq, seq_len):
    L = PROMPT_CHUNK
    nb = PROMPT_SEQS_PER_STEP
    nchunks = seq_len // L
    tile = lambda w: pl.BlockSpec((nb, L, w), lambda b, c: (b, c, 0))
    const_specs, const_args, levels, last_rows, n_mats = _mixer_args(
        L, L, False, lambda b, c: (0, 0), lambda b, c: (0, 0, 0), gbias, lb_logits, norm_a, norm_b)
    st_specs = _state_specs(nb, n_seq, lambda b, c: b)
    kern = functools.partial(_mixchunk_kernel, L=L, levels=levels, last_rows=last_rows)
    outs = pl.pallas_call(
        kern,
        grid=(n_seq // nb, nchunks),
        in_specs=[tile(Z_MAIN), tile(LANE)] + const_specs,
        out_specs=[tile(D_MODEL)] + st_specs,
        out_shape=[jax.ShapeDtypeStruct((n_seq, seq_len, D_MODEL), BF16)] + _state_shapes(n_seq),
        scratch_shapes=[pltpu.VMEM((nb, n_mats * L, B_WIDTH), F32)],
        compiler_params=pltpu.CompilerParams(dimension_semantics=("arbitrary", "arbitrary"),
                                             vmem_limit_bytes=VMEM_LIMIT),
        name="mixer_chunk",
    )(z.reshape(n_seq, seq_len, Z_MAIN), gates.reshape(n_seq, seq_len, LANE), *const_args)
    return [outs[0].reshape(n_seq * seq_len, D_MODEL)] + list(outs[1:])


def _res_kernel(*refs, sub, seq_rows, tiles_per_seq, emit_h2, nk, cast_w):
    if cast_w:
        refs, wbf_ref = refs[:-1], refs[-1]
    if emit_h2:
        a_ref, w_ref, x_ref, gm_ref, lg_ref, lb_ref, sh_ref, sc_ref, o_ref, h_ref = refs
    else:
        a_ref, w_ref, x_ref, gm_ref, lg_ref, lb_ref, o_ref = refs
    i = pl.program_id(0)
    k = pl.program_id(1)

    if cast_w:
        @pl.when((i == 0) & (k == 0))
        def _():
            wbf_ref[...] = w_ref[...].astype(BF16)
        w_ref = wbf_ref

    if nk > 1:
        @pl.when(k == 0)
        def _():
            o_ref[...] = _dot(a_ref[...], w_ref[...])

    if nk > 2:
        @pl.when((k > 0) & (k < nk - 1))
        def _():
            o_ref[...] += _dot(a_ref[...], w_ref[...])

    @pl.when(k == nk - 1)
    def _():
        seq = i // tiles_per_seq
        gm = _mod_rows(gm_ref, seq_rows, seq)
        if emit_h2:
            sh = _mod_rows(sh_ref, seq_rows, seq)
            sc = _mod_rows(sc_ref, seq_rows, seq)
        for r in range(0, o_ref.shape[0], sub):
            rows = slice(r, r + sub)
            row_sum = jnp.zeros((sub, 1), F32)
            for c in range(0, D_MODEL, RES_CHUNK):
                cols = slice(c, c + RES_CHUNK)
                acc = _dot(a_ref[rows, :], w_ref[:, cols])
                if nk > 1:
                    acc = acc + o_ref[rows, cols]
                y = ALPHA * x_ref[rows, cols] + gm[:, cols] * acc
                o_ref[rows, cols] = y
                row_sum = row_sum + jnp.sum(y, axis=1, keepdims=True)
            yc = o_ref[rows, :] - row_sum * (1.0 / D_MODEL)
            var = jnp.mean(yc * yc, axis=-1, keepdims=True)
            x1 = yc * lax.rsqrt(var + EPS) * lg_ref[...] + lb_ref[...]
            o_ref[rows, :] = x1
            if emit_h2:
                h_ref[rows, :] = (_ln(x1) * (1.0 + sc) + sh).astype(BF16)


def _res_block(a, w, x, mod, ln_g, ln_b, *, tm, bk, sub, seq_rows, seq_len, gate_col, emit_h2):
    tokens, kdim = a.shape
    nk = kdim // bk
    cast_w = w.dtype == F32
    assert not cast_w or nk == 1
    mod_spec = lambda c: pl.BlockSpec((mod.shape[0], D_MODEL), lambda i, k: (0, c))
    row_spec = pl.BlockSpec((tm, D_MODEL), lambda i, k: (i, 0))
    vec_spec = pl.BlockSpec((1, D_MODEL), lambda i, k: (0, 0))
    w_spec = pl.BlockSpec((bk, D_MODEL), lambda i, k: (k, 0),
                          **(dict(pipeline_mode=pl.Buffered(1)) if cast_w else {}))
    in_specs = [pl.BlockSpec((tm, bk), lambda i, k: (i, k)), w_spec,
                row_spec, mod_spec(gate_col), vec_spec, vec_spec]
    args = [a, w, x, mod, ln_g, ln_b]
    out_specs = [row_spec]
    out_shape = [jax.ShapeDtypeStruct((tokens, D_MODEL), F32)]
    if emit_h2:
        in_specs += [mod_spec(3), mod_spec(4)]
        args += [mod, mod]
        out_specs.append(row_spec)
        out_shape.append(jax.ShapeDtypeStruct((tokens, D_MODEL), BF16))
    if cast_w:
        out_specs.append(pl.BlockSpec((kdim, D_MODEL), lambda i, k: (0, 0)))
        out_shape.append(jax.ShapeDtypeStruct((kdim, D_MODEL), BF16))
    kern = functools.partial(_res_kernel, sub=sub, seq_rows=seq_rows, tiles_per_seq=max(seq_len // tm, 1),
                             emit_h2=emit_h2, nk=nk, cast_w=cast_w)
    return pl.pallas_call(
        kern,
        grid=(tokens // tm, nk),
        in_specs=in_specs,
        out_specs=out_specs,
        out_shape=out_shape,
        compiler_params=pltpu.CompilerParams(
            dimension_semantics=("arbitrary", "arbitrary"), vmem_limit_bytes=VMEM_LIMIT),
        name="res_block",
    )(*args)


def _up_kernel(*refs, tm, sub, seq_len, n_seq, t_major, tiles_per_seq):
    if t_major:
        h_ref, wa_scr, wu_scr, cw_ref, cb_ref, cache_ref, g_ref, tail_ref, abuf = refs
    else:
        (h_ref, wa_ref, uu_ref, wd_ref, cw_ref, cb_ref,
         g_ref, tail_ref, wa_scr, wu_scr, wd_out, carry) = refs
    j = pl.program_id(0)
    i = pl.program_id(1)
    bn = wa_scr.shape[1]
    valid = (j * bn + lax.broadcasted_iota(jnp.int32, (1, bn), 1)) < D_FF
    hist = (CONV_W - 1) * n_seq

    if t_major:
        @pl.when(i == 0)
        def _():
            abuf[0:hist, :] = jnp.where(valid, cache_ref[...].reshape(hist, bn), 0.0)
    else:
        last_j = pl.num_programs(0) - 1

        @pl.when((i == 0) & (j < last_j))
        def _():
            wa_scr[...] = wa_ref[...].astype(BF16)
            wu_scr[...] = uu_ref[...].astype(BF16)
            wd_out[...] = wd_ref[...].astype(BF16)

        @pl.when((i == 0) & (j == last_j))
        def _():
            wu = jnp.concatenate([uu_ref[:, LANE:], jnp.zeros((D_MODEL, LANE), F32)], axis=1)
            wa_scr[...] = jnp.where(valid, wa_ref[...], 0.0).astype(BF16)
            wu_scr[...] = jnp.where(valid, wu, 0.0).astype(BF16)
            wd_rows = j * bn + lax.broadcasted_iota(jnp.int32, (bn, 1), 0)
            wd_out[...] = jnp.where(wd_rows < D_FF, wd_ref[...], 0.0).astype(BF16)

    if not t_major:
        @pl.when(i % tiles_per_seq == 0)
        def _():
            carry[...] = jnp.zeros(carry.shape, F32)

    row8 = lax.broadcasted_iota(jnp.int32, (8, 1), 0)
    for r in range(0, tm, sub):
        h = h_ref[r:r + sub, :]
        a = _dot(h, wa_scr[...])
        u = _dot(h, wu_scr[...])
        if t_major:
            abuf[hist + r:hist + r + sub, :] = a
            a1 = abuf[hist + r - n_seq:hist + r - n_seq + sub, :]
            a2 = abuf[hist + r - 2 * n_seq:hist + r - 2 * n_seq + sub, :]
        else:
            p1 = carry[7:8, :]
            p2 = carry[6:7, :]
            r1 = pltpu.roll(a, 1, axis=0)
            r2 = pltpu.roll(a, 2, axis=0)
            a1 = jnp.concatenate([jnp.where(row8 == 0, p1, r1[0:8, :]), r1[8:, :]], axis=0)
            a2 = jnp.concatenate(
                [jnp.where(row8 == 0, p2, jnp.where(row8 == 1, p1, r2[0:8, :])), r2[8:, :]], axis=0)
            carry[...] = a[sub - 8:sub, :]
        conv = cb_ref[...] + cw_ref[0:1, :] * a2 + cw_ref[1:2, :] * a1 + cw_ref[2:3, :] * a
        gl = 0.5 * conv * (1.0 + lax.erf(conv * (2.0 ** -0.5)))
        g_ref[r:r + sub, :] = (gl * u).astype(BF16)

    if t_major:
        tail_ref[...] = abuf[tm:tm + hist, :].reshape(tail_ref.shape)
    else:
        @pl.when((i + 1) % tiles_per_seq == 0)
        def _():
            tail_ref[0] = carry[8 - (CONV_W - 1):8, :]


def _up_proj(h2, w_up, w_down, conv_w_p, conv_b_p, *, tm, sub, seq_len, n_seq, cache=None):
    tokens = h2.shape[0]
    t_major = cache is not None
    bn = FF_BLOCK
    nj = D_FF_PAD // bn
    assert D_FF_PAD - D_FF == LANE
    tiles_per_seq = max(seq_len // tm, 1)
    wspec = lambda f: pl.BlockSpec((D_MODEL, bn), f)
    col_spec = wspec(lambda j, i: (0, j))
    h_spec = pl.BlockSpec((tm, D_MODEL), lambda j, i: (i, 0))
    conv_specs = [pl.BlockSpec((CONV_W, bn), lambda j, i: (0, j)), pl.BlockSpec((1, bn), lambda j, i: (0, j))]
    out_specs = [pl.BlockSpec((tm, bn), lambda j, i: (i, j))]
    out_shape = [jax.ShapeDtypeStruct((tokens, D_FF_PAD), BF16)]
    if t_major:
        hist_spec = pl.BlockSpec((CONV_W - 1, n_seq, bn), lambda j, i: (0, 0, j))
        in_specs = [h_spec, col_spec, col_spec] + conv_specs + [hist_spec]
        args = [h2, w_up[0], w_up[1], conv_w_p, conv_b_p, cache]
        out_specs.append(hist_spec)
        out_shape.append(jax.ShapeDtypeStruct((CONV_W - 1, n_seq, D_FF_PAD), F32))
        scratch_rows = (CONV_W - 1) * n_seq + tm
    else:
        row_spec = pl.BlockSpec((bn, D_MODEL), lambda j, i: (j, 0))
        u_off = lambda j, i: (0, pl.multiple_of(jnp.minimum(D_FF + bn * j, 2 * D_FF - bn), LANE))
        in_specs = [h_spec, col_spec,
                    pl.BlockSpec((pl.Element(D_MODEL), pl.Element(bn)), u_off),
                    row_spec] + conv_specs
        args = [h2, w_up, w_up, w_down, conv_w_p, conv_b_p]
        out_specs += [pl.BlockSpec((1, CONV_W - 1, bn), lambda j, i: (i // tiles_per_seq, 0, j)),
                      col_spec, col_spec, row_spec]
        out_shape += [jax.ShapeDtypeStruct((n_seq, CONV_W - 1, D_FF_PAD), F32),
                      jax.ShapeDtypeStruct((D_MODEL, D_FF_PAD), BF16),
                      jax.ShapeDtypeStruct((D_MODEL, D_FF_PAD), BF16),
                      jax.ShapeDtypeStruct((D_FF_PAD, D_MODEL), BF16)]
        scratch_rows = 8
    kern = functools.partial(_up_kernel, tm=tm, sub=sub, seq_len=seq_len, n_seq=n_seq, t_major=t_major,
                             tiles_per_seq=tiles_per_seq)
    outs = pl.pallas_call(
        kern,
        grid=(nj, tokens // tm),
        in_specs=in_specs,
        out_specs=out_specs,
        out_shape=out_shape,
        scratch_shapes=[pltpu.VMEM((scratch_rows, bn), F32)],
        compiler_params=pltpu.CompilerParams(
            dimension_semantics=("arbitrary", "arbitrary"), vmem_limit_bytes=VMEM_LIMIT),
        name="up_proj",
    )(*args)
    if t_major:
        return outs[0], outs[1], w_up, w_down
    return outs[0], outs[1], (outs[2], outs[3]), outs[4]


def _path(x, mod, w, *, n_seq, seq_len, tm_in, tm_res, tm_up, tm_down, states=None, cache=None):
    sample = states is not None
    sub = n_seq if sample else 256
    sub_up = n_seq if sample else 512
    mixer_w = (w["gbias"], w["lb_logits"], w["norm_a"], w["norm_b"])
    z, gates = _in_proj(x, mod, w["w_main"], w["w_gate"], tm=tm_in, sub=sub, seq_rows=sample,
                        seq_len=seq_len, z_dtype=F32 if sample else BF16)
    if sample:
        mix, c1, n1, m1, s1 = _mixer(z.reshape(seq_len, n_seq, Z_MAIN), gates.reshape(seq_len, n_seq, LANE),
                                     *mixer_w, n_seq=n_seq, seq_len=seq_len, states=states)
        mix = mix.reshape(n_seq * seq_len, D_MODEL)
    else:
        mix, c1, n1, m1, s1 = _mixer_chunk(z, gates, *mixer_w, n_seq=n_seq, seq_len=seq_len)
    x1, h2, *w_out_bf = _res_block(mix, w["w_out"], x, mod, w["ln1_g"], w["ln1_b"], tm=tm_res, bk=D_MODEL,
                                   sub=sub, seq_rows=sample, seq_len=seq_len, gate_col=2, emit_h2=True)
    w_out_bf = w_out_bf[0] if w_out_bf else w["w_out"]
    g, conv_out, w_up_bf, w_down_bf = _up_proj(h2, w["w_up"], w["w_down"], w["conv_w"], w["conv_b"], tm=tm_up,
                                               sub=sub_up, seq_len=seq_len, n_seq=n_seq, cache=cache)
    (x2,) = _res_block(g, w_down_bf, x1, mod, w["ln2_g"], w["ln2_b"], tm=tm_down, bk=D_FF_PAD // 2, sub=sub,
                       seq_rows=sample, seq_len=seq_len, gate_col=5, emit_h2=False)
    return x2, (c1, n1, m1, s1), conv_out, dict(w_out=w_out_bf, w_up=w_up_bf, w_down=w_down_bf)


def kernel(x_prompt, x_sample, state_mlstm_C, state_mlstm_n, state_mlstm_m, state_hgrn_S, cache_ffn_conv,
           c_prompt, c_sample, hgrn_lb_logits, w_ada, b_ada, w_in, b_gate_a, norm_a, norm_b, w_out,
           ln1_g, ln1_b, w_up, conv_w, conv_b, w_down, ln2_g, ln2_b):
    bp, tp, _ = x_prompt.shape
    bs, ts, _ = x_sample.shape
    pad_ff = D_FF_PAD - D_FF
    w_main, w_gate = _prep_w_in(jnp.transpose(w_in[0]))
    weights = dict(
        w_main=w_main, w_gate=w_gate, w_out=w_out[0], w_up=w_up[0],
        w_down=w_down[0],
        gbias=jnp.pad(b_gate_a[0].reshape(1, N_GATE), ((0, 0), (0, LANE - N_GATE))),
        lb_logits=hgrn_lb_logits, norm_a=norm_a, norm_b=norm_b,
        ln1_g=ln1_g, ln1_b=ln1_b, ln2_g=ln2_g, ln2_b=ln2_b,
        conv_w=jnp.pad(conv_w[0], ((0, 0), (0, pad_ff))),
        conv_b=jnp.pad(conv_b, ((0, 0), (0, pad_ff))))

    c_p = jnp.pad(c_prompt, ((0, 8 - bp), (0, 0)))
    mod_p, mod_s = _ada(c_p, c_sample, w_ada[0], b_ada)

    yp, st_p, tail_p, w_ffn_bf = _path(x_prompt.reshape(bp * tp, D_MODEL), mod_p, weights, n_seq=bp,
                                       seq_len=tp, tm_in=1024, tm_res=512, tm_up=2048, tm_down=512)

    xs_t = jnp.swapaxes(x_sample, 0, 1).reshape(ts * bs, D_MODEL)
    cache_t = jnp.swapaxes(cache_ffn_conv[0], 0, 1)
    ys_t, st_s, tail_s, _ = _path(xs_t, mod_s, dict(weights, **w_ffn_bf), n_seq=bs, seq_len=ts,
                                  tm_in=ts * bs, tm_res=ts * bs, tm_up=ts * bs, tm_down=ts * bs,
                                  states=(state_mlstm_C, state_mlstm_n, state_mlstm_m, state_hgrn_S),
                                  cache=cache_t)

    ys = jnp.swapaxes(ys_t.reshape(ts, bs, D_MODEL), 0, 1)
    conv_p = tail_p[None, :, :, :D_FF]
    conv_s = jnp.swapaxes(tail_s[:, :, :D_FF], 0, 1)[None]
    return (yp.reshape(bp, tp, D_MODEL), ys,
            st_p[0], st_p[1], st_p[2], st_p[3], conv_p,
            st_s[0], st_s[1], st_s[2], st_s[3], conv_s)
```

```python
import functools

import numpy as np
import jax
import jax.numpy as jnp
from jax import lax
from jax.experimental import pallas as pl
from jax.experimental.pallas import tpu as pltpu

F32 = jnp.float32
BF16 = jnp.bfloat16

D_MODEL = 2048
N_HEADS_A, DK_A, DV_A = 4, 128, 256
N_HEADS_B, DK_B, DV_B = 8, 128, 128
A_WIDTH = N_HEADS_A * DV_A
B_WIDTH = N_HEADS_B * DV_B
N_GATE = 2 * N_HEADS_A
GATE_COL = 2 * N_HEADS_A * DK_A + A_WIDTH
D_IN = GATE_COL + N_GATE + A_WIDTH + 4 * B_WIDTH
D_FF = 5504
CONV_W = 3
EPS = 1e-5
ALPHA = 2.0 ** 0.25
LOG2_E = 1.4426950408889634
LANE = 128
FF_BLOCK = 512
D_FF_PAD = 5632
Z_MAIN = D_IN - N_GATE
OFF_QA, OFF_KA, OFF_VA, OFF_OA = 0, 512, 1024, 2048
OFF_FB, OFF_QB, OFF_VB, OFF_GB = 3072, 4096, 5120, 6144
VMEM_LIMIT = 60 * 1024 * 1024
PROMPT_CHUNK = 128
PROMPT_SEQS_PER_STEP = 2
SAMPLE_GROUP = 8


def _ln(x):
    mu = jnp.mean(x, axis=-1, keepdims=True)
    xc = x - mu
    var = jnp.mean(xc * xc, axis=-1, keepdims=True)
    return xc * lax.rsqrt(var + EPS)


def _sigmoid(x):
    return 1.0 / (1.0 + jnp.exp(-x))


def _dot(a, b):
    return jnp.dot(a.astype(BF16), b.astype(BF16), preferred_element_type=F32)


def _dot_nt(a, b):
    return lax.dot_general(a.astype(BF16), b.astype(BF16), (((1,), (1,)), ((), ())),
                           preferred_element_type=F32)


def _dot_tn(a, b):
    return lax.dot_general(a.astype(BF16), b.astype(BF16), (((0,), (0,)), ((), ())),
                           preferred_element_type=F32)


def _mod_rows(ref, seq_rows, seq):
    if seq_rows:
        return ref[...]
    return ref[pl.ds(seq, 1), :]


def _ada_kernel(cp_ref, cs_ref, w_ref, b_ref, op_ref, os_ref):
    w = w_ref[...].astype(BF16)
    for c_ref, o_ref in ((cp_ref, op_ref), (cs_ref, os_ref)):
        c = c_ref[...]
        o_ref[...] = _dot(c * _sigmoid(c), w) + b_ref[...]


def _ada(c_p, c_s, w_ada, b_ada):
    n = w_ada.shape[1]
    bn = 1024
    row = lambda r: pl.BlockSpec((r, D_MODEL), lambda j: (0, 0))
    out = lambda r: pl.BlockSpec((r, bn), lambda j: (0, j))
    return pl.pallas_call(
        _ada_kernel,
        grid=(n // bn,),
        in_specs=[row(c_p.shape[0]), row(c_s.shape[0]),
                  pl.BlockSpec((D_MODEL, bn), lambda j: (0, j)),
                  pl.BlockSpec((1, bn), lambda j: (0, j))],
        out_specs=[out(c_p.shape[0]), out(c_s.shape[0])],
        out_shape=[jax.ShapeDtypeStruct((c_p.shape[0], n), F32),
                   jax.ShapeDtypeStruct((c_s.shape[0], n), F32)],
        compiler_params=pltpu.CompilerParams(vmem_limit_bytes=VMEM_LIMIT),
        name="ada",
    )(c_p, c_s, w_ada, b_ada)


def _prep_in_kernel(front_ref, back_ref, gate_ref, o_ref, og_ref, *, first_shifted):
    j = pl.program_id(0)

    @pl.when(j == 0)
    def _():
        og_ref[...] = gate_ref[...].T.astype(BF16)

    def emit(src_ref):
        for r in range(0, src_ref.shape[0], 256):
            o_ref[:, r:r + 256] = src_ref[r:r + 256, :].T.astype(BF16)

    @pl.when(j < first_shifted)
    def _():
        emit(front_ref)

    @pl.when(j >= first_shifted)
    def _():
        emit(back_ref)


def _prep_w_in(w_in_t):
    bn = 1024
    first_shifted = GATE_COL // bn
    kern = functools.partial(_prep_in_kernel, first_shifted=first_shifted)
    return pl.pallas_call(
        kern,
        grid=(Z_MAIN // bn,),
        in_specs=[pl.BlockSpec((bn, D_MODEL), lambda j: (jnp.minimum(j, first_shifted - 1), 0)),
                  pl.BlockSpec((pl.Element(bn), pl.Element(D_MODEL)),
                               lambda j: (pl.multiple_of(jnp.maximum(j, first_shifted) * bn + N_GATE, N_GATE), 0)),
                  pl.BlockSpec((LANE, D_MODEL), lambda j: (GATE_COL // LANE, 0))],
        out_specs=[pl.BlockSpec((D_MODEL, bn), lambda j: (0, j)),
                   pl.BlockSpec((D_MODEL, LANE), lambda j: (0, 0))],
        out_shape=[jax.ShapeDtypeStruct((D_MODEL, Z_MAIN), BF16),
                   jax.ShapeDtypeStruct((D_MODEL, LANE), BF16)],
        compiler_params=pltpu.CompilerParams(vmem_limit_bytes=VMEM_LIMIT),
        name="prep_w_in",
    )(w_in_t, w_in_t, w_in_t)


def _in_kernel(x_ref, sh_ref, sc_ref, w_ref, wg_ref, z_ref, g_ref, h_scr, *, sub, seq_rows, tiles_per_seq):
    i = pl.program_id(0)
    j = pl.program_id(1)

    @pl.when(j == 0)
    def _():
        seq = i // tiles_per_seq
        sh = _mod_rows(sh_ref, seq_rows, seq)
        sc = _mod_rows(sc_ref, seq_rows, seq)
        for r in range(0, x_ref.shape[0], sub):
            rows = slice(r, r + sub)
            h = (_ln(x_ref[rows, :]) * (1.0 + sc) + sh).astype(BF16)
            h_scr[rows, :] = h
            g_ref[rows, :] = _dot(h, wg_ref[...])
            z_ref[rows, :] = _dot(h, w_ref[...]).astype(z_ref.dtype)

    @pl.when(j > 0)
    def _():
        z_ref[...] = _dot(h_scr[...], w_ref[...]).astype(z_ref.dtype)


def _in_proj(x, mod, w_main, w_gate, *, tm, sub, seq_rows, seq_len, z_dtype):
    tokens = x.shape[0]
    bn = 1792
    mod_spec = lambda k: pl.BlockSpec((mod.shape[0], D_MODEL), lambda i, j: (0, k))
    kern = functools.partial(_in_kernel, sub=min(sub, tm), seq_rows=seq_rows, tiles_per_seq=max(seq_len // tm, 1))
    return pl.pallas_call(
        kern,
        grid=(tokens // tm, Z_MAIN // bn),
        in_specs=[pl.BlockSpec((tm, D_MODEL), lambda i, j: (i, 0)),
                  mod_spec(0), mod_spec(1),
                  pl.BlockSpec((D_MODEL, bn), lambda i, j: (0, j)),
                  pl.BlockSpec((D_MODEL, LANE), lambda i, j: (0, 0))],
        out_specs=[pl.BlockSpec((tm, bn), lambda i, j: (i, j)),
                   pl.BlockSpec((tm, LANE), lambda i, j: (i, 0))],
        out_shape=[jax.ShapeDtypeStruct((tokens, Z_MAIN), z_dtype),
                   jax.ShapeDtypeStruct((tokens, LANE), F32)],
        scratch_shapes=[pltpu.VMEM((tm, D_MODEL), BF16)],
        compiler_params=pltpu.CompilerParams(
            dimension_semantics=("arbitrary", "arbitrary"), vmem_limit_bytes=VMEM_LIMIT),
        name="in_proj",
    )(x, mod, mod, w_main, w_gate)


def _mix_consts(L, Tg, t_major):
    G = L // Tg
    r = np.arange(L)
    seq, tim = (r % G, r // G) if t_major else (r // Tg, r % Tg)
    same = seq[:, None] == seq[None, :]
    t = tim[:, None]
    u = tim[None, :]
    mats = [same & (u <= t), same & (u > t)]
    masks = [same & (u <= t), same & (t <= u), same]
    levels = []
    m = Tg // 2
    while m >= 1:
        levels.append(m)
        m //= 2
    for m in levels:
        blk = same & (u // m == t // m)
        odd = (t // m) % 2 == 1
        mats.append(np.where(odd, blk & (u <= t), blk & (u > t)))
        masks.append(same & odd & ((u // m) % 2 == 0) & (t // (2 * m) == u // (2 * m)))
    mats = np.concatenate(mats, axis=0).astype(np.float32)
    masks = np.stack(masks).astype(np.float32)
    last_rows = tuple(int(np.nonzero((seq == g) & (tim == Tg - 1))[0][0]) for g in range(G))
    return mats, masks, tuple(levels), last_rows


def _mix_kernel(z_ref, g_ref, gbias_ref, lbl_ref, na_ref, nb_ref, mall_ref, msk_ref,
                c_in, n_in, m_in, s_in, mix_ref, c_out, n_out, m_out, s_out, e_scr,
                *, L, Tg, levels, last_rows):
    G = L // Tg

    def zs(off, w):
        return z_ref[:, :, off:off + w].reshape(L, w)

    def put_mix(off, w, val):
        mix_ref[:, :, off:off + w] = val.astype(mix_ref.dtype).reshape(Tg, G, w)

    _mix_body(zs, put_mix, g_ref[...].reshape(L, LANE), gbias_ref, lbl_ref, na_ref, nb_ref, mall_ref, msk_ref,
              c_in, n_in, m_in, s_in, c_out, n_out, m_out, s_out, e_scr, pl.program_id(0) * G,
              L=L, Tg=Tg, levels=levels, last_rows=last_rows, t_major=True)


def _mixchunk_kernel(z_ref, g_ref, gbias_ref, lbl_ref, na_ref, nb_ref, mall_ref, msk_ref,
                     mix_ref, c_out, n_out, m_out, s_out, e_scr, *, L, levels, last_rows):
    nb = z_ref.shape[0]
    b0 = pl.program_id(0) * nb

    @pl.when(pl.program_id(1) == 0)
    def _():
        c_out[...] = jnp.zeros(c_out.shape, F32)
        n_out[...] = jnp.zeros(n_out.shape, F32)
        s_out[...] = jnp.zeros(s_out.shape, F32)
        m_out[0, pl.ds(b0, nb), :] = jnp.zeros((nb, N_HEADS_A), F32)

    for k in range(nb):
        def zs(off, w, k=k):
            return z_ref[k, :, off:off + w]

        def put_mix(off, w, val, k=k):
            mix_ref[k, :, off:off + w] = val.astype(mix_ref.dtype)

        st = [r.at[:, pl.ds(k, 1)] for r in (c_out, n_out, s_out)]
        _mix_body(zs, put_mix, g_ref[k], gbias_ref, lbl_ref, na_ref, nb_ref, mall_ref, msk_ref,
                  st[0], st[1], m_out, st[2], st[0], st[1], m_out, st[2], e_scr.at[k], b0 + k,
                  L=L, Tg=L, levels=levels, last_rows=last_rows, t_major=False)


def _mix_body(zs, put_mix, gates, gbias_ref, lbl_ref, na_ref, nb_ref, mall_ref, msk_ref,
              c_in, n_in, m_in, s_in, c_out, n_out, m_out, s_out, e_scr, m_row0,
              *, L, Tg, levels, last_rows, t_major):
    G = L // Tg
    neg_inf = F32(-jnp.inf)
    causal = msk_ref[0] > 0.5
    causal_t = msk_ref[1] > 0.5
    same = msk_ref[2] > 0.5
    row1 = lax.broadcasted_iota(jnp.int32, (L, 1), 0)
    seq_of_row = (row1 & (G - 1)) if t_major else (row1 // Tg)
    in_group = [seq_of_row == g for g in range(G)]
    col1 = lax.broadcasted_iota(jnp.int32, (1, L), 1)
    seq_of_col = (col1 & (G - 1)) if t_major else (col1 // Tg)
    in_group_lane = [seq_of_col == g for g in range(G)]

    def state_updates(kw, v):
        if G == 1:
            return [_dot_tn(kw, v)]
        kw_t = kw.T
        return [_dot(jnp.where(in_group_lane[g], kw_t, 0.0), v) for g in range(G)]

    def by_group(vals):
        if G == 1:
            return vals[0]
        out = jnp.where(in_group[0], vals[0], 0.0)
        for g in range(1, G):
            out = out + jnp.where(in_group[g], vals[g], 0.0)
        return out

    pre = gates + gbias_ref[...]
    lsig = jnp.minimum(pre, 0.0) - jnp.log(1.0 + jnp.exp(-jnp.abs(pre)))
    pre_t = pre.T
    lsig_t = lsig.T
    m_prev = m_in[0, pl.ds(m_row0, G), :]
    scale = DK_A ** -0.5
    m_new_rows = []
    for h in range(N_HEADS_A):
        q = zs(OFF_QA + DK_A * h, DK_A)
        k = zs(OFF_KA + DK_A * h, DK_A)
        v = zs(OFF_VA + DV_A * h, DV_A)
        logi_c = pre[:, h:h + 1]
        logi_r = pre_t[h:h + 1, :]
        lf_c = lsig[:, N_HEADS_A + h:N_HEADS_A + h + 1]
        lf_r = lsig_t[N_HEADS_A + h:N_HEADS_A + h + 1, :]
        b_c = jnp.sum(jnp.where(causal, lf_r, 0.0), axis=1, keepdims=True)
        b_r = jnp.sum(jnp.where(causal_t, lf_c, 0.0), axis=0, keepdims=True)
        tots = [b_c[last_rows[g]:last_rows[g] + 1, :] for g in range(G)]
        tot_c = by_group(tots)
        tot_r = tots[0] if G == 1 else sum(jnp.where(in_group_lane[g], tots[g], 0.0) for g in range(G))
        m_col = by_group([m_prev[g:g + 1, h:h + 1] for g in range(G)])
        dmat = jnp.where(causal, b_c - b_r + logi_r, neg_inf)
        inter = b_c + m_col
        m_t = jnp.maximum(inter, jnp.max(dmat, axis=1, keepdims=True))
        w_inter = jnp.exp(inter - m_t)
        smat = _dot_nt(q, k) * (scale * jnp.exp(dmat - m_t))
        q_c = by_group([_dot(q, c_in[0, g, h]) for g in range(G)]) * scale
        n_rows = by_group([n_in[0, g, h:h + 1, :] for g in range(G)])
        q_n = jnp.sum(q.astype(F32) * n_rows, axis=1, keepdims=True) * scale
        num = w_inter * q_c + _dot(smat, v)
        den = w_inter * q_n + jnp.sum(smat, axis=1, keepdims=True)
        hh = num / jnp.maximum(jnp.abs(den), jnp.exp(-m_t))
        ms = jnp.mean(hh * hh, axis=1, keepdims=True)
        gate_a = _sigmoid(zs(OFF_OA + DV_A * h, DV_A)).astype(F32)
        ya = hh * lax.rsqrt(ms + EPS) * na_ref[:, DV_A * h:DV_A * (h + 1)] * gate_a
        put_mix(DV_A * h, DV_A, ya)
        dec_c = tot_c - b_c + logi_c
        dec_r = tot_r - b_r + logi_r
        if G == 1:
            dec_max = jnp.max(dec_r, axis=1, keepdims=True)
        else:
            dec_max = jnp.max(jnp.where(same, dec_r, neg_inf), axis=1, keepdims=True)
        m_new_c = jnp.maximum(tot_c + m_col, dec_max)
        wk = jnp.exp(dec_c - m_new_c)
        sc = jnp.exp(tot_c + m_col - m_new_c)
        kw = k.astype(F32) * wk
        upd = state_updates(kw, v)
        m_new_h = []
        for g in range(G):
            last = last_rows[g]
            kg = kw if G == 1 else jnp.where(in_group[g], kw, 0.0)
            sc_g = sc if G == 1 else sc[last:last + 1, :]
            c_new = sc_g * c_in[0, g, h] + upd[g]
            n_new = sc_g * n_in[0, g, h:h + 1, :] + jnp.sum(kg, axis=0, keepdims=True)
            c_out[0, g, h] = c_new
            n_out[0, g, h:h + 1, :] = n_new
            m_new_h.append(m_new_c if G == 1 else m_new_c[last:last + 1, :])
        m_new_rows.append(m_new_h)
    for g in range(G):
        m_out[0, pl.ds(m_row0 + g, 1), :] = jnp.concatenate(
            [m_new_rows[h][g] for h in range(N_HEADS_A)], axis=1)

    l0 = lbl_ref[0:1, :]
    l1 = lbl_ref[1:2, :]
    lmax = jnp.maximum(l0, l1)
    e0 = jnp.exp(l0 - lmax)
    e1 = jnp.exp(l1 - lmax)
    lb = e0 / (e0 + e1)
    fb = zs(OFF_FB, B_WIDTH).astype(F32)
    e = jnp.exp(-jnp.abs(fb))
    r = 1.0 / (1.0 + e)
    pos = fb >= 0.0
    sig = jnp.where(pos, r, e * r)
    nsig = jnp.where(pos, e * r, r)
    logf = jnp.log(lb + (1.0 - lb) * sig) * LOG2_E
    kb = (1.0 - lb) * nsig
    hi = logf.astype(BF16)
    r1 = logf - hi.astype(F32)
    mid = r1.astype(BF16)
    lo = (r1 - mid.astype(F32)).astype(BF16)
    mall = mall_ref[...]
    if L % 16 == 0:
        e_scr[...] = _dot(mall, jnp.concatenate([hi, mid, lo], axis=0))
    else:
        mall = mall[:, :L]
        e_scr[...] = _dot(mall, hi) + _dot(mall, mid) + _dot(mall, lo)

    dec_t = []
    for g in range(G):
        last = last_rows[g]
        bl = e_scr[last:last + 1, :]
        bl8 = jnp.concatenate([bl[:, DK_B * h:DK_B * (h + 1)] for h in range(N_HEADS_B)], axis=0)
        dec_t.append(jnp.exp2(bl8).T)

    for h in range(N_HEADS_B):
        sl = slice(DK_B * h, DK_B * (h + 1))
        q = zs(OFF_QB + DK_B * h, DK_B).astype(BF16)
        k = kb[:, sl]
        kbf = k.astype(BF16)
        v = zs(OFF_VB + DV_B * h, DV_B)
        amat = jnp.zeros((L, L), F32)
        for li in range(len(levels)):
            ex = jnp.exp2(e_scr[(2 + li) * L:(3 + li) * L, sl]).astype(BF16)
            amat = jnp.where(msk_ref[3 + li] > 0.5, _dot_nt(q * ex, kbf * ex), amat)
        qf = q.astype(F32)
        diag = jnp.sum(qf * k, axis=1, keepdims=True)
        qs = qf * jnp.exp2(e_scr[0:L, sl])
        o_inter = by_group([_dot(qs, s_in[0, g, h]) for g in range(G)])
        o = _dot(amat, v) + diag * v.astype(F32) + o_inter
        ms = jnp.mean(o * o, axis=1, keepdims=True)
        gb = zs(OFF_GB + DV_B * h, DV_B)
        yb = o * lax.rsqrt(ms + EPS) * nb_ref[:, sl] * (gb * _sigmoid(gb)).astype(F32)
        put_mix(A_WIDTH + DV_B * h, DV_B, yb)
        upd = state_updates(k * jnp.exp2(e_scr[L:2 * L, sl]), v)
        for g in range(G):
            s_out[0, g, h] = dec_t[g][:, h:h + 1] * s_in[0, g, h] + upd[g]


def _mixer_args(L, Tg, t_major, const2, const3, gbias, lb_logits, norm_a, norm_b):
    mats, masks, levels, last_rows = _mix_consts(L, Tg, t_major)
    n_mats = mats.shape[0] // L
    if L % 16 == 0:
        mats = np.concatenate([mats, mats, mats], axis=1)
    specs = [pl.BlockSpec((1, LANE), const2),
             pl.BlockSpec((2, B_WIDTH), const2),
             pl.BlockSpec((1, A_WIDTH), const2),
             pl.BlockSpec((1, B_WIDTH), const2),
             pl.BlockSpec(mats.shape, const2),
             pl.BlockSpec(masks.shape, const3)]
    args = [gbias, lb_logits, norm_a, norm_b, jnp.asarray(mats, BF16), jnp.asarray(masks)]
    return specs, args, levels, last_rows, n_mats


def _state_shapes(n_seq):
    return [jax.ShapeDtypeStruct((1, n_seq, N_HEADS_A, DK_A, DV_A), F32),
            jax.ShapeDtypeStruct((1, n_seq, N_HEADS_A, DK_A), F32),
            jax.ShapeDtypeStruct((1, n_seq, N_HEADS_A), F32),
            jax.ShapeDtypeStruct((1, n_seq, N_HEADS_B, DK_B, DV_B), F32)]


def _state_specs(G, n_seq, seq_of_step):
    return [pl.BlockSpec((1, G, N_HEADS_A, DK_A, DV_A), lambda *ids: (0, seq_of_step(*ids), 0, 0, 0)),
            pl.BlockSpec((1, G, N_HEADS_A, DK_A), lambda *ids: (0, seq_of_step(*ids), 0, 0)),
            pl.BlockSpec((1, n_seq, N_HEADS_A), lambda *ids: (0, 0, 0)),
            pl.BlockSpec((1, G, N_HEADS_B, DK_B, DV_B), lambda *ids: (0, seq_of_step(*ids), 0, 0, 0))]


def _mixer(z, gates, gbias, lb_logits, norm_a, norm_b, *, n_seq, seq_len, states):
    Tg, G = seq_len, SAMPLE_GROUP
    L = Tg * G
    tile = lambda w: pl.BlockSpec((Tg, G, w), lambda i: (0, i, 0))
    const_specs, const_args, levels, last_rows, n_mats = _mixer_args(
        L, Tg, True, lambda i: (0, 0), lambda i: (0, 0, 0), gbias, lb_logits, norm_a, norm_b)
    st_specs = _state_specs(G, n_seq, lambda i: i)
    kern = functools.partial(_mix_kernel, L=L, Tg=Tg, levels=levels, last_rows=last_rows)
    return pl.pallas_call(
        kern,
        grid=(n_seq // G,),
        in_specs=[tile(Z_MAIN), tile(LANE)] + const_specs + st_specs,
        out_specs=[tile(D_MODEL)] + st_specs,
        out_shape=[jax.ShapeDtypeStruct((Tg, n_seq, D_MODEL), F32)] + _state_shapes(n_seq),
        scratch_shapes=[pltpu.VMEM((n_mats * L, B_WIDTH), F32)],
        compiler_params=pltpu.CompilerParams(dimension_semantics=("arbitrary",), vmem_limit_bytes=VMEM_LIMIT),
        name="mixer_state",
    )(z, gates, *const_args, *states)


def _mixer_chunk(z, gates, gbias, lb_logits, norm_a, norm_b, *, n_seq, seq_len):
    L = PROMPT_CHUNK
    nb = PROMPT_SEQS_PER_STEP
    nchunks = seq_len // L
    tile = lambda w: pl.BlockSpec((nb, L, w), lambda b, c: (b, c, 0))
    const_specs, const_args, levels, last_rows, n_mats = _mixer_args(
        L, L, False, lambda b, c: (0, 0), lambda b, c: (0, 0, 0), gbias, lb_logits, norm_a, norm_b)
    st_specs = _state_specs(nb, n_seq, lambda b, c: b)
    kern = functools.partial(_mixchunk_kernel, L=L, levels=levels, last_rows=last_rows)
    outs = pl.pallas_call(
        kern,
        grid=(n_seq // nb, nchunks),
        in_specs=[tile(Z_MAIN), tile(LANE)] + const_specs,
        out_specs=[tile(D_MODEL)] + st_specs,
        out_shape=[jax.ShapeDtypeStruct((n_seq, seq_len, D_MODEL), BF16)] + _state_shapes(n_seq),
        scratch_shapes=[pltpu.VMEM((nb, n_mats * L, B_WIDTH), F32)],
        compiler_params=pltpu.CompilerParams(dimension_semantics=("arbitrary", "arbitrary"),
                                             vmem_limit_bytes=VMEM_LIMIT),
        name="mixer_chunk",
    )(z.reshape(n_seq, seq_len, Z_MAIN), gates.reshape(n_seq, seq_len, LANE), *const_args)
    return [outs[0].reshape(n_seq * seq_len, D_MODEL)] + list(outs[1:])


def _res_kernel(*refs, sub, seq_rows, tiles_per_seq, emit_h2, nk, cast_w):
    if cast_w:
        refs, wbf_ref = refs[:-1], refs[-1]
    if emit_h2:
        a_ref, w_ref, x_ref, gm_ref, lg_ref, lb_ref, sh_ref, sc_ref, o_ref, h_ref = refs
    else:
        a_ref, w_ref, x_ref, gm_ref, lg_ref, lb_ref, o_ref = refs
    i = pl.program_id(0)
    k = pl.program_id(1)

    if cast_w:
        @pl.when((i == 0) & (k == 0))
        def _():
            wbf_ref[...] = w_ref[...].astype(BF16)
        w_ref = wbf_ref

    if nk > 1:
        @pl.when(k == 0)
        def _():
            o_ref[...] = _dot(a_ref[...], w_ref[...])

    if nk > 2:
        @pl.when((k > 0) & (k < nk - 1))
        def _():
            o_ref[...] += _dot(a_ref[...], w_ref[...])

    @pl.when(k == nk - 1)
    def _():
        seq = i // tiles_per_seq
        gm = _mod_rows(gm_ref, seq_rows, seq)
        if emit_h2:
            sh = _mod_rows(sh_ref, seq_rows, seq)
            sc = _mod_rows(sc_ref, seq_rows, seq)
        for r in range(0, o_ref.shape[0], sub):
            rows = slice(r, r + sub)
            acc = _dot(a_ref[rows, :], w_ref[...])
            if nk > 1:
                acc = acc + o_ref[rows, :]
            x1 = _ln(ALPHA * x_ref[rows, :] + gm * acc) * lg_ref[...] + lb_ref[...]
            o_ref[rows, :] = x1
            if emit_h2:
                h_ref[rows, :] = (_ln(x1) * (1.0 + sc) + sh).astype(BF16)


def _res_block(a, w, x, mod, ln_g, ln_b, *, tm, bk, sub, seq_rows, seq_len, gate_col, emit_h2):
    tokens, kdim = a.shape
    nk = kdim // bk
    cast_w = w.dtype == F32
    assert not cast_w or nk == 1
    mod_spec = lambda c: pl.BlockSpec((mod.shape[0], D_MODEL), lambda i, k: (0, c))
    row_spec = pl.BlockSpec((tm, D_MODEL), lambda i, k: (i, 0))
    vec_spec = pl.BlockSpec((1, D_MODEL), lambda i, k: (0, 0))
    w_spec = pl.BlockSpec((bk, D_MODEL), lambda i, k: (k, 0),
                          **(dict(pipeline_mode=pl.Buffered(1)) if cast_w else {}))
    in_specs = [pl.BlockSpec((tm, bk), lambda i, k: (i, k)), w_spec,
                row_spec, mod_spec(gate_col), vec_spec, vec_spec]
    args = [a, w, x, mod, ln_g, ln_b]
    out_specs = [row_spec]
    out_shape = [jax.ShapeDtypeStruct((tokens, D_MODEL), F32)]
    if emit_h2:
        in_specs += [mod_spec(3), mod_spec(4)]
        args += [mod, mod]
        out_specs.append(row_spec)
        out_shape.append(jax.ShapeDtypeStruct((tokens, D_MODEL), BF16))
    if cast_w:
        out_specs.append(pl.BlockSpec((kdim, D_MODEL), lambda i, k: (0, 0)))
        out_shape.append(jax.ShapeDtypeStruct((kdim, D_MODEL), BF16))
    kern = functools.partial(_res_kernel, sub=sub, seq_rows=seq_rows, tiles_per_seq=max(seq_len // tm, 1),
                             emit_h2=emit_h2, nk=nk, cast_w=cast_w)
    return pl.pallas_call(
        kern,
        grid=(tokens // tm, nk),
        in_specs=in_specs,
        out_specs=out_specs,
        out_shape=out_shape,
        compiler_params=pltpu.CompilerParams(
            dimension_semantics=("arbitrary", "arbitrary"), vmem_limit_bytes=VMEM_LIMIT),
        name="res_block",
    )(*args)


def _up_kernel(*refs, tm, sub, seq_len, n_seq, t_major, tiles_per_seq):
    if t_major:
        h_ref, wa_scr, wu_scr, cw_ref, cb_ref, cache_ref, g_ref, tail_ref, abuf = refs
    else:
        (h_ref, wa_ref, uu_ref, wd_ref, cw_ref, cb_ref,
         g_ref, tail_ref, wa_scr, wu_scr, wd_out, carry) = refs
    j = pl.program_id(0)
    i = pl.program_id(1)
    bn = wa_scr.shape[1]
    valid = (j * bn + lax.broadcasted_iota(jnp.int32, (1, bn), 1)) < D_FF
    hist = (CONV_W - 1) * n_seq

    if t_major:
        @pl.when(i == 0)
        def _():
            abuf[0:hist, :] = jnp.where(valid, cache_ref[...].reshape(hist, bn), 0.0)
    else:
        last_j = pl.num_programs(0) - 1

        @pl.when((i == 0) & (j < last_j))
        def _():
            wa_scr[...] = wa_ref[...].astype(BF16)
            wu_scr[...] = uu_ref[...].astype(BF16)
            wd_out[...] = wd_ref[...].astype(BF16)

        @pl.when((i == 0) & (j == last_j))
        def _():
            wu = jnp.concatenate([uu_ref[:, LANE:], jnp.zeros((D_MODEL, LANE), F32)], axis=1)
            wa_scr[...] = jnp.where(valid, wa_ref[...], 0.0).astype(BF16)
            wu_scr[...] = jnp.where(valid, wu, 0.0).astype(BF16)
            wd_rows = j * bn + lax.broadcasted_iota(jnp.int32, (bn, 1), 0)
            wd_out[...] = jnp.where(wd_rows < D_FF, wd_ref[...], 0.0).astype(BF16)

    if not t_major:
        @pl.when(i % tiles_per_seq == 0)
        def _():
            carry[...] = jnp.zeros(carry.shape, F32)

    row8 = lax.broadcasted_iota(jnp.int32, (8, 1), 0)
    for r in range(0, tm, sub):
        h = h_ref[r:r + sub, :]
        a = _dot(h, wa_scr[...])
        u = _dot(h, wu_scr[...])
        if t_major:
            abuf[hist + r:hist + r + sub, :] = a
            a1 = abuf[hist + r - n_seq:hist + r - n_seq + sub, :]
            a2 = abuf[hist + r - 2 * n_seq:hist + r - 2 * n_seq + sub, :]
        else:
            p1 = carry[7:8, :]
            p2 = carry[6:7, :]
            r1 = pltpu.roll(a, 1, axis=0)
            r2 = pltpu.roll(a, 2, axis=0)
            a1 = jnp.concatenate([jnp.where(row8 == 0, p1, r1[0:8, :]), r1[8:, :]], axis=0)
            a2 = jnp.concatenate(
                [jnp.where(row8 == 0, p2, jnp.where(row8 == 1, p1, r2[0:8, :])), r2[8:, :]], axis=0)
            carry[...] = a[sub - 8:sub, :]
        conv = cb_ref[...] + cw_ref[0:1, :] * a2 + cw_ref[1:2, :] * a1 + cw_ref[2:3, :] * a
        gl = 0.5 * conv * (1.0 + lax.erf(conv * (2.0 ** -0.5)))
        g_ref[r:r + sub, :] = (gl * u).astype(BF16)

    if t_major:
        tail_ref[...] = abuf[tm:tm + hist, :].reshape(tail_ref.shape)
    else:
        @pl.when((i + 1) % tiles_per_seq == 0)
        def _():
            tail_ref[0] = carry[8 - (CONV_W - 1):8, :]


def _up_proj(h2, w_up, w_down, conv_w_p, conv_b_p, *, tm, sub, seq_len, n_seq, cache=None):
    tokens = h2.shape[0]
    t_major = cache is not None
    bn = FF_BLOCK
    nj = D_FF_PAD // bn
    assert D_FF_PAD - D_FF == LANE
    tiles_per_seq = max(seq_len // tm, 1)
    wspec = lambda f: pl.BlockSpec((D_MODEL, bn), f)
    col_spec = wspec(lambda j, i: (0, j))
    h_spec = pl.BlockSpec((tm, D_MODEL), lambda j, i: (i, 0))
    conv_specs = [pl.BlockSpec((CONV_W, bn), lambda j, i: (0, j)), pl.BlockSpec((1, bn), lambda j, i: (0, j))]
    out_specs = [pl.BlockSpec((tm, bn), lambda j, i: (i, j))]
    out_shape = [jax.ShapeDtypeStruct((tokens, D_FF_PAD), BF16)]
    if t_major:
        hist_spec = pl.BlockSpec((CONV_W - 1, n_seq, bn), lambda j, i: (0, 0, j))
        in_specs = [h_spec, col_spec, col_spec] + conv_specs + [hist_spec]
        args = [h2, w_up[0], w_up[1], conv_w_p, conv_b_p, cache]
        out_specs.append(hist_spec)
        out_shape.append(jax.ShapeDtypeStruct((CONV_W - 1, n_seq, D_FF_PAD), F32))
        scratch_rows = (CONV_W - 1) * n_seq + tm
    else:
        row_spec = pl.BlockSpec((bn, D_MODEL), lambda j, i: (j, 0))
        u_off = lambda j, i: (0, pl.multiple_of(jnp.minimum(D_FF + bn * j, 2 * D_FF - bn), LANE))
        in_specs = [h_spec, col_spec,
                    pl.BlockSpec((pl.Element(D_MODEL), pl.Element(bn)), u_off),
                    row_spec] + conv_specs
        args = [h2, w_up, w_up, w_down, conv_w_p, conv_b_p]
        out_specs += [pl.BlockSpec((1, CONV_W - 1, bn), lambda j, i: (i // tiles_per_seq, 0, j)),
                      col_spec, col_spec, row_spec]
        out_shape += [jax.ShapeDtypeStruct((n_seq, CONV_W - 1, D_FF_PAD), F32),
                      jax.ShapeDtypeStruct((D_MODEL, D_FF_PAD), BF16),
                      jax.ShapeDtypeStruct((D_MODEL, D_FF_PAD), BF16),
                      jax.ShapeDtypeStruct((D_FF_PAD, D_MODEL), BF16)]
        scratch_rows = 8
    kern = functools.partial(_up_kernel, tm=tm, sub=sub, seq_len=seq_len, n_seq=n_seq, t_major=t_major,
                             tiles_per_seq=tiles_per_seq)
    outs = pl.pallas_call(
        kern,
        grid=(nj, tokens // tm),
        in_specs=in_specs,
        out_specs=out_specs,
        out_shape=out_shape,
        scratch_shapes=[pltpu.VMEM((scratch_rows, bn), F32)],
        compiler_params=pltpu.CompilerParams(
            dimension_semantics=("arbitrary", "arbitrary"), vmem_limit_bytes=VMEM_LIMIT),
        name="up_proj",
    )(*args)
    if t_major:
        return outs[0], outs[1], w_up, w_down
    return outs[0], outs[1], (outs[2], outs[3]), outs[4]


def _path(x, mod, w, *, n_seq, seq_len, tm_in, tm_res, tm_up, tm_down, states=None, cache=None):
    sample = states is not None
    sub = n_seq if sample else 256
    sub_up = n_seq if sample else 512
    mixer_w = (w["gbias"], w["lb_logits"], w["norm_a"], w["norm_b"])
    z, gates = _in_proj(x, mod, w["w_main"], w["w_gate"], tm=tm_in, sub=sub, seq_rows=sample,
                        seq_len=seq_len, z_dtype=F32 if sample else BF16)
    if sample:
        mix, c1, n1, m1, s1 = _mixer(z.reshape(seq_len, n_seq, Z_MAIN), gates.reshape(seq_len, n_seq, LANE),
                                     *mixer_w, n_seq=n_seq, seq_len=seq_len, states=states)
        mix = mix.reshape(n_seq * seq_len, D_MODEL)
    else:
        mix, c1, n1, m1, s1 = _mixer_chunk(z, gates, *mixer_w, n_seq=n_seq, seq_len=seq_len)
    x1, h2, *w_out_bf = _res_block(mix, w["w_out"], x, mod, w["ln1_g"], w["ln1_b"], tm=tm_res, bk=D_MODEL,
                                   sub=sub, seq_rows=sample, seq_len=seq_len, gate_col=2, emit_h2=True)
    w_out_bf = w_out_bf[0] if w_out_bf else w["w_out"]
    g, conv_out, w_up_bf, w_down_bf = _up_proj(h2, w["w_up"], w["w_down"], w["conv_w"], w["conv_b"], tm=tm_up,
                                               sub=sub_up, seq_len=seq_len, n_seq=n_seq, cache=cache)
    (x2,) = _res_block(g, w_down_bf, x1, mod, w["ln2_g"], w["ln2_b"], tm=tm_down, bk=D_FF_PAD // 2, sub=sub,
                       seq_rows=sample, seq_len=seq_len, gate_col=5, emit_h2=False)
    return x2, (c1, n1, m1, s1), conv_out, dict(w_out=w_out_bf, w_up=w_up_bf, w_down=w_down_bf)


def kernel(x_prompt, x_sample, state_mlstm_C, state_mlstm_n, state_mlstm_m, state_hgrn_S, cache_ffn_conv,
           c_prompt, c_sample, hgrn_lb_logits, w_ada, b_ada, w_in, b_gate_a, norm_a, norm_b, w_out,
           ln1_g, ln1_b, w_up, conv_w, conv_b, w_down, ln2_g, ln2_b):
    bp, tp, _ = x_prompt.shape
    bs, ts, _ = x_sample.shape
    pad_ff = D_FF_PAD - D_FF
    w_main, w_gate = _prep_w_in(jnp.transpose(w_in[0]))
    weights = dict(
        w_main=w_main, w_gate=w_gate, w_out=w_out[0], w_up=w_up[0],
        w_down=w_down[0],
        gbias=jnp.pad(b_gate_a[0].reshape(1, N_GATE), ((0, 0), (0, LANE - N_GATE))),
        lb_logits=hgrn_lb_logits, norm_a=norm_a, norm_b=norm_b,
        ln1_g=ln1_g, ln1_b=ln1_b, ln2_g=ln2_g, ln2_b=ln2_b,
        conv_w=jnp.pad(conv_w[0], ((0, 0), (0, pad_ff))),
        conv_b=jnp.pad(conv_b, ((0, 0), (0, pad_ff))))

    c_p = jnp.pad(c_prompt, ((0, 8 - bp), (0, 0)))
    mod_p, mod_s = _ada(c_p, c_sample, w_ada[0], b_ada)

    yp, st_p, tail_p, w_ffn_bf = _path(x_prompt.reshape(bp * tp, D_MODEL), mod_p, weights, n_seq=bp,
                                       seq_len=tp, tm_in=1024, tm_res=512, tm_up=2048, tm_down=512)

    xs_t = jnp.swapaxes(x_sample, 0, 1).reshape(ts * bs, D_MODEL)
    cache_t = jnp.swapaxes(cache_ffn_conv[0], 0, 1)
    ys_t, st_s, tail_s, _ = _path(xs_t, mod_s, dict(weights, **w_ffn_bf), n_seq=bs, seq_len=ts,
                                  tm_in=ts * bs, tm_res=ts * bs, tm_up=ts * bs, tm_down=ts * bs,
                                  states=(state_mlstm_C, state_mlstm_n, state_mlstm_m, state_hgrn_S),
                                  cache=cache_t)

    ys = jnp.swapaxes(ys_t.reshape(ts, bs, D_MODEL), 0, 1)
    conv_p = tail_p[None, :, :, :D_FF]
    conv_s = jnp.swapaxes(tail_s[:, :, :D_FF], 0, 1)[None]
    return (yp.reshape(bp, tp, D_MODEL), ys,
            st_p[0], st_p[1], st_p[2], st_p[3], conv_p,
            st_s[0], st_s[1], st_s[2], st_s[3], conv_s)
```
